```python
import math
import jax
import jax.numpy as jnp
from jax import lax
import numpy as np

D_MODEL = 1024
BATCH = 16
SEQ = 2048
DEPTH = 4

GRID_W = 64
CTX_LEN = 256
N_MIXERS = 4
N_MOD = 9
BLOCK_Q = 128
ROPE_BASE = 10000.0
EPS = 1e-6
NEG_INF = -1e30
D_FF = 2816

GQA_HEADS = 16
GQA_KV_HEADS = 4
GQA_HEAD_DIM = 64
GQA_WINDOW = 128

DIFF_HEADS = 8
DIFF_HEAD_DIM = 64

HGRN_HEADS = 8
HGRN_KEY_DIM = 128
HGRN_VAL_DIM = D_MODEL // HGRN_HEADS
HGRN_CHUNK = 64

MLA_HEADS = 16
MLA_Q_LORA = 256
MLA_KV_LORA = 256
MLA_NOPE = 64
MLA_ROPE = 32
MLA_V_DIM = 64

kernel_name = 'hybrid_interleaved_diffusion_block'


def _rmsnorm(x, w):
    xf = x.astype(jnp.float32)
    y = xf * lax.rsqrt(jnp.mean(xf * xf, axis=-1, keepdims=True) + EPS)
    return (y * w.astype(jnp.float32)).astype(x.dtype)


def _pre(h, gain, shift, scale):
    return _rmsnorm(h, gain) * (1.0 + scale) + shift


def _swiglu(h, w_gate, w_up, w_down):
    return (jax.nn.silu(h @ w_gate) * (h @ w_up)) @ w_down


def _axial_rope_tables(n_tokens, rot_dim):
    rows = n_tokens // GRID_W
    row = jnp.repeat(jnp.arange(rows, dtype=jnp.float32), GRID_W)
    col = jnp.tile(jnp.arange(GRID_W, dtype=jnp.float32), rows)
    axis_dim = rot_dim // 2
    inv_freq = ROPE_BASE ** (-jnp.arange(0, axis_dim, 2, dtype=jnp.float32) / axis_dim)
    ang_r = row[:, None] * inv_freq[None, :]
    ang_c = col[:, None] * inv_freq[None, :]
    ang = jnp.concatenate([ang_r, ang_r, ang_c, ang_c], axis=-1)
    return jnp.cos(ang), jnp.sin(ang)


def _apply_rope(x, cos, sin):
    a = x.shape[-1] // 2
    f = a // 2
    rot = jnp.concatenate([-x[..., f:a], x[..., :f], -x[..., a + f:], x[..., a:a + f]], axis=-1)
    out = x.astype(jnp.float32) * cos[None, :, None, :] + rot.astype(jnp.float32) * sin[None, :, None, :]
    return out.astype(x.dtype)


def _to_blocks(a):
    b, s = a.shape[:2]
    return jnp.moveaxis(a.reshape((b, s // BLOCK_Q, BLOCK_Q) + a.shape[2:]), 1, 0)


def _from_blocks(o):
    nb, b, bq = o.shape[:3]
    return jnp.moveaxis(o, 0, 1).reshape((b, nb * bq) + o.shape[3:])


def _gqa_window_mixer(hx, hc, w_in, w_out, sinks, cos, sin, with_ctx_out):
    B, S, _ = hx.shape
    H, KV, HD = GQA_HEADS, GQA_KV_HEADS, GQA_HEAD_DIM
    G = H // KV
    QD, KD = H * HD, KV * HD
    scale = HD ** -0.5
    sink = sinks.astype(jnp.float32).reshape(1, KV, G, 1, 1)

    def proj_q(h):
        return (h @ w_in[:, :QD]).reshape(h.shape[0], h.shape[1], H, HD)

    def proj_kv(h):
        p = h @ w_in[:, QD:]
        return (p[..., :KD].reshape(h.shape[0], h.shape[1], KV, HD),
                p[..., KD:].reshape(h.shape[0], h.shape[1], KV, HD))

    def sink_softmax(s):
        m = jnp.maximum(jnp.max(s, axis=-1, keepdims=True), sink)
        e = jnp.exp(s - m)
        return e / (jnp.sum(e, axis=-1, keepdims=True) + jnp.exp(sink - m))

    kx, vx = proj_kv(hx)
    kc, vc = proj_kv(hc)
    kx = _apply_rope(kx, cos, sin)
    qx = _apply_rope(proj_q(hx), cos, sin).reshape(B, S, KV, G, HD)

    span = BLOCK_Q + 2 * GQA_WINDOW
    pad = ((0, 0), (GQA_WINDOW, GQA_WINDOW), (0, 0), (0, 0))
    kxp, vxp = jnp.pad(kx, pad), jnp.pad(vx, pad)
    k_off = jnp.arange(span) - GQA_WINDOW
    band = jnp.abs(jnp.arange(BLOCK_Q)[:, None] - k_off[None, :]) <= GQA_WINDOW

    def block(args):
        start, qb = args
        kb = lax.dynamic_slice_in_dim(kxp, start, span, axis=1)
        vb = lax.dynamic_slice_in_dim(vxp, start, span, axis=1)
        k_abs = start + k_off
        mask = band & ((k_abs >= 0) & (k_abs < S))[None, :]
        s_loc = jnp.where(mask, jnp.einsum('bqkgd,bskd->bkgqs', qb, kb).astype(jnp.float32) * scale, NEG_INF)
        s_ctx = jnp.einsum('bqkgd,bckd->bkgqc', qb, kc).astype(jnp.float32) * scale
        p = sink_softmax(jnp.concatenate([s_loc, s_ctx], axis=-1)).astype(vb.dtype)
        return (jnp.einsum('bkgqs,bskd->bqkgd', p[..., :span], vb)
                + jnp.einsum('bkgqc,bckd->bqkgd', p[..., span:], vc))

    starts = jnp.arange(S // BLOCK_Q, dtype=jnp.int32) * BLOCK_Q
    ox = _from_blocks(lax.map(block, (starts, _to_blocks(qx))))
    ox = ox.reshape(B, S, QD) @ w_out
    oc = None
    if with_ctx_out:
        bc, nc = hc.shape[:2]
        qc = proj_q(hc).reshape(bc, nc, KV, G, HD)
        s = jnp.einsum('bqkgd,bckd->bkgqc', qc, kc).astype(jnp.float32) * scale
        p = sink_softmax(s).astype(vc.dtype)
        oc = jnp.einsum('bkgqc,bckd->bqkgd', p, vc).reshape(bc, nc, QD) @ w_out
    return ox, oc


def _diff_attn_mixer(hx, hc, w_in, w_out, lam_params, subln_w, lambda_init, cos, sin, with_ctx_out):
    B, S, _ = hx.shape
    H, HD = DIFF_HEADS, DIFF_HEAD_DIM
    QK = H * 2 * HD
    scale = HD ** -0.5
    lp = lam_params.astype(jnp.float32)
    lam = jnp.exp(jnp.sum(lp[0] * lp[1])) - jnp.exp(jnp.sum(lp[2] * lp[3])) + lambda_init

    def proj_q(h):
        return (h @ w_in[:, :QK]).reshape(h.shape[0], h.shape[1], H, 2, HD)

    def proj_kv(h):
        p = h @ w_in[:, QK:]
        return (p[..., :QK].reshape(h.shape[0], h.shape[1], H, 2, HD),
                p[..., QK:].reshape(h.shape[0], h.shape[1], H, 2 * HD))

    def rope(t):
        return _apply_rope(t.reshape(B, S, H * 2, HD), cos, sin).reshape(B, S, H, 2, HD)

    def diff_probs(s):
        p = jax.nn.softmax(s * scale, axis=-1)
        return p[:, :, 0] - lam * p[:, :, 1]

    def head_out(o):
        o = _rmsnorm(o, subln_w) * (1.0 - lambda_init)
        return o.reshape(o.shape[0], o.shape[1], H * 2 * HD) @ w_out

    kx, vx = proj_kv(hx)
    kc, vc = proj_kv(hc)
    kx = rope(kx)
    qx = rope(proj_q(hx))

    def block(qb):
        s_lat = jnp.einsum('bqhjd,bkhjd->bhjqk', qb, kx)
        s_ctx = jnp.einsum('bqhjd,bchjd->bhjqc', qb, kc)
        p = diff_probs(jnp.concatenate([s_lat, s_ctx], axis=-1).astype(jnp.float32)).astype(vx.dtype)
        return (jnp.einsum('bhqk,bkhe->bqhe', p[..., :S], vx)
                + jnp.einsum('bhqc,bche->bqhe', p[..., S:], vc))

    ox = head_out(_from_blocks(lax.map(block, _to_blocks(qx))))
    oc = None
    if with_ctx_out:
        qc = proj_q(hc)
        p = diff_probs(jnp.einsum('bqhjd,bchjd->bhjqc', qc, kc).astype(jnp.float32)).astype(vc.dtype)
        oc = head_out(jnp.einsum('bhqc,bche->bqhe', p, vc))
    return ox, oc


def _hgrn2_chunk_scan(q, k, v, logf, s0):
    B, T, H, DK = q.shape
    DV = v.shape[-1]
    C = HGRN_CHUNK
    N = T // C

    def chunks(a):
        return a.reshape(B, N, C, H, a.shape[-1]).transpose(1, 0, 3, 2, 4)

    q, k, v, logf = chunks(q), chunks(k), chunks(v), chunks(logf)
    g = jnp.cumsum(logf, axis=3)
    g_mid = g[:, :, :, C // 2:C // 2 + 1, :]
    g_last = g[:, :, :, C - 1:, :]
    a = jnp.einsum('nbhtd,nbhsd->nbhts', q * jnp.exp(g - g_mid), k * jnp.exp(g_mid - g))
    lower = jnp.arange(C)[:, None] >= jnp.arange(C)[None, :]
    a = jnp.where(lower, a, 0.0)
    o_intra = jnp.einsum('nbhts,nbhse->nbhte', a, v)

    def step(state, inp):
        q_in, k_out, v_c, decay = inp
        o = jnp.einsum('bhtd,bhde->bhte', q_in, state)
        state = state * decay[..., None] + jnp.einsum('bhsd,bhse->bhde', k_out, v_c)
        return state, o

    s_fin, o_inter = lax.scan(step, s0, (q * jnp.exp(g), k * jnp.exp(g_last - g), v, jnp.exp(g_last[:, :, :, 0, :])))
    o = (o_intra + o_inter).transpose(1, 0, 3, 2, 4).reshape(B, T, H, DV)
    return o, s_fin


def _hgrn2_mixer(hx, hc, w_in, w_out, norm_w, lb, with_ctx_out):
    H, DK, DV = HGRN_HEADS, HGRN_KEY_DIM, HGRN_VAL_DIM
    KD, VD = H * DK, H * DV

    def proj(h):
        b, t = h.shape[:2]
        p = (h @ w_in).astype(jnp.float32)
        q = jax.nn.silu(p[..., :KD]).reshape(b, t, H, DK)

        def forget(z):
            f = lb + (1.0 - lb) * jax.nn.sigmoid(z)
            return jnp.log(f).reshape(b, t, H, DK), (1.0 - f).reshape(b, t, H, DK)

        fwd = forget(p[..., KD:2 * KD])
        bwd = forget(p[..., 2 * KD:3 * KD])
        v = p[..., 3 * KD:3 * KD + VD].reshape(b, t, H, DV)
        gate = p[..., 3 * KD + VD:]
        return q, fwd, bwd, v, gate

    def flip(a):
        return jnp.flip(a, axis=1)

    def readout(o, gate, dtype):
        b, t = o.shape[:2]
        y = _rmsnorm(o, norm_w).reshape(b, t, VD) * jax.nn.silu(gate)
        return y.astype(dtype) @ w_out

    qc, (lfc, kfc), (lbc, kbc), vc, gc = proj(hc)
    qx, (lfx, kfx), (lbx, kbx), vx, gx = proj(hx)
    zero = jnp.zeros((hx.shape[0], H, DK, DV), jnp.float32)
    oc_f, sc_f = _hgrn2_chunk_scan(qc, kfc, vc, lfc, zero)
    oc_b, sc_b = _hgrn2_chunk_scan(flip(qc), flip(kbc), flip(vc), flip(lbc), zero)
    ox_f, _ = _hgrn2_chunk_scan(qx, kfx, vx, lfx, sc_f)
    ox_b, _ = _hgrn2_chunk_scan(flip(qx), flip(kbx), flip(vx), flip(lbx), sc_b)
    ox = readout(ox_f + flip(ox_b), gx, hx.dtype)
    oc = readout(oc_f + flip(oc_b), gc, hc.dtype) if with_ctx_out else None
    return ox, oc


def _mla_mixer(hx, hc, w_down, q_norm_w, kv_norm_w, w_uq, w_ukv, w_out, cos, sin, with_ctx_out):
    B, S, _ = hx.shape
    H, NOPE, R, VD = MLA_HEADS, MLA_NOPE, MLA_ROPE, MLA_V_DIM
    scale = (NOPE + R) ** -0.5

    def proj_q(h, rotate):
        b, t = h.shape[:2]
        cq = _rmsnorm(h @ w_down[:, :MLA_Q_LORA], q_norm_w)
        q = (cq @ w_uq).reshape(b, t, H, NOPE + R)
        qn, qr = q[..., :NOPE], q[..., NOPE:]
        if rotate:
            qr = _apply_rope(qr, cos, sin)
        return qn, qr

    def proj_kv(h, rotate):
        b, t = h.shape[:2]
        p = h @ w_down[:, MLA_Q_LORA:]
        ckv = _rmsnorm(p[..., :MLA_KV_LORA], kv_norm_w)
        kr = p[..., MLA_KV_LORA:]
        if rotate:
            kr = _apply_rope(kr[:, :, None, :], cos, sin)[:, :, 0, :]
        kv = (ckv @ w_ukv).reshape(b, t, H, NOPE + VD)
        return kv[..., :NOPE], kr, kv[..., NOPE:]

    def logits(qn, qr, kn, kr):
        s = jnp.einsum('bqhd,bkhd->bhqk', qn, kn) + jnp.einsum('bqhr,bkr->bhqk', qr, kr)
        return s.astype(jnp.float32) * scale

    knx, krx, vx = proj_kv(hx, True)
    knc, krc, vc = proj_kv(hc, False)
    qnx, qrx = proj_q(hx, True)

    def block(args):
        qn, qr = args
        s = jnp.concatenate([logits(qn, qr, knx, krx), logits(qn, qr, knc, krc)], axis=-1)
        p = jax.nn.softmax(s, axis=-1).astype(vx.dtype)
        return (jnp.einsum('bhqk,bkhe->bqhe', p[..., :S], vx)
                + jnp.einsum('bhqc,bche->bqhe', p[..., S:], vc))

    ox = _from_blocks(lax.map(block, (_to_blocks(qnx), _to_blocks(qrx))))
    ox = ox.reshape(B, S, H * VD) @ w_out
    oc = None
    if with_ctx_out:
        qnc, qrc = proj_q(hc, False)
        p = jax.nn.softmax(logits(qnc, qrc, knc, krc), axis=-1).astype(vc.dtype)
        oc = jnp.einsum('bhqc,bche->bqhe', p, vc)
        oc = oc.reshape(oc.shape[0], oc.shape[1], H * VD) @ w_out
    return ox, oc


def _layers_of(kind):
    return len(range(kind, DEPTH, N_MIXERS))


def setup_inputs(seed: int = 0) -> dict:
    key = jax.random.key(seed)
    keys = iter(jax.random.split(key, 40))

    def nrm(shape, std):
        return jax.random.normal(next(keys), shape, jnp.float32) * std

    D = D_MODEL
    n_gqa, n_diff, n_hgrn, n_mla = (_layers_of(k) for k in range(N_MIXERS))
    gqa_in = (GQA_HEADS + 2 * GQA_KV_HEADS) * GQA_HEAD_DIM
    gqa_o = GQA_HEADS * GQA_HEAD_DIM
    diff_o = DIFF_HEADS * 2 * DIFF_HEAD_DIM
    hk, hv = HGRN_HEADS * HGRN_KEY_DIM, HGRN_HEADS * HGRN_VAL_DIM
    mla_down = MLA_Q_LORA + MLA_KV_LORA + MLA_ROPE
    mla_o = MLA_HEADS * MLA_V_DIM
    return {
        'x': nrm((BATCH, SEQ, D), 1.0),
        'c': nrm((BATCH, D), 1.0),
        'ctx': nrm((BATCH, CTX_LEN, D), 1.0),
        'c_ctx': nrm((D,), 1.0),
        'ada_w': nrm((DEPTH, D, N_MOD * D), 0.5 * D ** -0.5),
        'ada_b': nrm((DEPTH, N_MOD * D), 0.02),
        'norm_w': 1.0 + nrm((DEPTH, 3, D), 0.05),
        'final_norm_w': 1.0 + nrm((D,), 0.05),
        'ffn_w_gate': nrm((DEPTH, 2, D, D_FF), D ** -0.5),
        'ffn_w_up': nrm((DEPTH, 2, D, D_FF), D ** -0.5),
        'ffn_w_down': nrm((DEPTH, 2, D_FF, D), D_FF ** -0.5),
        'gqa_w_in': nrm((n_gqa, D, gqa_in), D ** -0.5),
        'gqa_w_out': nrm((n_gqa, gqa_o, D), gqa_o ** -0.5),
        'gqa_sinks': nrm((n_gqa, GQA_HEADS), 0.5),
        'diff_w_in': nrm((n_diff, D, 3 * diff_o), D ** -0.5),
        'diff_w_out': nrm((n_diff, diff_o, D), diff_o ** -0.5),
        'diff_lambda': nrm((n_diff, 4, DIFF_HEAD_DIM), 0.1),
        'diff_subln_w': 1.0 + nrm((n_diff, 2 * DIFF_HEAD_DIM), 0.05),
        'hgrn_w_in': nrm((n_hgrn, D, 3 * hk + 2 * hv), D ** -0.5),
        'hgrn_w_out': nrm((n_hgrn, hv, D), hv ** -0.5),
        'hgrn_norm_w': 1.0 + nrm((n_hgrn, HGRN_VAL_DIM), 0.05),
        'hgrn_lower_bounds': nrm((DEPTH, hk), 0.5),
        'mla_w_down': nrm((n_mla, D, mla_down), D ** -0.5),
        'mla_q_norm_w': 1.0 + nrm((n_mla, MLA_Q_LORA), 0.05),
        'mla_kv_norm_w': 1.0 + nrm((n_mla, MLA_KV_LORA), 0.05),
        'mla_w_uq': nrm((n_mla, MLA_Q_LORA, MLA_HEADS * (MLA_NOPE + MLA_ROPE)), MLA_Q_LORA ** -0.5),
        'mla_w_ukv': nrm((n_mla, MLA_KV_LORA, MLA_HEADS * (MLA_NOPE + MLA_V_DIM)), MLA_KV_LORA ** -0.5),
        'mla_w_out': nrm((n_mla, mla_o, D), mla_o ** -0.5),
    }


def reference(x, c, ctx, c_ctx, ada_w, ada_b, norm_w, final_norm_w, ffn_w_gate, ffn_w_up, ffn_w_down,
              gqa_w_in, gqa_w_out, gqa_sinks, diff_w_in, diff_w_out, diff_lambda, diff_subln_w,
              hgrn_w_in, hgrn_w_out, hgrn_norm_w, hgrn_lower_bounds,
              mla_w_down, mla_q_norm_w, mla_kv_norm_w, mla_w_uq, mla_w_ukv, mla_w_out):
    B, S, D = x.shape
    cos_h, sin_h = _axial_rope_tables(S, GQA_HEAD_DIM)
    cos_m, sin_m = _axial_rope_tables(S, MLA_ROPE)
    lb_soft = jax.nn.softmax(hgrn_lower_bounds.astype(jnp.float32), axis=0)
    lb_all = jnp.cumsum(lb_soft, axis=0) - lb_soft[0:1]
    silu_c = jax.nn.silu(c)
    silu_cc = jax.nn.silu(c_ctx)
    hx, hc = x, ctx
    for i in range(DEPTH):
        kind, j = i % N_MIXERS, i // N_MIXERS
        with_ctx_out = i < DEPTH - 1
        mx = (silu_c @ ada_w[i] + ada_b[i]).reshape(B, N_MOD, 1, D)
        mc = (silu_cc @ ada_w[i] + ada_b[i]).reshape(N_MOD, D)
        fx = _swiglu(_pre(hx, norm_w[i, 0], mx[:, 0], mx[:, 1]), ffn_w_gate[i, 0], ffn_w_up[i, 0], ffn_w_down[i, 0])
        fc = _swiglu(_pre(hc, norm_w[i, 0], mc[0], mc[1]), ffn_w_gate[i, 0], ffn_w_up[i, 0], ffn_w_down[i, 0])
        hx = hx + 0.5 * mx[:, 2] * fx
        hc = hc + 0.5 * mc[2] * fc
        ax = _pre(hx, norm_w[i, 1], mx[:, 3], mx[:, 4])
        ac = _pre(hc, norm_w[i, 1], mc[3], mc[4])
        if kind == 0:
            ox, oc = _gqa_window_mixer(ax, ac, gqa_w_in[j], gqa_w_out[j], gqa_sinks[j], cos_h, sin_h, with_ctx_out)
        elif kind == 1:
            lambda_init = 0.8 - 0.6 * math.exp(-0.3 * i)
            ox, oc = _diff_attn_mixer(ax, ac, diff_w_in[j], diff_w_out[j], diff_lambda[j], diff_subln_w[j],
                                      lambda_init, cos_h, sin_h, with_ctx_out)
        elif kind == 2:
            ox, oc = _hgrn2_mixer(ax, ac, hgrn_w_in[j], hgrn_w_out[j], hgrn_norm_w[j], lb_all[i], with_ctx_out)
        else:
            ox, oc = _mla_mixer(ax, ac, mla_w_down[j], mla_q_norm_w[j], mla_kv_norm_w[j], mla_w_uq[j],
                                mla_w_ukv[j], mla_w_out[j], cos_m, sin_m, with_ctx_out)
        hx = hx + mx[:, 5] * ox
        fx = _swiglu(_pre(hx, norm_w[i, 2], mx[:, 6], mx[:, 7]), ffn_w_gate[i, 1], ffn_w_up[i, 1], ffn_w_down[i, 1])
        hx = hx + 0.5 * mx[:, 8] * fx
        if with_ctx_out:
            hc = hc + mc[5] * oc
            fc = _swiglu(_pre(hc, norm_w[i, 2], mc[6], mc[7]), ffn_w_gate[i, 1], ffn_w_up[i, 1], ffn_w_down[i, 1])
            hc = hc + 0.5 * mc[8] * fc
    return _rmsnorm(hx, final_norm_w)
```

```python
import functools
import math

import jax
import jax.numpy as jnp
from jax import lax
from jax.experimental import pallas as pl
from jax.experimental.pallas import tpu as pltpu

F32 = jnp.float32
BF16 = jnp.bfloat16

D_MODEL = 1024
DEPTH = 4
GRID_W = 64
N_MOD = 9
ROPE_BASE = 10000.0
EPS = 1e-6
NEG_INF = -1e30
D_FF = 2816

GQA_HEADS = 16
GQA_KV_HEADS = 4
GQA_HEAD_DIM = 64
GQA_WINDOW = 128
GQA_BLOCK = 128

DIFF_HEADS = 8
DIFF_HEAD_DIM = 64

HGRN_HEADS = 8
HGRN_KEY_DIM = 128
HGRN_VAL_DIM = D_MODEL // HGRN_HEADS
HGRN_CHUNK = 64

MLA_HEADS = 16
MLA_Q_LORA = 256
MLA_KV_LORA = 256
MLA_NOPE = 64
MLA_ROPE = 32
MLA_V_DIM = 64

LANES = 128
VMEM_LIMIT = 56 * 1024 * 1024


def _dot(a, b):
    return jnp.dot(a, b, preferred_element_type=F32)


def _dot_nt(a, b):
    return lax.dot_general(a, b, (((1,), (1,)), ((), ())), preferred_element_type=F32)


def _dot_tn(a, b):
    return lax.dot_general(a, b, (((0,), (0,)), ((), ())), preferred_element_type=F32)


def _params(*sem):
    return pltpu.CompilerParams(dimension_semantics=sem, vmem_limit_bytes=VMEM_LIMIT)


def _divisor(n, pref):
    t = min(n, pref)
    while n % t:
        t -= 8
    return t


class _Layout:
    def __init__(self, B, S, C):
        self.B, self.S, self.C = B, S, C
        self.n_lat = B * S
        self.n_all = B * S + B * C

    def tile(self, pref):
        return _divisor(math.gcd(self.S, self.B * self.C), pref)

    def n_tiles(self, tm, lat_only=False):
        return (self.n_lat if lat_only else self.n_all) // tm

    def mod_index(self, tm):
        n_lat_tiles, per_batch, B = self.n_lat // tm, self.S // tm, self.B
        return lambda t: jnp.where(t < n_lat_tiles, t // per_batch, B)

    def pos_index(self, tm):
        n_lat_tiles, per_batch = self.n_lat // tm, self.S // tm
        return lambda t: jnp.where(t < n_lat_tiles, t % per_batch, per_batch)


def _rmsnorm(x, w):
    return (x * lax.rsqrt(jnp.mean(x * x, axis=-1, keepdims=True) + EPS)) * w


def _pre(h, gain, m, off):
    return _rmsnorm(h, gain) * (1.0 + m[off + 1:off + 2]) + m[off:off + 1]


def _silu(x):
    return x * jax.nn.sigmoid(x)


def _rope(x, cos, sa, sb, shift):
    return x * cos + pltpu.roll(x, LANES - shift, 1) * sa + pltpu.roll(x, shift, 1) * sb


def _ada_kernel(c_ref, w_ref, b_ref, o_ref):
    sc = _silu(c_ref[...]).astype(BF16)
    o_ref[0] = _dot(sc, w_ref[0].astype(BF16)) + b_ref[0]


def _ada_table(cc, ada_w, ada_b):
    rows, D = cc.shape
    n_out = ada_w.shape[2]
    tn = n_out // 8
    return pl.pallas_call(
        _ada_kernel,
        grid=(DEPTH, n_out // tn),
        in_specs=[pl.BlockSpec((rows, D), lambda i, j: (0, 0)),
                  pl.BlockSpec((1, D, tn), lambda i, j: (i, 0, j)),
                  pl.BlockSpec((1, 1, tn), lambda i, j: (i, 0, j))],
        out_specs=pl.BlockSpec((1, rows, tn), lambda i, j: (i, 0, j)),
        out_shape=jax.ShapeDtypeStruct((DEPTH, rows, n_out), F32),
        compiler_params=_params("arbitrary", "arbitrary"),
        name="ada_table",
    )(cc, ada_w, ada_b.reshape(DEPTH, 1, n_out))


def _ffn_kernel(h_ref, mod_ref, gain_ref, fin_ref, wg_ref, wu_ref, wd_ref, o_ref, xm_ref, acc_ref,
                *, off, final_norm):
    j = pl.program_id(1)

    @pl.when(j == 0)
    def _():
        xm_ref[...] = _pre(h_ref[...], gain_ref[...], mod_ref[0], off).astype(BF16)
        acc_ref[...] = jnp.zeros_like(acc_ref)

    xm = xm_ref[...]
    g = _dot(xm, wg_ref[...])
    u = _dot(xm, wu_ref[...])
    acc_ref[...] += _dot((_silu(g) * u).astype(BF16), wd_ref[...])

    @pl.when(j == pl.num_programs(1) - 1)
    def _():
        m = mod_ref[0]
        out = h_ref[...] + (0.5 * m[off + 2:off + 3]) * acc_ref[...]
        if final_norm:
            out = _rmsnorm(out, fin_ref[...])
        o_ref[...] = out


def _ffn(lay, h, mod, gain, fin_w, wg, wu, wd, off, lat_only=False, final_norm=False):
    D = h.shape[1]
    tm = lay.tile(1024)
    tf = 256
    n_rows = lay.n_lat if lat_only else lay.n_all
    midx = lay.mod_index(tm)
    return pl.pallas_call(
        functools.partial(_ffn_kernel, off=off, final_norm=final_norm),
        grid=(n_rows // tm, D_FF // tf),
        in_specs=[pl.BlockSpec((tm, D), lambda t, j: (t, 0)),
                  pl.BlockSpec((1, N_MOD, D), lambda t, j: (midx(t), 0, 0)),
                  pl.BlockSpec((1, D), lambda t, j: (0, 0)),
                  pl.BlockSpec((1, D), lambda t, j: (0, 0)),
                  pl.BlockSpec((D, tf), lambda t, j: (0, j)),
                  pl.BlockSpec((D, tf), lambda t, j: (0, j)),
                  pl.BlockSpec((tf, D), lambda t, j: (j, 0))],
        out_specs=pl.BlockSpec((tm, D), lambda t, j: (t, 0)),
        out_shape=jax.ShapeDtypeStruct((n_rows, D), F32),
        scratch_shapes=[pltpu.VMEM((tm, D), BF16), pltpu.VMEM((tm, D), F32)],
        compiler_params=_params("arbitrary", "arbitrary"),
        name="ffn",
    )(h, mod, gain.reshape(1, D), fin_w.reshape(1, D), wg, wu, wd)


def _rope_tables(S, rot_dim, pad_rows, lane_off):
    rows = S // GRID_W
    row = jnp.repeat(jnp.arange(rows, dtype=F32), GRID_W)
    col = jnp.tile(jnp.arange(GRID_W, dtype=F32), rows)
    axis_dim = rot_dim // 2
    inv_freq = ROPE_BASE ** (-jnp.arange(0, axis_dim, 2, dtype=F32) / axis_dim)
    ang_r = row[:, None] * inv_freq[None, :]
    ang_c = col[:, None] * inv_freq[None, :]
    ang = jnp.concatenate([ang_r, ang_r, ang_c, ang_c], axis=-1)
    cos, sin = jnp.cos(ang), jnp.sin(ang)
    f = rot_dim // 4
    first = (jnp.arange(rot_dim) % (2 * f)) < f
    sa = jnp.where(first[None, :], -sin, 0.0)
    sb = jnp.where(first[None, :], 0.0, sin)
    period = 64 if lane_off + rot_dim <= 64 else LANES

    def widen(t, fill):
        blk = jnp.full((S, period), fill, F32).at[:, lane_off:lane_off + rot_dim].set(t)
        blk = jnp.tile(blk, (1, LANES // period))
        return jnp.concatenate([blk, jnp.full((pad_rows, LANES), fill, F32)], axis=0)

    return widen(cos, 1.0), widen(sa, 0.0), widen(sb, 0.0)


def _attn_specs(lay, tq, widths):
    S, C, B = lay.S, lay.C, lay.B
    nq, nc = S // tq, C // tq
    lat_blocks = lay.n_lat // C

    def q_index(b, i):
        return jnp.where(i < nq, b * nq + i, B * nq + b * nc + (i - nq))

    lat = [pl.BlockSpec((S, w), lambda b, i: (b, 0)) for w in widths]
    ctx = [pl.BlockSpec((C, w), lambda b, i: (lat_blocks + b, 0)) for w in widths]
    return q_index, nq, nc, lat, ctx


def _softmax_parts(parts):
    m = functools.reduce(jnp.maximum, [jnp.max(s, axis=-1, keepdims=True) for s in parts])
    es = [jnp.exp(s - m) for s in parts]
    l = functools.reduce(lambda a, b: a + b, [jnp.sum(e, axis=-1, keepdims=True) for e in es])
    inv = 1.0 / l
    return [e * inv for e in es]


def _gqa_proj_kernel(h_ref, mod_ref, gain_ref, w_ref, cos_ref, sa_ref, sb_ref, q_ref, k_ref, v_ref):
    qd, kd = GQA_HEADS * GQA_HEAD_DIM, GQA_KV_HEADS * GQA_HEAD_DIM
    a = _pre(h_ref[...], gain_ref[...], mod_ref[0], 3).astype(BF16)
    p = _dot(a, w_ref[...])
    cos, sa, sb = cos_ref[...], sa_ref[...], sb_ref[...]
    f = GQA_HEAD_DIM // 4
    scale = GQA_HEAD_DIM ** -0.5
    for c in range(qd // LANES):
        x = _rope(p[:, c * LANES:(c + 1) * LANES], cos, sa, sb, f)
        q_ref[:, c * LANES:(c + 1) * LANES] = (x * scale).astype(BF16)
    for c in range(kd // LANES):
        x = _rope(p[:, qd + c * LANES:qd + (c + 1) * LANES], cos, sa, sb, f)
        k_ref[:, c * LANES:(c + 1) * LANES] = x.astype(BF16)
    v_ref[...] = p[:, qd + kd:].astype(BF16)


def _gqa_attn_kernel(sink_ref, q_ref, kp_ref, kc_ref, kn_ref, vp_ref, vc_ref, vn_ref, kx_ref, vx_ref,
                     h_ref, mod_ref, wo_ref, o_ref, kcat_ref, vcat_ref, *, nq, S, C):
    i = pl.program_id(1)
    blk = GQA_BLOCK
    G = GQA_HEADS // GQA_KV_HEADS
    HD = GQA_HEAD_DIM
    span = 3 * blk

    def finish(heads):
        o = jnp.concatenate(heads, axis=-1).astype(BF16)
        m = mod_ref[0]
        o_ref[...] = h_ref[...] + m[5:6] * _dot(o, wo_ref[...])

    def head_out(h, s, v):
        sink = sink_ref[h]
        m = jnp.maximum(jnp.max(s, axis=-1, keepdims=True), sink)
        e = jnp.exp(s - m)
        inv = 1.0 / (jnp.sum(e, axis=-1, keepdims=True) + jnp.exp(sink - m))
        return _dot((e * inv).astype(BF16), v)

    @pl.when(i < nq)
    def _():
        kcat_ref[0:blk] = kp_ref[...]
        kcat_ref[blk:2 * blk] = kc_ref[...]
        kcat_ref[2 * blk:span] = kn_ref[...]
        kcat_ref[span:] = kx_ref[...]
        vcat_ref[0:blk] = vp_ref[...]
        vcat_ref[blk:2 * blk] = vc_ref[...]
        vcat_ref[2 * blk:span] = vn_ref[...]
        vcat_ref[span:] = vx_ref[...]
        q_abs = i * blk + lax.broadcasted_iota(jnp.int32, (blk, span + C), 0)
        col = lax.broadcasted_iota(jnp.int32, (blk, span + C), 1)
        k_abs = (i - 1) * blk + col
        local = (jnp.abs(q_abs - k_abs) <= GQA_WINDOW) & (k_abs >= 0) & (k_abs < S)
        mask = local | (col >= span)
        heads = []
        for h in range(GQA_HEADS):
            g = h // G
            s = _dot_nt(q_ref[:, h * HD:(h + 1) * HD], kcat_ref[:, g * HD:(g + 1) * HD])
            s = jnp.where(mask, s, NEG_INF)
            heads.append(head_out(h, s, vcat_ref[:, g * HD:(g + 1) * HD]))
        finish(heads)

    @pl.when(i >= nq)
    def _():
        heads = []
        for h in range(GQA_HEADS):
            g = h // G
            s = _dot_nt(q_ref[:, h * HD:(h + 1) * HD], kx_ref[:, g * HD:(g + 1) * HD])
            heads.append(head_out(h, s, vx_ref[:, g * HD:(g + 1) * HD]))
        finish(heads)


def _gqa_mixer(lay, h, mod, gain, w_in, w_out, sinks, with_ctx_out):
    D = h.shape[1]
    B, S, C = lay.B, lay.S, lay.C
    qd, kd = GQA_HEADS * GQA_HEAD_DIM, GQA_KV_HEADS * GQA_HEAD_DIM
    tm = lay.tile(512)
    midx, pidx = lay.mod_index(tm), lay.pos_index(tm)
    cos, sa, sb = _rope_tables(S, GQA_HEAD_DIM, tm, 0)
    tab = pl.BlockSpec((tm, LANES), lambda t: (pidx(t), 0))
    q, k, v = pl.pallas_call(
        _gqa_proj_kernel,
        grid=(lay.n_all // tm,),
        in_specs=[pl.BlockSpec((tm, D), lambda t: (t, 0)),
                  pl.BlockSpec((1, N_MOD, D), lambda t: (midx(t), 0, 0)),
                  pl.BlockSpec((1, D), lambda t: (0, 0)),
                  pl.BlockSpec((D, qd + 2 * kd), lambda t: (0, 0)),
                  tab, tab, tab],
        out_specs=[pl.BlockSpec((tm, qd), lambda t: (t, 0)),
                   pl.BlockSpec((tm, kd), lambda t: (t, 0)),
                   pl.BlockSpec((tm, kd), lambda t: (t, 0))],
        out_shape=[jax.ShapeDtypeStruct((lay.n_all, qd), BF16),
                   jax.ShapeDtypeStruct((lay.n_all, kd), BF16),
                   jax.ShapeDtypeStruct((lay.n_all, kd), BF16)],
        compiler_params=_params("arbitrary"),
        name="gqa_proj",
    )(h, mod, gain.reshape(1, D), w_in.astype(BF16), cos, sa, sb)

    blk = GQA_BLOCK
    nq, nc = S // blk, C // blk
    n_i = nq + (nc if with_ctx_out else 0)
    lat_blocks = lay.n_lat // C

    def q_index(b, i):
        return jnp.where(i < nq, b * nq + i, B * nq + b * nc + (i - nq))

    def win(delta):
        return lambda b, i: (b * nq + jnp.clip(i + delta, 0, nq - 1), 0)

    kv_win = [pl.BlockSpec((blk, kd), win(d)) for d in (-1, 0, 1)]
    kv_ctx = pl.BlockSpec((C, kd), lambda b, i: (lat_blocks + b, 0))
    row = pl.BlockSpec((blk, D), lambda b, i: (q_index(b, i), 0))
    return pl.pallas_call(
        functools.partial(_gqa_attn_kernel, nq=nq, S=S, C=C),
        grid=(B, n_i),
        in_specs=[pl.BlockSpec(memory_space=pltpu.SMEM),
                  pl.BlockSpec((blk, qd), lambda b, i: (q_index(b, i), 0)),
                  *kv_win, *kv_win, kv_ctx, kv_ctx, row,
                  pl.BlockSpec((1, N_MOD, D), lambda b, i: (jnp.where(i < nq, b, B), 0, 0)),
                  pl.BlockSpec((qd, D), lambda b, i: (0, 0))],
        out_specs=row,
        out_shape=jax.ShapeDtypeStruct(h.shape, F32),
        input_output_aliases={10: 0},
        scratch_shapes=[pltpu.VMEM((3 * blk + C, kd), BF16), pltpu.VMEM((3 * blk + C, kd), BF16)],
        compiler_params=_params("arbitrary", "arbitrary"),
        name="gqa_attn",
    )(sinks.astype(F32), q, k, k, k, v, v, v, k, v, h, mod, w_out.astype(BF16))


def _diff_proj_kernel(h_ref, mod_ref, gain_ref, w_ref, cos_ref, sa_ref, sb_ref, q_ref, k_ref, v_ref):
    qk = DIFF_HEADS * 2 * DIFF_HEAD_DIM
    a = _pre(h_ref[...], gain_ref[...], mod_ref[0], 3).astype(BF16)
    p = _dot(a, w_ref[...])
    cos, sa, sb = cos_ref[...], sa_ref[...], sb_ref[...]
    f = DIFF_HEAD_DIM // 4
    scale = DIFF_HEAD_DIM ** -0.5
    for c in range(qk // LANES):
        x = _rope(p[:, c * LANES:(c + 1) * LANES], cos, sa, sb, f)
        q_ref[:, c * LANES:(c + 1) * LANES] = (x * scale).astype(BF16)
        x = _rope(p[:, qk + c * LANES:qk + (c + 1) * LANES], cos, sa, sb, f)
        k_ref[:, c * LANES:(c + 1) * LANES] = x.astype(BF16)
    v_ref[...] = p[:, 2 * qk:].astype(BF16)


def _diff_attn_kernel(lam_ref, sub_ref, q_ref, kl_ref, vl_ref, kx_ref, vx_ref, h_ref, mod_ref, wo_ref, o_ref,
                      *, nq, lambda_init):
    i = pl.program_id(1)
    HD = DIFF_HEAD_DIM
    lp = lam_ref[...]
    lam = (jnp.exp(jnp.sum(lp[0:1] * lp[1:2], keepdims=True))
           - jnp.exp(jnp.sum(lp[2:3] * lp[3:4], keepdims=True)) + lambda_init)

    def run(kv_pairs):
        heads = []
        for h in range(DIFF_HEADS):
            probs = []
            for j in range(2):
                lo = (2 * h + j) * HD
                qh = q_ref[:, lo:lo + HD]
                probs.append(_softmax_parts([_dot_nt(qh, k_ref[:, lo:lo + HD]) for k_ref, _ in kv_pairs]))
            o = None
            for n, (_, v_ref) in enumerate(kv_pairs):
                p = (probs[0][n] - lam * probs[1][n]).astype(BF16)
                t = _dot(p, v_ref[:, 2 * h * HD:2 * (h + 1) * HD])
                o = t if o is None else o + t
            heads.append(_rmsnorm(o, sub_ref[...]) * (1.0 - lambda_init))
        o = jnp.concatenate(heads, axis=-1).astype(BF16)
        m = mod_ref[0]
        o_ref[...] = h_ref[...] + m[5:6] * _dot(o, wo_ref[...])

    @pl.when(i < nq)
    def _():
        run([(kl_ref, vl_ref), (kx_ref, vx_ref)])

    @pl.when(i >= nq)
    def _():
        run([(kx_ref, vx_ref)])


def _diff_mixer(lay, h, mod, gain, w_in, w_out, lam_params, subln_w, lambda_init, with_ctx_out):
    D = h.shape[1]
    B, S, C = lay.B, lay.S, lay.C
    qk = DIFF_HEADS * 2 * DIFF_HEAD_DIM
    tm = lay.tile(512)
    midx, pidx = lay.mod_index(tm), lay.pos_index(tm)
    cos, sa, sb = _rope_tables(S, DIFF_HEAD_DIM, tm, 0)
    tab = pl.BlockSpec((tm, LANES), lambda t: (pidx(t), 0))
    out = jax.ShapeDtypeStruct((lay.n_all, qk), BF16)
    q, k, v = pl.pallas_call(
        _diff_proj_kernel,
        grid=(lay.n_all // tm,),
        in_specs=[pl.BlockSpec((tm, D), lambda t: (t, 0)),
                  pl.BlockSpec((1, N_MOD, D), lambda t: (midx(t), 0, 0)),
                  pl.BlockSpec((1, D), lambda t: (0, 0)),
                  pl.BlockSpec((D, 3 * qk), lambda t: (0, 0)),
                  tab, tab, tab],
        out_specs=[pl.BlockSpec((tm, qk), lambda t: (t, 0))] * 3,
        out_shape=[out, out, out],
        compiler_params=_params("arbitrary"),
        name="diff_proj",
    )(h, mod, gain.reshape(1, D), w_in.astype(BF16), cos, sa, sb)

    tq = _divisor(math.gcd(S, C), 256)
    q_index, nq, nc, lat, ctx = _attn_specs(lay, tq, [qk, qk])
    row = pl.BlockSpec((tq, D), lambda b, i: (q_index(b, i), 0))
    return pl.pallas_call(
        functools.partial(_diff_attn_kernel, nq=nq, lambda_init=lambda_init),
        grid=(B, nq + (nc if with_ctx_out else 0)),
        in_specs=[pl.BlockSpec((4, DIFF_HEAD_DIM), lambda b, i: (0, 0)),
                  pl.BlockSpec((1, 2 * DIFF_HEAD_DIM), lambda b, i: (0, 0)),
                  pl.BlockSpec((tq, qk), lambda b, i: (q_index(b, i), 0)),
                  *lat, *ctx, row,
                  pl.BlockSpec((1, N_MOD, D), lambda b, i: (jnp.where(i < nq, b, B), 0, 0)),
                  pl.BlockSpec((qk, D), lambda b, i: (0, 0))],
        out_specs=row,
        out_shape=jax.ShapeDtypeStruct(h.shape, F32),
        input_output_aliases={7: 0},
        compiler_params=_params("arbitrary", "arbitrary"),
        name="diff_attn",
    )(lam_params.astype(F32), subln_w.reshape(1, -1), q, k, v, k, v, h, mod, w_out.astype(BF16))


def _hgrn_proj_kernel(h_ref, mod_ref, gain_ref, lb_ref, w_ref, o_ref, a_ref, *, layer):
    j = pl.program_id(1)

    @pl.when(j == 0)
    def _():
        a_ref[...] = _pre(h_ref[...], gain_ref[...], mod_ref[0], 3).astype(BF16)

    p = _dot(a_ref[...], w_ref[...])

    @pl.when((j == 0) | (j == 4))
    def _():
        o_ref[...] = _silu(p)

    @pl.when((j == 1) | (j == 2))
    def _():
        raw = lb_ref[...]
        e = jnp.exp(raw - jnp.max(raw, axis=0, keepdims=True))
        soft = e * (1.0 / jnp.sum(e, axis=0, keepdims=True))
        lb = jnp.sum(soft[1:layer + 1], axis=0, keepdims=True) if layer else jnp.zeros_like(soft[0:1])
        o_ref[...] = lb + (1.0 - lb) * jax.nn.sigmoid(p)

    @pl.when(j == 3)
    def _():
        o_ref[...] = p


def _split3(x):
    hi = x.astype(BF16)
    r = x - hi.astype(F32)
    mid = r.astype(BF16)
    lo = (r - mid.astype(F32)).astype(BF16)
    return hi, mid, lo


def _hgrn_scan_kernel(q_ref, f_ref, v_ref, o_ref, st_ref, *, n_chunks):
    d = pl.program_id(1)
    step = pl.program_id(2)
    C = HGRN_CHUNK
    DK, DV, H = HGRN_KEY_DIM, HGRN_VAL_DIM, HGRN_HEADS

    @pl.when(step == 0)
    def _():
        st_ref[...] = jnp.zeros_like(st_ref)

    fwd = d == 0
    r = lax.broadcasted_iota(jnp.int32, (C, C), 0)
    c = lax.broadcasted_iota(jnp.int32, (C, C), 1)
    sign = jnp.where(fwd, 1, -1)
    tri = (r - c) * sign >= 0
    tri_b = jnp.where(tri, 1.0, 0.0).astype(BF16)
    rows = lax.broadcasted_iota(jnp.int32, (C, 1), 0)
    mid_rows = (rows - jnp.where(fwd, C // 2, C - 1 - C // 2)) * sign <= 0

    def chunk(n, carry):
        base = pl.multiple_of(jnp.where(fwd, n, n_chunks - 1 - n) * C, C)
        q = q_ref[pl.ds(base, C), :]
        f = f_ref[pl.ds(base, C), :]
        v = v_ref[pl.ds(base, C), :].astype(BF16)
        logf = jnp.log(f)
        k = 1.0 - f
        g = functools.reduce(lambda a, b: a + b, [_dot(tri_b, part) for part in _split3(logf)])
        g_last = jnp.sum(logf, axis=0, keepdims=True)
        g_mid = jnp.sum(jnp.where(mid_rows, logf, 0.0), axis=0, keepdims=True)
        qa = (q * jnp.exp(g - g_mid)).astype(BF16)
        kb = (k * jnp.exp(g_mid - g)).astype(BF16)
        qg = (q * jnp.exp(g)).astype(BF16)
        kd = (k * jnp.exp(g_last - g)).astype(BF16)
        decay = jnp.exp(g_last)
        outs = []
        for h in range(H):
            ks = slice(h * DK, (h + 1) * DK)
            vs = slice(h * DV, (h + 1) * DV)
            a = jnp.where(tri, _dot_nt(qa[:, ks], kb[:, ks]), 0.0)
            st = st_ref[vs, :]
            outs.append(_dot(a.astype(BF16), v[:, vs]) + _dot_nt(qg[:, ks], st.astype(BF16)))
            st_ref[vs, :] = st * decay[:, ks] + _dot_tn(v[:, vs], kd[:, ks])
        o_ref[0, pl.ds(base, C), :] = jnp.concatenate(outs, axis=-1)
        return carry

    lax.fori_loop(0, n_chunks, chunk, 0)


def _hgrn_out_kernel(o_ref, gate_ref, nw_ref, h_ref, mod_ref, wo_ref, out_ref):
    DV = HGRN_VAL_DIM
    o = o_ref[0] + o_ref[1]
    gate = gate_ref[...]
    nw = nw_ref[...]
    ys = [_rmsnorm(o[:, h * DV:(h + 1) * DV], nw) * gate[:, h * DV:(h + 1) * DV] for h in range(HGRN_HEADS)]
    y = jnp.concatenate(ys, axis=-1).astype(BF16)
    m = mod_ref[0]
    out_ref[...] = h_ref[...] + m[5:6] * _dot(y, wo_ref[...])


def _hgrn_mixer(lay, h, mod, gain, w_in, w_out, norm_w, lower_bounds, layer):
    D = h.shape[1]
    B, S, C = lay.B, lay.S, lay.C
    KD = HGRN_HEADS * HGRN_KEY_DIM
    tm = lay.tile(512)
    midx = lay.mod_index(tm)
    n_cols = w_in.shape[1] // KD
    proj = pl.pallas_call(
        functools.partial(_hgrn_proj_kernel, layer=layer),
        grid=(lay.n_all // tm, n_cols),
        in_specs=[pl.BlockSpec((tm, D), lambda t, j: (t, 0)),
                  pl.BlockSpec((1, N_MOD, D), lambda t, j: (midx(t), 0, 0)),
                  pl.BlockSpec((1, D), lambda t, j: (0, 0)),
                  pl.BlockSpec(lower_bounds.shape, lambda t, j: (0, 0)),
                  pl.BlockSpec((D, KD), lambda t, j: (0, j))],
        out_specs=pl.BlockSpec((tm, KD), lambda t, j: (t, j)),
        out_shape=jax.ShapeDtypeStruct((lay.n_all, n_cols * KD), F32),
        scratch_shapes=[pltpu.VMEM((tm, D), BF16)],
        compiler_params=_params("arbitrary", "arbitrary"),
        name="hgrn_proj",
    )(h, mod, gain.reshape(1, D), lower_bounds.astype(F32), w_in.astype(BF16))

    ts = C
    n_lat_steps = S // ts
    lat_blocks = lay.n_lat // ts

    def rows(b, d, s):
        lat = b * n_lat_steps + jnp.where(d == 0, s - 1, n_lat_steps - s)
        return jnp.where(s == 0, lat_blocks + b, lat)

    o = pl.pallas_call(
        functools.partial(_hgrn_scan_kernel, n_chunks=ts // HGRN_CHUNK),
        grid=(B, 2, 1 + n_lat_steps),
        in_specs=[pl.BlockSpec((ts, KD), lambda b, d, s: (rows(b, d, s), 0)),
                  pl.BlockSpec((ts, KD), lambda b, d, s: (rows(b, d, s), 1 + d)),
                  pl.BlockSpec((ts, KD), lambda b, d, s: (rows(b, d, s), 3))],
        out_specs=pl.BlockSpec((1, ts, KD), lambda b, d, s: (d, rows(b, d, s), 0)),
        out_shape=jax.ShapeDtypeStruct((2, lay.n_all, KD), F32),
        scratch_shapes=[pltpu.VMEM((HGRN_HEADS * HGRN_VAL_DIM, HGRN_KEY_DIM), F32)],
        compiler_params=_params("arbitrary", "arbitrary", "arbitrary"),
        name="hgrn_scan",
    )(proj, proj, proj)

    row = pl.BlockSpec((tm, D), lambda t: (t, 0))
    return pl.pallas_call(
        _hgrn_out_kernel,
        grid=(lay.n_all // tm,),
        in_specs=[pl.BlockSpec((2, tm, KD), lambda t: (0, t, 0)),
                  pl.BlockSpec((tm, KD), lambda t: (t, 4)),
                  pl.BlockSpec((1, HGRN_VAL_DIM), lambda t: (0, 0)),
                  row,
                  pl.BlockSpec((1, N_MOD, D), lambda t: (midx(t), 0, 0)),
                  pl.BlockSpec((KD, D), lambda t: (0, 0))],
        out_specs=row,
        out_shape=jax.ShapeDtypeStruct(h.shape, F32),
        input_output_aliases={3: 0},
        compiler_params=_params("arbitrary"),
        name="hgrn_out",
    )(o, proj, norm_w.reshape(1, -1), h, mod, w_out.astype(BF16))


MLA_QK_PAD = LANES


def _mla_proj_kernel(h_ref, mod_ref, gain_ref, wd_ref, qn_ref, kvn_ref, wuq_ref, wukv_ref,
                     cos_ref, sa_ref, sb_ref, q_ref, k_ref, v_ref):
    H, P = MLA_HEADS, MLA_QK_PAD
    a = _pre(h_ref[...], gain_ref[...], mod_ref[0], 3).astype(BF16)
    dn = _dot(a, wd_ref[...])
    cos, sa, sb = cos_ref[...], sa_ref[...], sb_ref[...]
    f = MLA_ROPE // 4
    cq = _rmsnorm(dn[:, :MLA_Q_LORA], qn_ref[...]).astype(BF16)
    q = _dot(cq, wuq_ref[...])
    for hd in range(H):
        q_ref[:, hd * P:(hd + 1) * P] = _rope(q[:, hd * P:(hd + 1) * P], cos, sa, sb, f).astype(BF16)
    ckv = _rmsnorm(dn[:, MLA_Q_LORA:MLA_Q_LORA + MLA_KV_LORA], kvn_ref[...]).astype(BF16)
    kv = _dot(ckv, wukv_ref[...])
    kr = _rope(dn[:, MLA_Q_LORA + MLA_KV_LORA:], cos, sa, sb, f)
    for hd in range(H):
        k_ref[:, hd * P:(hd + 1) * P] = (kv[:, hd * P:(hd + 1) * P] + kr).astype(BF16)
    v_ref[...] = kv[:, H * P:].astype(BF16)


def _mla_attn_kernel(q_ref, kl_ref, vl_ref, kx_ref, vx_ref, h_ref, mod_ref, wo_ref, o_ref, *, nq):
    i = pl.program_id(1)
    P, VD = MLA_QK_PAD, MLA_V_DIM
    scale = (MLA_NOPE + MLA_ROPE) ** -0.5

    def run(kv_pairs):
        heads = []
        for hd in range(MLA_HEADS):
            qh = q_ref[:, hd * P:(hd + 1) * P]
            probs = _softmax_parts([_dot_nt(qh, k_ref[:, hd * P:(hd + 1) * P]) * scale for k_ref, _ in kv_pairs])
            o = None
            for p, (_, v_ref) in zip(probs, kv_pairs):
                t = _dot(p.astype(BF16), v_ref[:, hd * VD:(hd + 1) * VD])
                o = t if o is None else o + t
            heads.append(o)
        o = jnp.concatenate(heads, axis=-1).astype(BF16)
        m = mod_ref[0]
        o_ref[...] = h_ref[...] + m[5:6] * _dot(o, wo_ref[...])

    @pl.when(i < nq)
    def _():
        run([(kl_ref, vl_ref), (kx_ref, vx_ref)])

    @pl.when(i >= nq)
    def _():
        run([(kx_ref, vx_ref)])


def _mla_mixer(lay, h, mod, gain, w_down, q_norm_w, kv_norm_w, w_uq, w_ukv, w_out, with_ctx_out):
    D = h.shape[1]
    B, S, C = lay.B, lay.S, lay.C
    H, P, VD = MLA_HEADS, MLA_QK_PAD, MLA_V_DIM
    qk = MLA_NOPE + MLA_ROPE
    lora = MLA_Q_LORA + MLA_KV_LORA
    kr_cols = jnp.zeros((D, P), F32).at[:, MLA_NOPE:qk].set(w_down[:, lora:])
    wd = jnp.concatenate([w_down[:, :lora], kr_cols], axis=1).astype(BF16)
    wuq = jnp.pad(w_uq.reshape(MLA_Q_LORA, H, qk), ((0, 0), (0, 0), (0, P - qk))).reshape(MLA_Q_LORA, H * P)
    ukv = w_ukv.reshape(MLA_KV_LORA, H, MLA_NOPE + VD)
    wkn = jnp.pad(ukv[..., :MLA_NOPE], ((0, 0), (0, 0), (0, P - MLA_NOPE))).reshape(MLA_KV_LORA, H * P)
    wukv = jnp.concatenate([wkn, ukv[..., MLA_NOPE:].reshape(MLA_KV_LORA, H * VD)], axis=1)

    tm = lay.tile(512)
    midx, pidx = lay.mod_index(tm), lay.pos_index(tm)
    cos, sa, sb = _rope_tables(S, MLA_ROPE, tm, MLA_NOPE)
    tab = pl.BlockSpec((tm, LANES), lambda t: (pidx(t), 0))
    full = lambda shape: pl.BlockSpec(shape, lambda t: (0, 0))
    q, k, v = pl.pallas_call(
        _mla_proj_kernel,
        grid=(lay.n_all // tm,),
        in_specs=[pl.BlockSpec((tm, D), lambda t: (t, 0)),
                  pl.BlockSpec((1, N_MOD, D), lambda t: (midx(t), 0, 0)),
                  full((1, D)), full((D, lora + P)), full((1, MLA_Q_LORA)), full((1, MLA_KV_LORA)),
                  full((MLA_Q_LORA, H * P)), full((MLA_KV_LORA, H * (P + VD))),
                  tab, tab, tab],
        out_specs=[pl.BlockSpec((tm, H * P), lambda t: (t, 0)),
                   pl.BlockSpec((tm, H * P), lambda t: (t, 0)),
                   pl.BlockSpec((tm, H * VD), lambda t: (t, 0))],
        out_shape=[jax.ShapeDtypeStruct((lay.n_all, H * P), BF16),
                   jax.ShapeDtypeStruct((lay.n_all, H * P), BF16),
                   jax.ShapeDtypeStruct((lay.n_all, H * VD), BF16)],
        compiler_params=_params("arbitrary"),
        name="mla_proj",
    )(h, mod, gain.reshape(1, D), wd, q_norm_w.reshape(1, -1), kv_norm_w.reshape(1, -1),
      wuq.astype(BF16), wukv.astype(BF16), cos, sa, sb)

    tq = _divisor(math.gcd(S, C), 256)
    q_index, nq, nc, lat, ctx = _attn_specs(lay, tq, [H * P, H * VD])
    row = pl.BlockSpec((tq, D), lambda b, i: (q_index(b, i), 0))
    return pl.pallas_call(
        functools.partial(_mla_attn_kernel, nq=nq),
        grid=(B, nq + (nc if with_ctx_out else 0)),
        in_specs=[pl.BlockSpec((tq, H * P), lambda b, i: (q_index(b, i), 0)),
                  *lat, *ctx, row,
                  pl.BlockSpec((1, N_MOD, D), lambda b, i: (jnp.where(i < nq, b, B), 0, 0)),
                  pl.BlockSpec((H * VD, D), lambda b, i: (0, 0))],
        out_specs=row,
        out_shape=jax.ShapeDtypeStruct(h.shape, F32),
        input_output_aliases={5: 0},
        compiler_params=_params("arbitrary", "arbitrary"),
        name="mla_attn",
    )(q, k, v, k, v, h, mod, w_out.astype(BF16))


def kernel(x, c, ctx, c_ctx, ada_w, ada_b, norm_w, final_norm_w, ffn_w_gate, ffn_w_up, ffn_w_down,
           gqa_w_in, gqa_w_out, gqa_sinks, diff_w_in, diff_w_out, diff_lambda, diff_subln_w,
           hgrn_w_in, hgrn_w_out, hgrn_norm_w, hgrn_lower_bounds,
           mla_w_down, mla_q_norm_w, mla_kv_norm_w, mla_w_uq, mla_w_ukv, mla_w_out):
    B, S, D = x.shape
    C = ctx.shape[1]
    lay = _Layout(B, S, C)
    depth = ada_w.shape[0]

    rows = -(-(B + 1) // 8) * 8
    cc = jnp.zeros((rows, D), F32).at[:B].set(c).at[B].set(c_ctx)
    mods = _ada_table(cc, ada_w, ada_b).reshape(depth, rows, N_MOD, D)

    wg, wu, wd = ffn_w_gate.astype(BF16), ffn_w_up.astype(BF16), ffn_w_down.astype(BF16)
    h = jnp.concatenate([x.reshape(B * S, D), ctx.reshape(B * C, D)], axis=0)
    for i in range(depth):
        kind, j = i % 4, i // 4
        last = i == depth - 1
        mod = mods[i]
        h = _ffn(lay, h, mod, norm_w[i, 0], final_norm_w, wg[i, 0], wu[i, 0], wd[i, 0], 0)
        if kind == 0:
            h = _gqa_mixer(lay, h, mod, norm_w[i, 1], gqa_w_in[j], gqa_w_out[j], gqa_sinks[j], not last)
        elif kind == 1:
            lambda_init = 0.8 - 0.6 * math.exp(-0.3 * i)
            h = _diff_mixer(lay, h, mod, norm_w[i, 1], diff_w_in[j], diff_w_out[j], diff_lambda[j],
                            diff_subln_w[j], lambda_init, not last)
        elif kind == 2:
            h = _hgrn_mixer(lay, h, mod, norm_w[i, 1], hgrn_w_in[j], hgrn_w_out[j], hgrn_norm_w[j],
                            hgrn_lower_bounds, i)
        else:
            h = _mla_mixer(lay, h, mod, norm_w[i, 1], mla_w_down[j], mla_q_norm_w[j], mla_kv_norm_w[j],
                           mla_w_uq[j], mla_w_ukv[j], mla_w_out[j], not last)
        h = _ffn(lay, h, mod, norm_w[i, 2], final_norm_w, wg[i, 1], wu[i, 1], wd[i, 1], 6,
                 lat_only=last, final_norm=last)
    return h[:B * S].reshape(B, S, D)
```

```python
import functools
import math

import jax
import jax.numpy as jnp
from jax import lax
from jax.experimental import pallas as pl
from jax.experimental.pallas import tpu as pltpu

F32 = jnp.float32
BF16 = jnp.bfloat16

D_MODEL = 1024
DEPTH = 4
GRID_W = 64
N_MOD = 9
ROPE_BASE = 10000.0
EPS = 1e-6
NEG_INF = -1e30
D_FF = 2816

GQA_HEADS = 16
GQA_KV_HEADS = 4
GQA_HEAD_DIM = 64
GQA_WINDOW = 128
GQA_BLOCK = 128

DIFF_HEADS = 8
DIFF_HEAD_DIM = 64

HGRN_HEADS = 8
HGRN_KEY_DIM = 128
HGRN_VAL_DIM = D_MODEL // HGRN_HEADS
HGRN_CHUNK = 64

MLA_HEADS = 16
MLA_Q_LORA = 256
MLA_KV_LORA = 256
MLA_NOPE = 64
MLA_ROPE = 32
MLA_V_DIM = 64

LOG2_E = 1.4426950408889634
LANES = 128
VMEM_LIMIT = 56 * 1024 * 1024


def _dot(a, b):
    return jnp.dot(a, b, preferred_element_type=F32)


def _dot_nt(a, b):
    return lax.dot_general(a, b, (((1,), (1,)), ((), ())), preferred_element_type=F32)


def _dot_tn(a, b):
    return lax.dot_general(a, b, (((0,), (0,)), ((), ())), preferred_element_type=F32)


def _params(*sem):
    return pltpu.CompilerParams(dimension_semantics=sem, vmem_limit_bytes=VMEM_LIMIT)


def _divisor(n, pref):
    t = min(n, pref)
    while n % t:
        t -= 8
    return t


class _Layout:
    def __init__(self, B, S, C):
        self.B, self.S, self.C = B, S, C
        self.n_lat = B * S
        self.n_all = B * S + B * C

    def tile(self, pref):
        return _divisor(math.gcd(self.S, self.B * self.C), pref)

    def n_tiles(self, tm, lat_only=False):
        return (self.n_lat if lat_only else self.n_all) // tm

    def mod_index(self, tm):
        n_lat_tiles, per_batch, B = self.n_lat // tm, self.S // tm, self.B
        return lambda t: jnp.where(t < n_lat_tiles, t // per_batch, B)

    def pos_index(self, tm):
        n_lat_tiles, per_batch = self.n_lat // tm, self.S // tm
        return lambda t: jnp.where(t < n_lat_tiles, t % per_batch, per_batch)


def _rmsnorm(x, w):
    return (x * lax.rsqrt(jnp.mean(x * x, axis=-1, keepdims=True) + EPS)) * w


def _pre(h, gain, m, off):
    return _rmsnorm(h, gain) * (1.0 + m[off + 1:off + 2]) + m[off:off + 1]


def _silu(x):
    return x * jax.nn.sigmoid(x)


def _rope(x, cos, sa, sb, shift):
    return x * cos + pltpu.roll(x, LANES - shift, 1) * sa + pltpu.roll(x, shift, 1) * sb


def _ada_kernel(c_ref, w_ref, b_ref, o_ref):
    sc = _silu(c_ref[...]).astype(BF16)
    o_ref[0] = _dot(sc, w_ref[0].astype(BF16)) + b_ref[0]


def _ada_table(cc, ada_w, ada_b):
    rows, D = cc.shape
    n_out = ada_w.shape[2]
    tn = n_out // 8
    return pl.pallas_call(
        _ada_kernel,
        grid=(DEPTH, n_out // tn),
        in_specs=[pl.BlockSpec((rows, D), lambda i, j: (0, 0)),
                  pl.BlockSpec((1, D, tn), lambda i, j: (i, 0, j)),
                  pl.BlockSpec((1, 1, tn), lambda i, j: (i, 0, j))],
        out_specs=pl.BlockSpec((1, rows, tn), lambda i, j: (i, 0, j)),
        out_shape=jax.ShapeDtypeStruct((DEPTH, rows, n_out), F32),
        compiler_params=_params("arbitrary", "arbitrary"),
        name="ada_table",
    )(cc, ada_w, ada_b.reshape(DEPTH, 1, n_out))


def _ffn_kernel(h_ref, mod_ref, gain_ref, fin_ref, wg_ref, wu_ref, wd_ref, o_ref, xm_ref, acc_ref,
                *, off, final_norm):
    j = pl.program_id(1)

    @pl.when(j == 0)
    def _():
        xm_ref[...] = _pre(h_ref[...], gain_ref[...], mod_ref[0], off).astype(BF16)
        acc_ref[...] = jnp.zeros_like(acc_ref)

    xm = xm_ref[...]
    g = _dot(xm, wg_ref[...])
    u = _dot(xm, wu_ref[...])
    acc_ref[...] += _dot((_silu(g) * u).astype(BF16), wd_ref[...])

    @pl.when(j == pl.num_programs(1) - 1)
    def _():
        m = mod_ref[0]
        out = h_ref[...] + (0.5 * m[off + 2:off + 3]) * acc_ref[...]
        if final_norm:
            out = _rmsnorm(out, fin_ref[...])
        o_ref[...] = out


def _ffn(lay, h, mod, gain, fin_w, wg, wu, wd, off, lat_only=False, final_norm=False):
    D = h.shape[1]
    tm = lay.tile(1024)
    tf = 256
    n_rows = lay.n_lat if lat_only else lay.n_all
    midx = lay.mod_index(tm)
    return pl.pallas_call(
        functools.partial(_ffn_kernel, off=off, final_norm=final_norm),
        grid=(n_rows // tm, D_FF // tf),
        in_specs=[pl.BlockSpec((tm, D), lambda t, j: (t, 0)),
                  pl.BlockSpec((1, N_MOD, D), lambda t, j: (midx(t), 0, 0)),
                  pl.BlockSpec((1, D), lambda t, j: (0, 0)),
                  pl.BlockSpec((1, D), lambda t, j: (0, 0)),
                  pl.BlockSpec((D, tf), lambda t, j: (0, j)),
                  pl.BlockSpec((D, tf), lambda t, j: (0, j)),
                  pl.BlockSpec((tf, D), lambda t, j: (j, 0))],
        out_specs=pl.BlockSpec((tm, D), lambda t, j: (t, 0)),
        out_shape=jax.ShapeDtypeStruct((n_rows, D), F32),
        scratch_shapes=[pltpu.VMEM((tm, D), BF16), pltpu.VMEM((tm, D), F32)],
        compiler_params=_params("arbitrary", "arbitrary"),
        name="ffn",
    )(h, mod, gain.reshape(1, D), fin_w.reshape(1, D), wg, wu, wd)


def _rope_tables(S, rot_dim, pad_rows, lane_off):
    rows = S // GRID_W
    row = jnp.repeat(jnp.arange(rows, dtype=F32), GRID_W)
    col = jnp.tile(jnp.arange(GRID_W, dtype=F32), rows)
    axis_dim = rot_dim // 2
    inv_freq = ROPE_BASE ** (-jnp.arange(0, axis_dim, 2, dtype=F32) / axis_dim)
    ang_r = row[:, None] * inv_freq[None, :]
    ang_c = col[:, None] * inv_freq[None, :]
    ang = jnp.concatenate([ang_r, ang_r, ang_c, ang_c], axis=-1)
    cos, sin = jnp.cos(ang), jnp.sin(ang)
    f = rot_dim // 4
    first = (jnp.arange(rot_dim) % (2 * f)) < f
    sa = jnp.where(first[None, :], -sin, 0.0)
    sb = jnp.where(first[None, :], 0.0, sin)
    period = 64 if lane_off + rot_dim <= 64 else LANES

    def widen(t, fill):
        blk = jnp.full((S, period), fill, F32).at[:, lane_off:lane_off + rot_dim].set(t)
        blk = jnp.tile(blk, (1, LANES // period))
        return jnp.concatenate([blk, jnp.full((pad_rows, LANES), fill, F32)], axis=0)

    return widen(cos, 1.0), widen(sa, 0.0), widen(sb, 0.0)


def _attn_specs(lay, tq, widths):
    S, C, B = lay.S, lay.C, lay.B
    nq, nc = S // tq, C // tq
    lat_blocks = lay.n_lat // C

    def q_index(b, i):
        return jnp.where(i < nq, b * nq + i, B * nq + b * nc + (i - nq))

    one = pl.Buffered(1)
    lat = [pl.BlockSpec((S, w), lambda b, i: (b, 0), pipeline_mode=one) for w in widths]
    ctx = [pl.BlockSpec((C, w), lambda b, i: (lat_blocks + b, 0), pipeline_mode=one) for w in widths]
    return q_index, nq, nc, lat, ctx


def _gqa_proj_kernel(h_ref, mod_ref, gain_ref, w_ref, cos_ref, sa_ref, sb_ref, q_ref, k_ref, v_ref):
    qd, kd = GQA_HEADS * GQA_HEAD_DIM, GQA_KV_HEADS * GQA_HEAD_DIM
    a = _pre(h_ref[...], gain_ref[...], mod_ref[0], 3).astype(BF16)
    p = _dot(a, w_ref[...])
    cos, sa, sb = cos_ref[...], sa_ref[...], sb_ref[...]
    f = GQA_HEAD_DIM // 4
    scale = GQA_HEAD_DIM ** -0.5
    for c in range(qd // LANES):
        x = _rope(p[:, c * LANES:(c + 1) * LANES], cos, sa, sb, f)
        q_ref[:, c * LANES:(c + 1) * LANES] = (x * scale).astype(BF16)
    for c in range(kd // LANES):
        x = _rope(p[:, qd + c * LANES:qd + (c + 1) * LANES], cos, sa, sb, f)
        k_ref[:, c * LANES:(c + 1) * LANES] = x.astype(BF16)
    v_ref[...] = p[:, qd + kd:].astype(BF16)


def _gqa_attn_kernel(sink_ref, q_ref, kp_ref, kc_ref, kn_ref, vp_ref, vc_ref, vn_ref, kx_ref, vx_ref,
                     h_ref, mod_ref, wo_ref, o_ref, kcat_ref, vcat_ref, *, nq, S, C):
    i = pl.program_id(1)
    blk = GQA_BLOCK
    G = GQA_HEADS // GQA_KV_HEADS
    HD = GQA_HEAD_DIM
    span = 3 * blk

    def finish(heads):
        o = jnp.concatenate(heads, axis=-1).astype(BF16)
        m = mod_ref[0]
        o_ref[...] = h_ref[...] + m[5:6] * _dot(o, wo_ref[...])

    def group_out(g, k, v, mask):
        qg = jnp.concatenate([q_ref[:, (G * g + n) * HD:(G * g + n + 1) * HD] for n in range(G)], axis=0)
        sink = jnp.concatenate([jnp.full((blk, 1), sink_ref[G * g + n], F32) for n in range(G)], axis=0)
        s = _dot_nt(qg, k)
        if mask is not None:
            s = jnp.where(mask, s, NEG_INF)
        m = jnp.maximum(jnp.max(s, axis=-1, keepdims=True), sink)
        e = jnp.exp(s - m)
        inv = 1.0 / (jnp.sum(e, axis=-1, keepdims=True) + jnp.exp(sink - m))
        o = _dot(e.astype(BF16), v) * inv
        return [o[n * blk:(n + 1) * blk] for n in range(G)]

    @pl.when(i < nq)
    def _():
        kcat_ref[0:blk] = kp_ref[...]
        kcat_ref[blk:2 * blk] = kc_ref[...]
        kcat_ref[2 * blk:span] = kn_ref[...]
        kcat_ref[span:] = kx_ref[...]
        vcat_ref[0:blk] = vp_ref[...]
        vcat_ref[blk:2 * blk] = vc_ref[...]
        vcat_ref[2 * blk:span] = vn_ref[...]
        vcat_ref[span:] = vx_ref[...]
        row = lax.broadcasted_iota(jnp.int32, (G * blk, span + C), 0) & (blk - 1)
        col = lax.broadcasted_iota(jnp.int32, (G * blk, span + C), 1)
        k_abs = (i - 1) * blk + col
        local = (jnp.abs(row + blk - col) <= GQA_WINDOW) & (k_abs >= 0) & (k_abs < S)
        mask = local | (col >= span)
        heads = []
        for g in range(GQA_KV_HEADS):
            heads += group_out(g, kcat_ref[:, g * HD:(g + 1) * HD], vcat_ref[:, g * HD:(g + 1) * HD], mask)
        finish(heads)

    @pl.when(i >= nq)
    def _():
        heads = []
        for g in range(GQA_KV_HEADS):
            heads += group_out(g, kx_ref[:, g * HD:(g + 1) * HD], vx_ref[:, g * HD:(g + 1) * HD], None)
        finish(heads)


def _gqa_mixer(lay, h, mod, gain, w_in, w_out, sinks, with_ctx_out):
    D = h.shape[1]
    B, S, C = lay.B, lay.S, lay.C
    qd, kd = GQA_HEADS * GQA_HEAD_DIM, GQA_KV_HEADS * GQA_HEAD_DIM
    tm = lay.tile(512)
    midx, pidx = lay.mod_index(tm), lay.pos_index(tm)
    cos, sa, sb = _rope_tables(S, GQA_HEAD_DIM, tm, 0)
    tab = pl.BlockSpec((tm, LANES), lambda t: (pidx(t), 0))
    q, k, v = pl.pallas_call(
        _gqa_proj_kernel,
        grid=(lay.n_all // tm,),
        in_specs=[pl.BlockSpec((tm, D), lambda t: (t, 0)),
                  pl.BlockSpec((1, N_MOD, D), lambda t: (midx(t), 0, 0)),
                  pl.BlockSpec((1, D), lambda t: (0, 0)),
                  pl.BlockSpec((D, qd + 2 * kd), lambda t: (0, 0)),
                  tab, tab, tab],
        out_specs=[pl.BlockSpec((tm, qd), lambda t: (t, 0)),
                   pl.BlockSpec((tm, kd), lambda t: (t, 0)),
                   pl.BlockSpec((tm, kd), lambda t: (t, 0))],
        out_shape=[jax.ShapeDtypeStruct((lay.n_all, qd), BF16),
                   jax.ShapeDtypeStruct((lay.n_all, kd), BF16),
                   jax.ShapeDtypeStruct((lay.n_all, kd), BF16)],
        compiler_params=_params("arbitrary"),
        name="gqa_proj",
    )(h, mod, gain.reshape(1, D), w_in.astype(BF16), cos, sa, sb)

    blk = GQA_BLOCK
    nq, nc = S // blk, C // blk
    n_i = nq + (nc if with_ctx_out else 0)
    lat_blocks = lay.n_lat // C

    def q_index(b, i):
        return jnp.where(i < nq, b * nq + i, B * nq + b * nc + (i - nq))

    def win(delta):
        return lambda b, i: (b * nq + jnp.clip(i + delta, 0, nq - 1), 0)

    kv_win = [pl.BlockSpec((blk, kd), win(d)) for d in (-1, 0, 1)]
    kv_ctx = pl.BlockSpec((C, kd), lambda b, i: (lat_blocks + b, 0))
    row = pl.BlockSpec((blk, D), lambda b, i: (q_index(b, i), 0))
    return pl.pallas_call(
        functools.partial(_gqa_attn_kernel, nq=nq, S=S, C=C),
        grid=(B, n_i),
        in_specs=[pl.BlockSpec(memory_space=pltpu.SMEM),
                  pl.BlockSpec((blk, qd), lambda b, i: (q_index(b, i), 0)),
                  *kv_win, *kv_win, kv_ctx, kv_ctx, row,
                  pl.BlockSpec((1, N_MOD, D), lambda b, i: (jnp.where(i < nq, b, B), 0, 0)),
                  pl.BlockSpec((qd, D), lambda b, i: (0, 0))],
        out_specs=row,
        out_shape=jax.ShapeDtypeStruct(h.shape, F32),
        input_output_aliases={10: 0},
        scratch_shapes=[pltpu.VMEM((3 * blk + C, kd), BF16), pltpu.VMEM((3 * blk + C, kd), BF16)],
        compiler_params=_params("arbitrary", "arbitrary"),
        name="gqa_attn",
    )(sinks.astype(F32), q, k, k, k, v, v, v, k, v, h, mod, w_out.astype(BF16))


def _diff_proj_kernel(h_ref, mod_ref, gain_ref, w_ref, cos_ref, sa_ref, sb_ref, q_ref, k_ref, v_ref):
    qk = DIFF_HEADS * 2 * DIFF_HEAD_DIM
    a = _pre(h_ref[...], gain_ref[...], mod_ref[0], 3).astype(BF16)
    p = _dot(a, w_ref[...])
    cos, sa, sb = cos_ref[...], sa_ref[...], sb_ref[...]
    f = DIFF_HEAD_DIM // 4
    scale = DIFF_HEAD_DIM ** -0.5
    for c in range(qk // LANES):
        x = _rope(p[:, c * LANES:(c + 1) * LANES], cos, sa, sb, f)
        q_ref[:, c * LANES:(c + 1) * LANES] = (x * scale).astype(BF16)
        x = _rope(p[:, qk + c * LANES:qk + (c + 1) * LANES], cos, sa, sb, f)
        k_ref[:, c * LANES:(c + 1) * LANES] = x.astype(BF16)
    v_ref[...] = p[:, 2 * qk:].astype(BF16)


def _diff_attn_kernel(lam_ref, sub_ref, q_ref, kl_ref, vl_ref, kx_ref, vx_ref, h_ref, mod_ref, wo_ref, o_ref,
                      *, nq, lambda_init):
    i = pl.program_id(1)
    HD = DIFF_HEAD_DIM
    lp = lam_ref[...]
    lam = (jnp.exp(jnp.sum(lp[0:1] * lp[1:2], keepdims=True))
           - jnp.exp(jnp.sum(lp[2:3] * lp[3:4], keepdims=True)) + lambda_init)

    def run(kv_pairs):
        def scores(h):
            los = [(2 * h + j) * HD for j in range(2)]
            return [[_dot_nt(q_ref[:, lo:lo + HD], k_ref[:, lo:lo + HD]) for k_ref, _ in kv_pairs] for lo in los]

        heads = []
        ss = scores(0)
        for h in range(DIFF_HEADS):
            nxt = scores(h + 1) if h + 1 < DIFF_HEADS else None
            es, inv = [], []
            for parts in ss:
                m = functools.reduce(jnp.maximum, [jnp.max(s, axis=-1, keepdims=True) for s in parts])
                e = [jnp.exp(s - m) for s in parts]
                es.append(e)
                inv.append(1.0 / functools.reduce(lambda a, b: a + b, [jnp.sum(x, axis=-1, keepdims=True) for x in e]))
            a0, a1 = inv[0], lam * inv[1]
            o = None
            for n, (_, v_ref) in enumerate(kv_pairs):
                p = (es[0][n] * a0 - es[1][n] * a1).astype(BF16)
                t = _dot(p, v_ref[:, 2 * h * HD:2 * (h + 1) * HD])
                o = t if o is None else o + t
            heads.append(_rmsnorm(o, sub_ref[...]) * (1.0 - lambda_init))
            ss = nxt
        o = jnp.concatenate(heads, axis=-1).astype(BF16)
        m = mod_ref[0]
        o_ref[...] = h_ref[...] + m[5:6] * _dot(o, wo_ref[...])

    @pl.when(i < nq)
    def _():
        run([(kl_ref, vl_ref), (kx_ref, vx_ref)])

    @pl.when(i >= nq)
    def _():
        run([(kx_ref, vx_ref)])


def _diff_mixer(lay, h, mod, gain, w_in, w_out, lam_params, subln_w, lambda_init, with_ctx_out):
    D = h.shape[1]
    B, S, C = lay.B, lay.S, lay.C
    qk = DIFF_HEADS * 2 * DIFF_HEAD_DIM
    tm = lay.tile(512)
    midx, pidx = lay.mod_index(tm), lay.pos_index(tm)
    cos, sa, sb = _rope_tables(S, DIFF_HEAD_DIM, tm, 0)
    tab = pl.BlockSpec((tm, LANES), lambda t: (pidx(t), 0))
    out = jax.ShapeDtypeStruct((lay.n_all, qk), BF16)
    q, k, v = pl.pallas_call(
        _diff_proj_kernel,
        grid=(lay.n_all // tm,),
        in_specs=[pl.BlockSpec((tm, D), lambda t: (t, 0)),
                  pl.BlockSpec((1, N_MOD, D), lambda t: (midx(t), 0, 0)),
                  pl.BlockSpec((1, D), lambda t: (0, 0)),
                  pl.BlockSpec((D, 3 * qk), lambda t: (0, 0)),
                  tab, tab, tab],
        out_specs=[pl.BlockSpec((tm, qk), lambda t: (t, 0))] * 3,
        out_shape=[out, out, out],
        compiler_params=_params("arbitrary"),
        name="diff_proj",
    )(h, mod, gain.reshape(1, D), w_in.astype(BF16), cos, sa, sb)

    tq = _divisor(math.gcd(S, C), 256)
    q_index, nq, nc, lat, ctx = _attn_specs(lay, tq, [qk, qk])
    row = pl.BlockSpec((tq, D), lambda b, i: (q_index(b, i), 0))
    return pl.pallas_call(
        functools.partial(_diff_attn_kernel, nq=nq, lambda_init=lambda_init),
        grid=(B, nq + (nc if with_ctx_out else 0)),
        in_specs=[pl.BlockSpec((4, DIFF_HEAD_DIM), lambda b, i: (0, 0)),
                  pl.BlockSpec((1, 2 * DIFF_HEAD_DIM), lambda b, i: (0, 0)),
                  pl.BlockSpec((tq, qk), lambda b, i: (q_index(b, i), 0)),
                  *lat, *ctx, row,
                  pl.BlockSpec((1, N_MOD, D), lambda b, i: (jnp.where(i < nq, b, B), 0, 0)),
                  pl.BlockSpec((qk, D), lambda b, i: (0, 0))],
        out_specs=row,
        out_shape=jax.ShapeDtypeStruct(h.shape, F32),
        input_output_aliases={7: 0},
        compiler_params=_params("arbitrary", "arbitrary"),
        name="diff_attn",
    )(lam_params.astype(F32), subln_w.reshape(1, -1), q, k, v, k, v, h, mod, w_out.astype(BF16))


def _hgrn_proj_kernel(h_ref, mod_ref, gain_ref, lb_ref, w_ref, o_ref, a_ref, *, layer):
    j = pl.program_id(1)

    @pl.when(j == 0)
    def _():
        a_ref[...] = _pre(h_ref[...], gain_ref[...], mod_ref[0], 3).astype(BF16)

    p = _dot(a_ref[...], w_ref[...])

    @pl.when((j == 0) | (j == 4))
    def _():
        o_ref[...] = _silu(p)

    @pl.when((j == 1) | (j == 2))
    def _():
        raw = lb_ref[...]
        e = jnp.exp(raw - jnp.max(raw, axis=0, keepdims=True))
        soft = e * (1.0 / jnp.sum(e, axis=0, keepdims=True))
        lb = jnp.sum(soft[1:layer + 1], axis=0, keepdims=True) if layer else jnp.zeros_like(soft[0:1])
        o_ref[...] = lb + (1.0 - lb) * jax.nn.sigmoid(p)

    @pl.when(j == 3)
    def _():
        o_ref[...] = p


def _split3(x):
    hi = x.astype(BF16)
    r = x - hi.astype(F32)
    mid = r.astype(BF16)
    lo = (r - mid.astype(F32)).astype(BF16)
    return hi, mid, lo


def _hgrn_scan_kernel(q_ref, f_ref, v_ref, o_ref, st_ref, *, n_chunks):
    d = pl.program_id(1)
    step = pl.program_id(2)
    C = HGRN_CHUNK
    DK, DV, H = HGRN_KEY_DIM, HGRN_VAL_DIM, HGRN_HEADS

    @pl.when(step == 0)
    def _():
        st_ref[...] = jnp.zeros_like(st_ref)

    fwd = d == 0
    r = lax.broadcasted_iota(jnp.int32, (C, C), 0)
    c = lax.broadcasted_iota(jnp.int32, (C, C), 1)
    sign = jnp.where(fwd, 1, -1)
    tri = (r - c) * sign >= 0
    tri_b = jnp.where(tri, 1.0, 0.0).astype(BF16)
    rows = lax.broadcasted_iota(jnp.int32, (C, 1), 0)
    mid_rows = (rows - jnp.where(fwd, C // 2, C - 1 - C // 2)) * sign <= 0

    def chunk(n, carry):
        base = pl.multiple_of(jnp.where(fwd, n, n_chunks - 1 - n) * C, C)
        q = q_ref[pl.ds(base, C), :]
        f = f_ref[pl.ds(base, C), :]
        v = v_ref[pl.ds(base, C), :].astype(BF16)
        logf = jnp.log(f)
        k = 1.0 - f
        g = functools.reduce(lambda a, b: a + b, [_dot(tri_b, part) for part in _split3(logf)])
        g_last = jnp.sum(logf, axis=0, keepdims=True)
        g_mid = jnp.sum(jnp.where(mid_rows, logf, 0.0), axis=0, keepdims=True)
        qa = (q * jnp.exp(g - g_mid)).astype(BF16)
        kb = (k * jnp.exp(g_mid - g)).astype(BF16)
        qg = (q * jnp.exp(g)).astype(BF16)
        kd = (k * jnp.exp(g_last - g)).astype(BF16)
        decay = jnp.exp(g_last)
        outs = []
        for h in range(H):
            ks = slice(h * DK, (h + 1) * DK)
            vs = slice(h * DV, (h + 1) * DV)
            a = jnp.where(tri, _dot_nt(qa[:, ks], kb[:, ks]), 0.0)
            st = st_ref[vs, :]
            outs.append(_dot(a.astype(BF16), v[:, vs]) + _dot_nt(qg[:, ks], st.astype(BF16)))
            st_ref[vs, :] = st * decay[:, ks] + _dot_tn(v[:, vs], kd[:, ks])
        o_ref[0, pl.ds(base, C), :] = jnp.concatenate(outs, axis=-1)
        return carry

    lax.fori_loop(0, n_chunks, chunk, 0)


def _hgrn_out_kernel(o_ref, gate_ref, nw_ref, h_ref, mod_ref, wo_ref, out_ref):
    DV = HGRN_VAL_DIM
    o = o_ref[0] + o_ref[1]
    gate = gate_ref[...]
    nw = nw_ref[...]
    ys = [_rmsnorm(o[:, h * DV:(h + 1) * DV], nw) * gate[:, h * DV:(h + 1) * DV] for h in range(HGRN_HEADS)]
    y = jnp.concatenate(ys, axis=-1).astype(BF16)
    m = mod_ref[0]
    out_ref[...] = h_ref[...] + m[5:6] * _dot(y, wo_ref[...])


def _hgrn_mixer(lay, h, mod, gain, w_in, w_out, norm_w, lower_bounds, layer):
    D = h.shape[1]
    B, S, C = lay.B, lay.S, lay.C
    KD = HGRN_HEADS * HGRN_KEY_DIM
    tm = lay.tile(512)
    midx = lay.mod_index(tm)
    n_cols = w_in.shape[1] // KD
    proj = pl.pallas_call(
        functools.partial(_hgrn_proj_kernel, layer=layer),
        grid=(lay.n_all // tm, n_cols),
        in_specs=[pl.BlockSpec((tm, D), lambda t, j: (t, 0)),
                  pl.BlockSpec((1, N_MOD, D), lambda t, j: (midx(t), 0, 0)),
                  pl.BlockSpec((1, D), lambda t, j: (0, 0)),
                  pl.BlockSpec(lower_bounds.shape, lambda t, j: (0, 0)),
                  pl.BlockSpec((D, KD), lambda t, j: (0, j))],
        out_specs=pl.BlockSpec((tm, KD), lambda t, j: (t, j)),
        out_shape=jax.ShapeDtypeStruct((lay.n_all, n_cols * KD), F32),
        scratch_shapes=[pltpu.VMEM((tm, D), BF16)],
        compiler_params=_params("arbitrary", "arbitrary"),
        name="hgrn_proj",
    )(h, mod, gain.reshape(1, D), lower_bounds.astype(F32), w_in.astype(BF16))

    ts = C
    n_lat_steps = S // ts
    lat_blocks = lay.n_lat // ts

    def rows(b, d, s):
        lat = b * n_lat_steps + jnp.where(d == 0, s - 1, n_lat_steps - s)
        return jnp.where(s == 0, lat_blocks + b, lat)

    o = pl.pallas_call(
        functools.partial(_hgrn_scan_kernel, n_chunks=ts // HGRN_CHUNK),
        grid=(B, 2, 1 + n_lat_steps),
        in_specs=[pl.BlockSpec((ts, KD), lambda b, d, s: (rows(b, d, s), 0)),
                  pl.BlockSpec((ts, KD), lambda b, d, s: (rows(b, d, s), 1 + d)),
                  pl.BlockSpec((ts, KD), lambda b, d, s: (rows(b, d, s), 3))],
        out_specs=pl.BlockSpec((1, ts, KD), lambda b, d, s: (d, rows(b, d, s), 0)),
        out_shape=jax.ShapeDtypeStruct((2, lay.n_all, KD), F32),
        scratch_shapes=[pltpu.VMEM((HGRN_HEADS * HGRN_VAL_DIM, HGRN_KEY_DIM), F32)],
        compiler_params=_params("arbitrary", "arbitrary", "arbitrary"),
        name="hgrn_scan",
    )(proj, proj, proj)

    row = pl.BlockSpec((tm, D), lambda t: (t, 0))
    return pl.pallas_call(
        _hgrn_out_kernel,
        grid=(lay.n_all // tm,),
        in_specs=[pl.BlockSpec((2, tm, KD), lambda t: (0, t, 0)),
                  pl.BlockSpec((tm, KD), lambda t: (t, 4)),
                  pl.BlockSpec((1, HGRN_VAL_DIM), lambda t: (0, 0)),
                  row,
                  pl.BlockSpec((1, N_MOD, D), lambda t: (midx(t), 0, 0)),
                  pl.BlockSpec((KD, D), lambda t: (0, 0))],
        out_specs=row,
        out_shape=jax.ShapeDtypeStruct(h.shape, F32),
        input_output_aliases={3: 0},
        compiler_params=_params("arbitrary"),
        name="hgrn_out",
    )(o, proj, norm_w.reshape(1, -1), h, mod, w_out.astype(BF16))


MLA_QK_PAD = LANES


def _mla_proj_kernel(h_ref, mod_ref, gain_ref, wd_ref, qn_ref, kvn_ref, wuq_ref, wukv_ref,
                     cos_ref, sa_ref, sb_ref, q_ref, k_ref, v_ref):
    H, P = MLA_HEADS, MLA_QK_PAD
    a = _pre(h_ref[...], gain_ref[...], mod_ref[0], 3).astype(BF16)
    dn = _dot(a, wd_ref[...])
    cos, sa, sb = cos_ref[...], sa_ref[...], sb_ref[...]
    f = MLA_ROPE // 4
    cq = _rmsnorm(dn[:, :MLA_Q_LORA], qn_ref[...]).astype(BF16)
    q = _dot(cq, wuq_ref[...])
    for hd in range(H):
        q_ref[:, hd * P:(hd + 1) * P] = _rope(q[:, hd * P:(hd + 1) * P], cos, sa, sb, f).astype(BF16)
    ckv = _rmsnorm(dn[:, MLA_Q_LORA:MLA_Q_LORA + MLA_KV_LORA], kvn_ref[...]).astype(BF16)
    kv = _dot(ckv, wukv_ref[...])
    kr = _rope(dn[:, MLA_Q_LORA + MLA_KV_LORA:], cos, sa, sb, f)
    for hd in range(H):
        k_ref[:, hd * P:(hd + 1) * P] = (kv[:, hd * P:(hd + 1) * P] + kr).astype(BF16)
    v_ref[...] = kv[:, H * P:].astype(BF16)


def _mla_attn_kernel(q_ref, kl_ref, vl_ref, kx_ref, vx_ref, h_ref, mod_ref, wo_ref, o_ref, *, nq, with_ctx):
    i = pl.program_id(1)
    P, VD = MLA_QK_PAD, MLA_V_DIM
    c = ((MLA_NOPE + MLA_ROPE) ** -0.5) * LOG2_E

    def run(kv_pairs):
        def scores(hd):
            qh = q_ref[:, hd * P:(hd + 1) * P]
            return [_dot_nt(qh, k_ref[:, hd * P:(hd + 1) * P]) for k_ref, _ in kv_pairs]

        heads = []
        ss = scores(0)
        for hd in range(MLA_HEADS):
            nxt = scores(hd + 1) if hd + 1 < MLA_HEADS else None
            m = functools.reduce(jnp.maximum, [jnp.max(s, axis=-1, keepdims=True) for s in ss])
            mc = m * c
            o, l = None, None
            for s, (_, v_ref) in zip(ss, kv_pairs):
                e = jnp.exp2(s * c - mc)
                ls = jnp.sum(e, axis=-1, keepdims=True)
                l = ls if l is None else l + ls
                t = _dot(e.astype(BF16), v_ref[:, hd * VD:(hd + 1) * VD])
                o = t if o is None else o + t
            heads.append(o * (1.0 / l))
            ss = nxt
        o = jnp.concatenate(heads, axis=-1).astype(BF16)
        m = mod_ref[0]
        o_ref[...] = h_ref[...] + m[5:6] * _dot(o, wo_ref[...])

    if not with_ctx:
        run([(kl_ref, vl_ref), (kx_ref, vx_ref)])
        return

    @pl.when(i < nq)
    def _():
        run([(kl_ref, vl_ref), (kx_ref, vx_ref)])

    @pl.when(i >= nq)
    def _():
        run([(kx_ref, vx_ref)])


def _mla_mixer(lay, h, mod, gain, w_down, q_norm_w, kv_norm_w, w_uq, w_ukv, w_out, with_ctx_out):
    D = h.shape[1]
    B, S, C = lay.B, lay.S, lay.C
    H, P, VD = MLA_HEADS, MLA_QK_PAD, MLA_V_DIM
    qk = MLA_NOPE + MLA_ROPE
    lora = MLA_Q_LORA + MLA_KV_LORA
    kr_cols = jnp.zeros((D, P), F32).at[:, MLA_NOPE:qk].set(w_down[:, lora:])
    wd = jnp.concatenate([w_down[:, :lora], kr_cols], axis=1).astype(BF16)
    wuq = jnp.pad(w_uq.reshape(MLA_Q_LORA, H, qk), ((0, 0), (0, 0), (0, P - qk))).reshape(MLA_Q_LORA, H * P)
    ukv = w_ukv.reshape(MLA_KV_LORA, H, MLA_NOPE + VD)
    wkn = jnp.pad(ukv[..., :MLA_NOPE], ((0, 0), (0, 0), (0, P - MLA_NOPE))).reshape(MLA_KV_LORA, H * P)
    wukv = jnp.concatenate([wkn, ukv[..., MLA_NOPE:].reshape(MLA_KV_LORA, H * VD)], axis=1)

    tm = lay.tile(512)
    midx, pidx = lay.mod_index(tm), lay.pos_index(tm)
    cos, sa, sb = _rope_tables(S, MLA_ROPE, tm, MLA_NOPE)
    tab = pl.BlockSpec((tm, LANES), lambda t: (pidx(t), 0))
    full = lambda shape: pl.BlockSpec(shape, lambda t: (0, 0))
    q, k, v = pl.pallas_call(
        _mla_proj_kernel,
        grid=(lay.n_all // tm,),
        in_specs=[pl.BlockSpec((tm, D), lambda t: (t, 0)),
                  pl.BlockSpec((1, N_MOD, D), lambda t: (midx(t), 0, 0)),
                  full((1, D)), full((D, lora + P)), full((1, MLA_Q_LORA)), full((1, MLA_KV_LORA)),
                  full((MLA_Q_LORA, H * P)), full((MLA_KV_LORA, H * (P + VD))),
                  tab, tab, tab],
        out_specs=[pl.BlockSpec((tm, H * P), lambda t: (t, 0)),
                   pl.BlockSpec((tm, H * P), lambda t: (t, 0)),
                   pl.BlockSpec((tm, H * VD), lambda t: (t, 0))],
        out_shape=[jax.ShapeDtypeStruct((lay.n_all, H * P), BF16),
                   jax.ShapeDtypeStruct((lay.n_all, H * P), BF16),
                   jax.ShapeDtypeStruct((lay.n_all, H * VD), BF16)],
        compiler_params=_params("arbitrary"),
        name="mla_proj",
    )(h, mod, gain.reshape(1, D), wd, q_norm_w.reshape(1, -1), kv_norm_w.reshape(1, -1),
      wuq.astype(BF16), wukv.astype(BF16), cos, sa, sb)

    tq = _divisor(math.gcd(S, C), 256)
    q_index, nq, nc, lat, ctx = _attn_specs(lay, tq, [H * P, H * VD])
    row = pl.BlockSpec((tq, D), lambda b, i: (q_index(b, i), 0))
    return pl.pallas_call(
        functools.partial(_mla_attn_kernel, nq=nq, with_ctx=with_ctx_out),
        grid=(B, nq + (nc if with_ctx_out else 0)),
        in_specs=[pl.BlockSpec((tq, H * P), lambda b, i: (q_index(b, i), 0)),
                  *lat, *ctx, row,
                  pl.BlockSpec((1, N_MOD, D), lambda b, i: (jnp.where(i < nq, b, B), 0, 0)),
                  pl.BlockSpec((H * VD, D), lambda b, i: (0, 0))],
        out_specs=row,
        out_shape=jax.ShapeDtypeStruct(h.shape, F32),
        input_output_aliases={5: 0},
        compiler_params=_params("arbitrary", "arbitrary"),
        name="mla_attn",
    )(q, k, v, k, v, h, mod, w_out.astype(BF16))


def kernel(x, c, ctx, c_ctx, ada_w, ada_b, norm_w, final_norm_w, ffn_w_gate, ffn_w_up, ffn_w_down,
           gqa_w_in, gqa_w_out, gqa_sinks, diff_w_in, diff_w_out, diff_lambda, diff_subln_w,
           hgrn_w_in, hgrn_w_out, hgrn_norm_w, hgrn_lower_bounds,
           mla_w_down, mla_q_norm_w, mla_kv_norm_w, mla_w_uq, mla_w_ukv, mla_w_out):
    B, S, D = x.shape
    C = ctx.shape[1]
    lay = _Layout(B, S, C)
    depth = ada_w.shape[0]

    rows = -(-(B + 1) // 8) * 8
    cc = jnp.zeros((rows, D), F32).at[:B].set(c).at[B].set(c_ctx)
    mods = _ada_table(cc, ada_w, ada_b).reshape(depth, rows, N_MOD, D)

    wg, wu, wd = ffn_w_gate.astype(BF16), ffn_w_up.astype(BF16), ffn_w_down.astype(BF16)
    h = jnp.concatenate([x.reshape(B * S, D), ctx.reshape(B * C, D)], axis=0)
    for i in range(depth):
        kind, j = i % 4, i // 4
        last = i == depth - 1
        mod = mods[i]
        h = _ffn(lay, h, mod, norm_w[i, 0], final_norm_w, wg[i, 0], wu[i, 0], wd[i, 0], 0)
        if kind == 0:
            h = _gqa_mixer(lay, h, mod, norm_w[i, 1], gqa_w_in[j], gqa_w_out[j], gqa_sinks[j], not last)
        elif kind == 1:
            lambda_init = 0.8 - 0.6 * math.exp(-0.3 * i)
            h = _diff_mixer(lay, h, mod, norm_w[i, 1], diff_w_in[j], diff_w_out[j], diff_lambda[j],
                            diff_subln_w[j], lambda_init, not last)
        elif kind == 2:
            h = _hgrn_mixer(lay, h, mod, norm_w[i, 1], hgrn_w_in[j], hgrn_w_out[j], hgrn_norm_w[j],
                            hgrn_lower_bounds, i)
        else:
            h = _mla_mixer(lay, h, mod, norm_w[i, 1], mla_w_down[j], mla_q_norm_w[j], mla_kv_norm_w[j],
                           mla_w_uq[j], mla_w_ukv[j], mla_w_out[j], not last)
        h = _ffn(lay, h, mod, norm_w[i, 2], final_norm_w, wg[i, 1], wu[i, 1], wd[i, 1], 6,
                 lat_only=last, final_norm=last)
    return h[:B * S].reshape(B, S, D)
```

```python
import functools
import math

import jax
import jax.numpy as jnp
from jax import lax
from jax.experimental import pallas as pl
from jax.experimental.pallas import tpu as pltpu

F32 = jnp.float32
BF16 = jnp.bfloat16

D_MODEL = 1024
DEPTH = 4
GRID_W = 64
N_MOD = 9
ROPE_BASE = 10000.0
EPS = 1e-6
NEG_INF = -1e30
D_FF = 2816

GQA_HEADS = 16
GQA_KV_HEADS = 4
GQA_HEAD_DIM = 64
GQA_WINDOW = 128
GQA_BLOCK = 128

DIFF_HEADS = 8
DIFF_HEAD_DIM = 64

HGRN_HEADS = 8
HGRN_KEY_DIM = 128
HGRN_VAL_DIM = D_MODEL // HGRN_HEADS
HGRN_CHUNK = 64

MLA_HEADS = 16
MLA_Q_LORA = 256
MLA_KV_LORA = 256
MLA_NOPE = 64
MLA_ROPE = 32
MLA_V_DIM = 64

LOG2_E = 1.4426950408889634
LANES = 128
VMEM_LIMIT = 56 * 1024 * 1024


def _dot(a, b):
    return jnp.dot(a, b, preferred_element_type=F32)


def _dot_nt(a, b):
    return lax.dot_general(a, b, (((1,), (1,)), ((), ())), preferred_element_type=F32)


def _dot_tn(a, b):
    return lax.dot_general(a, b, (((0,), (0,)), ((), ())), preferred_element_type=F32)


def _params(*sem):
    return pltpu.CompilerParams(dimension_semantics=sem, vmem_limit_bytes=VMEM_LIMIT)


def _divisor(n, pref):
    t = min(n, pref)
    while n % t:
        t -= 8
    return t


class _Layout:
    def __init__(self, B, S, C):
        self.B, self.S, self.C = B, S, C
        self.n_lat = B * S
        self.n_all = B * S + B * C

    def tile(self, pref):
        return _divisor(math.gcd(self.S, self.B * self.C), pref)

    def n_tiles(self, tm, lat_only=False):
        return (self.n_lat if lat_only else self.n_all) // tm

    def mod_index(self, tm):
        n_lat_tiles, per_batch, B = self.n_lat // tm, self.S // tm, self.B
        return lambda t: jnp.where(t < n_lat_tiles, t // per_batch, B)

    def pos_index(self, tm):
        n_lat_tiles, per_batch = self.n_lat // tm, self.S // tm
        return lambda t: jnp.where(t < n_lat_tiles, t % per_batch, per_batch)


def _rmsnorm(x, w):
    return (x * lax.rsqrt(jnp.mean(x * x, axis=-1, keepdims=True) + EPS)) * w


def _pre(h, gain, m, off):
    return _rmsnorm(h, gain) * (1.0 + m[off + 1:off + 2]) + m[off:off + 1]


def _silu(x):
    return x * jax.nn.sigmoid(x)


def _rope(x, cos, sa, sb, shift):
    return x * cos + pltpu.roll(x, LANES - shift, 1) * sa + pltpu.roll(x, shift, 1) * sb


def _ada_kernel(c_ref, w_ref, b_ref, o_ref):
    sc = _silu(c_ref[...]).astype(BF16)
    o_ref[0] = _dot(sc, w_ref[0].astype(BF16)) + b_ref[0]


def _ada_table(cc, ada_w, ada_b):
    rows, D = cc.shape
    n_out = ada_w.shape[2]
    tn = n_out // 8
    return pl.pallas_call(
        _ada_kernel,
        grid=(DEPTH, n_out // tn),
        in_specs=[pl.BlockSpec((rows, D), lambda i, j: (0, 0)),
                  pl.BlockSpec((1, D, tn), lambda i, j: (i, 0, j)),
                  pl.BlockSpec((1, 1, tn), lambda i, j: (i, 0, j))],
        out_specs=pl.BlockSpec((1, rows, tn), lambda i, j: (i, 0, j)),
        out_shape=jax.ShapeDtypeStruct((DEPTH, rows, n_out), F32),
        compiler_params=_params("arbitrary", "arbitrary"),
        name="ada_table",
    )(cc, ada_w, ada_b.reshape(DEPTH, 1, n_out))


FFN_CHUNK = 2 * LANES


def _ffn_kernel(h_ref, mod_ref, gain_ref, fin_ref, wg_ref, wu_ref, wd_ref, o_ref, a_ref, *, off, final_norm):
    m = mod_ref[0]
    xm = _pre(h_ref[...], gain_ref[...], m, off).astype(BF16)
    n = D_FF // FFN_CHUNK

    def gate_up(c):
        cols = slice(c * FFN_CHUNK, (c + 1) * FFN_CHUNK)
        return _dot(xm, wg_ref[:, cols]), _dot(xm, wu_ref[:, cols])

    cur = gate_up(0)
    for c in range(n):
        nxt = gate_up(c + 1) if c + 1 < n else None
        g, u = cur
        a_ref[:, c * FFN_CHUNK:(c + 1) * FFN_CHUNK] = (_silu(g) * u).astype(BF16)
        cur = nxt
    out = h_ref[...] + (0.5 * m[off + 2:off + 3]) * _dot(a_ref[...], wd_ref[...])
    if final_norm:
        out = _rmsnorm(out, fin_ref[...])
    o_ref[...] = out


def _ffn(lay, h, mod, gain, fin_w, wg, wu, wd, off, lat_only=False, final_norm=False):
    D = h.shape[1]
    tm = lay.tile(512)
    n_rows = lay.n_lat if lat_only else lay.n_all
    midx = lay.mod_index(tm)
    const = lambda t: (0, 0)
    one = pl.Buffered(1)
    return pl.pallas_call(
        functools.partial(_ffn_kernel, off=off, final_norm=final_norm),
        grid=(n_rows // tm,),
        in_specs=[pl.BlockSpec((tm, D), lambda t: (t, 0)),
                  pl.BlockSpec((1, N_MOD, D), lambda t: (midx(t), 0, 0)),
                  pl.BlockSpec((1, D), const),
                  pl.BlockSpec((1, D), const),
                  pl.BlockSpec((D, D_FF), const, pipeline_mode=one),
                  pl.BlockSpec((D, D_FF), const, pipeline_mode=one),
                  pl.BlockSpec((D_FF, D), const, pipeline_mode=one)],
        out_specs=pl.BlockSpec((tm, D), lambda t: (t, 0)),
        out_shape=jax.ShapeDtypeStruct((n_rows, D), F32),
        scratch_shapes=[pltpu.VMEM((tm, D_FF), BF16)],
        compiler_params=_params("arbitrary"),
        name="ffn",
    )(h, mod, gain.reshape(1, D), fin_w.reshape(1, D), wg, wu, wd)


def _rope_tables(S, rot_dim, pad_rows, lane_off):
    rows = S // GRID_W
    row = jnp.repeat(jnp.arange(rows, dtype=F32), GRID_W)
    col = jnp.tile(jnp.arange(GRID_W, dtype=F32), rows)
    axis_dim = rot_dim // 2
    inv_freq = ROPE_BASE ** (-jnp.arange(0, axis_dim, 2, dtype=F32) / axis_dim)
    ang_r = row[:, None] * inv_freq[None, :]
    ang_c = col[:, None] * inv_freq[None, :]
    ang = jnp.concatenate([ang_r, ang_r, ang_c, ang_c], axis=-1)
    cos, sin = jnp.cos(ang), jnp.sin(ang)
    f = rot_dim // 4
    first = (jnp.arange(rot_dim) % (2 * f)) < f
    sa = jnp.where(first[None, :], -sin, 0.0)
    sb = jnp.where(first[None, :], 0.0, sin)
    period = 64 if lane_off + rot_dim <= 64 else LANES

    def widen(t, fill):
        blk = jnp.full((S, period), fill, F32).at[:, lane_off:lane_off + rot_dim].set(t)
        blk = jnp.tile(blk, (1, LANES // period))
        return jnp.concatenate([blk, jnp.full((pad_rows, LANES), fill, F32)], axis=0)

    return widen(cos, 1.0), widen(sa, 0.0), widen(sb, 0.0)


def _attn_specs(lay, tq, widths):
    S, C, B = lay.S, lay.C, lay.B
    nq, nc = S // tq, C // tq
    lat_blocks = lay.n_lat // C

    def q_index(b, i):
        return jnp.where(i < nq, b * nq + i, B * nq + b * nc + (i - nq))

    one = pl.Buffered(1)
    lat = [pl.BlockSpec((S, w), lambda b, i: (b, 0), pipeline_mode=one) for w in widths]
    ctx = [pl.BlockSpec((C, w), lambda b, i: (lat_blocks + b, 0), pipeline_mode=one) for w in widths]
    return q_index, nq, nc, lat, ctx


def _gqa_proj_kernel(h_ref, mod_ref, gain_ref, w_ref, cos_ref, sa_ref, sb_ref, q_ref, k_ref, v_ref):
    qd, kd = GQA_HEADS * GQA_HEAD_DIM, GQA_KV_HEADS * GQA_HEAD_DIM
    a = _pre(h_ref[...], gain_ref[...], mod_ref[0], 3).astype(BF16)
    p = _dot(a, w_ref[...])
    cos, sa, sb = cos_ref[...], sa_ref[...], sb_ref[...]
    f = GQA_HEAD_DIM // 4
    scale = GQA_HEAD_DIM ** -0.5
    for c in range(qd // LANES):
        x = _rope(p[:, c * LANES:(c + 1) * LANES], cos, sa, sb, f)
        q_ref[:, c * LANES:(c + 1) * LANES] = (x * scale).astype(BF16)
    for c in range(kd // LANES):
        x = _rope(p[:, qd + c * LANES:qd + (c + 1) * LANES], cos, sa, sb, f)
        k_ref[:, c * LANES:(c + 1) * LANES] = x.astype(BF16)
    v_ref[...] = p[:, qd + kd:].astype(BF16)


def _gqa_attn_kernel(sink_ref, q_ref, kp_ref, kc_ref, kn_ref, vp_ref, vc_ref, vn_ref, kx_ref, vx_ref,
                     h_ref, mod_ref, wo_ref, o_ref, kcat_ref, vcat_ref, *, nq, S, C):
    i = pl.program_id(1)
    blk = GQA_BLOCK
    G = GQA_HEADS // GQA_KV_HEADS
    HD = GQA_HEAD_DIM
    span = 3 * blk

    def finish(heads):
        o = jnp.concatenate(heads, axis=-1).astype(BF16)
        m = mod_ref[0]
        o_ref[...] = h_ref[...] + m[5:6] * _dot(o, wo_ref[...])

    def run(k_ref, v_ref, mask):
        kvs = range(GQA_KV_HEADS)
        qs = [jnp.concatenate([q_ref[:, (G * g + n) * HD:(G * g + n + 1) * HD] for n in range(G)], axis=0)
              for g in kvs]
        sinks = [jnp.concatenate([jnp.full((blk, 1), sink_ref[G * g + n], F32) for n in range(G)], axis=0)
                 for g in kvs]
        ss = [_dot_nt(qs[g], k_ref[:, g * HD:(g + 1) * HD]) for g in kvs]
        if mask is not None:
            ss = [jnp.where(mask, s, NEG_INF) for s in ss]
        ms = [jnp.maximum(jnp.max(ss[g], axis=-1, keepdims=True), sinks[g]) for g in kvs]
        es = [jnp.exp(ss[g] - ms[g]) for g in kvs]
        invs = [1.0 / (jnp.sum(es[g], axis=-1, keepdims=True) + jnp.exp(sinks[g] - ms[g])) for g in kvs]
        outs = [_dot(es[g].astype(BF16), v_ref[:, g * HD:(g + 1) * HD]) * invs[g] for g in kvs]
        finish([outs[g][n * blk:(n + 1) * blk] for g in kvs for n in range(G)])

    @pl.when(i < nq)
    def _():
        kcat_ref[0:blk] = kp_ref[...]
        kcat_ref[blk:2 * blk] = kc_ref[...]
        kcat_ref[2 * blk:span] = kn_ref[...]
        kcat_ref[span:] = kx_ref[...]
        vcat_ref[0:blk] = vp_ref[...]
        vcat_ref[blk:2 * blk] = vc_ref[...]
        vcat_ref[2 * blk:span] = vn_ref[...]
        vcat_ref[span:] = vx_ref[...]
        row = lax.broadcasted_iota(jnp.int32, (G * blk, span + C), 0) & (blk - 1)
        col = lax.broadcasted_iota(jnp.int32, (G * blk, span + C), 1)
        k_abs = (i - 1) * blk + col
        local = (jnp.abs(row + blk - col) <= GQA_WINDOW) & (k_abs >= 0) & (k_abs < S)
        run(kcat_ref, vcat_ref, local | (col >= span))

    @pl.when(i >= nq)
    def _():
        run(kx_ref, vx_ref, None)


def _gqa_mixer(lay, h, mod, gain, w_in, w_out, sinks, with_ctx_out):
    D = h.shape[1]
    B, S, C = lay.B, lay.S, lay.C
    qd, kd = GQA_HEADS * GQA_HEAD_DIM, GQA_KV_HEADS * GQA_HEAD_DIM
    tm = lay.tile(512)
    midx, pidx = lay.mod_index(tm), lay.pos_index(tm)
    cos, sa, sb = _rope_tables(S, GQA_HEAD_DIM, tm, 0)
    tab = pl.BlockSpec((tm, LANES), lambda t: (pidx(t), 0))
    q, k, v = pl.pallas_call(
        _gqa_proj_kernel,
        grid=(lay.n_all // tm,),
        in_specs=[pl.BlockSpec((tm, D), lambda t: (t, 0)),
                  pl.BlockSpec((1, N_MOD, D), lambda t: (midx(t), 0, 0)),
                  pl.BlockSpec((1, D), lambda t: (0, 0)),
                  pl.BlockSpec((D, qd + 2 * kd), lambda t: (0, 0)),
                  tab, tab, tab],
        out_specs=[pl.BlockSpec((tm, qd), lambda t: (t, 0)),
                   pl.BlockSpec((tm, kd), lambda t: (t, 0)),
                   pl.BlockSpec((tm, kd), lambda t: (t, 0))],
        out_shape=[jax.ShapeDtypeStruct((lay.n_all, qd), BF16),
                   jax.ShapeDtypeStruct((lay.n_all, kd), BF16),
                   jax.ShapeDtypeStruct((lay.n_all, kd), BF16)],
        compiler_params=_params("arbitrary"),
        name="gqa_proj",
    )(h, mod, gain.reshape(1, D), w_in.astype(BF16), cos, sa, sb)

    blk = GQA_BLOCK
    nq, nc = S // blk, C // blk
    n_i = nq + (nc if with_ctx_out else 0)
    lat_blocks = lay.n_lat // C

    def q_index(b, i):
        return jnp.where(i < nq, b * nq + i, B * nq + b * nc + (i - nq))

    def win(delta):
        return lambda b, i: (b * nq + jnp.clip(i + delta, 0, nq - 1), 0)

    kv_win = [pl.BlockSpec((blk, kd), win(d)) for d in (-1, 0, 1)]
    kv_ctx = pl.BlockSpec((C, kd), lambda b, i: (lat_blocks + b, 0))
    row = pl.BlockSpec((blk, D), lambda b, i: (q_index(b, i), 0))
    return pl.pallas_call(
        functools.partial(_gqa_attn_kernel, nq=nq, S=S, C=C),
        grid=(B, n_i),
        in_specs=[pl.BlockSpec(memory_space=pltpu.SMEM),
                  pl.BlockSpec((blk, qd), lambda b, i: (q_index(b, i), 0)),
                  *kv_win, *kv_win, kv_ctx, kv_ctx, row,
                  pl.BlockSpec((1, N_MOD, D), lambda b, i: (jnp.where(i < nq, b, B), 0, 0)),
                  pl.BlockSpec((qd, D), lambda b, i: (0, 0))],
        out_specs=row,
        out_shape=jax.ShapeDtypeStruct(h.shape, F32),
        input_output_aliases={10: 0},
        scratch_shapes=[pltpu.VMEM((3 * blk + C, kd), BF16), pltpu.VMEM((3 * blk + C, kd), BF16)],
        compiler_params=_params("arbitrary", "arbitrary"),
        name="gqa_attn",
    )(sinks.astype(F32), q, k, k, k, v, v, v, k, v, h, mod, w_out.astype(BF16))


def _diff_proj_kernel(h_ref, mod_ref, gain_ref, w_ref, cos_ref, sa_ref, sb_ref, q_ref, k_ref, v_ref):
    qk = DIFF_HEADS * 2 * DIFF_HEAD_DIM
    a = _pre(h_ref[...], gain_ref[...], mod_ref[0], 3).astype(BF16)
    p = _dot(a, w_ref[...])
    cos, sa, sb = cos_ref[...], sa_ref[...], sb_ref[...]
    f = DIFF_HEAD_DIM // 4
    scale = DIFF_HEAD_DIM ** -0.5
    for c in range(qk // LANES):
        x = _rope(p[:, c * LANES:(c + 1) * LANES], cos, sa, sb, f)
        q_ref[:, c * LANES:(c + 1) * LANES] = (x * scale).astype(BF16)
        x = _rope(p[:, qk + c * LANES:qk + (c + 1) * LANES], cos, sa, sb, f)
        k_ref[:, c * LANES:(c + 1) * LANES] = x.astype(BF16)
    v_ref[...] = p[:, 2 * qk:].astype(BF16)


def _diff_attn_kernel(lam_ref, sub_ref, q_ref, kl_ref, vl_ref, kx_ref, vx_ref, h_ref, mod_ref, wo_ref, o_ref,
                      *, nq, lambda_init):
    i = pl.program_id(1)
    HD = DIFF_HEAD_DIM
    lp = lam_ref[...]
    lam = (jnp.exp(jnp.sum(lp[0:1] * lp[1:2], keepdims=True))
           - jnp.exp(jnp.sum(lp[2:3] * lp[3:4], keepdims=True)) + lambda_init)

    def run(kv_pairs):
        def scores(h):
            los = [(2 * h + j) * HD for j in range(2)]
            return [[_dot_nt(q_ref[:, lo:lo + HD], k_ref[:, lo:lo + HD]) for k_ref, _ in kv_pairs] for lo in los]

        heads = []
        ss = scores(0)
        for h in range(DIFF_HEADS):
            nxt = scores(h + 1) if h + 1 < DIFF_HEADS else None
            es, inv = [], []
            for parts in ss:
                m = functools.reduce(jnp.maximum, [jnp.max(s, axis=-1, keepdims=True) for s in parts])
                e = [jnp.exp(s - m) for s in parts]
                es.append(e)
                inv.append(1.0 / functools.reduce(lambda a, b: a + b, [jnp.sum(x, axis=-1, keepdims=True) for x in e]))
            a0, a1 = inv[0], lam * inv[1]
            o = None
            for n, (_, v_ref) in enumerate(kv_pairs):
                p = (es[0][n] * a0 - es[1][n] * a1).astype(BF16)
                t = _dot(p, v_ref[:, 2 * h * HD:2 * (h + 1) * HD])
                o = t if o is None else o + t
            heads.append(_rmsnorm(o, sub_ref[...]) * (1.0 - lambda_init))
            ss = nxt
        o = jnp.concatenate(heads, axis=-1).astype(BF16)
        m = mod_ref[0]
        o_ref[...] = h_ref[...] + m[5:6] * _dot(o, wo_ref[...])

    @pl.when(i < nq)
    def _():
        run([(kl_ref, vl_ref), (kx_ref, vx_ref)])

    @pl.when(i >= nq)
    def _():
        run([(kx_ref, vx_ref)])


def _diff_mixer(lay, h, mod, gain, w_in, w_out, lam_params, subln_w, lambda_init, with_ctx_out):
    D = h.shape[1]
    B, S, C = lay.B, lay.S, lay.C
    qk = DIFF_HEADS * 2 * DIFF_HEAD_DIM
    tm = lay.tile(512)
    midx, pidx = lay.mod_index(tm), lay.pos_index(tm)
    cos, sa, sb = _rope_tables(S, DIFF_HEAD_DIM, tm, 0)
    tab = pl.BlockSpec((tm, LANES), lambda t: (pidx(t), 0))
    out = jax.ShapeDtypeStruct((lay.n_all, qk), BF16)
    q, k, v = pl.pallas_call(
        _diff_proj_kernel,
        grid=(lay.n_all // tm,),
        in_specs=[pl.BlockSpec((tm, D), lambda t: (t, 0)),
                  pl.BlockSpec((1, N_MOD, D), lambda t: (midx(t), 0, 0)),
                  pl.BlockSpec((1, D), lambda t: (0, 0)),
                  pl.BlockSpec((D, 3 * qk), lambda t: (0, 0)),
                  tab, tab, tab],
        out_specs=[pl.BlockSpec((tm, qk), lambda t: (t, 0))] * 3,
        out_shape=[out, out, out],
        compiler_params=_params("arbitrary"),
        name="diff_proj",
    )(h, mod, gain.reshape(1, D), w_in.astype(BF16), cos, sa, sb)

    tq = _divisor(math.gcd(S, C), 256)
    q_index, nq, nc, lat, ctx = _attn_specs(lay, tq, [qk, qk])
    row = pl.BlockSpec((tq, D), lambda b, i: (q_index(b, i), 0))
    return pl.pallas_call(
        functools.partial(_diff_attn_kernel, nq=nq, lambda_init=lambda_init),
        grid=(B, nq + (nc if with_ctx_out else 0)),
        in_specs=[pl.BlockSpec((4, DIFF_HEAD_DIM), lambda b, i: (0, 0)),
                  pl.BlockSpec((1, 2 * DIFF_HEAD_DIM), lambda b, i: (0, 0)),
                  pl.BlockSpec((tq, qk), lambda b, i: (q_index(b, i), 0)),
                  *lat, *ctx, row,
                  pl.BlockSpec((1, N_MOD, D), lambda b, i: (jnp.where(i < nq, b, B), 0, 0)),
                  pl.BlockSpec((qk, D), lambda b, i: (0, 0))],
        out_specs=row,
        out_shape=jax.ShapeDtypeStruct(h.shape, F32),
        input_output_aliases={7: 0},
        compiler_params=_params("arbitrary", "arbitrary"),
        name="diff_attn",
    )(lam_params.astype(F32), subln_w.reshape(1, -1), q, k, v, k, v, h, mod, w_out.astype(BF16))


def _hgrn_proj_kernel(h_ref, mod_ref, gain_ref, lb_ref, w_ref, o_ref, *, layer):
    KD = HGRN_HEADS * HGRN_KEY_DIM
    W = 2 * LANES
    a = _pre(h_ref[...], gain_ref[...], mod_ref[0], 3).astype(BF16)
    raw = lb_ref[...]
    e = jnp.exp(raw - jnp.max(raw, axis=0, keepdims=True))
    soft = e * (1.0 / jnp.sum(e, axis=0, keepdims=True))
    lb = jnp.sum(soft[1:layer + 1], axis=0, keepdims=True) if layer else jnp.zeros_like(soft[0:1])
    n = w_ref.shape[1] // W

    def proj(c):
        return _dot(a, w_ref[:, c * W:(c + 1) * W])

    cur = proj(0)
    for c in range(n):
        nxt = proj(c + 1) if c + 1 < n else None
        group, col = divmod(c * W, KD)
        if group in (0, 4):
            out = _silu(cur)
        elif group in (1, 2):
            lbc = lb[:, col:col + W]
            out = lbc + (1.0 - lbc) * jax.nn.sigmoid(cur)
        else:
            out = cur
        o_ref[:, c * W:(c + 1) * W] = out
        cur = nxt


def _split3(x):
    hi = x.astype(BF16)
    r = x - hi.astype(F32)
    mid = r.astype(BF16)
    lo = (r - mid.astype(F32)).astype(BF16)
    return hi, mid, lo


def _hgrn_scan_kernel(q_ref, f_ref, v_ref, o_ref, st_ref, *, n_chunks):
    d = pl.program_id(1)
    step = pl.program_id(2)
    C = HGRN_CHUNK
    DK, DV, H = HGRN_KEY_DIM, HGRN_VAL_DIM, HGRN_HEADS

    @pl.when(step == 0)
    def _():
        st_ref[...] = jnp.zeros_like(st_ref)

    fwd = d == 0
    r = lax.broadcasted_iota(jnp.int32, (C, C), 0)
    c = lax.broadcasted_iota(jnp.int32, (C, C), 1)
    sign = jnp.where(fwd, 1, -1)
    tri = (r - c) * sign >= 0
    tri_b = jnp.where(tri, 1.0, 0.0).astype(BF16)
    rows = lax.broadcasted_iota(jnp.int32, (C, 1), 0)
    mid_rows = (rows - jnp.where(fwd, C // 2, C - 1 - C // 2)) * sign <= 0

    ks = [slice(h * DK, (h + 1) * DK) for h in range(H)]
    vs = [slice(h * DV, (h + 1) * DV) for h in range(H)]

    def prepare(n):
        base = pl.multiple_of(jnp.where(fwd, n, n_chunks - 1 - n) * C, C)
        q = q_ref[pl.ds(base, C), :]
        f = f_ref[pl.ds(base, C), :]
        v = v_ref[pl.ds(base, C), :].astype(BF16)
        logf = jnp.log(f)
        k = 1.0 - f
        g = functools.reduce(lambda a, b: a + b, [_dot(tri_b, part) for part in _split3(logf)])
        g_last = jnp.sum(logf, axis=0, keepdims=True)
        g_mid = jnp.sum(jnp.where(mid_rows, logf, 0.0), axis=0, keepdims=True)
        qa = (q * jnp.exp(g - g_mid)).astype(BF16)
        kb = (k * jnp.exp(g_mid - g)).astype(BF16)
        qg = (q * jnp.exp(g)).astype(BF16)
        kd = (k * jnp.exp(g_last - g)).astype(BF16)
        a = [jnp.where(tri, _dot_nt(qa[:, ks[h]], kb[:, ks[h]]), 0.0).astype(BF16) for h in range(H)]
        intra = [_dot(a[h], v[:, vs[h]]) for h in range(H)]
        update = [_dot_tn(v[:, vs[h]], kd[:, ks[h]]) for h in range(H)]
        return base, qg, jnp.exp(g_last), intra, update

    cur = prepare(0)
    for n in range(n_chunks):
        nxt = prepare(n + 1) if n + 1 < n_chunks else None
        base, qg, decay, intra, update = cur
        st = [st_ref[vs[h], :] for h in range(H)]
        outs = [intra[h] + _dot_nt(qg[:, ks[h]], st[h].astype(BF16)) for h in range(H)]
        for h in range(H):
            st_ref[vs[h], :] = st[h] * decay[:, ks[h]] + update[h]
        o_ref[0, pl.ds(base, C), :] = jnp.concatenate(outs, axis=-1)
        cur = nxt


def _hgrn_out_kernel(o_ref, gate_ref, nw_ref, h_ref, mod_ref, wo_ref, out_ref):
    DV = HGRN_VAL_DIM
    o = o_ref[0] + o_ref[1]
    gate = gate_ref[...]
    nw = nw_ref[...]
    ys = [_rmsnorm(o[:, h * DV:(h + 1) * DV], nw) * gate[:, h * DV:(h + 1) * DV] for h in range(HGRN_HEADS)]
    y = jnp.concatenate(ys, axis=-1).astype(BF16)
    m = mod_ref[0]
    out_ref[...] = h_ref[...] + m[5:6] * _dot(y, wo_ref[...])


def _hgrn_mixer(lay, h, mod, gain, w_in, w_out, norm_w, lower_bounds, layer):
    D = h.shape[1]
    B, S, C = lay.B, lay.S, lay.C
    KD = HGRN_HEADS * HGRN_KEY_DIM
    tm = lay.tile(512)
    midx = lay.mod_index(tm)
    n_cols = w_in.shape[1] // KD
    proj = pl.pallas_call(
        functools.partial(_hgrn_proj_kernel, layer=layer),
        grid=(lay.n_all // tm,),
        in_specs=[pl.BlockSpec((tm, D), lambda t: (t, 0)),
                  pl.BlockSpec((1, N_MOD, D), lambda t: (midx(t), 0, 0)),
                  pl.BlockSpec((1, D), lambda t: (0, 0)),
                  pl.BlockSpec(lower_bounds.shape, lambda t: (0, 0)),
                  pl.BlockSpec((D, n_cols * KD), lambda t: (0, 0), pipeline_mode=pl.Buffered(1))],
        out_specs=pl.BlockSpec((tm, n_cols * KD), lambda t: (t, 0)),
        out_shape=jax.ShapeDtypeStruct((lay.n_all, n_cols * KD), F32),
        compiler_params=_params("arbitrary"),
        name="hgrn_proj",
    )(h, mod, gain.reshape(1, D), lower_bounds.astype(F32), w_in.astype(BF16))

    ts = C
    n_lat_steps = S // ts
    lat_blocks = lay.n_lat // ts

    def rows(b, d, s):
        lat = b * n_lat_steps + jnp.where(d == 0, s - 1, n_lat_steps - s)
        return jnp.where(s == 0, lat_blocks + b, lat)

    o = pl.pallas_call(
        functools.partial(_hgrn_scan_kernel, n_chunks=ts // HGRN_CHUNK),
        grid=(B, 2, 1 + n_lat_steps),
        in_specs=[pl.BlockSpec((ts, KD), lambda b, d, s: (rows(b, d, s), 0)),
                  pl.BlockSpec((ts, KD), lambda b, d, s: (rows(b, d, s), 1 + d)),
                  pl.BlockSpec((ts, KD), lambda b, d, s: (rows(b, d, s), 3))],
        out_specs=pl.BlockSpec((1, ts, KD), lambda b, d, s: (d, rows(b, d, s), 0)),
        out_shape=jax.ShapeDtypeStruct((2, lay.n_all, KD), F32),
        scratch_shapes=[pltpu.VMEM((HGRN_HEADS * HGRN_VAL_DIM, HGRN_KEY_DIM), F32)],
        compiler_params=_params("arbitrary", "arbitrary", "arbitrary"),
        name="hgrn_scan",
    )(proj, proj, proj)

    row = pl.BlockSpec((tm, D), lambda t: (t, 0))
    return pl.pallas_call(
        _hgrn_out_kernel,
        grid=(lay.n_all // tm,),
        in_specs=[pl.BlockSpec((2, tm, KD), lambda t: (0, t, 0)),
                  pl.BlockSpec((tm, KD), lambda t: (t, 4)),
                  pl.BlockSpec((1, HGRN_VAL_DIM), lambda t: (0, 0)),
                  row,
                  pl.BlockSpec((1, N_MOD, D), lambda t: (midx(t), 0, 0)),
                  pl.BlockSpec((KD, D), lambda t: (0, 0))],
        out_specs=row,
        out_shape=jax.ShapeDtypeStruct(h.shape, F32),
        input_output_aliases={3: 0},
        compiler_params=_params("arbitrary"),
        name="hgrn_out",
    )(o, proj, norm_w.reshape(1, -1), h, mod, w_out.astype(BF16))


MLA_QK_PAD = LANES


def _mla_proj_kernel(h_ref, mod_ref, gain_ref, wd_ref, qn_ref, kvn_ref, wuq_ref, wukv_ref,
                     cos_ref, sa_ref, sb_ref, q_ref, k_ref, v_ref):
    H, P = MLA_HEADS, MLA_QK_PAD
    a = _pre(h_ref[...], gain_ref[...], mod_ref[0], 3).astype(BF16)
    dn = _dot(a, wd_ref[...])
    cos, sa, sb = cos_ref[...], sa_ref[...], sb_ref[...]
    f = MLA_ROPE // 4
    cq = _rmsnorm(dn[:, :MLA_Q_LORA], qn_ref[...]).astype(BF16)
    q = _dot(cq, wuq_ref[...])
    for hd in range(H):
        q_ref[:, hd * P:(hd + 1) * P] = _rope(q[:, hd * P:(hd + 1) * P], cos, sa, sb, f).astype(BF16)
    ckv = _rmsnorm(dn[:, MLA_Q_LORA:MLA_Q_LORA + MLA_KV_LORA], kvn_ref[...]).astype(BF16)
    kv = _dot(ckv, wukv_ref[...])
    kr = _rope(dn[:, MLA_Q_LORA + MLA_KV_LORA:], cos, sa, sb, f)
    for hd in range(H):
        k_ref[:, hd * P:(hd + 1) * P] = (kv[:, hd * P:(hd + 1) * P] + kr).astype(BF16)
    v_ref[...] = kv[:, H * P:].astype(BF16)


def _mla_attn_kernel(q_ref, kl_ref, vl_ref, kx_ref, vx_ref, h_ref, mod_ref, wo_ref, o_ref, *, nq, with_ctx):
    i = pl.program_id(1)
    P, VD = MLA_QK_PAD, MLA_V_DIM
    c = ((MLA_NOPE + MLA_ROPE) ** -0.5) * LOG2_E

    def run(kv_pairs):
        def scores(hd):
            qh = q_ref[:, hd * P:(hd + 1) * P]
            return [_dot_nt(qh, k_ref[:, hd * P:(hd + 1) * P]) for k_ref, _ in kv_pairs]

        heads = []
        ss = scores(0)
        for hd in range(MLA_HEADS):
            nxt = scores(hd + 1) if hd + 1 < MLA_HEADS else None
            m = functools.reduce(jnp.maximum, [jnp.max(s, axis=-1, keepdims=True) for s in ss])
            mc = m * c
            o, l = None, None
            for s, (_, v_ref) in zip(ss, kv_pairs):
                e = jnp.exp2(s * c - mc)
                ls = jnp.sum(e, axis=-1, keepdims=True)
                l = ls if l is None else l + ls
                t = _dot(e.astype(BF16), v_ref[:, hd * VD:(hd + 1) * VD])
                o = t if o is None else o + t
            heads.append(o * (1.0 / l))
            ss = nxt
        o = jnp.concatenate(heads, axis=-1).astype(BF16)
        m = mod_ref[0]
        o_ref[...] = h_ref[...] + m[5:6] * _dot(o, wo_ref[...])

    if not with_ctx:
        run([(kl_ref, vl_ref), (kx_ref, vx_ref)])
        return

    @pl.when(i < nq)
    def _():
        run([(kl_ref, vl_ref), (kx_ref, vx_ref)])

    @pl.when(i >= nq)
    def _():
        run([(kx_ref, vx_ref)])


def _mla_mixer(lay, h, mod, gain, w_down, q_norm_w, kv_norm_w, w_uq, w_ukv, w_out, with_ctx_out):
    D = h.shape[1]
    B, S, C = lay.B, lay.S, lay.C
    H, P, VD = MLA_HEADS, MLA_QK_PAD, MLA_V_DIM
    qk = MLA_NOPE + MLA_ROPE
    lora = MLA_Q_LORA + MLA_KV_LORA
    kr_cols = jnp.zeros((D, P), F32).at[:, MLA_NOPE:qk].set(w_down[:, lora:])
    wd = jnp.concatenate([w_down[:, :lora], kr_cols], axis=1).astype(BF16)
    wuq = jnp.pad(w_uq.reshape(MLA_Q_LORA, H, qk), ((0, 0), (0, 0), (0, P - qk))).reshape(MLA_Q_LORA, H * P)
    ukv = w_ukv.reshape(MLA_KV_LORA, H, MLA_NOPE + VD)
    wkn = jnp.pad(ukv[..., :MLA_NOPE], ((0, 0), (0, 0), (0, P - MLA_NOPE))).reshape(MLA_KV_LORA, H * P)
    wukv = jnp.concatenate([wkn, ukv[..., MLA_NOPE:].reshape(MLA_KV_LORA, H * VD)], axis=1)

    tm = lay.tile(512)
    midx, pidx = lay.mod_index(tm), lay.pos_index(tm)
    cos, sa, sb = _rope_tables(S, MLA_ROPE, tm, MLA_NOPE)
    tab = pl.BlockSpec((tm, LANES), lambda t: (pidx(t), 0))
    full = lambda shape: pl.BlockSpec(shape, lambda t: (0, 0))
    q, k, v = pl.pallas_call(
        _mla_proj_kernel,
        grid=(lay.n_all // tm,),
        in_specs=[pl.BlockSpec((tm, D), lambda t: (t, 0)),
                  pl.BlockSpec((1, N_MOD, D), lambda t: (midx(t), 0, 0)),
                  full((1, D)), full((D, lora + P)), full((1, MLA_Q_LORA)), full((1, MLA_KV_LORA)),
                  full((MLA_Q_LORA, H * P)), full((MLA_KV_LORA, H * (P + VD))),
                  tab, tab, tab],
        out_specs=[pl.BlockSpec((tm, H * P), lambda t: (t, 0)),
                   pl.BlockSpec((tm, H * P), lambda t: (t, 0)),
                   pl.BlockSpec((tm, H * VD), lambda t: (t, 0))],
        out_shape=[jax.ShapeDtypeStruct((lay.n_all, H * P), BF16),
                   jax.ShapeDtypeStruct((lay.n_all, H * P), BF16),
                   jax.ShapeDtypeStruct((lay.n_all, H * VD), BF16)],
        compiler_params=_params("arbitrary"),
        name="mla_proj",
    )(h, mod, gain.reshape(1, D), wd, q_norm_w.reshape(1, -1), kv_norm_w.reshape(1, -1),
      wuq.astype(BF16), wukv.astype(BF16), cos, sa, sb)

    tq = _divisor(math.gcd(S, C), 256)
    q_index, nq, nc, lat, ctx = _attn_specs(lay, tq, [H * P, H * VD])
    row = pl.BlockSpec((tq, D), lambda b, i: (q_index(b, i), 0))
    return pl.pallas_call(
        functools.partial(_mla_attn_kernel, nq=nq, with_ctx=with_ctx_out),
        grid=(B, nq + (nc if with_ctx_out else 0)),
        in_specs=[pl.BlockSpec((tq, H * P), lambda b, i: (q_index(b, i), 0)),
                  *lat, *ctx, row,
                  pl.BlockSpec((1, N_MOD, D), lambda b, i: (jnp.where(i < nq, b, B), 0, 0)),
                  pl.BlockSpec((H * VD, D), lambda b, i: (0, 0))],
        out_specs=row,
        out_shape=jax.ShapeDtypeStruct(h.shape, F32),
        input_output_aliases={5: 0},
        compiler_params=_params("arbitrary", "arbitrary"),
        name="mla_attn",
    )(q, k, v, k, v, h, mod, w_out.astype(BF16))


def kernel(x, c, ctx, c_ctx, ada_w, ada_b, norm_w, final_norm_w, ffn_w_gate, ffn_w_up, ffn_w_down,
           gqa_w_in, gqa_w_out, gqa_sinks, diff_w_in, diff_w_out, diff_lambda, diff_subln_w,
           hgrn_w_in, hgrn_w_out, hgrn_norm_w, hgrn_lower_bounds,
           mla_w_down, mla_q_norm_w, mla_kv_norm_w, mla_w_uq, mla_w_ukv, mla_w_out):
    B, S, D = x.shape
    C = ctx.shape[1]
    lay = _Layout(B, S, C)
    depth = ada_w.shape[0]

    rows = -(-(B + 1) // 8) * 8
    cc = jnp.zeros((rows, D), F32).at[:B].set(c).at[B].set(c_ctx)
    mods = _ada_table(cc, ada_w, ada_b).reshape(depth, rows, N_MOD, D)

    wg, wu, wd = ffn_w_gate.astype(BF16), ffn_w_up.astype(BF16), ffn_w_down.astype(BF16)
    h = jnp.concatenate([x.reshape(B * S, D), ctx.reshape(B * C, D)], axis=0)
    for i in range(depth):
        kind, j = i % 4, i // 4
        last = i == depth - 1
        mod = mods[i]
        h = _ffn(lay, h, mod, norm_w[i, 0], final_norm_w, wg[i, 0], wu[i, 0], wd[i, 0], 0)
        if kind == 0:
            h = _gqa_mixer(lay, h, mod, norm_w[i, 1], gqa_w_in[j], gqa_w_out[j], gqa_sinks[j], not last)
        elif kind == 1:
            lambda_init = 0.8 - 0.6 * math.exp(-0.3 * i)
            h = _diff_mixer(lay, h, mod, norm_w[i, 1], diff_w_in[j], diff_w_out[j], diff_lambda[j],
                            diff_subln_w[j], lambda_init, not last)
        elif kind == 2:
            h = _hgrn_mixer(lay, h, mod, norm_w[i, 1], hgrn_w_in[j], hgrn_w_out[j], hgrn_norm_w[j],
                            hgrn_lower_bounds, i)
        else:
            h = _mla_mixer(lay, h, mod, norm_w[i, 1], mla_w_down[j], mla_q_norm_w[j], mla_kv_norm_w[j],
                           mla_w_uq[j], mla_w_ukv[j], mla_w_out[j], not last)
        h = _ffn(lay, h, mod, norm_w[i, 2], final_norm_w, wg[i, 1], wu[i, 1], wd[i, 1], 6,
                 lat_only=last, final_norm=last)
    return h[:B * S].reshape(B, S, D)
```

```python
import functools
import math

import jax
import jax.numpy as jnp
from jax import lax
from jax.experimental import pallas as pl
from jax.experimental.pallas import tpu as pltpu

F32 = jnp.float32
BF16 = jnp.bfloat16

D_MODEL = 1024
DEPTH = 4
GRID_W = 64
N_MOD = 9
ROPE_BASE = 10000.0
EPS = 1e-6
NEG_INF = -1e30
D_FF = 2816

GQA_HEADS = 16
GQA_KV_HEADS = 4
GQA_HEAD_DIM = 64
GQA_WINDOW = 128
GQA_BLOCK = 128

DIFF_HEADS = 8
DIFF_HEAD_DIM = 64

HGRN_HEADS = 8
HGRN_KEY_DIM = 128
HGRN_VAL_DIM = D_MODEL // HGRN_HEADS
HGRN_CHUNK = 64

MLA_HEADS = 16
MLA_Q_LORA = 256
MLA_KV_LORA = 256
MLA_NOPE = 64
MLA_ROPE = 32
MLA_V_DIM = 64

LOG2_E = 1.4426950408889634
LANES = 128
VMEM_LIMIT = 56 * 1024 * 1024


def _dot(a, b):
    return jnp.dot(a, b, preferred_element_type=F32)


def _dot_nt(a, b):
    return lax.dot_general(a, b, (((1,), (1,)), ((), ())), preferred_element_type=F32)


def _dot_tn(a, b):
    return lax.dot_general(a, b, (((0,), (0,)), ((), ())), preferred_element_type=F32)


def _params(*sem):
    return pltpu.CompilerParams(dimension_semantics=sem, vmem_limit_bytes=VMEM_LIMIT)


def _divisor(n, pref):
    t = min(n, pref)
    while n % t:
        t -= 8
    return t


class _Layout:
    def __init__(self, B, S, C):
        self.B, self.S, self.C = B, S, C
        self.n_lat = B * S
        self.n_all = B * S + B * C

    def tile(self, pref):
        return _divisor(math.gcd(self.S, self.B * self.C), pref)

    def n_tiles(self, tm, lat_only=False):
        return (self.n_lat if lat_only else self.n_all) // tm

    def mod_index(self, tm):
        n_lat_tiles, per_batch, B = self.n_lat // tm, self.S // tm, self.B
        return lambda t: jnp.where(t < n_lat_tiles, t // per_batch, B)

    def pos_index(self, tm):
        n_lat_tiles, per_batch = self.n_lat // tm, self.S // tm
        return lambda t: jnp.where(t < n_lat_tiles, t % per_batch, per_batch)


def _rmsnorm(x, w):
    return (x * lax.rsqrt(jnp.mean(x * x, axis=-1, keepdims=True) + EPS)) * w


def _pre(h, gain, m, off):
    return _rmsnorm(h, gain) * (1.0 + m[off + 1:off + 2]) + m[off:off + 1]


def _silu(x):
    return x * jax.nn.sigmoid(x)


def _rope(x, cos, sa, sb, shift):
    return x * cos + pltpu.roll(x, LANES - shift, 1) * sa + pltpu.roll(x, shift, 1) * sb


def _ada_kernel(c_ref, w_ref, b_ref, o_ref):
    sc = _silu(c_ref[...]).astype(BF16)
    o_ref[0] = _dot(sc, w_ref[0].astype(BF16)) + b_ref[0]


def _ada_table(cc, ada_w, ada_b):
    rows, D = cc.shape
    n_out = ada_w.shape[2]
    tn = n_out // 8
    return pl.pallas_call(
        _ada_kernel,
        grid=(DEPTH, n_out // tn),
        in_specs=[pl.BlockSpec((rows, D), lambda i, j: (0, 0)),
                  pl.BlockSpec((1, D, tn), lambda i, j: (i, 0, j)),
                  pl.BlockSpec((1, 1, tn), lambda i, j: (i, 0, j))],
        out_specs=pl.BlockSpec((1, rows, tn), lambda i, j: (i, 0, j)),
        out_shape=jax.ShapeDtypeStruct((DEPTH, rows, n_out), F32),
        compiler_params=_params("arbitrary", "arbitrary"),
        name="ada_table",
    )(cc, ada_w, ada_b.reshape(DEPTH, 1, n_out))


FFN_CHUNK = 2 * LANES


def _ffn_kernel(h_ref, mod_ref, gain_ref, fin_ref, wg_ref, wu_ref, wd_ref, o_ref, a_ref, *, off, final_norm):
    m = mod_ref[0]
    xm = _pre(h_ref[...], gain_ref[...], m, off).astype(BF16)
    n = D_FF // FFN_CHUNK

    def gate_up(c):
        cols = slice(c * FFN_CHUNK, (c + 1) * FFN_CHUNK)
        return _dot(xm, wg_ref[:, cols]), _dot(xm, wu_ref[:, cols])

    cur = gate_up(0)
    for c in range(n):
        nxt = gate_up(c + 1) if c + 1 < n else None
        g, u = cur
        a_ref[:, c * FFN_CHUNK:(c + 1) * FFN_CHUNK] = (_silu(g) * u).astype(BF16)
        cur = nxt
    out = h_ref[...] + (0.5 * m[off + 2:off + 3]) * _dot(a_ref[...], wd_ref[...])
    if final_norm:
        out = _rmsnorm(out, fin_ref[...])
    o_ref[...] = out


def _ffn(lay, h, mod, gain, fin_w, wg, wu, wd, off, lat_only=False, final_norm=False):
    D = h.shape[1]
    tm = lay.tile(512)
    n_rows = lay.n_lat if lat_only else lay.n_all
    midx = lay.mod_index(tm)
    const = lambda t: (0, 0)
    one = pl.Buffered(1)
    return pl.pallas_call(
        functools.partial(_ffn_kernel, off=off, final_norm=final_norm),
        grid=(n_rows // tm,),
        in_specs=[pl.BlockSpec((tm, D), lambda t: (t, 0)),
                  pl.BlockSpec((1, N_MOD, D), lambda t: (midx(t), 0, 0)),
                  pl.BlockSpec((1, D), const),
                  pl.BlockSpec((1, D), const),
                  pl.BlockSpec((D, D_FF), const, pipeline_mode=one),
                  pl.BlockSpec((D, D_FF), const, pipeline_mode=one),
                  pl.BlockSpec((D_FF, D), const, pipeline_mode=one)],
        out_specs=pl.BlockSpec((tm, D), lambda t: (t, 0)),
        out_shape=jax.ShapeDtypeStruct((n_rows, D), F32),
        scratch_shapes=[pltpu.VMEM((tm, D_FF), BF16)],
        compiler_params=_params("arbitrary"),
        name="ffn",
    )(h, mod, gain.reshape(1, D), fin_w.reshape(1, D), wg, wu, wd)


def _rope_tables(S, rot_dim, pad_rows, lane_off):
    rows = S // GRID_W
    row = jnp.repeat(jnp.arange(rows, dtype=F32), GRID_W)
    col = jnp.tile(jnp.arange(GRID_W, dtype=F32), rows)
    axis_dim = rot_dim // 2
    inv_freq = ROPE_BASE ** (-jnp.arange(0, axis_dim, 2, dtype=F32) / axis_dim)
    ang_r = row[:, None] * inv_freq[None, :]
    ang_c = col[:, None] * inv_freq[None, :]
    ang = jnp.concatenate([ang_r, ang_r, ang_c, ang_c], axis=-1)
    cos, sin = jnp.cos(ang), jnp.sin(ang)
    f = rot_dim // 4
    first = (jnp.arange(rot_dim) % (2 * f)) < f
    sa = jnp.where(first[None, :], -sin, 0.0)
    sb = jnp.where(first[None, :], 0.0, sin)
    period = 64 if lane_off + rot_dim <= 64 else LANES

    def widen(t, fill):
        blk = jnp.full((S, period), fill, F32).at[:, lane_off:lane_off + rot_dim].set(t)
        blk = jnp.tile(blk, (1, LANES // period))
        return jnp.concatenate([blk, jnp.full((pad_rows, LANES), fill, F32)], axis=0)

    return widen(cos, 1.0), widen(sa, 0.0), widen(sb, 0.0)


def _attn_specs(lay, tq, widths):
    S, C, B = lay.S, lay.C, lay.B
    nq, nc = S // tq, C // tq
    lat_blocks = lay.n_lat // C

    def q_index(b, i):
        return jnp.where(i < nq, b * nq + i, B * nq + b * nc + (i - nq))

    one = pl.Buffered(1)
    lat = [pl.BlockSpec((S, w), lambda b, i: (b, 0), pipeline_mode=one) for w in widths]
    ctx = [pl.BlockSpec((C, w), lambda b, i: (lat_blocks + b, 0), pipeline_mode=one) for w in widths]
    return q_index, nq, nc, lat, ctx


def _gqa_proj_kernel(h_ref, mod_ref, gain_ref, w_ref, cos_ref, sa_ref, sb_ref, q_ref, k_ref, v_ref):
    qd, kd = GQA_HEADS * GQA_HEAD_DIM, GQA_KV_HEADS * GQA_HEAD_DIM
    a = _pre(h_ref[...], gain_ref[...], mod_ref[0], 3).astype(BF16)
    p = _dot(a, w_ref[...])
    cos, sa, sb = cos_ref[...], sa_ref[...], sb_ref[...]
    f = GQA_HEAD_DIM // 4
    scale = GQA_HEAD_DIM ** -0.5
    for c in range(qd // LANES):
        x = _rope(p[:, c * LANES:(c + 1) * LANES], cos, sa, sb, f)
        q_ref[:, c * LANES:(c + 1) * LANES] = (x * scale).astype(BF16)
    for c in range(kd // LANES):
        x = _rope(p[:, qd + c * LANES:qd + (c + 1) * LANES], cos, sa, sb, f)
        k_ref[:, c * LANES:(c + 1) * LANES] = x.astype(BF16)
    v_ref[...] = p[:, qd + kd:].astype(BF16)


GQA_TILE = 2 * GQA_BLOCK


def _gqa_attn_kernel(sink_ref, q_ref, k0_ref, k1_ref, k2_ref, k3_ref, v0_ref, v1_ref, v2_ref, v3_ref,
                     kx_ref, vx_ref, h_ref, mod_ref, wo_ref, o_ref, kcat_ref, vcat_ref, *, nq, S, C):
    i = pl.program_id(1)
    kb = GQA_BLOCK
    blk = GQA_TILE
    G = GQA_HEADS // GQA_KV_HEADS
    HD = GQA_HEAD_DIM
    span = 4 * kb

    def finish(heads):
        o = jnp.concatenate(heads, axis=-1).astype(BF16)
        m = mod_ref[0]
        o_ref[...] = h_ref[...] + m[5:6] * _dot(o, wo_ref[...])

    def run(k_ref, v_ref, mask):
        kvs = range(GQA_KV_HEADS)
        qs = [jnp.concatenate([q_ref[:, (G * g + n) * HD:(G * g + n + 1) * HD] for n in range(G)], axis=0)
              for g in kvs]
        sinks = [jnp.concatenate([jnp.full((blk, 1), sink_ref[G * g + n], F32) for n in range(G)], axis=0)
                 for g in kvs]
        ss = [_dot_nt(qs[g], k_ref[:, g * HD:(g + 1) * HD]) for g in kvs]
        if mask is not None:
            ss = [jnp.where(mask, s, NEG_INF) for s in ss]
        ms = [jnp.maximum(jnp.max(ss[g], axis=-1, keepdims=True), sinks[g]) for g in kvs]
        es = [jnp.exp(ss[g] - ms[g]) for g in kvs]
        invs = [1.0 / (jnp.sum(es[g], axis=-1, keepdims=True) + jnp.exp(sinks[g] - ms[g])) for g in kvs]
        outs = [_dot(es[g].astype(BF16), v_ref[:, g * HD:(g + 1) * HD]) * invs[g] for g in kvs]
        finish([outs[g][n * blk:(n + 1) * blk] for g in kvs for n in range(G)])

    @pl.when(i < nq)
    def _():
        for n, (k_ref, v_ref) in enumerate([(k0_ref, v0_ref), (k1_ref, v1_ref), (k2_ref, v2_ref), (k3_ref, v3_ref)]):
            kcat_ref[n * kb:(n + 1) * kb] = k_ref[...]
            vcat_ref[n * kb:(n + 1) * kb] = v_ref[...]
        kcat_ref[span:] = kx_ref[...]
        vcat_ref[span:] = vx_ref[...]
        row = lax.broadcasted_iota(jnp.int32, (G * blk, span + C), 0) & (blk - 1)
        col = lax.broadcasted_iota(jnp.int32, (G * blk, span + C), 1)
        k_abs = (2 * i - 1) * kb + col
        local = (jnp.abs(row + kb - col) <= GQA_WINDOW) & (k_abs >= 0) & (k_abs < S)
        run(kcat_ref, vcat_ref, local | (col >= span))

    @pl.when(i >= nq)
    def _():
        run(kx_ref, vx_ref, None)


def _gqa_mixer(lay, h, mod, gain, w_in, w_out, sinks, with_ctx_out):
    D = h.shape[1]
    B, S, C = lay.B, lay.S, lay.C
    qd, kd = GQA_HEADS * GQA_HEAD_DIM, GQA_KV_HEADS * GQA_HEAD_DIM
    tm = lay.tile(512)
    midx, pidx = lay.mod_index(tm), lay.pos_index(tm)
    cos, sa, sb = _rope_tables(S, GQA_HEAD_DIM, tm, 0)
    tab = pl.BlockSpec((tm, LANES), lambda t: (pidx(t), 0))
    q, k, v = pl.pallas_call(
        _gqa_proj_kernel,
        grid=(lay.n_all // tm,),
        in_specs=[pl.BlockSpec((tm, D), lambda t: (t, 0)),
                  pl.BlockSpec((1, N_MOD, D), lambda t: (midx(t), 0, 0)),
                  pl.BlockSpec((1, D), lambda t: (0, 0)),
                  pl.BlockSpec((D, qd + 2 * kd), lambda t: (0, 0)),
                  tab, tab, tab],
        out_specs=[pl.BlockSpec((tm, qd), lambda t: (t, 0)),
                   pl.BlockSpec((tm, kd), lambda t: (t, 0)),
                   pl.BlockSpec((tm, kd), lambda t: (t, 0))],
        out_shape=[jax.ShapeDtypeStruct((lay.n_all, qd), BF16),
                   jax.ShapeDtypeStruct((lay.n_all, kd), BF16),
                   jax.ShapeDtypeStruct((lay.n_all, kd), BF16)],
        compiler_params=_params("arbitrary"),
        name="gqa_proj",
    )(h, mod, gain.reshape(1, D), w_in.astype(BF16), cos, sa, sb)

    blk, kb = GQA_TILE, GQA_BLOCK
    assert S % blk == 0 and C % blk == 0
    nq, nc = S // blk, C // blk
    n_kb = S // kb
    n_i = nq + (nc if with_ctx_out else 0)
    lat_blocks = lay.n_lat // C

    def q_index(b, i):
        return jnp.where(i < nq, b * nq + i, B * nq + b * nc + (i - nq))

    def win(delta):
        return lambda b, i: (b * n_kb + jnp.clip(2 * i + delta, 0, n_kb - 1), 0)

    kv_win = [pl.BlockSpec((kb, kd), win(d)) for d in (-1, 0, 1, 2)]
    kv_ctx = pl.BlockSpec((C, kd), lambda b, i: (lat_blocks + b, 0))
    row = pl.BlockSpec((blk, D), lambda b, i: (q_index(b, i), 0))
    return pl.pallas_call(
        functools.partial(_gqa_attn_kernel, nq=nq, S=S, C=C),
        grid=(B, n_i),
        in_specs=[pl.BlockSpec(memory_space=pltpu.SMEM),
                  pl.BlockSpec((blk, qd), lambda b, i: (q_index(b, i), 0)),
                  *kv_win, *kv_win, kv_ctx, kv_ctx, row,
                  pl.BlockSpec((1, N_MOD, D), lambda b, i: (jnp.where(i < nq, b, B), 0, 0)),
                  pl.BlockSpec((qd, D), lambda b, i: (0, 0))],
        out_specs=row,
        out_shape=jax.ShapeDtypeStruct(h.shape, F32),
        input_output_aliases={12: 0},
        scratch_shapes=[pltpu.VMEM((4 * kb + C, kd), BF16), pltpu.VMEM((4 * kb + C, kd), BF16)],
        compiler_params=_params("arbitrary", "arbitrary"),
        name="gqa_attn",
    )(sinks.astype(F32), q, k, k, k, k, v, v, v, v, k, v, h, mod, w_out.astype(BF16))


def _diff_proj_kernel(h_ref, mod_ref, gain_ref, w_ref, cos_ref, sa_ref, sb_ref, q_ref, k_ref, v_ref):
    qk = DIFF_HEADS * 2 * DIFF_HEAD_DIM
    a = _pre(h_ref[...], gain_ref[...], mod_ref[0], 3).astype(BF16)
    p = _dot(a, w_ref[...])
    cos, sa, sb = cos_ref[...], sa_ref[...], sb_ref[...]
    f = DIFF_HEAD_DIM // 4
    scale = DIFF_HEAD_DIM ** -0.5
    for c in range(qk // LANES):
        x = _rope(p[:, c * LANES:(c + 1) * LANES], cos, sa, sb, f)
        q_ref[:, c * LANES:(c + 1) * LANES] = (x * scale).astype(BF16)
        x = _rope(p[:, qk + c * LANES:qk + (c + 1) * LANES], cos, sa, sb, f)
        k_ref[:, c * LANES:(c + 1) * LANES] = x.astype(BF16)
    vw = 2 * DIFF_HEAD_DIM
    ones = jnp.ones((p.shape[0], vw), BF16)
    for hd in range(DIFF_HEADS):
        v_ref[:, 2 * hd * vw:(2 * hd + 1) * vw] = p[:, 2 * qk + hd * vw:2 * qk + (hd + 1) * vw].astype(BF16)
        v_ref[:, (2 * hd + 1) * vw:(2 * hd + 2) * vw] = ones


def _diff_attn_kernel(lam_ref, sub_ref, q_ref, kl_ref, vl_ref, kx_ref, vx_ref, h_ref, mod_ref, wo_ref, o_ref,
                      *, nq, lambda_init):
    i = pl.program_id(1)
    HD = DIFF_HEAD_DIM
    VW = 4 * HD
    lp = lam_ref[...]
    lam = (jnp.exp(jnp.sum(lp[0:1] * lp[1:2], keepdims=True))
           - jnp.exp(jnp.sum(lp[2:3] * lp[3:4], keepdims=True)) + lambda_init)

    def run(kv_pairs):
        def scores(h):
            los = [(2 * h + j) * HD for j in range(2)]
            return [[_dot_nt(q_ref[:, lo:lo + HD], k_ref[:, lo:lo + HD]) for k_ref, _ in kv_pairs] for lo in los]

        heads = []
        ss = scores(0)
        for h in range(DIFF_HEADS):
            nxt = scores(h + 1) if h + 1 < DIFF_HEADS else None
            maps = []
            for parts in ss:
                m = functools.reduce(jnp.maximum, [jnp.max(s, axis=-1, keepdims=True) for s in parts])
                acc = None
                for s, (_, v_ref) in zip(parts, kv_pairs):
                    t = _dot(jnp.exp(s - m).astype(BF16), v_ref[:, h * VW:(h + 1) * VW])
                    acc = t if acc is None else acc + t
                maps.append(acc[:, :VW // 2] * (1.0 / acc[:, VW // 2:]))
            o = maps[0] - lam * maps[1]
            heads.append(_rmsnorm(o, sub_ref[...]) * (1.0 - lambda_init))
            ss = nxt
        o = jnp.concatenate(heads, axis=-1).astype(BF16)
        m = mod_ref[0]
        o_ref[...] = h_ref[...] + m[5:6] * _dot(o, wo_ref[...])

    @pl.when(i < nq)
    def _():
        run([(kl_ref, vl_ref), (kx_ref, vx_ref)])

    @pl.when(i >= nq)
    def _():
        run([(kx_ref, vx_ref)])


def _diff_mixer(lay, h, mod, gain, w_in, w_out, lam_params, subln_w, lambda_init, with_ctx_out):
    D = h.shape[1]
    B, S, C = lay.B, lay.S, lay.C
    qk = DIFF_HEADS * 2 * DIFF_HEAD_DIM
    tm = lay.tile(512)
    midx, pidx = lay.mod_index(tm), lay.pos_index(tm)
    cos, sa, sb = _rope_tables(S, DIFF_HEAD_DIM, tm, 0)
    tab = pl.BlockSpec((tm, LANES), lambda t: (pidx(t), 0))
    out = jax.ShapeDtypeStruct((lay.n_all, qk), BF16)
    q, k, v = pl.pallas_call(
        _diff_proj_kernel,
        grid=(lay.n_all // tm,),
        in_specs=[pl.BlockSpec((tm, D), lambda t: (t, 0)),
                  pl.BlockSpec((1, N_MOD, D), lambda t: (midx(t), 0, 0)),
                  pl.BlockSpec((1, D), lambda t: (0, 0)),
                  pl.BlockSpec((D, 3 * qk), lambda t: (0, 0)),
                  tab, tab, tab],
        out_specs=[pl.BlockSpec((tm, qk), lambda t: (t, 0))] * 2 + [pl.BlockSpec((tm, 2 * qk), lambda t: (t, 0))],
        out_shape=[out, out, jax.ShapeDtypeStruct((lay.n_all, 2 * qk), BF16)],
        compiler_params=_params("arbitrary"),
        name="diff_proj",
    )(h, mod, gain.reshape(1, D), w_in.astype(BF16), cos, sa, sb)

    tq = _divisor(math.gcd(S, C), 256)
    q_index, nq, nc, lat, ctx = _attn_specs(lay, tq, [qk, 2 * qk])
    row = pl.BlockSpec((tq, D), lambda b, i: (q_index(b, i), 0))
    return pl.pallas_call(
        functools.partial(_diff_attn_kernel, nq=nq, lambda_init=lambda_init),
        grid=(B, nq + (nc if with_ctx_out else 0)),
        in_specs=[pl.BlockSpec((4, DIFF_HEAD_DIM), lambda b, i: (0, 0)),
                  pl.BlockSpec((1, 2 * DIFF_HEAD_DIM), lambda b, i: (0, 0)),
                  pl.BlockSpec((tq, qk), lambda b, i: (q_index(b, i), 0)),
                  *lat, *ctx, row,
                  pl.BlockSpec((1, N_MOD, D), lambda b, i: (jnp.where(i < nq, b, B), 0, 0)),
                  pl.BlockSpec((qk, D), lambda b, i: (0, 0))],
        out_specs=row,
        out_shape=jax.ShapeDtypeStruct(h.shape, F32),
        input_output_aliases={7: 0},
        compiler_params=_params("arbitrary", "arbitrary"),
        name="diff_attn",
    )(lam_params.astype(F32), subln_w.reshape(1, -1), q, k, v, k, v, h, mod, w_out.astype(BF16))


def _hgrn_proj_kernel(h_ref, mod_ref, gain_ref, lb_ref, w_ref, o_ref, *, layer):
    KD = HGRN_HEADS * HGRN_KEY_DIM
    W = 2 * LANES
    a = _pre(h_ref[...], gain_ref[...], mod_ref[0], 3).astype(BF16)
    raw = lb_ref[...]
    e = jnp.exp(raw - jnp.max(raw, axis=0, keepdims=True))
    soft = e * (1.0 / jnp.sum(e, axis=0, keepdims=True))
    lb = jnp.sum(soft[1:layer + 1], axis=0, keepdims=True) if layer else jnp.zeros_like(soft[0:1])
    n = w_ref.shape[1] // W

    def proj(c):
        return _dot(a, w_ref[:, c * W:(c + 1) * W])

    cur = proj(0)
    for c in range(n):
        nxt = proj(c + 1) if c + 1 < n else None
        group, col = divmod(c * W, KD)
        if group in (0, 4):
            out = _silu(cur)
        elif group in (1, 2):
            lbc = lb[:, col:col + W]
            out = lbc + (1.0 - lbc) * jax.nn.sigmoid(cur)
        else:
            out = cur
        o_ref[:, c * W:(c + 1) * W] = out
        cur = nxt


def _split3(x):
    hi = x.astype(BF16)
    r = x - hi.astype(F32)
    mid = r.astype(BF16)
    lo = (r - mid.astype(F32)).astype(BF16)
    return hi, mid, lo


def _hgrn_scan_kernel(q_ref, f_ref, v_ref, o_ref, st_ref, *, n_chunks):
    d = pl.program_id(1)
    step = pl.program_id(2)
    C = HGRN_CHUNK
    DK, DV, H = HGRN_KEY_DIM, HGRN_VAL_DIM, HGRN_HEADS

    @pl.when(step == 0)
    def _():
        st_ref[...] = jnp.zeros_like(st_ref)

    fwd = d == 0
    r = lax.broadcasted_iota(jnp.int32, (C, C), 0)
    c = lax.broadcasted_iota(jnp.int32, (C, C), 1)
    sign = jnp.where(fwd, 1, -1)
    tri = (r - c) * sign >= 0
    tri_b = jnp.where(tri, 1.0, 0.0).astype(BF16)
    rows = lax.broadcasted_iota(jnp.int32, (C, 1), 0)
    mid_rows = (rows - jnp.where(fwd, C // 2, C - 1 - C // 2)) * sign <= 0

    ks = [slice(h * DK, (h + 1) * DK) for h in range(H)]
    vs = [slice(h * DV, (h + 1) * DV) for h in range(H)]

    def prepare(n):
        base = pl.multiple_of(jnp.where(fwd, n, n_chunks - 1 - n) * C, C)
        q = q_ref[pl.ds(base, C), :]
        f = f_ref[pl.ds(base, C), :]
        v = v_ref[pl.ds(base, C), :].astype(BF16)
        logf = jnp.log(f)
        k = 1.0 - f
        g = functools.reduce(lambda a, b: a + b, [_dot(tri_b, part) for part in _split3(logf)])
        g_last = jnp.sum(logf, axis=0, keepdims=True)
        g_mid = jnp.sum(jnp.where(mid_rows, logf, 0.0), axis=0, keepdims=True)
        qa = (q * jnp.exp(g - g_mid)).astype(BF16)
        kb = (k * jnp.exp(g_mid - g)).astype(BF16)
        qg = (q * jnp.exp(g)).astype(BF16)
        kd = (k * jnp.exp(g_last - g)).astype(BF16)
        a = [jnp.where(tri, _dot_nt(qa[:, ks[h]], kb[:, ks[h]]), 0.0).astype(BF16) for h in range(H)]
        intra = [_dot(a[h], v[:, vs[h]]) for h in range(H)]
        update = [_dot_tn(v[:, vs[h]], kd[:, ks[h]]) for h in range(H)]
        return base, qg, jnp.exp(g_last), intra, update

    cur = prepare(0)
    for n in range(n_chunks):
        nxt = prepare(n + 1) if n + 1 < n_chunks else None
        base, qg, decay, intra, update = cur
        st = [st_ref[vs[h], :] for h in range(H)]
        outs = [intra[h] + _dot_nt(qg[:, ks[h]], st[h].astype(BF16)) for h in range(H)]
        for h in range(H):
            st_ref[vs[h], :] = st[h] * decay[:, ks[h]] + update[h]
        o_ref[0, pl.ds(base, C), :] = jnp.concatenate(outs, axis=-1)
        cur = nxt


def _hgrn_out_kernel(o_ref, gate_ref, nw_ref, h_ref, mod_ref, wo_ref, out_ref):
    DV = HGRN_VAL_DIM
    o = o_ref[0] + o_ref[1]
    gate = gate_ref[...]
    nw = nw_ref[...]
    ys = [_rmsnorm(o[:, h * DV:(h + 1) * DV], nw) * gate[:, h * DV:(h + 1) * DV] for h in range(HGRN_HEADS)]
    y = jnp.concatenate(ys, axis=-1).astype(BF16)
    m = mod_ref[0]
    out_ref[...] = h_ref[...] + m[5:6] * _dot(y, wo_ref[...])


def _hgrn_mixer(lay, h, mod, gain, w_in, w_out, norm_w, lower_bounds, layer):
    D = h.shape[1]
    B, S, C = lay.B, lay.S, lay.C
    KD = HGRN_HEADS * HGRN_KEY_DIM
    tm = lay.tile(512)
    midx = lay.mod_index(tm)
    n_cols = w_in.shape[1] // KD
    proj = pl.pallas_call(
        functools.partial(_hgrn_proj_kernel, layer=layer),
        grid=(lay.n_all // tm,),
        in_specs=[pl.BlockSpec((tm, D), lambda t: (t, 0)),
                  pl.BlockSpec((1, N_MOD, D), lambda t: (midx(t), 0, 0)),
                  pl.BlockSpec((1, D), lambda t: (0, 0)),
                  pl.BlockSpec(lower_bounds.shape, lambda t: (0, 0)),
                  pl.BlockSpec((D, n_cols * KD), lambda t: (0, 0), pipeline_mode=pl.Buffered(1))],
        out_specs=pl.BlockSpec((tm, n_cols * KD), lambda t: (t, 0)),
        out_shape=jax.ShapeDtypeStruct((lay.n_all, n_cols * KD), F32),
        compiler_params=_params("arbitrary"),
        name="hgrn_proj",
    )(h, mod, gain.reshape(1, D), lower_bounds.astype(F32), w_in.astype(BF16))

    ts = C
    n_lat_steps = S // ts
    lat_blocks = lay.n_lat // ts

    def rows(b, d, s):
        lat = b * n_lat_steps + jnp.where(d == 0, s - 1, n_lat_steps - s)
        return jnp.where(s == 0, lat_blocks + b, lat)

    o = pl.pallas_call(
        functools.partial(_hgrn_scan_kernel, n_chunks=ts // HGRN_CHUNK),
        grid=(B, 2, 1 + n_lat_steps),
        in_specs=[pl.BlockSpec((ts, KD), lambda b, d, s: (rows(b, d, s), 0)),
                  pl.BlockSpec((ts, KD), lambda b, d, s: (rows(b, d, s), 1 + d)),
                  pl.BlockSpec((ts, KD), lambda b, d, s: (rows(b, d, s), 3))],
        out_specs=pl.BlockSpec((1, ts, KD), lambda b, d, s: (d, rows(b, d, s), 0)),
        out_shape=jax.ShapeDtypeStruct((2, lay.n_all, KD), F32),
        scratch_shapes=[pltpu.VMEM((HGRN_HEADS * HGRN_VAL_DIM, HGRN_KEY_DIM), F32)],
        compiler_params=_params("arbitrary", "arbitrary", "arbitrary"),
        name="hgrn_scan",
    )(proj, proj, proj)

    row = pl.BlockSpec((tm, D), lambda t: (t, 0))
    return pl.pallas_call(
        _hgrn_out_kernel,
        grid=(lay.n_all // tm,),
        in_specs=[pl.BlockSpec((2, tm, KD), lambda t: (0, t, 0)),
                  pl.BlockSpec((tm, KD), lambda t: (t, 4)),
                  pl.BlockSpec((1, HGRN_VAL_DIM), lambda t: (0, 0)),
                  row,
                  pl.BlockSpec((1, N_MOD, D), lambda t: (midx(t), 0, 0)),
                  pl.BlockSpec((KD, D), lambda t: (0, 0))],
        out_specs=row,
        out_shape=jax.ShapeDtypeStruct(h.shape, F32),
        input_output_aliases={3: 0},
        compiler_params=_params("arbitrary"),
        name="hgrn_out",
    )(o, proj, norm_w.reshape(1, -1), h, mod, w_out.astype(BF16))


MLA_QK_PAD = LANES


def _mla_proj_kernel(h_ref, mod_ref, gain_ref, wd_ref, qn_ref, kvn_ref, wuq_ref, wukv_ref,
                     cos_ref, sa_ref, sb_ref, q_ref, k_ref, v_ref):
    H, P = MLA_HEADS, MLA_QK_PAD
    a = _pre(h_ref[...], gain_ref[...], mod_ref[0], 3).astype(BF16)
    dn = _dot(a, wd_ref[...])
    cos, sa, sb = cos_ref[...], sa_ref[...], sb_ref[...]
    f = MLA_ROPE // 4
    cq = _rmsnorm(dn[:, :MLA_Q_LORA], qn_ref[...]).astype(BF16)
    q = _dot(cq, wuq_ref[...])
    for hd in range(H):
        q_ref[:, hd * P:(hd + 1) * P] = _rope(q[:, hd * P:(hd + 1) * P], cos, sa, sb, f).astype(BF16)
    ckv = _rmsnorm(dn[:, MLA_Q_LORA:MLA_Q_LORA + MLA_KV_LORA], kvn_ref[...]).astype(BF16)
    kv = _dot(ckv, wukv_ref[...])
    kr = _rope(dn[:, MLA_Q_LORA + MLA_KV_LORA:], cos, sa, sb, f)
    for hd in range(H):
        k_ref[:, hd * P:(hd + 1) * P] = (kv[:, hd * P:(hd + 1) * P] + kr).astype(BF16)
    v_ref[...] = kv[:, H * P:].astype(BF16)


def _mla_attn_kernel(q_ref, kl_ref, vl_ref, kx_ref, vx_ref, h_ref, mod_ref, wo_ref, o_ref, *, nq, with_ctx):
    i = pl.program_id(1)
    P, VD = MLA_QK_PAD, MLA_V_DIM
    c = ((MLA_NOPE + MLA_ROPE) ** -0.5) * LOG2_E

    def run(kv_pairs):
        def scores(hd):
            qh = q_ref[:, hd * P:(hd + 1) * P]
            return [_dot_nt(qh, k_ref[:, hd * P:(hd + 1) * P]) for k_ref, _ in kv_pairs]

        heads = []
        ss = scores(0)
        for hd in range(MLA_HEADS):
            nxt = scores(hd + 1) if hd + 1 < MLA_HEADS else None
            m = functools.reduce(jnp.maximum, [jnp.max(s, axis=-1, keepdims=True) for s in ss])
            mc = m * c
            o, l = None, None
            for s, (_, v_ref) in zip(ss, kv_pairs):
                e = jnp.exp2(s * c - mc)
                ls = jnp.sum(e, axis=-1, keepdims=True)
                l = ls if l is None else l + ls
                t = _dot(e.astype(BF16), v_ref[:, hd * VD:(hd + 1) * VD])
                o = t if o is None else o + t
            heads.append(o * (1.0 / l))
            ss = nxt
        o = jnp.concatenate(heads, axis=-1).astype(BF16)
        m = mod_ref[0]
        o_ref[...] = h_ref[...] + m[5:6] * _dot(o, wo_ref[...])

    if not with_ctx:
        run([(kl_ref, vl_ref), (kx_ref, vx_ref)])
        return

    @pl.when(i < nq)
    def _():
        run([(kl_ref, vl_ref), (kx_ref, vx_ref)])

    @pl.when(i >= nq)
    def _():
        run([(kx_ref, vx_ref)])


def _mla_mixer(lay, h, mod, gain, w_down, q_norm_w, kv_norm_w, w_uq, w_ukv, w_out, with_ctx_out):
    D = h.shape[1]
    B, S, C = lay.B, lay.S, lay.C
    H, P, VD = MLA_HEADS, MLA_QK_PAD, MLA_V_DIM
    qk = MLA_NOPE + MLA_ROPE
    lora = MLA_Q_LORA + MLA_KV_LORA
    kr_cols = jnp.zeros((D, P), F32).at[:, MLA_NOPE:qk].set(w_down[:, lora:])
    wd = jnp.concatenate([w_down[:, :lora], kr_cols], axis=1).astype(BF16)
    wuq = jnp.pad(w_uq.reshape(MLA_Q_LORA, H, qk), ((0, 0), (0, 0), (0, P - qk))).reshape(MLA_Q_LORA, H * P)
    ukv = w_ukv.reshape(MLA_KV_LORA, H, MLA_NOPE + VD)
    wkn = jnp.pad(ukv[..., :MLA_NOPE], ((0, 0), (0, 0), (0, P - MLA_NOPE))).reshape(MLA_KV_LORA, H * P)
    wukv = jnp.concatenate([wkn, ukv[..., MLA_NOPE:].reshape(MLA_KV_LORA, H * VD)], axis=1)

    tm = lay.tile(512)
    midx, pidx = lay.mod_index(tm), lay.pos_index(tm)
    cos, sa, sb = _rope_tables(S, MLA_ROPE, tm, MLA_NOPE)
    tab = pl.BlockSpec((tm, LANES), lambda t: (pidx(t), 0))
    full = lambda shape: pl.BlockSpec(shape, lambda t: (0, 0))
    q, k, v = pl.pallas_call(
        _mla_proj_kernel,
        grid=(lay.n_all // tm,),
        in_specs=[pl.BlockSpec((tm, D), lambda t: (t, 0)),
                  pl.BlockSpec((1, N_MOD, D), lambda t: (midx(t), 0, 0)),
                  full((1, D)), full((D, lora + P)), full((1, MLA_Q_LORA)), full((1, MLA_KV_LORA)),
                  full((MLA_Q_LORA, H * P)), full((MLA_KV_LORA, H * (P + VD))),
                  tab, tab, tab],
        out_specs=[pl.BlockSpec((tm, H * P), lambda t: (t, 0)),
                   pl.BlockSpec((tm, H * P), lambda t: (t, 0)),
                   pl.BlockSpec((tm, H * VD), lambda t: (t, 0))],
        out_shape=[jax.ShapeDtypeStruct((lay.n_all, H * P), BF16),
                   jax.ShapeDtypeStruct((lay.n_all, H * P), BF16),
                   jax.ShapeDtypeStruct((lay.n_all, H * VD), BF16)],
        compiler_params=_params("arbitrary"),
        name="mla_proj",
    )(h, mod, gain.reshape(1, D), wd, q_norm_w.reshape(1, -1), kv_norm_w.reshape(1, -1),
      wuq.astype(BF16), wukv.astype(BF16), cos, sa, sb)

    tq = _divisor(math.gcd(S, C), 256)
    q_index, nq, nc, lat, ctx = _attn_specs(lay, tq, [H * P, H * VD])
    row = pl.BlockSpec((tq, D), lambda b, i: (q_index(b, i), 0))
    return pl.pallas_call(
        functools.partial(_mla_attn_kernel, nq=nq, with_ctx=with_ctx_out),
        grid=(B, nq + (nc if with_ctx_out else 0)),
        in_specs=[pl.BlockSpec((tq, H * P), lambda b, i: (q_index(b, i), 0)),
                  *lat, *ctx, row,
                  pl.BlockSpec((1, N_MOD, D), lambda b, i: (jnp.where(i < nq, b, B), 0, 0)),
                  pl.BlockSpec((H * VD, D), lambda b, i: (0, 0))],
        out_specs=row,
        out_shape=jax.ShapeDtypeStruct(h.shape, F32),
        input_output_aliases={5: 0},
        compiler_params=_params("arbitrary", "arbitrary"),
        name="mla_attn",
    )(q, k, v, k, v, h, mod, w_out.astype(BF16))


def kernel(x, c, ctx, c_ctx, ada_w, ada_b, norm_w, final_norm_w, ffn_w_gate, ffn_w_up, ffn_w_down,
           gqa_w_in, gqa_w_out, gqa_sinks, diff_w_in, diff_w_out, diff_lambda, diff_subln_w,
           hgrn_w_in, hgrn_w_out, hgrn_norm_w, hgrn_lower_bounds,
           mla_w_down, mla_q_norm_w, mla_kv_norm_w, mla_w_uq, mla_w_ukv, mla_w_out):
    B, S, D = x.shape
    C = ctx.shape[1]
    lay = _Layout(B, S, C)
    depth = ada_w.shape[0]

    rows = -(-(B + 1) // 8) * 8
    cc = jnp.zeros((rows, D), F32).at[:B].set(c).at[B].set(c_ctx)
    mods = _ada_table(cc, ada_w, ada_b).reshape(depth, rows, N_MOD, D)

    wg, wu, wd = ffn_w_gate.astype(BF16), ffn_w_up.astype(BF16), ffn_w_down.astype(BF16)
    h = jnp.concatenate([x.reshape(B * S, D), ctx.reshape(B * C, D)], axis=0)
    for i in range(depth):
        kind, j = i % 4, i // 4
        last = i == depth - 1
        mod = mods[i]
        h = _ffn(lay, h, mod, norm_w[i, 0], final_norm_w, wg[i, 0], wu[i, 0], wd[i, 0], 0)
        if kind == 0:
            h = _gqa_mixer(lay, h, mod, norm_w[i, 1], gqa_w_in[j], gqa_w_out[j], gqa_sinks[j], not last)
        elif kind == 1:
            lambda_init = 0.8 - 0.6 * math.exp(-0.3 * i)
            h = _diff_mixer(lay, h, mod, norm_w[i, 1], diff_w_in[j], diff_w_out[j], diff_lambda[j],
                            diff_subln_w[j], lambda_init, not last)
        elif kind == 2:
            h = _hgrn_mixer(lay, h, mod, norm_w[i, 1], hgrn_w_in[j], hgrn_w_out[j], hgrn_norm_w[j],
                            hgrn_lower_bounds, i)
        else:
            h = _mla_mixer(lay, h, mod, norm_w[i, 1], mla_w_down[j], mla_q_norm_w[j], mla_kv_norm_w[j],
                           mla_w_uq[j], mla_w_ukv[j], mla_w_out[j], not last)
        h = _ffn(lay, h, mod, norm_w[i, 2], final_norm_w, wg[i, 1], wu[i, 1], wd[i, 1], 6,
                 lat_only=last, final_norm=last)
    return h[:B * S].reshape(B, S, D)
```

```python
import functools
import math

import jax
import jax.numpy as jnp
from jax import lax
from jax.experimental import pallas as pl
from jax.experimental.pallas import tpu as pltpu

F32 = jnp.float32
BF16 = jnp.bfloat16

D_MODEL = 1024
DEPTH = 4
GRID_W = 64
N_MOD = 9
ROPE_BASE = 10000.0
EPS = 1e-6
NEG_INF = -1e30
D_FF = 2816

GQA_HEADS = 16
GQA_KV_HEADS = 4
GQA_HEAD_DIM = 64
GQA_WINDOW = 128
GQA_BLOCK = 128

DIFF_HEADS = 8
DIFF_HEAD_DIM = 64

HGRN_HEADS = 8
HGRN_KEY_DIM = 128
HGRN_VAL_DIM = D_MODEL // HGRN_HEADS
HGRN_CHUNK = 64

MLA_HEADS = 16
MLA_Q_LORA = 256
MLA_KV_LORA = 256
MLA_NOPE = 64
MLA_ROPE = 32
MLA_V_DIM = 64

LOG2_E = 1.4426950408889634
LANES = 128
VMEM_LIMIT = 56 * 1024 * 1024


def _dot(a, b):
    return jnp.dot(a, b, preferred_element_type=F32)


def _dot_nt(a, b):
    return lax.dot_general(a, b, (((1,), (1,)), ((), ())), preferred_element_type=F32)


def _dot_tn(a, b):
    return lax.dot_general(a, b, (((0,), (0,)), ((), ())), preferred_element_type=F32)


def _params(*sem):
    return pltpu.CompilerParams(dimension_semantics=sem, vmem_limit_bytes=VMEM_LIMIT)


def _divisor(n, pref):
    t = min(n, pref)
    while n % t:
        t -= 8
    return t


class _Layout:
    def __init__(self, B, S, C):
        self.B, self.S, self.C = B, S, C
        self.n_lat = B * S
        self.n_all = B * S + B * C

    def tile(self, pref):
        return _divisor(math.gcd(self.S, self.B * self.C), pref)

    def n_tiles(self, tm, lat_only=False):
        return (self.n_lat if lat_only else self.n_all) // tm

    def mod_index(self, tm):
        n_lat_tiles, per_batch, B = self.n_lat // tm, self.S // tm, self.B
        return lambda t: jnp.where(t < n_lat_tiles, t // per_batch, B)

    def pos_index(self, tm):
        n_lat_tiles, per_batch = self.n_lat // tm, self.S // tm
        return lambda t: jnp.where(t < n_lat_tiles, t % per_batch, per_batch)


def _rmsnorm(x, w):
    return (x * lax.rsqrt(jnp.mean(x * x, axis=-1, keepdims=True) + EPS)) * w


def _pre(h, gain, m, off):
    return _rmsnorm(h, gain) * (1.0 + m[off + 1:off + 2]) + m[off:off + 1]


def _silu(x):
    return x * jax.nn.sigmoid(x)


def _rope(x, cos, sa, sb, shift):
    return x * cos + pltpu.roll(x, LANES - shift, 1) * sa + pltpu.roll(x, shift, 1) * sb


def _ada_kernel(c_ref, w_ref, b_ref, o_ref):
    sc = _silu(c_ref[...]).astype(BF16)
    o_ref[0] = _dot(sc, w_ref[0].astype(BF16)) + b_ref[0]


def _ada_table(cc, ada_w, ada_b):
    rows, D = cc.shape
    n_out = ada_w.shape[2]
    tn = n_out // 8
    return pl.pallas_call(
        _ada_kernel,
        grid=(DEPTH, n_out // tn),
        in_specs=[pl.BlockSpec((rows, D), lambda i, j: (0, 0)),
                  pl.BlockSpec((1, D, tn), lambda i, j: (i, 0, j)),
                  pl.BlockSpec((1, 1, tn), lambda i, j: (i, 0, j))],
        out_specs=pl.BlockSpec((1, rows, tn), lambda i, j: (i, 0, j)),
        out_shape=jax.ShapeDtypeStruct((DEPTH, rows, n_out), F32),
        compiler_params=_params("arbitrary", "arbitrary"),
        name="ada_table",
    )(cc, ada_w, ada_b.reshape(DEPTH, 1, n_out))


FFN_CHUNK = 2 * LANES


def _ffn_kernel(*refs, off, final_norm, n_lat_tiles):
    if n_lat_tiles is None:
        h_ref, mod_ref, gain_ref, fin_ref, wg_ref, wu_ref, wd_ref, o_ref, a_ref = refs
        h = h_ref[...]
    else:
        hx_ref, hc_ref, mod_ref, gain_ref, fin_ref, wg_ref, wu_ref, wd_ref, o_ref, a_ref = refs
        h = jnp.where(pl.program_id(0) < n_lat_tiles, hx_ref[...], hc_ref[...])
    m = mod_ref[0]
    xm = _pre(h, gain_ref[...], m, off).astype(BF16)
    n = D_FF // FFN_CHUNK

    def gate_up(c):
        cols = slice(c * FFN_CHUNK, (c + 1) * FFN_CHUNK)
        return _dot(xm, wg_ref[:, cols]), _dot(xm, wu_ref[:, cols])

    cur = gate_up(0)
    for c in range(n):
        nxt = gate_up(c + 1) if c + 1 < n else None
        g, u = cur
        a_ref[:, c * FFN_CHUNK:(c + 1) * FFN_CHUNK] = (_silu(g) * u).astype(BF16)
        cur = nxt
    out = h + (0.5 * m[off + 2:off + 3]) * _dot(a_ref[...], wd_ref[...])
    if final_norm:
        out = _rmsnorm(out, fin_ref[...])
    o_ref[...] = out


def _ffn(lay, h, mod, gain, fin_w, wg, wu, wd, layer, half, lat_only=False, final_norm=False):
    split = isinstance(h, tuple)
    D = wg.shape[2]
    tm = lay.tile(512)
    n_rows = lay.n_lat if lat_only else lay.n_all
    midx = lay.mod_index(tm)
    n_lat_tiles = lay.n_lat // tm
    const = lambda t: (0, 0)
    one = pl.Buffered(1)
    weight = lambda shape: pl.BlockSpec((None, None) + shape, lambda t: (layer, half, 0, 0), pipeline_mode=one)
    if split:
        h_specs = [pl.BlockSpec((tm, D), lambda t: (jnp.minimum(t, n_lat_tiles - 1), 0)),
                   pl.BlockSpec((tm, D), lambda t: (jnp.maximum(t - n_lat_tiles, 0), 0))]
        h_args = list(h)
    else:
        h_specs = [pl.BlockSpec((tm, D), lambda t: (t, 0))]
        h_args = [h]
    return pl.pallas_call(
        functools.partial(_ffn_kernel, off=6 * half, final_norm=final_norm,
                          n_lat_tiles=n_lat_tiles if split else None),
        grid=(n_rows // tm,),
        in_specs=h_specs + [pl.BlockSpec((1, N_MOD, D), lambda t: (midx(t), 0, 0)),
                            pl.BlockSpec((1, D), const),
                            pl.BlockSpec((1, D), const),
                            weight((D, D_FF)), weight((D, D_FF)), weight((D_FF, D))],
        out_specs=pl.BlockSpec((tm, D), lambda t: (t, 0)),
        out_shape=jax.ShapeDtypeStruct((n_rows, D), F32),
        scratch_shapes=[pltpu.VMEM((tm, D_FF), BF16)],
        compiler_params=_params("arbitrary"),
        name="ffn",
    )(*h_args, mod, gain.reshape(1, D), fin_w.reshape(1, D), wg, wu, wd)


def _rope_tables(S, rot_dim, pad_rows, lane_off):
    rows = S // GRID_W
    row = jnp.repeat(jnp.arange(rows, dtype=F32), GRID_W)
    col = jnp.tile(jnp.arange(GRID_W, dtype=F32), rows)
    axis_dim = rot_dim // 2
    inv_freq = ROPE_BASE ** (-jnp.arange(0, axis_dim, 2, dtype=F32) / axis_dim)
    ang_r = row[:, None] * inv_freq[None, :]
    ang_c = col[:, None] * inv_freq[None, :]
    ang = jnp.concatenate([ang_r, ang_r, ang_c, ang_c], axis=-1)
    cos, sin = jnp.cos(ang), jnp.sin(ang)
    f = rot_dim // 4
    first = (jnp.arange(rot_dim) % (2 * f)) < f
    sa = jnp.where(first[None, :], -sin, 0.0)
    sb = jnp.where(first[None, :], 0.0, sin)
    period = 64 if lane_off + rot_dim <= 64 else LANES

    def widen(t, fill):
        blk = jnp.full((S, period), fill, F32).at[:, lane_off:lane_off + rot_dim].set(t)
        blk = jnp.tile(blk, (1, LANES // period))
        return jnp.concatenate([blk, jnp.full((pad_rows, LANES), fill, F32)], axis=0)

    return widen(cos, 1.0), widen(sa, 0.0), widen(sb, 0.0)


def _attn_specs(lay, tq, widths):
    S, C, B = lay.S, lay.C, lay.B
    nq, nc = S // tq, C // tq
    lat_blocks = lay.n_lat // C

    def q_index(b, i):
        return jnp.where(i < nq, b * nq + i, B * nq + b * nc + (i - nq))

    one = pl.Buffered(1)
    lat = [pl.BlockSpec((S, w), lambda b, i: (b, 0), pipeline_mode=one) for w in widths]
    ctx = [pl.BlockSpec((C, w), lambda b, i: (lat_blocks + b, 0), pipeline_mode=one) for w in widths]
    return q_index, nq, nc, lat, ctx


def _gqa_proj_kernel(h_ref, mod_ref, gain_ref, w_ref, cos_ref, sa_ref, sb_ref, q_ref, k_ref, v_ref):
    qd, kd = GQA_HEADS * GQA_HEAD_DIM, GQA_KV_HEADS * GQA_HEAD_DIM
    a = _pre(h_ref[...], gain_ref[...], mod_ref[0], 3).astype(BF16)
    p = _dot(a, w_ref[...])
    cos, sa, sb = cos_ref[...], sa_ref[...], sb_ref[...]
    f = GQA_HEAD_DIM // 4
    scale = GQA_HEAD_DIM ** -0.5
    for c in range(qd // LANES):
        x = _rope(p[:, c * LANES:(c + 1) * LANES], cos, sa, sb, f)
        q_ref[:, c * LANES:(c + 1) * LANES] = (x * scale).astype(BF16)
    for c in range(kd // LANES):
        x = _rope(p[:, qd + c * LANES:qd + (c + 1) * LANES], cos, sa, sb, f)
        k_ref[:, c * LANES:(c + 1) * LANES] = x.astype(BF16)
    v_ref[...] = p[:, qd + kd:].astype(BF16)


def _gqa_attn_kernel(sink_ref, q_ref, kp_ref, kc_ref, kn_ref, vp_ref, vc_ref, vn_ref, kx_ref, vx_ref,
                     h_ref, mod_ref, wo_ref, o_ref, kcat_ref, vcat_ref, *, nq, S, C):
    i = pl.program_id(1)
    blk = GQA_BLOCK
    G = GQA_HEADS // GQA_KV_HEADS
    HD = GQA_HEAD_DIM
    span = 3 * blk

    def finish(heads):
        o = jnp.concatenate(heads, axis=-1).astype(BF16)
        m = mod_ref[0]
        o_ref[...] = h_ref[...] + m[5:6] * _dot(o, wo_ref[...])

    def run(k_ref, v_ref, mask):
        kvs = range(GQA_KV_HEADS)
        qs = [jnp.concatenate([q_ref[:, (G * g + n) * HD:(G * g + n + 1) * HD] for n in range(G)], axis=0)
              for g in kvs]
        sinks = [jnp.concatenate([jnp.full((blk, 1), sink_ref[G * g + n], F32) for n in range(G)], axis=0)
                 for g in kvs]
        ss = [_dot_nt(qs[g], k_ref[:, g * HD:(g + 1) * HD]) for g in kvs]
        if mask is not None:
            ss = [jnp.where(mask, s, NEG_INF) for s in ss]
        ms = [jnp.maximum(jnp.max(ss[g], axis=-1, keepdims=True), sinks[g]) for g in kvs]
        es = [jnp.exp(ss[g] - ms[g]) for g in kvs]
        invs = [1.0 / (jnp.sum(es[g], axis=-1, keepdims=True) + jnp.exp(sinks[g] - ms[g])) for g in kvs]
        outs = [_dot(es[g].astype(BF16), v_ref[:, g * HD:(g + 1) * HD]) * invs[g] for g in kvs]
        finish([outs[g][n * blk:(n + 1) * blk] for g in kvs for n in range(G)])

    @pl.when(i < nq)
    def _():
        for n, (k_ref, v_ref) in enumerate([(kp_ref, vp_ref), (kc_ref, vc_ref), (kn_ref, vn_ref)]):
            kcat_ref[n * blk:(n + 1) * blk] = k_ref[...]
            vcat_ref[n * blk:(n + 1) * blk] = v_ref[...]
        kcat_ref[span:] = kx_ref[...]
        vcat_ref[span:] = vx_ref[...]
        row = lax.broadcasted_iota(jnp.int32, (G * blk, span + C), 0) & (blk - 1)
        col = lax.broadcasted_iota(jnp.int32, (G * blk, span + C), 1)
        k_abs = (i - 1) * blk + col
        local = (jnp.abs(row + blk - col) <= GQA_WINDOW) & (k_abs >= 0) & (k_abs < S)
        run(kcat_ref, vcat_ref, local | (col >= span))

    @pl.when(i >= nq)
    def _():
        run(kx_ref, vx_ref, None)


def _gqa_mixer(lay, h, mod, gain, w_in, w_out, sinks, with_ctx_out):
    D = h.shape[1]
    B, S, C = lay.B, lay.S, lay.C
    qd, kd = GQA_HEADS * GQA_HEAD_DIM, GQA_KV_HEADS * GQA_HEAD_DIM
    tm = lay.tile(512)
    midx, pidx = lay.mod_index(tm), lay.pos_index(tm)
    cos, sa, sb = _rope_tables(S, GQA_HEAD_DIM, tm, 0)
    tab = pl.BlockSpec((tm, LANES), lambda t: (pidx(t), 0))
    q, k, v = pl.pallas_call(
        _gqa_proj_kernel,
        grid=(lay.n_all // tm,),
        in_specs=[pl.BlockSpec((tm, D), lambda t: (t, 0)),
                  pl.BlockSpec((1, N_MOD, D), lambda t: (midx(t), 0, 0)),
                  pl.BlockSpec((1, D), lambda t: (0, 0)),
                  pl.BlockSpec((D, qd + 2 * kd), lambda t: (0, 0)),
                  tab, tab, tab],
        out_specs=[pl.BlockSpec((tm, qd), lambda t: (t, 0)),
                   pl.BlockSpec((tm, kd), lambda t: (t, 0)),
                   pl.BlockSpec((tm, kd), lambda t: (t, 0))],
        out_shape=[jax.ShapeDtypeStruct((lay.n_all, qd), BF16),
                   jax.ShapeDtypeStruct((lay.n_all, kd), BF16),
                   jax.ShapeDtypeStruct((lay.n_all, kd), BF16)],
        compiler_params=_params("arbitrary"),
        name="gqa_proj",
    )(h, mod, gain.reshape(1, D), w_in.astype(BF16), cos, sa, sb)

    blk = GQA_BLOCK
    assert S % blk == 0 and C % blk == 0
    nq, nc = S // blk, C // blk
    n_i = nq + (nc if with_ctx_out else 0)
    lat_blocks = lay.n_lat // C

    def q_index(b, i):
        return jnp.where(i < nq, b * nq + i, B * nq + b * nc + (i - nq))

    def win(delta):
        return lambda b, i: (b * nq + jnp.clip(i + delta, 0, nq - 1), 0)

    kv_win = [pl.BlockSpec((blk, kd), win(d)) for d in (-1, 0, 1)]
    kv_ctx = pl.BlockSpec((C, kd), lambda b, i: (lat_blocks + b, 0))
    row = pl.BlockSpec((blk, D), lambda b, i: (q_index(b, i), 0))
    return pl.pallas_call(
        functools.partial(_gqa_attn_kernel, nq=nq, S=S, C=C),
        grid=(B, n_i),
        in_specs=[pl.BlockSpec(memory_space=pltpu.SMEM),
                  pl.BlockSpec((blk, qd), lambda b, i: (q_index(b, i), 0)),
                  *kv_win, *kv_win, kv_ctx, kv_ctx, row,
                  pl.BlockSpec((1, N_MOD, D), lambda b, i: (jnp.where(i < nq, b, B), 0, 0)),
                  pl.BlockSpec((qd, D), lambda b, i: (0, 0))],
        out_specs=row,
        out_shape=jax.ShapeDtypeStruct(h.shape, F32),
        input_output_aliases={10: 0},
        scratch_shapes=[pltpu.VMEM((3 * blk + C, kd), BF16), pltpu.VMEM((3 * blk + C, kd), BF16)],
        compiler_params=_params("arbitrary", "arbitrary"),
        name="gqa_attn",
    )(sinks.astype(F32), q, k, k, k, v, v, v, k, v, h, mod, w_out.astype(BF16))


def _diff_proj_kernel(h_ref, mod_ref, gain_ref, w_ref, cos_ref, sa_ref, sb_ref, q_ref, k_ref, v_ref):
    qk = DIFF_HEADS * 2 * DIFF_HEAD_DIM
    a = _pre(h_ref[...], gain_ref[...], mod_ref[0], 3).astype(BF16)
    p = _dot(a, w_ref[...])
    cos, sa, sb = cos_ref[...], sa_ref[...], sb_ref[...]
    f = DIFF_HEAD_DIM // 4
    scale = DIFF_HEAD_DIM ** -0.5
    for c in range(qk // LANES):
        x = _rope(p[:, c * LANES:(c + 1) * LANES], cos, sa, sb, f)
        q_ref[:, c * LANES:(c + 1) * LANES] = (x * scale).astype(BF16)
        x = _rope(p[:, qk + c * LANES:qk + (c + 1) * LANES], cos, sa, sb, f)
        k_ref[:, c * LANES:(c + 1) * LANES] = x.astype(BF16)
    vw = 2 * DIFF_HEAD_DIM
    ones = jnp.ones((p.shape[0], vw), BF16)
    for hd in range(DIFF_HEADS):
        v_ref[:, 2 * hd * vw:(2 * hd + 1) * vw] = p[:, 2 * qk + hd * vw:2 * qk + (hd + 1) * vw].astype(BF16)
        v_ref[:, (2 * hd + 1) * vw:(2 * hd + 2) * vw] = ones


def _diff_attn_kernel(lam_ref, sub_ref, q_ref, kl_ref, vl_ref, kx_ref, vx_ref, h_ref, mod_ref, wo_ref, o_ref,
                      *, nq, lambda_init):
    i = pl.program_id(1)
    HD = DIFF_HEAD_DIM
    VW = 4 * HD
    lp = lam_ref[...]
    lam = (jnp.exp(jnp.sum(lp[0:1] * lp[1:2], keepdims=True))
           - jnp.exp(jnp.sum(lp[2:3] * lp[3:4], keepdims=True)) + lambda_init)

    def run(kv_pairs):
        def scores(h):
            los = [(2 * h + j) * HD for j in range(2)]
            return [[_dot_nt(q_ref[:, lo:lo + HD], k_ref[:, lo:lo + HD]) for k_ref, _ in kv_pairs] for lo in los]

        heads = []
        ss = scores(0)
        for h in range(DIFF_HEADS):
            nxt = scores(h + 1) if h + 1 < DIFF_HEADS else None
            maps = []
            for parts in ss:
                m = functools.reduce(jnp.maximum, [jnp.max(s, axis=-1, keepdims=True) for s in parts])
                acc = None
                for s, (_, v_ref) in zip(parts, kv_pairs):
                    t = _dot(jnp.exp(s - m).astype(BF16), v_ref[:, h * VW:(h + 1) * VW])
                    acc = t if acc is None else acc + t
                maps.append(acc[:, :VW // 2] * (1.0 / acc[:, VW // 2:]))
            o = maps[0] - lam * maps[1]
            heads.append(_rmsnorm(o, sub_ref[...]) * (1.0 - lambda_init))
            ss = nxt
        o = jnp.concatenate(heads, axis=-1).astype(BF16)
        m = mod_ref[0]
        o_ref[...] = h_ref[...] + m[5:6] * _dot(o, wo_ref[...])

    @pl.when(i < nq)
    def _():
        run([(kl_ref, vl_ref), (kx_ref, vx_ref)])

    @pl.when(i >= nq)
    def _():
        run([(kx_ref, vx_ref)])


def _diff_mixer(lay, h, mod, gain, w_in, w_out, lam_params, subln_w, lambda_init, with_ctx_out):
    D = h.shape[1]
    B, S, C = lay.B, lay.S, lay.C
    qk = DIFF_HEADS * 2 * DIFF_HEAD_DIM
    tm = lay.tile(512)
    midx, pidx = lay.mod_index(tm), lay.pos_index(tm)
    cos, sa, sb = _rope_tables(S, DIFF_HEAD_DIM, tm, 0)
    tab = pl.BlockSpec((tm, LANES), lambda t: (pidx(t), 0))
    out = jax.ShapeDtypeStruct((lay.n_all, qk), BF16)
    q, k, v = pl.pallas_call(
        _diff_proj_kernel,
        grid=(lay.n_all // tm,),
        in_specs=[pl.BlockSpec((tm, D), lambda t: (t, 0)),
                  pl.BlockSpec((1, N_MOD, D), lambda t: (midx(t), 0, 0)),
                  pl.BlockSpec((1, D), lambda t: (0, 0)),
                  pl.BlockSpec((D, 3 * qk), lambda t: (0, 0)),
                  tab, tab, tab],
        out_specs=[pl.BlockSpec((tm, qk), lambda t: (t, 0))] * 2 + [pl.BlockSpec((tm, 2 * qk), lambda t: (t, 0))],
        out_shape=[out, out, jax.ShapeDtypeStruct((lay.n_all, 2 * qk), BF16)],
        compiler_params=_params("arbitrary"),
        name="diff_proj",
    )(h, mod, gain.reshape(1, D), w_in.astype(BF16), cos, sa, sb)

    tq = _divisor(math.gcd(S, C), 256)
    q_index, nq, nc, lat, ctx = _attn_specs(lay, tq, [qk, 2 * qk])
    row = pl.BlockSpec((tq, D), lambda b, i: (q_index(b, i), 0))
    return pl.pallas_call(
        functools.partial(_diff_attn_kernel, nq=nq, lambda_init=lambda_init),
        grid=(B, nq + (nc if with_ctx_out else 0)),
        in_specs=[pl.BlockSpec((4, DIFF_HEAD_DIM), lambda b, i: (0, 0)),
                  pl.BlockSpec((1, 2 * DIFF_HEAD_DIM), lambda b, i: (0, 0)),
                  pl.BlockSpec((tq, qk), lambda b, i: (q_index(b, i), 0)),
                  *lat, *ctx, row,
                  pl.BlockSpec((1, N_MOD, D), lambda b, i: (jnp.where(i < nq, b, B), 0, 0)),
                  pl.BlockSpec((qk, D), lambda b, i: (0, 0))],
        out_specs=row,
        out_shape=jax.ShapeDtypeStruct(h.shape, F32),
        input_output_aliases={7: 0},
        compiler_params=_params("arbitrary", "arbitrary"),
        name="diff_attn",
    )(lam_params.astype(F32), subln_w.reshape(1, -1), q, k, v, k, v, h, mod, w_out.astype(BF16))


def _hgrn_proj_kernel(h_ref, mod_ref, gain_ref, lb_ref, w_ref, o_ref, *, layer):
    KD = HGRN_HEADS * HGRN_KEY_DIM
    W = 2 * LANES
    a = _pre(h_ref[...], gain_ref[...], mod_ref[0], 3).astype(BF16)
    raw = lb_ref[...]
    e = jnp.exp(raw - jnp.max(raw, axis=0, keepdims=True))
    soft = e * (1.0 / jnp.sum(e, axis=0, keepdims=True))
    lb = jnp.sum(soft[1:layer + 1], axis=0, keepdims=True) if layer else jnp.zeros_like(soft[0:1])
    n = w_ref.shape[1] // W

    def proj(c):
        return _dot(a, w_ref[:, c * W:(c + 1) * W])

    cur = proj(0)
    for c in range(n):
        nxt = proj(c + 1) if c + 1 < n else None
        group, col = divmod(c * W, KD)
        if group in (0, 4):
            out = _silu(cur)
        elif group in (1, 2):
            lbc = lb[:, col:col + W]
            out = lbc + (1.0 - lbc) * jax.nn.sigmoid(cur)
        else:
            out = cur
        o_ref[:, c * W:(c + 1) * W] = out
        cur = nxt


def _split3(x):
    hi = x.astype(BF16)
    r = x - hi.astype(F32)
    mid = r.astype(BF16)
    lo = (r - mid.astype(F32)).astype(BF16)
    return hi, mid, lo


def _hgrn_scan_kernel(qf_ref, ff_ref, vf_ref, qb_ref, fb_ref, vb_ref, of_ref, ob_ref, st_ref, *, n_chunks):
    step = pl.program_id(1)
    C = HGRN_CHUNK
    DK, DV, H = HGRN_KEY_DIM, HGRN_VAL_DIM, HGRN_HEADS

    @pl.when(step == 0)
    def _():
        st_ref[...] = jnp.zeros_like(st_ref)

    r = lax.broadcasted_iota(jnp.int32, (C, C), 0)
    c = lax.broadcasted_iota(jnp.int32, (C, C), 1)
    rows = lax.broadcasted_iota(jnp.int32, (C, 1), 0)
    ks = [slice(h * DK, (h + 1) * DK) for h in range(H)]
    vs = [slice(h * DV, (h + 1) * DV) for h in range(H)]
    dirs = [(qf_ref, ff_ref, vf_ref, of_ref, c <= r, rows <= C // 2),
            (qb_ref, fb_ref, vb_ref, ob_ref, c >= r, rows >= C - 1 - C // 2)]

    def prepare(d, n):
        q_ref, f_ref, v_ref, _, tri, mid_rows = dirs[d]
        base = (n if d == 0 else n_chunks - 1 - n) * C
        q = q_ref[base:base + C, :]
        f = f_ref[base:base + C, :]
        v = v_ref[base:base + C, :].astype(BF16)
        logf = jnp.log(f)
        k = 1.0 - f
        tri_b = jnp.where(tri, 1.0, 0.0).astype(BF16)
        g = functools.reduce(lambda a, b: a + b, [_dot(tri_b, part) for part in _split3(logf)])
        g_last = jnp.sum(logf, axis=0, keepdims=True)
        g_mid = jnp.sum(jnp.where(mid_rows, logf, 0.0), axis=0, keepdims=True)
        qa = (q * jnp.exp(g - g_mid)).astype(BF16)
        kb = (k * jnp.exp(g_mid - g)).astype(BF16)
        qg = (q * jnp.exp(g)).astype(BF16)
        kd = (k * jnp.exp(g_last - g)).astype(BF16)
        a = [jnp.where(tri, _dot_nt(qa[:, ks[h]], kb[:, ks[h]]), 0.0).astype(BF16) for h in range(H)]
        intra = [_dot(a[h], v[:, vs[h]]) for h in range(H)]
        update = [_dot_tn(v[:, vs[h]], kd[:, ks[h]]) for h in range(H)]
        return base, qg, jnp.exp(g_last), intra, update

    cur = [prepare(d, 0) for d in range(2)]
    for n in range(n_chunks):
        nxt = [prepare(d, n + 1) for d in range(2)] if n + 1 < n_chunks else None
        for d in range(2):
            base, qg, decay, intra, update = cur[d]
            st = [st_ref[d, vs[h], :] for h in range(H)]
            outs = [intra[h] + _dot_nt(qg[:, ks[h]], st[h].astype(BF16)) for h in range(H)]
            for h in range(H):
                st_ref[d, vs[h], :] = st[h] * decay[:, ks[h]] + update[h]
            dirs[d][3][base:base + C, :] = jnp.concatenate(outs, axis=-1)
        cur = nxt


def _hgrn_out_kernel(of_ref, ob_ref, gate_ref, nw_ref, h_ref, mod_ref, wo_ref, out_ref):
    DV = HGRN_VAL_DIM
    o = of_ref[...] + ob_ref[...]
    gate = gate_ref[...]
    nw = nw_ref[...]
    ys = [_rmsnorm(o[:, h * DV:(h + 1) * DV], nw) * gate[:, h * DV:(h + 1) * DV] for h in range(HGRN_HEADS)]
    y = jnp.concatenate(ys, axis=-1).astype(BF16)
    m = mod_ref[0]
    out_ref[...] = h_ref[...] + m[5:6] * _dot(y, wo_ref[...])


def _hgrn_mixer(lay, h, mod, gain, w_in, w_out, norm_w, lower_bounds, layer):
    D = h.shape[1]
    B, S, C = lay.B, lay.S, lay.C
    KD = HGRN_HEADS * HGRN_KEY_DIM
    tm = lay.tile(512)
    midx = lay.mod_index(tm)
    n_cols = w_in.shape[1] // KD
    proj = pl.pallas_call(
        functools.partial(_hgrn_proj_kernel, layer=layer),
        grid=(lay.n_all // tm,),
        in_specs=[pl.BlockSpec((tm, D), lambda t: (t, 0)),
                  pl.BlockSpec((1, N_MOD, D), lambda t: (midx(t), 0, 0)),
                  pl.BlockSpec((1, D), lambda t: (0, 0)),
                  pl.BlockSpec(lower_bounds.shape, lambda t: (0, 0)),
                  pl.BlockSpec((D, n_cols * KD), lambda t: (0, 0), pipeline_mode=pl.Buffered(1))],
        out_specs=pl.BlockSpec((tm, n_cols * KD), lambda t: (t, 0)),
        out_shape=jax.ShapeDtypeStruct((lay.n_all, n_cols * KD), F32),
        compiler_params=_params("arbitrary"),
        name="hgrn_proj",
    )(h, mod, gain.reshape(1, D), lower_bounds.astype(F32), w_in.astype(BF16))

    ts = C
    n_lat_steps = S // ts
    lat_blocks = lay.n_lat // ts

    def rows(d):
        def index(b, s):
            lat = b * n_lat_steps + (s - 1 if d == 0 else n_lat_steps - s)
            return jnp.where(s == 0, lat_blocks + b, lat)
        return index

    def col(d, j):
        return pl.BlockSpec((ts, KD), lambda b, s: (rows(d)(b, s), j))

    o_f, o_b = pl.pallas_call(
        functools.partial(_hgrn_scan_kernel, n_chunks=ts // HGRN_CHUNK),
        grid=(B, 1 + n_lat_steps),
        in_specs=[col(0, 0), col(0, 1), col(0, 3), col(1, 0), col(1, 2), col(1, 3)],
        out_specs=[col(0, 0), col(1, 0)],
        out_shape=[jax.ShapeDtypeStruct((lay.n_all, KD), F32)] * 2,
        scratch_shapes=[pltpu.VMEM((2, HGRN_HEADS * HGRN_VAL_DIM, HGRN_KEY_DIM), F32)],
        compiler_params=_params("arbitrary", "arbitrary"),
        name="hgrn_scan",
    )(proj, proj, proj, proj, proj, proj)

    row = pl.BlockSpec((tm, D), lambda t: (t, 0))
    return pl.pallas_call(
        _hgrn_out_kernel,
        grid=(lay.n_all // tm,),
        in_specs=[pl.BlockSpec((tm, KD), lambda t: (t, 0)),
                  pl.BlockSpec((tm, KD), lambda t: (t, 0)),
                  pl.BlockSpec((tm, KD), lambda t: (t, 4)),
                  pl.BlockSpec((1, HGRN_VAL_DIM), lambda t: (0, 0)),
                  row,
                  pl.BlockSpec((1, N_MOD, D), lambda t: (midx(t), 0, 0)),
                  pl.BlockSpec((KD, D), lambda t: (0, 0))],
        out_specs=row,
        out_shape=jax.ShapeDtypeStruct(h.shape, F32),
        input_output_aliases={4: 0},
        compiler_params=_params("arbitrary"),
        name="hgrn_out",
    )(o_f, o_b, proj, norm_w.reshape(1, -1), h, mod, w_out.astype(BF16))


MLA_QK_PAD = LANES


def _mla_proj_kernel(h_ref, mod_ref, gain_ref, wd_ref, qn_ref, kvn_ref, wuq_ref, wukv_ref,
                     cos_ref, sa_ref, sb_ref, q_ref, k_ref, v_ref):
    H, P = MLA_HEADS, MLA_QK_PAD
    a = _pre(h_ref[...], gain_ref[...], mod_ref[0], 3).astype(BF16)
    dn = _dot(a, wd_ref[...])
    cos, sa, sb = cos_ref[...], sa_ref[...], sb_ref[...]
    f = MLA_ROPE // 4
    cq = _rmsnorm(dn[:, :MLA_Q_LORA], qn_ref[...]).astype(BF16)
    q = _dot(cq, wuq_ref[...])
    for hd in range(H):
        q_ref[:, hd * P:(hd + 1) * P] = _rope(q[:, hd * P:(hd + 1) * P], cos, sa, sb, f).astype(BF16)
    ckv = _rmsnorm(dn[:, MLA_Q_LORA:MLA_Q_LORA + MLA_KV_LORA], kvn_ref[...]).astype(BF16)
    kv = _dot(ckv, wukv_ref[...])
    kr = _rope(dn[:, MLA_Q_LORA + MLA_KV_LORA:], cos, sa, sb, f)
    for hd in range(H):
        k_ref[:, hd * P:(hd + 1) * P] = (kv[:, hd * P:(hd + 1) * P] + kr).astype(BF16)
    lane = lax.broadcasted_iota(jnp.int32, (1, H * P), 1)
    ones_half = jnp.where((lane & (P - 1)) >= MLA_V_DIM, 1.0, 0.0)
    v_ref[...] = (kv[:, H * P:] + ones_half).astype(BF16)


def _mla_attn_kernel(q_ref, kl_ref, vl_ref, kx_ref, vx_ref, h_ref, mod_ref, wo_ref, o_ref, *, nq, with_ctx):
    i = pl.program_id(1)
    P, VD = MLA_QK_PAD, MLA_V_DIM
    c = ((MLA_NOPE + MLA_ROPE) ** -0.5) * LOG2_E

    def run(kv_pairs):
        def scores(hd):
            qh = q_ref[:, hd * P:(hd + 1) * P]
            return [_dot_nt(qh, k_ref[:, hd * P:(hd + 1) * P]) for k_ref, _ in kv_pairs]

        heads = []
        ss = scores(0)
        for hd in range(MLA_HEADS):
            nxt = scores(hd + 1) if hd + 1 < MLA_HEADS else None
            m = functools.reduce(jnp.maximum, [jnp.max(s, axis=-1, keepdims=True) for s in ss])
            mc = m * c
            acc = None
            for s, (_, v_ref) in zip(ss, kv_pairs):
                t = _dot(jnp.exp2(s * c - mc).astype(BF16), v_ref[:, hd * P:(hd + 1) * P])
                acc = t if acc is None else acc + t
            heads.append(acc[:, :VD] * (1.0 / acc[:, VD:]))
            ss = nxt
        o = jnp.concatenate(heads, axis=-1).astype(BF16)
        m = mod_ref[0]
        o_ref[...] = h_ref[...] + m[5:6] * _dot(o, wo_ref[...])

    if not with_ctx:
        run([(kl_ref, vl_ref), (kx_ref, vx_ref)])
        return

    @pl.when(i < nq)
    def _():
        run([(kl_ref, vl_ref), (kx_ref, vx_ref)])

    @pl.when(i >= nq)
    def _():
        run([(kx_ref, vx_ref)])


def _mla_mixer(lay, h, mod, gain, w_down, q_norm_w, kv_norm_w, w_uq, w_ukv, w_out, with_ctx_out):
    D = h.shape[1]
    B, S, C = lay.B, lay.S, lay.C
    H, P, VD = MLA_HEADS, MLA_QK_PAD, MLA_V_DIM
    qk = MLA_NOPE + MLA_ROPE
    lora = MLA_Q_LORA + MLA_KV_LORA
    kr_cols = jnp.zeros((D, P), F32).at[:, MLA_NOPE:qk].set(w_down[:, lora:])
    wd = jnp.concatenate([w_down[:, :lora], kr_cols], axis=1).astype(BF16)
    wuq = jnp.pad(w_uq.reshape(MLA_Q_LORA, H, qk), ((0, 0), (0, 0), (0, P - qk))).reshape(MLA_Q_LORA, H * P)
    ukv = w_ukv.reshape(MLA_KV_LORA, H, MLA_NOPE + VD)
    wkn = jnp.pad(ukv[..., :MLA_NOPE], ((0, 0), (0, 0), (0, P - MLA_NOPE))).reshape(MLA_KV_LORA, H * P)
    wv = jnp.pad(ukv[..., MLA_NOPE:], ((0, 0), (0, 0), (0, P - VD))).reshape(MLA_KV_LORA, H * P)
    wukv = jnp.concatenate([wkn, wv], axis=1)

    tm = lay.tile(512)
    midx, pidx = lay.mod_index(tm), lay.pos_index(tm)
    cos, sa, sb = _rope_tables(S, MLA_ROPE, tm, MLA_NOPE)
    tab = pl.BlockSpec((tm, LANES), lambda t: (pidx(t), 0))
    full = lambda shape: pl.BlockSpec(shape, lambda t: (0, 0))
    q, k, v = pl.pallas_call(
        _mla_proj_kernel,
        grid=(lay.n_all // tm,),
        in_specs=[pl.BlockSpec((tm, D), lambda t: (t, 0)),
                  pl.BlockSpec((1, N_MOD, D), lambda t: (midx(t), 0, 0)),
                  full((1, D)), full((D, lora + P)), full((1, MLA_Q_LORA)), full((1, MLA_KV_LORA)),
                  full((MLA_Q_LORA, H * P)), full((MLA_KV_LORA, 2 * H * P)),
                  tab, tab, tab],
        out_specs=[pl.BlockSpec((tm, H * P), lambda t: (t, 0))] * 3,
        out_shape=[jax.ShapeDtypeStruct((lay.n_all, H * P), BF16)] * 3,
        compiler_params=_params("arbitrary"),
        name="mla_proj",
    )(h, mod, gain.reshape(1, D), wd, q_norm_w.reshape(1, -1), kv_norm_w.reshape(1, -1),
      wuq.astype(BF16), wukv.astype(BF16), cos, sa, sb)

    tq = _divisor(math.gcd(S, C), 256)
    q_index, nq, nc, lat, ctx = _attn_specs(lay, tq, [H * P, H * P])
    row = pl.BlockSpec((tq, D), lambda b, i: (q_index(b, i), 0))
    return pl.pallas_call(
        functools.partial(_mla_attn_kernel, nq=nq, with_ctx=with_ctx_out),
        grid=(B, nq + (nc if with_ctx_out else 0)),
        in_specs=[pl.BlockSpec((tq, H * P), lambda b, i: (q_index(b, i), 0)),
                  *lat, *ctx, row,
                  pl.BlockSpec((1, N_MOD, D), lambda b, i: (jnp.where(i < nq, b, B), 0, 0)),
                  pl.BlockSpec((H * VD, D), lambda b, i: (0, 0))],
        out_specs=row,
        out_shape=jax.ShapeDtypeStruct(h.shape, F32),
        input_output_aliases={5: 0},
        compiler_params=_params("arbitrary", "arbitrary"),
        name="mla_attn",
    )(q, k, v, k, v, h, mod, w_out.astype(BF16))


def kernel(x, c, ctx, c_ctx, ada_w, ada_b, norm_w, final_norm_w, ffn_w_gate, ffn_w_up, ffn_w_down,
           gqa_w_in, gqa_w_out, gqa_sinks, diff_w_in, diff_w_out, diff_lambda, diff_subln_w,
           hgrn_w_in, hgrn_w_out, hgrn_norm_w, hgrn_lower_bounds,
           mla_w_down, mla_q_norm_w, mla_kv_norm_w, mla_w_uq, mla_w_ukv, mla_w_out):
    B, S, D = x.shape
    C = ctx.shape[1]
    lay = _Layout(B, S, C)
    depth = ada_w.shape[0]

    rows = -(-(B + 1) // 8) * 8
    cc = jnp.zeros((rows, D), F32).at[:B].set(c).at[B].set(c_ctx)
    mods = _ada_table(cc, ada_w, ada_b).reshape(depth, rows, N_MOD, D)

    wg, wu, wd = ffn_w_gate.astype(BF16), ffn_w_up.astype(BF16), ffn_w_down.astype(BF16)
    h = (x.reshape(B * S, D), ctx.reshape(B * C, D))
    for i in range(depth):
        kind, j = i % 4, i // 4
        last = i == depth - 1
        mod = mods[i]
        h = _ffn(lay, h, mod, norm_w[i, 0], final_norm_w, wg, wu, wd, i, 0)
        if kind == 0:
            h = _gqa_mixer(lay, h, mod, norm_w[i, 1], gqa_w_in[j], gqa_w_out[j], gqa_sinks[j], not last)
        elif kind == 1:
            lambda_init = 0.8 - 0.6 * math.exp(-0.3 * i)
            h = _diff_mixer(lay, h, mod, norm_w[i, 1], diff_w_in[j], diff_w_out[j], diff_lambda[j],
                            diff_subln_w[j], lambda_init, not last)
        elif kind == 2:
            h = _hgrn_mixer(lay, h, mod, norm_w[i, 1], hgrn_w_in[j], hgrn_w_out[j], hgrn_norm_w[j],
                            hgrn_lower_bounds, i)
        else:
            h = _mla_mixer(lay, h, mod, norm_w[i, 1], mla_w_down[j], mla_q_norm_w[j], mla_kv_norm_w[j],
                           mla_w_uq[j], mla_w_ukv[j], mla_w_out[j], not last)
        h = _ffn(lay, h, mod, norm_w[i, 2], final_norm_w, wg, wu, wd, i, 1, lat_only=last, final_norm=last)
    return h[:B * S].reshape(B, S, D)
```

```python
import functools
import math

import jax
import jax.numpy as jnp
from jax import lax
from jax.experimental import pallas as pl
from jax.experimental.pallas import tpu as pltpu

F32 = jnp.float32
BF16 = jnp.bfloat16

D_MODEL = 1024
DEPTH = 4
GRID_W = 64
N_MOD = 9
ROPE_BASE = 10000.0
EPS = 1e-6
NEG_INF = -1e30
D_FF = 2816

GQA_HEADS = 16
GQA_KV_HEADS = 4
GQA_HEAD_DIM = 64
GQA_WINDOW = 128
GQA_BLOCK = 128

DIFF_HEADS = 8
DIFF_HEAD_DIM = 64

HGRN_HEADS = 8
HGRN_KEY_DIM = 128
HGRN_VAL_DIM = D_MODEL // HGRN_HEADS
HGRN_CHUNK = 64

MLA_HEADS = 16
MLA_Q_LORA = 256
MLA_KV_LORA = 256
MLA_NOPE = 64
MLA_ROPE = 32
MLA_V_DIM = 64

LOG2_E = 1.4426950408889634
LANES = 128
VMEM_LIMIT = 56 * 1024 * 1024


def _dot(a, b):
    return jnp.dot(a, b, preferred_element_type=F32)


def _dot_nt(a, b):
    return lax.dot_general(a, b, (((1,), (1,)), ((), ())), preferred_element_type=F32)


def _dot_tn(a, b):
    return lax.dot_general(a, b, (((0,), (0,)), ((), ())), preferred_element_type=F32)


def _params(*sem):
    return pltpu.CompilerParams(dimension_semantics=sem, vmem_limit_bytes=VMEM_LIMIT)


def _divisor(n, pref):
    t = min(n, pref)
    while n % t:
        t -= 8
    return t


class _Layout:
    def __init__(self, B, S, C):
        self.B, self.S, self.C = B, S, C
        self.n_lat = B * S
        self.n_all = B * S + B * C

    def tile(self, pref):
        return _divisor(math.gcd(self.S, self.B * self.C), pref)

    def n_tiles(self, tm, lat_only=False):
        return (self.n_lat if lat_only else self.n_all) // tm

    def mod_index(self, tm):
        n_lat_tiles, per_batch, B = self.n_lat // tm, self.S // tm, self.B
        return lambda t: jnp.where(t < n_lat_tiles, t // per_batch, B)

    def pos_index(self, tm):
        n_lat_tiles, per_batch = self.n_lat // tm, self.S // tm
        return lambda t: jnp.where(t < n_lat_tiles, t % per_batch, per_batch)


def _rmsnorm(x, w):
    return (x * lax.rsqrt(jnp.mean(x * x, axis=-1, keepdims=True) + EPS)) * w


def _pre(h, gain, m, off):
    return _rmsnorm(h, gain) * (1.0 + m[off + 1:off + 2]) + m[off:off + 1]


def _silu(x):
    return x * jax.nn.sigmoid(x)


MXU_COLS = 2 * LANES


def _chunked_proj(a, w_ref, emit):
    n = w_ref.shape[1] // MXU_COLS
    cur = _dot(a, w_ref[:, :MXU_COLS])
    for c in range(n):
        nxt = _dot(a, w_ref[:, (c + 1) * MXU_COLS:(c + 2) * MXU_COLS]) if c + 1 < n else None
        emit(c * MXU_COLS, cur)
        cur = nxt


def _rope(x, cos, sa, sb, shift):
    return x * cos + pltpu.roll(x, LANES - shift, 1) * sa + pltpu.roll(x, shift, 1) * sb


def _ada_kernel(c_ref, w_ref, b_ref, o_ref):
    sc = _silu(c_ref[...]).astype(BF16)
    o_ref[0] = _dot(sc, w_ref[0].astype(BF16)) + b_ref[0]


def _ada_table(cc, ada_w, ada_b):
    rows, D = cc.shape
    n_out = ada_w.shape[2]
    tn = n_out // 8
    return pl.pallas_call(
        _ada_kernel,
        grid=(DEPTH, n_out // tn),
        in_specs=[pl.BlockSpec((rows, D), lambda i, j: (0, 0)),
                  pl.BlockSpec((1, D, tn), lambda i, j: (i, 0, j)),
                  pl.BlockSpec((1, 1, tn), lambda i, j: (i, 0, j))],
        out_specs=pl.BlockSpec((1, rows, tn), lambda i, j: (i, 0, j)),
        out_shape=jax.ShapeDtypeStruct((DEPTH, rows, n_out), F32),
        compiler_params=_params("arbitrary", "arbitrary"),
        name="ada_table",
    )(cc, ada_w, ada_b.reshape(DEPTH, 1, n_out))


FFN_CHUNK = 2 * LANES


def _ffn_kernel(*refs, off, final_norm, n_lat_tiles, mixed):
    refs = list(refs)
    if n_lat_tiles is None:
        h = refs.pop(0)[...]
    else:
        hx_ref, hc_ref = refs.pop(0), refs.pop(0)
        h = jnp.where(pl.program_id(0) < n_lat_tiles, hx_ref[...], hc_ref[...])
    if mixed:
        mix_ref, wo_ref = refs.pop(0), refs.pop(0)
    mod_ref, gain_ref, fin_ref, wg_ref, wu_ref, wd_ref, o_ref, a_ref = refs
    m = mod_ref[0]
    if mixed:
        h = h + m[5:6] * _dot(mix_ref[...], wo_ref[...])
    xm = _pre(h, gain_ref[...], m, off).astype(BF16)
    n = D_FF // FFN_CHUNK

    def gate_up(c):
        cols = slice(c * FFN_CHUNK, (c + 1) * FFN_CHUNK)
        return _dot(xm, wg_ref[:, cols]), _dot(xm, wu_ref[:, cols])

    cur = gate_up(0)
    for c in range(n):
        nxt = gate_up(c + 1) if c + 1 < n else None
        g, u = cur
        a_ref[:, c * FFN_CHUNK:(c + 1) * FFN_CHUNK] = (_silu(g) * u).astype(BF16)
        cur = nxt
    out = h + (0.5 * m[off + 2:off + 3]) * _dot(a_ref[...], wd_ref[...])
    if final_norm:
        out = _rmsnorm(out, fin_ref[...])
    o_ref[...] = out


def _ffn(lay, h, mod, gain, fin_w, wg, wu, wd, layer, half, lat_only=False, final_norm=False, mix=None):
    split = isinstance(h, tuple)
    D = wg.shape[2]
    tm = lay.tile(512)
    n_rows = lay.n_lat if lat_only else lay.n_all
    midx = lay.mod_index(tm)
    n_lat_tiles = lay.n_lat // tm
    const = lambda t: (0, 0)
    one = pl.Buffered(1)
    weight = lambda shape: pl.BlockSpec((None, None) + shape, lambda t: (layer, half, 0, 0), pipeline_mode=one)
    if split:
        h_specs = [pl.BlockSpec((tm, D), lambda t: (jnp.minimum(t, n_lat_tiles - 1), 0)),
                   pl.BlockSpec((tm, D), lambda t: (jnp.maximum(t - n_lat_tiles, 0), 0))]
        h_args = list(h)
    else:
        h_specs = [pl.BlockSpec((tm, D), lambda t: (t, 0))]
        h_args = [h]
    if mix is not None:
        o, w_out = mix
        h_specs += [pl.BlockSpec((tm, o.shape[1]), lambda t: (t, 0)),
                    pl.BlockSpec(w_out.shape, const, pipeline_mode=one)]
        h_args += [o, w_out.astype(BF16)]
    return pl.pallas_call(
        functools.partial(_ffn_kernel, off=6 * half, final_norm=final_norm,
                          n_lat_tiles=n_lat_tiles if split else None, mixed=mix is not None),
        grid=(n_rows // tm,),
        in_specs=h_specs + [pl.BlockSpec((1, N_MOD, D), lambda t: (midx(t), 0, 0)),
                            pl.BlockSpec((1, D), const),
                            pl.BlockSpec((1, D), const),
                            weight((D, D_FF)), weight((D, D_FF)), weight((D_FF, D))],
        out_specs=pl.BlockSpec((tm, D), lambda t: (t, 0)),
        out_shape=jax.ShapeDtypeStruct((n_rows, D), F32),
        scratch_shapes=[pltpu.VMEM((tm, D_FF), BF16)],
        compiler_params=_params("arbitrary"),
        name="ffn",
    )(*h_args, mod, gain.reshape(1, D), fin_w.reshape(1, D), wg, wu, wd)


def _rope_tables(S, rot_dim, pad_rows, lane_off):
    rows = S // GRID_W
    row = jnp.repeat(jnp.arange(rows, dtype=F32), GRID_W)
    col = jnp.tile(jnp.arange(GRID_W, dtype=F32), rows)
    axis_dim = rot_dim // 2
    inv_freq = ROPE_BASE ** (-jnp.arange(0, axis_dim, 2, dtype=F32) / axis_dim)
    ang_r = row[:, None] * inv_freq[None, :]
    ang_c = col[:, None] * inv_freq[None, :]
    ang = jnp.concatenate([ang_r, ang_r, ang_c, ang_c], axis=-1)
    cos, sin = jnp.cos(ang), jnp.sin(ang)
    f = rot_dim // 4
    first = (jnp.arange(rot_dim) % (2 * f)) < f
    sa = jnp.where(first[None, :], -sin, 0.0)
    sb = jnp.where(first[None, :], 0.0, sin)
    period = 64 if lane_off + rot_dim <= 64 else LANES

    def widen(t, fill):
        blk = jnp.full((S, period), fill, F32).at[:, lane_off:lane_off + rot_dim].set(t)
        blk = jnp.tile(blk, (1, LANES // period))
        return jnp.concatenate([blk, jnp.full((pad_rows, LANES), fill, F32)], axis=0)

    return widen(cos, 1.0), widen(sa, 0.0), widen(sb, 0.0)


def _attn_specs(lay, tq, widths):
    S, C, B = lay.S, lay.C, lay.B
    nq, nc = S // tq, C // tq
    lat_blocks = lay.n_lat // C

    def q_index(b, i):
        return jnp.where(i < nq, b * nq + i, B * nq + b * nc + (i - nq))

    one = pl.Buffered(1)
    lat = [pl.BlockSpec((S, w), lambda b, i: (b, 0), pipeline_mode=one) for w in widths]
    ctx = [pl.BlockSpec((C, w), lambda b, i: (lat_blocks + b, 0), pipeline_mode=one) for w in widths]
    return q_index, nq, nc, lat, ctx


def _gqa_proj_kernel(h_ref, mod_ref, gain_ref, w_ref, cos_ref, sa_ref, sb_ref, q_ref, k_ref, v_ref):
    qd, kd = GQA_HEADS * GQA_HEAD_DIM, GQA_KV_HEADS * GQA_HEAD_DIM
    a = _pre(h_ref[...], gain_ref[...], mod_ref[0], 3).astype(BF16)
    cos, sa, sb = cos_ref[...], sa_ref[...], sb_ref[...]
    f = GQA_HEAD_DIM // 4
    scale = GQA_HEAD_DIM ** -0.5
    lane = lax.broadcasted_iota(jnp.int32, (1, MXU_COLS), 1)
    ones_half = jnp.where((lane & (2 * GQA_HEAD_DIM - 1)) >= GQA_HEAD_DIM, 1.0, 0.0)

    def emit(col, tile):
        if col >= qd + kd:
            v_ref[:, col - qd - kd:col - qd - kd + MXU_COLS] = (tile + ones_half).astype(BF16)
            return
        for j in range(MXU_COLS // LANES):
            x = _rope(tile[:, j * LANES:(j + 1) * LANES], cos, sa, sb, f)
            lo = col + j * LANES
            if lo < qd:
                q_ref[:, lo:lo + LANES] = (x * scale).astype(BF16)
            else:
                k_ref[:, lo - qd:lo - qd + LANES] = x.astype(BF16)

    _chunked_proj(a, w_ref, emit)


def _gqa_attn_kernel(sink_ref, q_ref, kp_ref, kc_ref, kn_ref, vp_ref, vc_ref, vn_ref, kx_ref, vx_ref,
                     o_ref, kcat_ref, vcat_ref, *, nq):
    i = pl.program_id(1)
    blk = GQA_BLOCK
    G = GQA_HEADS // GQA_KV_HEADS
    HD = GQA_HEAD_DIM
    VW = 2 * HD
    span = 3 * blk

    def run(k_ref, v_ref, masks):
        kvs = range(GQA_KV_HEADS)
        qs = [jnp.concatenate([q_ref[:, (G * g + n) * HD:(G * g + n + 1) * HD] for n in range(G)], axis=0)
              for g in kvs]
        sinks = [jnp.concatenate([jnp.full((blk, 1), sink_ref[G * g + n], F32) for n in range(G)], axis=0)
                 for g in kvs]
        ss = [_dot_nt(qs[g], k_ref[:, g * HD:(g + 1) * HD]) for g in kvs]
        if masks is not None:
            prev_ok, next_ok = masks
            ss = [jnp.concatenate([jnp.where(prev_ok, s[:, :blk], NEG_INF), s[:, blk:2 * blk],
                                   jnp.where(next_ok, s[:, 2 * blk:span], NEG_INF), s[:, span:]], axis=1)
                  for s in ss]
        ms = [jnp.maximum(jnp.max(ss[g], axis=-1, keepdims=True), sinks[g]) for g in kvs]
        accs = [_dot(jnp.exp(ss[g] - ms[g]).astype(BF16), v_ref[:, g * VW:(g + 1) * VW]) for g in kvs]
        outs = [accs[g][:, :HD] * (1.0 / (accs[g][:, HD:] + jnp.exp(sinks[g] - ms[g]))) for g in kvs]
        heads = [outs[g][n * blk:(n + 1) * blk] for g in kvs for n in range(G)]
        o_ref[...] = jnp.concatenate(heads, axis=-1).astype(BF16)

    @pl.when(i < nq)
    def _():
        for n, (k_ref, v_ref) in enumerate([(kp_ref, vp_ref), (kc_ref, vc_ref), (kn_ref, vn_ref)]):
            kcat_ref[n * blk:(n + 1) * blk] = k_ref[...]
            vcat_ref[n * blk:(n + 1) * blk] = v_ref[...]
        kcat_ref[span:] = kx_ref[...]
        vcat_ref[span:] = vx_ref[...]
        row = lax.broadcasted_iota(jnp.int32, (G * blk, blk), 0) & (blk - 1)
        col = lax.broadcasted_iota(jnp.int32, (G * blk, blk), 1)
        prev_ok = (row + blk - col <= GQA_WINDOW) & (i >= 1)
        next_ok = (col + blk - row <= GQA_WINDOW) & (i + 1 < nq)
        run(kcat_ref, vcat_ref, (prev_ok, next_ok))

    @pl.when(i >= nq)
    def _():
        run(kx_ref, vx_ref, None)


def _gqa_mixer(lay, h, mod, gain, w_in, sinks, with_ctx_out):
    D = h.shape[1]
    B, S, C = lay.B, lay.S, lay.C
    qd, kd = GQA_HEADS * GQA_HEAD_DIM, GQA_KV_HEADS * GQA_HEAD_DIM
    w_v = jnp.pad(w_in[:, qd + kd:].reshape(D, GQA_KV_HEADS, GQA_HEAD_DIM), ((0, 0), (0, 0), (0, GQA_HEAD_DIM)))
    w_pad = jnp.concatenate([w_in[:, :qd + kd], w_v.reshape(D, 2 * kd)], axis=1)
    tm = lay.tile(512)
    midx, pidx = lay.mod_index(tm), lay.pos_index(tm)
    cos, sa, sb = _rope_tables(S, GQA_HEAD_DIM, tm, 0)
    tab = pl.BlockSpec((tm, LANES), lambda t: (pidx(t), 0))
    q, k, v = pl.pallas_call(
        _gqa_proj_kernel,
        grid=(lay.n_all // tm,),
        in_specs=[pl.BlockSpec((tm, D), lambda t: (t, 0)),
                  pl.BlockSpec((1, N_MOD, D), lambda t: (midx(t), 0, 0)),
                  pl.BlockSpec((1, D), lambda t: (0, 0)),
                  pl.BlockSpec((D, qd + 3 * kd), lambda t: (0, 0)),
                  tab, tab, tab],
        out_specs=[pl.BlockSpec((tm, qd), lambda t: (t, 0)),
                   pl.BlockSpec((tm, kd), lambda t: (t, 0)),
                   pl.BlockSpec((tm, 2 * kd), lambda t: (t, 0))],
        out_shape=[jax.ShapeDtypeStruct((lay.n_all, qd), BF16),
                   jax.ShapeDtypeStruct((lay.n_all, kd), BF16),
                   jax.ShapeDtypeStruct((lay.n_all, 2 * kd), BF16)],
        compiler_params=_params("arbitrary"),
        name="gqa_proj",
    )(h, mod, gain.reshape(1, D), w_pad.astype(BF16), cos, sa, sb)

    blk = GQA_BLOCK
    assert S % blk == 0 and C % blk == 0
    nq, nc = S // blk, C // blk
    n_i = nq + (nc if with_ctx_out else 0)
    lat_blocks = lay.n_lat // C

    def q_index(b, i):
        return jnp.where(i < nq, b * nq + i, B * nq + b * nc + (i - nq))

    def win(delta):
        return lambda b, i: (b * nq + jnp.clip(i + delta, 0, nq - 1), 0)

    def windows(width):
        return [pl.BlockSpec((blk, width), win(d)) for d in (-1, 0, 1)]

    def ctx_spec(width):
        return pl.BlockSpec((C, width), lambda b, i: (lat_blocks + b, 0))

    row = pl.BlockSpec((blk, qd), lambda b, i: (q_index(b, i), 0))
    return pl.pallas_call(
        functools.partial(_gqa_attn_kernel, nq=nq),
        grid=(B, n_i),
        in_specs=[pl.BlockSpec(memory_space=pltpu.SMEM), row,
                  *windows(kd), *windows(2 * kd), ctx_spec(kd), ctx_spec(2 * kd)],
        out_specs=row,
        out_shape=jax.ShapeDtypeStruct((lay.n_all if with_ctx_out else lay.n_lat, qd), BF16),
        scratch_shapes=[pltpu.VMEM((3 * blk + C, kd), BF16), pltpu.VMEM((3 * blk + C, 2 * kd), BF16)],
        compiler_params=_params("arbitrary", "arbitrary"),
        name="gqa_attn",
    )(sinks.astype(F32), q, k, k, k, v, v, v, k, v)


def _diff_proj_kernel(h_ref, mod_ref, gain_ref, w_ref, cos_ref, sa_ref, sb_ref, q_ref, k_ref, v_ref):
    qk = DIFF_HEADS * 2 * DIFF_HEAD_DIM
    a = _pre(h_ref[...], gain_ref[...], mod_ref[0], 3).astype(BF16)
    cos, sa, sb = cos_ref[...], sa_ref[...], sb_ref[...]
    f = DIFF_HEAD_DIM // 4
    scale = DIFF_HEAD_DIM ** -0.5
    vw = 2 * DIFF_HEAD_DIM
    ones = jnp.ones((a.shape[0], vw), BF16)

    def emit(col, tile):
        for j in range(MXU_COLS // LANES):
            lo = col + j * LANES
            x = tile[:, j * LANES:(j + 1) * LANES]
            if lo < qk:
                q_ref[:, lo:lo + LANES] = (_rope(x, cos, sa, sb, f) * scale).astype(BF16)
            elif lo < 2 * qk:
                k_ref[:, lo - qk:lo - qk + LANES] = _rope(x, cos, sa, sb, f).astype(BF16)
            else:
                hd = (lo - 2 * qk) // vw
                v_ref[:, 2 * hd * vw:(2 * hd + 1) * vw] = x.astype(BF16)
                v_ref[:, (2 * hd + 1) * vw:(2 * hd + 2) * vw] = ones

    _chunked_proj(a, w_ref, emit)


def _diff_attn_kernel(lam_ref, sub_ref, q_ref, kl_ref, vl_ref, kx_ref, vx_ref, o_ref, *, nq, lambda_init):
    i = pl.program_id(1)
    HD = DIFF_HEAD_DIM
    VW = 4 * HD
    lp = lam_ref[...]
    lam = (jnp.exp(jnp.sum(lp[0:1] * lp[1:2], keepdims=True))
           - jnp.exp(jnp.sum(lp[2:3] * lp[3:4], keepdims=True)) + lambda_init)

    def run(kv_pairs):
        def scores(h):
            los = [(2 * h + j) * HD for j in range(2)]
            return [[_dot_nt(q_ref[:, lo:lo + HD], k_ref[:, lo:lo + HD]) for k_ref, _ in kv_pairs] for lo in los]

        heads = []
        ss = scores(0)
        for h in range(DIFF_HEADS):
            nxt = scores(h + 1) if h + 1 < DIFF_HEADS else None
            maps = []
            for parts in ss:
                m = functools.reduce(jnp.maximum, [jnp.max(s, axis=-1, keepdims=True) for s in parts])
                acc = None
                for s, (_, v_ref) in zip(parts, kv_pairs):
                    t = _dot(jnp.exp(s - m).astype(BF16), v_ref[:, h * VW:(h + 1) * VW])
                    acc = t if acc is None else acc + t
                maps.append(acc[:, :VW // 2] * (1.0 / acc[:, VW // 2:]))
            o = maps[0] - lam * maps[1]
            heads.append(_rmsnorm(o, sub_ref[...]) * (1.0 - lambda_init))
            ss = nxt
        o_ref[...] = jnp.concatenate(heads, axis=-1).astype(BF16)

    @pl.when(i < nq)
    def _():
        run([(kl_ref, vl_ref), (kx_ref, vx_ref)])

    @pl.when(i >= nq)
    def _():
        run([(kx_ref, vx_ref)])


def _diff_mixer(lay, h, mod, gain, w_in, lam_params, subln_w, lambda_init, with_ctx_out):
    D = h.shape[1]
    B, S, C = lay.B, lay.S, lay.C
    qk = DIFF_HEADS * 2 * DIFF_HEAD_DIM
    tm = lay.tile(512)
    midx, pidx = lay.mod_index(tm), lay.pos_index(tm)
    cos, sa, sb = _rope_tables(S, DIFF_HEAD_DIM, tm, 0)
    tab = pl.BlockSpec((tm, LANES), lambda t: (pidx(t), 0))
    out = jax.ShapeDtypeStruct((lay.n_all, qk), BF16)
    q, k, v = pl.pallas_call(
        _diff_proj_kernel,
        grid=(lay.n_all // tm,),
        in_specs=[pl.BlockSpec((tm, D), lambda t: (t, 0)),
                  pl.BlockSpec((1, N_MOD, D), lambda t: (midx(t), 0, 0)),
                  pl.BlockSpec((1, D), lambda t: (0, 0)),
                  pl.BlockSpec((D, 3 * qk), lambda t: (0, 0)),
                  tab, tab, tab],
        out_specs=[pl.BlockSpec((tm, qk), lambda t: (t, 0))] * 2 + [pl.BlockSpec((tm, 2 * qk), lambda t: (t, 0))],
        out_shape=[out, out, jax.ShapeDtypeStruct((lay.n_all, 2 * qk), BF16)],
        compiler_params=_params("arbitrary"),
        name="diff_proj",
    )(h, mod, gain.reshape(1, D), w_in.astype(BF16), cos, sa, sb)

    tq = _divisor(math.gcd(S, C), 256)
    q_index, nq, nc, lat, ctx = _attn_specs(lay, tq, [qk, 2 * qk])
    row = pl.BlockSpec((tq, qk), lambda b, i: (q_index(b, i), 0))
    return pl.pallas_call(
        functools.partial(_diff_attn_kernel, nq=nq, lambda_init=lambda_init),
        grid=(B, nq + (nc if with_ctx_out else 0)),
        in_specs=[pl.BlockSpec((4, DIFF_HEAD_DIM), lambda b, i: (0, 0)),
                  pl.BlockSpec((1, 2 * DIFF_HEAD_DIM), lambda b, i: (0, 0)),
                  row, *lat, *ctx],
        out_specs=row,
        out_shape=jax.ShapeDtypeStruct((lay.n_all if with_ctx_out else lay.n_lat, qk), BF16),
        compiler_params=_params("arbitrary", "arbitrary"),
        name="diff_attn",
    )(lam_params.astype(F32), subln_w.reshape(1, -1), q, k, v, k, v)


def _hgrn_proj_kernel(h_ref, mod_ref, gain_ref, lb_ref, w_ref, o_ref, *, layer):
    KD = HGRN_HEADS * HGRN_KEY_DIM
    W = 2 * LANES
    a = _pre(h_ref[...], gain_ref[...], mod_ref[0], 3).astype(BF16)
    raw = lb_ref[...]
    e = jnp.exp(raw - jnp.max(raw, axis=0, keepdims=True))
    soft = e * (1.0 / jnp.sum(e, axis=0, keepdims=True))
    lb = jnp.sum(soft[1:layer + 1], axis=0, keepdims=True) if layer else jnp.zeros_like(soft[0:1])
    n = w_ref.shape[1] // W

    def proj(c):
        return _dot(a, w_ref[:, c * W:(c + 1) * W])

    cur = proj(0)
    for c in range(n):
        nxt = proj(c + 1) if c + 1 < n else None
        group, col = divmod(c * W, KD)
        if group in (0, 4):
            out = _silu(cur)
        elif group in (1, 2):
            lbc = lb[:, col:col + W]
            out = lbc + (1.0 - lbc) * jax.nn.sigmoid(cur)
        else:
            out = cur
        o_ref[:, c * W:(c + 1) * W] = out
        cur = nxt


def _split3(x):
    hi = x.astype(BF16)
    r = x - hi.astype(F32)
    mid = r.astype(BF16)
    lo = (r - mid.astype(F32)).astype(BF16)
    return hi, mid, lo


def _hgrn_scan_kernel(qf_ref, ff_ref, vf_ref, qb_ref, fb_ref, vb_ref, of_ref, ob_ref, st_ref, *, n_chunks):
    step = pl.program_id(1)
    C = HGRN_CHUNK
    DK, DV, H = HGRN_KEY_DIM, HGRN_VAL_DIM, HGRN_HEADS

    @pl.when(step == 0)
    def _():
        st_ref[...] = jnp.zeros_like(st_ref)

    r = lax.broadcasted_iota(jnp.int32, (C, C), 0)
    c = lax.broadcasted_iota(jnp.int32, (C, C), 1)
    rows = lax.broadcasted_iota(jnp.int32, (C, 1), 0)
    ks = [slice(h * DK, (h + 1) * DK) for h in range(H)]
    vs = [slice(h * DV, (h + 1) * DV) for h in range(H)]
    dirs = [(qf_ref, ff_ref, vf_ref, of_ref, c <= r, rows <= C // 2),
            (qb_ref, fb_ref, vb_ref, ob_ref, c >= r, rows >= C - 1 - C // 2)]

    def prepare(d, n):
        q_ref, f_ref, v_ref, _, tri, mid_rows = dirs[d]
        base = (n if d == 0 else n_chunks - 1 - n) * C
        q = q_ref[base:base + C, :]
        f = f_ref[base:base + C, :]
        v = v_ref[base:base + C, :].astype(BF16)
        logf = jnp.log(f)
        k = 1.0 - f
        tri_b = jnp.where(tri, 1.0, 0.0).astype(BF16)
        g = functools.reduce(lambda a, b: a + b, [_dot(tri_b, part) for part in _split3(logf)])
        g_last = jnp.sum(logf, axis=0, keepdims=True)
        g_mid = jnp.sum(jnp.where(mid_rows, logf, 0.0), axis=0, keepdims=True)
        qa = (q * jnp.exp(g - g_mid)).astype(BF16)
        kb = (k * jnp.exp(g_mid - g)).astype(BF16)
        qg = (q * jnp.exp(g)).astype(BF16)
        kd = (k * jnp.exp(g_last - g)).astype(BF16)
        a = [jnp.where(tri, _dot_nt(qa[:, ks[h]], kb[:, ks[h]]), 0.0).astype(BF16) for h in range(H)]
        intra = [_dot(a[h], v[:, vs[h]]) for h in range(H)]
        update = [_dot_tn(v[:, vs[h]], kd[:, ks[h]]) for h in range(H)]
        return base, qg, jnp.exp(g_last), intra, update

    cur = [prepare(d, 0) for d in range(2)]
    for n in range(n_chunks):
        nxt = [prepare(d, n + 1) for d in range(2)] if n + 1 < n_chunks else None
        for d in range(2):
            base, qg, decay, intra, update = cur[d]
            st = [st_ref[d, vs[h], :] for h in range(H)]
            outs = [intra[h] + _dot_nt(qg[:, ks[h]], st[h].astype(BF16)) for h in range(H)]
            for h in range(H):
                st_ref[d, vs[h], :] = st[h] * decay[:, ks[h]] + update[h]
            dirs[d][3][base:base + C, :] = jnp.concatenate(outs, axis=-1)
        cur = nxt


def _hgrn_out_kernel(of_ref, ob_ref, gate_ref, nw_ref, out_ref):
    DV = HGRN_VAL_DIM
    o = of_ref[...] + ob_ref[...]
    gate = gate_ref[...]
    nw = nw_ref[...]
    ys = [_rmsnorm(o[:, h * DV:(h + 1) * DV], nw) * gate[:, h * DV:(h + 1) * DV] for h in range(HGRN_HEADS)]
    out_ref[...] = jnp.concatenate(ys, axis=-1).astype(BF16)


def _hgrn_mixer(lay, h, mod, gain, w_in, norm_w, lower_bounds, layer):
    D = h.shape[1]
    B, S, C = lay.B, lay.S, lay.C
    KD = HGRN_HEADS * HGRN_KEY_DIM
    tm = lay.tile(512)
    midx = lay.mod_index(tm)
    n_cols = w_in.shape[1] // KD
    proj = pl.pallas_call(
        functools.partial(_hgrn_proj_kernel, layer=layer),
        grid=(lay.n_all // tm,),
        in_specs=[pl.BlockSpec((tm, D), lambda t: (t, 0)),
                  pl.BlockSpec((1, N_MOD, D), lambda t: (midx(t), 0, 0)),
                  pl.BlockSpec((1, D), lambda t: (0, 0)),
                  pl.BlockSpec(lower_bounds.shape, lambda t: (0, 0)),
                  pl.BlockSpec((D, n_cols * KD), lambda t: (0, 0), pipeline_mode=pl.Buffered(1))],
        out_specs=pl.BlockSpec((tm, n_cols * KD), lambda t: (t, 0)),
        out_shape=jax.ShapeDtypeStruct((lay.n_all, n_cols * KD), F32),
        compiler_params=_params("arbitrary"),
        name="hgrn_proj",
    )(h, mod, gain.reshape(1, D), lower_bounds.astype(F32), w_in.astype(BF16))

    ts = C
    n_lat_steps = S // ts
    lat_blocks = lay.n_lat // ts

    def rows(d):
        def index(b, s):
            lat = b * n_lat_steps + (s - 1 if d == 0 else n_lat_steps - s)
            return jnp.where(s == 0, lat_blocks + b, lat)
        return index

    def col(d, j):
        return pl.BlockSpec((ts, KD), lambda b, s: (rows(d)(b, s), j))

    o_f, o_b = pl.pallas_call(
        functools.partial(_hgrn_scan_kernel, n_chunks=ts // HGRN_CHUNK),
        grid=(B, 1 + n_lat_steps),
        in_specs=[col(0, 0), col(0, 1), col(0, 3), col(1, 0), col(1, 2), col(1, 3)],
        out_specs=[col(0, 0), col(1, 0)],
        out_shape=[jax.ShapeDtypeStruct((lay.n_all, KD), F32)] * 2,
        scratch_shapes=[pltpu.VMEM((2, HGRN_HEADS * HGRN_VAL_DIM, HGRN_KEY_DIM), F32)],
        compiler_params=_params("arbitrary", "arbitrary"),
        name="hgrn_scan",
    )(proj, proj, proj, proj, proj, proj)

    tr = lay.tile(1024)
    row = pl.BlockSpec((tr, KD), lambda t: (t, 0))
    return pl.pallas_call(
        _hgrn_out_kernel,
        grid=(lay.n_all // tr,),
        in_specs=[row, row,
                  pl.BlockSpec((tr, KD), lambda t: (t, 4)),
                  pl.BlockSpec((1, HGRN_VAL_DIM), lambda t: (0, 0))],
        out_specs=row,
        out_shape=jax.ShapeDtypeStruct((lay.n_all, KD), BF16),
        compiler_params=_params("arbitrary"),
        name="hgrn_out",
    )(o_f, o_b, proj, norm_w.reshape(1, -1))


MLA_QK_PAD = LANES


def _mla_proj_kernel(h_ref, mod_ref, gain_ref, wd_ref, qn_ref, kvn_ref, wuq_ref, wukv_ref,
                     cos_ref, sa_ref, sb_ref, q_ref, k_ref, v_ref):
    H, P = MLA_HEADS, MLA_QK_PAD
    a = _pre(h_ref[...], gain_ref[...], mod_ref[0], 3).astype(BF16)
    dn = _dot(a, wd_ref[...])
    cos, sa, sb = cos_ref[...], sa_ref[...], sb_ref[...]
    f = MLA_ROPE // 4
    cq = _rmsnorm(dn[:, :MLA_Q_LORA], qn_ref[...]).astype(BF16)
    ckv = _rmsnorm(dn[:, MLA_Q_LORA:MLA_Q_LORA + MLA_KV_LORA], kvn_ref[...]).astype(BF16)
    kr = _rope(dn[:, MLA_Q_LORA + MLA_KV_LORA:], cos, sa, sb, f)
    lane = lax.broadcasted_iota(jnp.int32, (1, P), 1)
    ones_half = jnp.where(lane >= MLA_V_DIM, 1.0, 0.0)

    def emit_q(col, tile):
        for j in range(MXU_COLS // P):
            lo = col + j * P
            q_ref[:, lo:lo + P] = _rope(tile[:, j * P:(j + 1) * P], cos, sa, sb, f).astype(BF16)

    def emit_kv(col, tile):
        for j in range(MXU_COLS // P):
            lo = col + j * P
            x = tile[:, j * P:(j + 1) * P]
            if lo < H * P:
                k_ref[:, lo:lo + P] = (x + kr).astype(BF16)
            else:
                v_ref[:, lo - H * P:lo - H * P + P] = (x + ones_half).astype(BF16)

    _chunked_proj(cq, wuq_ref, emit_q)
    _chunked_proj(ckv, wukv_ref, emit_kv)


def _mla_attn_kernel(q_ref, kl_ref, vl_ref, kx_ref, vx_ref, o_ref, *, nq, with_ctx):
    i = pl.program_id(1)
    P, VD = MLA_QK_PAD, MLA_V_DIM
    c = ((MLA_NOPE + MLA_ROPE) ** -0.5) * LOG2_E

    def run(kv_pairs):
        def scores(hd):
            qh = q_ref[:, hd * P:(hd + 1) * P]
            return [_dot_nt(qh, k_ref[:, hd * P:(hd + 1) * P]) for k_ref, _ in kv_pairs]

        heads = []
        ss = scores(0)
        for hd in range(MLA_HEADS):
            nxt = scores(hd + 1) if hd + 1 < MLA_HEADS else None
            m = functools.reduce(jnp.maximum, [jnp.max(s, axis=-1, keepdims=True) for s in ss])
            mc = m * c
            acc = None
            for s, (_, v_ref) in zip(ss, kv_pairs):
                t = _dot(jnp.exp2(s * c - mc).astype(BF16), v_ref[:, hd * P:(hd + 1) * P])
                acc = t if acc is None else acc + t
            heads.append(acc[:, :VD] * (1.0 / acc[:, VD:]))
            ss = nxt
        o_ref[...] = jnp.concatenate(heads, axis=-1).astype(BF16)

    if not with_ctx:
        run([(kl_ref, vl_ref), (kx_ref, vx_ref)])
        return

    @pl.when(i < nq)
    def _():
        run([(kl_ref, vl_ref), (kx_ref, vx_ref)])

    @pl.when(i >= nq)
    def _():
        run([(kx_ref, vx_ref)])


def _mla_mixer(lay, h, mod, gain, w_down, q_norm_w, kv_norm_w, w_uq, w_ukv, with_ctx_out):
    D = h.shape[1]
    B, S, C = lay.B, lay.S, lay.C
    H, P, VD = MLA_HEADS, MLA_QK_PAD, MLA_V_DIM
    qk = MLA_NOPE + MLA_ROPE
    lora = MLA_Q_LORA + MLA_KV_LORA
    kr_cols = jnp.zeros((D, P), F32).at[:, MLA_NOPE:qk].set(w_down[:, lora:])
    wd = jnp.concatenate([w_down[:, :lora], kr_cols], axis=1).astype(BF16)
    wuq = jnp.pad(w_uq.reshape(MLA_Q_LORA, H, qk), ((0, 0), (0, 0), (0, P - qk))).reshape(MLA_Q_LORA, H * P)
    ukv = w_ukv.reshape(MLA_KV_LORA, H, MLA_NOPE + VD)
    wkn = jnp.pad(ukv[..., :MLA_NOPE], ((0, 0), (0, 0), (0, P - MLA_NOPE))).reshape(MLA_KV_LORA, H * P)
    wv = jnp.pad(ukv[..., MLA_NOPE:], ((0, 0), (0, 0), (0, P - VD))).reshape(MLA_KV_LORA, H * P)
    wukv = jnp.concatenate([wkn, wv], axis=1)

    tm = lay.tile(512)
    midx, pidx = lay.mod_index(tm), lay.pos_index(tm)
    cos, sa, sb = _rope_tables(S, MLA_ROPE, tm, MLA_NOPE)
    tab = pl.BlockSpec((tm, LANES), lambda t: (pidx(t), 0))
    full = lambda shape: pl.BlockSpec(shape, lambda t: (0, 0))
    q, k, v = pl.pallas_call(
        _mla_proj_kernel,
        grid=(lay.n_all // tm,),
        in_specs=[pl.BlockSpec((tm, D), lambda t: (t, 0)),
                  pl.BlockSpec((1, N_MOD, D), lambda t: (midx(t), 0, 0)),
                  full((1, D)), full((D, lora + P)), full((1, MLA_Q_LORA)), full((1, MLA_KV_LORA)),
                  full((MLA_Q_LORA, H * P)), full((MLA_KV_LORA, 2 * H * P)),
                  tab, tab, tab],
        out_specs=[pl.BlockSpec((tm, H * P), lambda t: (t, 0))] * 3,
        out_shape=[jax.ShapeDtypeStruct((lay.n_all, H * P), BF16)] * 3,
        compiler_params=_params("arbitrary"),
        name="mla_proj",
    )(h, mod, gain.reshape(1, D), wd, q_norm_w.reshape(1, -1), kv_norm_w.reshape(1, -1),
      wuq.astype(BF16), wukv.astype(BF16), cos, sa, sb)

    tq = _divisor(math.gcd(S, C), 256)
    q_index, nq, nc, lat, ctx = _attn_specs(lay, tq, [H * P, H * P])
    return pl.pallas_call(
        functools.partial(_mla_attn_kernel, nq=nq, with_ctx=with_ctx_out),
        grid=(B, nq + (nc if with_ctx_out else 0)),
        in_specs=[pl.BlockSpec((tq, H * P), lambda b, i: (q_index(b, i), 0)), *lat, *ctx],
        out_specs=pl.BlockSpec((tq, H * VD), lambda b, i: (q_index(b, i), 0)),
        out_shape=jax.ShapeDtypeStruct((lay.n_all if with_ctx_out else lay.n_lat, H * VD), BF16),
        compiler_params=_params("arbitrary", "arbitrary"),
        name="mla_attn",
    )(q, k, v, k, v)


def kernel(x, c, ctx, c_ctx, ada_w, ada_b, norm_w, final_norm_w, ffn_w_gate, ffn_w_up, ffn_w_down,
           gqa_w_in, gqa_w_out, gqa_sinks, diff_w_in, diff_w_out, diff_lambda, diff_subln_w,
           hgrn_w_in, hgrn_w_out, hgrn_norm_w, hgrn_lower_bounds,
           mla_w_down, mla_q_norm_w, mla_kv_norm_w, mla_w_uq, mla_w_ukv, mla_w_out):
    B, S, D = x.shape
    C = ctx.shape[1]
    lay = _Layout(B, S, C)
    depth = ada_w.shape[0]

    rows = -(-(B + 1) // 8) * 8
    cc = jnp.zeros((rows, D), F32).at[:B].set(c).at[B].set(c_ctx)
    mods = _ada_table(cc, ada_w, ada_b).reshape(depth, rows, N_MOD, D)

    wg, wu, wd = ffn_w_gate.astype(BF16), ffn_w_up.astype(BF16), ffn_w_down.astype(BF16)
    h = (x.reshape(B * S, D), ctx.reshape(B * C, D))
    for i in range(depth):
        kind, j = i % 4, i // 4
        last = i == depth - 1
        mod = mods[i]
        h = _ffn(lay, h, mod, norm_w[i, 0], final_norm_w, wg, wu, wd, i, 0)
        if kind == 0:
            o = _gqa_mixer(lay, h, mod, norm_w[i, 1], gqa_w_in[j], gqa_sinks[j], not last)
            w_out = gqa_w_out[j]
        elif kind == 1:
            lambda_init = 0.8 - 0.6 * math.exp(-0.3 * i)
            o = _diff_mixer(lay, h, mod, norm_w[i, 1], diff_w_in[j], diff_lambda[j], diff_subln_w[j],
                            lambda_init, not last)
            w_out = diff_w_out[j]
        elif kind == 2:
            o = _hgrn_mixer(lay, h, mod, norm_w[i, 1], hgrn_w_in[j], hgrn_norm_w[j], hgrn_lower_bounds, i)
            w_out = hgrn_w_out[j]
        else:
            o = _mla_mixer(lay, h, mod, norm_w[i, 1], mla_w_down[j], mla_q_norm_w[j], mla_kv_norm_w[j],
                           mla_w_uq[j], mla_w_ukv[j], not last)
            w_out = mla_w_out[j]
        h = _ffn(lay, h, mod, norm_w[i, 2], final_norm_w, wg, wu, wd, i, 1, lat_only=last, final_norm=last,
                 mix=(o, w_out))
    return h[:B * S].reshape(B, S, D)
```

```python
import functools
import math

import jax
import jax.numpy as jnp
from jax import lax
from jax.experimental import pallas as pl
from jax.experimental.pallas import tpu as pltpu

F32 = jnp.float32
BF16 = jnp.bfloat16

D_MODEL = 1024
DEPTH = 4
GRID_W = 64
N_MOD = 9
ROPE_BASE = 10000.0
EPS = 1e-6
NEG_INF = -1e30
D_FF = 2816

GQA_HEADS = 16
GQA_KV_HEADS = 4
GQA_HEAD_DIM = 64
GQA_WINDOW = 128
GQA_BLOCK = 128

DIFF_HEADS = 8
DIFF_HEAD_DIM = 64

HGRN_HEADS = 8
HGRN_KEY_DIM = 128
HGRN_VAL_DIM = D_MODEL // HGRN_HEADS
HGRN_CHUNK = 64

MLA_HEADS = 16
MLA_Q_LORA = 256
MLA_KV_LORA = 256
MLA_NOPE = 64
MLA_ROPE = 32
MLA_V_DIM = 64

LOG2_E = 1.4426950408889634
LANES = 128
VMEM_LIMIT = 56 * 1024 * 1024


def _dot(a, b):
    return jnp.dot(a, b, preferred_element_type=F32)


def _dot_nt(a, b):
    return lax.dot_general(a, b, (((1,), (1,)), ((), ())), preferred_element_type=F32)


def _dot_tn(a, b):
    return lax.dot_general(a, b, (((0,), (0,)), ((), ())), preferred_element_type=F32)


def _params(*sem):
    return pltpu.CompilerParams(dimension_semantics=sem, vmem_limit_bytes=VMEM_LIMIT)


def _divisor(n, pref):
    t = min(n, pref)
    while n % t:
        t -= 8
    return t


class _Layout:
    def __init__(self, B, S, C):
        self.B, self.S, self.C = B, S, C
        self.n_lat = B * S
        self.n_all = B * S + B * C

    def tile(self, pref):
        return _divisor(math.gcd(self.S, self.B * self.C), pref)

    def n_tiles(self, tm, lat_only=False):
        return (self.n_lat if lat_only else self.n_all) // tm

    def mod_index(self, tm):
        n_lat_tiles, per_batch, B = self.n_lat // tm, self.S // tm, self.B
        return lambda t: jnp.where(t < n_lat_tiles, t // per_batch, B)

    def pos_index(self, tm):
        n_lat_tiles, per_batch = self.n_lat // tm, self.S // tm
        return lambda t: jnp.where(t < n_lat_tiles, t % per_batch, per_batch)


def _rmsnorm(x, w):
    return (x * lax.rsqrt(jnp.mean(x * x, axis=-1, keepdims=True) + EPS)) * w


def _pre(h, gain, m, off):
    return _rmsnorm(h, gain) * (1.0 + m[off + 1:off + 2]) + m[off:off + 1]


def _silu(x):
    return x * jax.nn.sigmoid(x)


MXU_COLS = 2 * LANES


def _chunked_proj(a, w_ref, emit):
    n = w_ref.shape[1] // MXU_COLS
    cur = _dot(a, w_ref[:, :MXU_COLS])
    for c in range(n):
        nxt = _dot(a, w_ref[:, (c + 1) * MXU_COLS:(c + 2) * MXU_COLS]) if c + 1 < n else None
        emit(c * MXU_COLS, cur)
        cur = nxt


def _rope(x, cos, sa, sb, shift):
    return x * cos + pltpu.roll(x, LANES - shift, 1) * sa + pltpu.roll(x, shift, 1) * sb


def _ada_kernel(c_ref, w_ref, b_ref, o_ref):
    sc = _silu(c_ref[...]).astype(BF16)
    o_ref[0] = _dot(sc, w_ref[0].astype(BF16)) + b_ref[0]


def _ada_table(cc, ada_w, ada_b):
    rows, D = cc.shape
    n_out = ada_w.shape[2]
    tn = n_out // 8
    return pl.pallas_call(
        _ada_kernel,
        grid=(DEPTH, n_out // tn),
        in_specs=[pl.BlockSpec((rows, D), lambda i, j: (0, 0)),
                  pl.BlockSpec((1, D, tn), lambda i, j: (i, 0, j)),
                  pl.BlockSpec((1, 1, tn), lambda i, j: (i, 0, j))],
        out_specs=pl.BlockSpec((1, rows, tn), lambda i, j: (i, 0, j)),
        out_shape=jax.ShapeDtypeStruct((DEPTH, rows, n_out), F32),
        compiler_params=_params("arbitrary", "arbitrary"),
        name="ada_table",
    )(cc, ada_w, ada_b.reshape(DEPTH, 1, n_out))


FFN_CHUNK = 2 * LANES


def _ffn_kernel(*refs, off, final_norm, n_lat_tiles, mixed):
    refs = list(refs)
    if n_lat_tiles is None:
        h = refs.pop(0)[...]
    else:
        hx_ref, hc_ref = refs.pop(0), refs.pop(0)
        h = jnp.where(pl.program_id(0) < n_lat_tiles, hx_ref[...], hc_ref[...])
    if mixed:
        mix_ref, wo_ref = refs.pop(0), refs.pop(0)
    mod_ref, gain_ref, fin_ref, wg_ref, wu_ref, wd_ref, o_ref, a_ref = refs
    m = mod_ref[0]
    if mixed:
        h = h + m[5:6] * _dot(mix_ref[...], wo_ref[...])
    xm = _pre(h, gain_ref[...], m, off).astype(BF16)
    n = D_FF // FFN_CHUNK

    def gate_up(c):
        cols = slice(c * FFN_CHUNK, (c + 1) * FFN_CHUNK)
        return _dot(xm, wg_ref[:, cols]), _dot(xm, wu_ref[:, cols])

    cur = gate_up(0)
    for c in range(n):
        nxt = gate_up(c + 1) if c + 1 < n else None
        g, u = cur
        a_ref[:, c * FFN_CHUNK:(c + 1) * FFN_CHUNK] = (_silu(g) * u).astype(BF16)
        cur = nxt
    out = h + (0.5 * m[off + 2:off + 3]) * _dot(a_ref[...], wd_ref[...])
    if final_norm:
        out = _rmsnorm(out, fin_ref[...])
    o_ref[...] = out


def _ffn(lay, h, mod, gain, fin_w, wg, wu, wd, layer, half, lat_only=False, final_norm=False, mix=None):
    split = isinstance(h, tuple)
    D = wg.shape[2]
    tm = lay.tile(512)
    n_rows = lay.n_lat if lat_only else lay.n_all
    midx = lay.mod_index(tm)
    n_lat_tiles = lay.n_lat // tm
    const = lambda t: (0, 0)
    one = pl.Buffered(1)
    weight = lambda shape: pl.BlockSpec((None, None) + shape, lambda t: (layer, half, 0, 0), pipeline_mode=one)
    if split:
        h_specs = [pl.BlockSpec((tm, D), lambda t: (jnp.minimum(t, n_lat_tiles - 1), 0)),
                   pl.BlockSpec((tm, D), lambda t: (jnp.maximum(t - n_lat_tiles, 0), 0))]
        h_args = list(h)
    else:
        h_specs = [pl.BlockSpec((tm, D), lambda t: (t, 0))]
        h_args = [h]
    if mix is not None:
        o, w_out = mix
        h_specs += [pl.BlockSpec((tm, o.shape[1]), lambda t: (t, 0)),
                    pl.BlockSpec(w_out.shape, const, pipeline_mode=one)]
        h_args += [o, w_out.astype(BF16)]
    return pl.pallas_call(
        functools.partial(_ffn_kernel, off=6 * half, final_norm=final_norm,
                          n_lat_tiles=n_lat_tiles if split else None, mixed=mix is not None),
        grid=(n_rows // tm,),
        in_specs=h_specs + [pl.BlockSpec((1, N_MOD, D), lambda t: (midx(t), 0, 0)),
                            pl.BlockSpec((1, D), const),
                            pl.BlockSpec((1, D), const),
                            weight((D, D_FF)), weight((D, D_FF)), weight((D_FF, D))],
        out_specs=pl.BlockSpec((tm, D), lambda t: (t, 0)),
        out_shape=jax.ShapeDtypeStruct((n_rows, D), F32),
        scratch_shapes=[pltpu.VMEM((tm, D_FF), BF16)],
        compiler_params=_params("arbitrary"),
        name="ffn",
    )(*h_args, mod, gain.reshape(1, D), fin_w.reshape(1, D), wg, wu, wd)


def _rope_tables(S, rot_dim, pad_rows, lane_off):
    rows = S // GRID_W
    row = jnp.repeat(jnp.arange(rows, dtype=F32), GRID_W)
    col = jnp.tile(jnp.arange(GRID_W, dtype=F32), rows)
    axis_dim = rot_dim // 2
    inv_freq = ROPE_BASE ** (-jnp.arange(0, axis_dim, 2, dtype=F32) / axis_dim)
    ang_r = row[:, None] * inv_freq[None, :]
    ang_c = col[:, None] * inv_freq[None, :]
    ang = jnp.concatenate([ang_r, ang_r, ang_c, ang_c], axis=-1)
    cos, sin = jnp.cos(ang), jnp.sin(ang)
    f = rot_dim // 4
    first = (jnp.arange(rot_dim) % (2 * f)) < f
    sa = jnp.where(first[None, :], -sin, 0.0)
    sb = jnp.where(first[None, :], 0.0, sin)
    period = 64 if lane_off + rot_dim <= 64 else LANES

    def widen(t, fill):
        blk = jnp.full((S, period), fill, F32).at[:, lane_off:lane_off + rot_dim].set(t)
        blk = jnp.tile(blk, (1, LANES // period))
        return jnp.concatenate([blk, jnp.full((pad_rows, LANES), fill, F32)], axis=0)

    return widen(cos, 1.0), widen(sa, 0.0), widen(sb, 0.0)


def _attn_specs(lay, tq, widths):
    S, C, B = lay.S, lay.C, lay.B
    nq, nc = S // tq, C // tq
    lat_blocks = lay.n_lat // C

    def q_index(b, i):
        return jnp.where(i < nq, b * nq + i, B * nq + b * nc + (i - nq))

    lat = [pl.BlockSpec((S, w), lambda b, i: (b, 0)) for w in widths]
    ctx = [pl.BlockSpec((C, w), lambda b, i: (lat_blocks + b, 0)) for w in widths]
    return q_index, nq, nc, lat, ctx


def _gqa_proj_kernel(h_ref, mod_ref, gain_ref, w_ref, cos_ref, sa_ref, sb_ref, q_ref, k_ref, v_ref):
    qd, kd = GQA_HEADS * GQA_HEAD_DIM, GQA_KV_HEADS * GQA_HEAD_DIM
    a = _pre(h_ref[...], gain_ref[...], mod_ref[0], 3).astype(BF16)
    cos, sa, sb = cos_ref[...], sa_ref[...], sb_ref[...]
    f = GQA_HEAD_DIM // 4
    scale = GQA_HEAD_DIM ** -0.5
    lane = lax.broadcasted_iota(jnp.int32, (1, MXU_COLS), 1)
    ones_half = jnp.where((lane & (2 * GQA_HEAD_DIM - 1)) >= GQA_HEAD_DIM, 1.0, 0.0)

    def emit(col, tile):
        if col >= qd + kd:
            v_ref[:, col - qd - kd:col - qd - kd + MXU_COLS] = (tile + ones_half).astype(BF16)
            return
        for j in range(MXU_COLS // LANES):
            x = _rope(tile[:, j * LANES:(j + 1) * LANES], cos, sa, sb, f)
            lo = col + j * LANES
            if lo < qd:
                q_ref[:, lo:lo + LANES] = (x * scale).astype(BF16)
            else:
                k_ref[:, lo - qd:lo - qd + LANES] = x.astype(BF16)

    _chunked_proj(a, w_ref, emit)


def _gqa_attn_kernel(sink_ref, q_ref, kp_ref, kc_ref, kn_ref, vp_ref, vc_ref, vn_ref, kx_ref, vx_ref,
                     o_ref, kcat_ref, vcat_ref, *, nq):
    i = pl.program_id(1)
    blk = GQA_BLOCK
    G = GQA_HEADS // GQA_KV_HEADS
    HD = GQA_HEAD_DIM
    VW = 2 * HD
    span = 3 * blk

    def run(k_ref, v_ref, masks):
        kvs = range(GQA_KV_HEADS)
        qs = [jnp.concatenate([q_ref[:, (G * g + n) * HD:(G * g + n + 1) * HD] for n in range(G)], axis=0)
              for g in kvs]
        sinks = [jnp.concatenate([jnp.full((blk, 1), sink_ref[G * g + n], F32) for n in range(G)], axis=0)
                 for g in kvs]
        ss = [_dot_nt(qs[g], k_ref[:, g * HD:(g + 1) * HD]) for g in kvs]
        if masks is not None:
            prev_ok, next_ok = masks
            ss = [jnp.concatenate([jnp.where(prev_ok, s[:, :blk], NEG_INF), s[:, blk:2 * blk],
                                   jnp.where(next_ok, s[:, 2 * blk:span], NEG_INF), s[:, span:]], axis=1)
                  for s in ss]
        ms = [jnp.maximum(jnp.max(ss[g], axis=-1, keepdims=True), sinks[g]) for g in kvs]
        accs = [_dot(jnp.exp(ss[g] - ms[g]).astype(BF16), v_ref[:, g * VW:(g + 1) * VW]) for g in kvs]
        outs = [accs[g][:, :HD] * (1.0 / (accs[g][:, HD:] + jnp.exp(sinks[g] - ms[g]))) for g in kvs]
        heads = [outs[g][n * blk:(n + 1) * blk] for g in kvs for n in range(G)]
        o_ref[...] = jnp.concatenate(heads, axis=-1).astype(BF16)

    @pl.when(i < nq)
    def _():
        for n, (k_ref, v_ref) in enumerate([(kp_ref, vp_ref), (kc_ref, vc_ref), (kn_ref, vn_ref)]):
            kcat_ref[n * blk:(n + 1) * blk] = k_ref[...]
            vcat_ref[n * blk:(n + 1) * blk] = v_ref[...]
        kcat_ref[span:] = kx_ref[...]
        vcat_ref[span:] = vx_ref[...]
        row = lax.broadcasted_iota(jnp.int32, (G * blk, blk), 0) & (blk - 1)
        col = lax.broadcasted_iota(jnp.int32, (G * blk, blk), 1)
        prev_ok = (row + blk - col <= GQA_WINDOW) & (i >= 1)
        next_ok = (col + blk - row <= GQA_WINDOW) & (i + 1 < nq)
        run(kcat_ref, vcat_ref, (prev_ok, next_ok))

    @pl.when(i >= nq)
    def _():
        run(kx_ref, vx_ref, None)


def _gqa_mixer(lay, h, mod, gain, w_in, sinks, with_ctx_out):
    D = h.shape[1]
    B, S, C = lay.B, lay.S, lay.C
    qd, kd = GQA_HEADS * GQA_HEAD_DIM, GQA_KV_HEADS * GQA_HEAD_DIM
    w_v = jnp.pad(w_in[:, qd + kd:].reshape(D, GQA_KV_HEADS, GQA_HEAD_DIM), ((0, 0), (0, 0), (0, GQA_HEAD_DIM)))
    w_pad = jnp.concatenate([w_in[:, :qd + kd], w_v.reshape(D, 2 * kd)], axis=1)
    tm = lay.tile(512)
    midx, pidx = lay.mod_index(tm), lay.pos_index(tm)
    cos, sa, sb = _rope_tables(S, GQA_HEAD_DIM, tm, 0)
    tab = pl.BlockSpec((tm, LANES), lambda t: (pidx(t), 0))
    q, k, v = pl.pallas_call(
        _gqa_proj_kernel,
        grid=(lay.n_all // tm,),
        in_specs=[pl.BlockSpec((tm, D), lambda t: (t, 0)),
                  pl.BlockSpec((1, N_MOD, D), lambda t: (midx(t), 0, 0)),
                  pl.BlockSpec((1, D), lambda t: (0, 0)),
                  pl.BlockSpec((D, qd + 3 * kd), lambda t: (0, 0)),
                  tab, tab, tab],
        out_specs=[pl.BlockSpec((tm, qd), lambda t: (t, 0)),
                   pl.BlockSpec((tm, kd), lambda t: (t, 0)),
                   pl.BlockSpec((tm, 2 * kd), lambda t: (t, 0))],
        out_shape=[jax.ShapeDtypeStruct((lay.n_all, qd), BF16),
                   jax.ShapeDtypeStruct((lay.n_all, kd), BF16),
                   jax.ShapeDtypeStruct((lay.n_all, 2 * kd), BF16)],
        compiler_params=_params("arbitrary"),
        name="gqa_proj",
    )(h, mod, gain.reshape(1, D), w_pad.astype(BF16), cos, sa, sb)

    blk = GQA_BLOCK
    assert S % blk == 0 and C % blk == 0
    nq, nc = S // blk, C // blk
    n_i = nq + (nc if with_ctx_out else 0)
    lat_blocks = lay.n_lat // C

    def q_index(b, i):
        return jnp.where(i < nq, b * nq + i, B * nq + b * nc + (i - nq))

    def win(delta):
        return lambda b, i: (b * nq + jnp.clip(i + delta, 0, nq - 1), 0)

    def windows(width):
        return [pl.BlockSpec((blk, width), win(d)) for d in (-1, 0, 1)]

    def ctx_spec(width):
        return pl.BlockSpec((C, width), lambda b, i: (lat_blocks + b, 0))

    row = pl.BlockSpec((blk, qd), lambda b, i: (q_index(b, i), 0))
    return pl.pallas_call(
        functools.partial(_gqa_attn_kernel, nq=nq),
        grid=(B, n_i),
        in_specs=[pl.BlockSpec(memory_space=pltpu.SMEM), row,
                  *windows(kd), *windows(2 * kd), ctx_spec(kd), ctx_spec(2 * kd)],
        out_specs=row,
        out_shape=jax.ShapeDtypeStruct((lay.n_all if with_ctx_out else lay.n_lat, qd), BF16),
        scratch_shapes=[pltpu.VMEM((3 * blk + C, kd), BF16), pltpu.VMEM((3 * blk + C, 2 * kd), BF16)],
        compiler_params=_params("arbitrary", "arbitrary"),
        name="gqa_attn",
    )(sinks.astype(F32), q, k, k, k, v, v, v, k, v)


def _diff_proj_kernel(h_ref, mod_ref, gain_ref, w_ref, cos_ref, sa_ref, sb_ref, q_ref, k_ref, v_ref):
    qk = DIFF_HEADS * 2 * DIFF_HEAD_DIM
    a = _pre(h_ref[...], gain_ref[...], mod_ref[0], 3).astype(BF16)
    cos, sa, sb = cos_ref[...], sa_ref[...], sb_ref[...]
    f = DIFF_HEAD_DIM // 4
    scale = DIFF_HEAD_DIM ** -0.5
    vw = 2 * DIFF_HEAD_DIM
    ones = jnp.ones((a.shape[0], vw), BF16)

    def emit(col, tile):
        for j in range(MXU_COLS // LANES):
            lo = col + j * LANES
            x = tile[:, j * LANES:(j + 1) * LANES]
            if lo < qk:
                q_ref[:, lo:lo + LANES] = (_rope(x, cos, sa, sb, f) * scale).astype(BF16)
            elif lo < 2 * qk:
                k_ref[:, lo - qk:lo - qk + LANES] = _rope(x, cos, sa, sb, f).astype(BF16)
            else:
                hd = (lo - 2 * qk) // vw
                v_ref[:, 2 * hd * vw:(2 * hd + 1) * vw] = x.astype(BF16)
                v_ref[:, (2 * hd + 1) * vw:(2 * hd + 2) * vw] = ones

    _chunked_proj(a, w_ref, emit)


def _diff_attn_kernel(lam_ref, sub_ref, q_ref, kl_ref, vl_ref, kx_ref, vx_ref, o_ref, *, nq, lambda_init):
    i = pl.program_id(1)
    HD = DIFF_HEAD_DIM
    VW = 4 * HD
    lp = lam_ref[...]
    lam = (jnp.exp(jnp.sum(lp[0:1] * lp[1:2], keepdims=True))
           - jnp.exp(jnp.sum(lp[2:3] * lp[3:4], keepdims=True)) + lambda_init)

    def run(kv_pairs):
        def scores(h):
            los = [(2 * h + j) * HD for j in range(2)]
            return [[_dot_nt(q_ref[:, lo:lo + HD], k_ref[:, lo:lo + HD]) for k_ref, _ in kv_pairs] for lo in los]

        heads = []
        ss = scores(0)
        for h in range(DIFF_HEADS):
            nxt = scores(h + 1) if h + 1 < DIFF_HEADS else None
            maps = []
            for parts in ss:
                m = functools.reduce(jnp.maximum, [jnp.max(s, axis=-1, keepdims=True) for s in parts])
                acc = None
                for s, (_, v_ref) in zip(parts, kv_pairs):
                    t = _dot(jnp.exp(s - m).astype(BF16), v_ref[:, h * VW:(h + 1) * VW])
                    acc = t if acc is None else acc + t
                maps.append(acc[:, :VW // 2] * (1.0 / acc[:, VW // 2:]))
            o = maps[0] - lam * maps[1]
            heads.append(_rmsnorm(o, sub_ref[...]) * (1.0 - lambda_init))
            ss = nxt
        o_ref[...] = jnp.concatenate(heads, axis=-1).astype(BF16)

    @pl.when(i < nq)
    def _():
        run([(kl_ref, vl_ref), (kx_ref, vx_ref)])

    @pl.when(i >= nq)
    def _():
        run([(kx_ref, vx_ref)])


def _diff_mixer(lay, h, mod, gain, w_in, lam_params, subln_w, lambda_init, with_ctx_out):
    D = h.shape[1]
    B, S, C = lay.B, lay.S, lay.C
    qk = DIFF_HEADS * 2 * DIFF_HEAD_DIM
    tm = lay.tile(512)
    midx, pidx = lay.mod_index(tm), lay.pos_index(tm)
    cos, sa, sb = _rope_tables(S, DIFF_HEAD_DIM, tm, 0)
    tab = pl.BlockSpec((tm, LANES), lambda t: (pidx(t), 0))
    out = jax.ShapeDtypeStruct((lay.n_all, qk), BF16)
    q, k, v = pl.pallas_call(
        _diff_proj_kernel,
        grid=(lay.n_all // tm,),
        in_specs=[pl.BlockSpec((tm, D), lambda t: (t, 0)),
                  pl.BlockSpec((1, N_MOD, D), lambda t: (midx(t), 0, 0)),
                  pl.BlockSpec((1, D), lambda t: (0, 0)),
                  pl.BlockSpec((D, 3 * qk), lambda t: (0, 0)),
                  tab, tab, tab],
        out_specs=[pl.BlockSpec((tm, qk), lambda t: (t, 0))] * 2 + [pl.BlockSpec((tm, 2 * qk), lambda t: (t, 0))],
        out_shape=[out, out, jax.ShapeDtypeStruct((lay.n_all, 2 * qk), BF16)],
        compiler_params=_params("arbitrary"),
        name="diff_proj",
    )(h, mod, gain.reshape(1, D), w_in.astype(BF16), cos, sa, sb)

    tq = _divisor(math.gcd(S, C), 256)
    q_index, nq, nc, lat, ctx = _attn_specs(lay, tq, [qk, 2 * qk])
    row = pl.BlockSpec((tq, qk), lambda b, i: (q_index(b, i), 0))
    return pl.pallas_call(
        functools.partial(_diff_attn_kernel, nq=nq, lambda_init=lambda_init),
        grid=(B, nq + (nc if with_ctx_out else 0)),
        in_specs=[pl.BlockSpec((4, DIFF_HEAD_DIM), lambda b, i: (0, 0)),
                  pl.BlockSpec((1, 2 * DIFF_HEAD_DIM), lambda b, i: (0, 0)),
                  row, *lat, *ctx],
        out_specs=row,
        out_shape=jax.ShapeDtypeStruct((lay.n_all if with_ctx_out else lay.n_lat, qk), BF16),
        compiler_params=_params("arbitrary", "arbitrary"),
        name="diff_attn",
    )(lam_params.astype(F32), subln_w.reshape(1, -1), q, k, v, k, v)


def _hgrn_proj_kernel(h_ref, mod_ref, gain_ref, lb_ref, w_ref, o_ref, *, layer):
    KD = HGRN_HEADS * HGRN_KEY_DIM
    W = 2 * LANES
    a = _pre(h_ref[...], gain_ref[...], mod_ref[0], 3).astype(BF16)
    raw = lb_ref[...]
    e = jnp.exp(raw - jnp.max(raw, axis=0, keepdims=True))
    soft = e * (1.0 / jnp.sum(e, axis=0, keepdims=True))
    lb = jnp.sum(soft[1:layer + 1], axis=0, keepdims=True) if layer else jnp.zeros_like(soft[0:1])
    n = w_ref.shape[1] // W

    def proj(c):
        return _dot(a, w_ref[:, c * W:(c + 1) * W])

    cur = proj(0)
    for c in range(n):
        nxt = proj(c + 1) if c + 1 < n else None
        group, col = divmod(c * W, KD)
        if group in (0, 4):
            out = _silu(cur)
        elif group in (1, 2):
            lbc = lb[:, col:col + W]
            out = lbc + (1.0 - lbc) * jax.nn.sigmoid(cur)
        else:
            out = cur
        o_ref[:, c * W:(c + 1) * W] = out
        cur = nxt


SUBLANES = 8


def _running_sum(x, reverse):
    n, w = x.shape
    n_groups = n // SUBLANES
    x = x.reshape(n_groups, SUBLANES, w)
    within = lax.broadcasted_iota(jnp.int32, (1, SUBLANES, 1), 1)
    step = 1
    while step < SUBLANES:
        if reverse:
            x = x + jnp.where(within < SUBLANES - step, pltpu.roll(x, SUBLANES - step, 1), 0.0)
        else:
            x = x + jnp.where(within >= step, pltpu.roll(x, step, 1), 0.0)
        step *= 2
    out = [None] * n_groups
    carry = None
    for j in (reversed(range(n_groups)) if reverse else range(n_groups)):
        out[j] = x[j] if carry is None else x[j] + carry
        carry = out[j][0:1] if reverse else out[j][SUBLANES - 1:SUBLANES]
    return jnp.concatenate(out, axis=0)


def _hgrn_scan_kernel(qf_ref, ff_ref, vf_ref, qb_ref, fb_ref, vb_ref, of_ref, ob_ref, st_ref, *, n_chunks):
    step = pl.program_id(1)
    C = HGRN_CHUNK
    DK, DV, H = HGRN_KEY_DIM, HGRN_VAL_DIM, HGRN_HEADS

    @pl.when(step == 0)
    def _():
        st_ref[...] = jnp.zeros_like(st_ref)

    r = lax.broadcasted_iota(jnp.int32, (C, C), 0)
    c = lax.broadcasted_iota(jnp.int32, (C, C), 1)
    ks = [slice(h * DK, (h + 1) * DK) for h in range(H)]
    vs = [slice(h * DV, (h + 1) * DV) for h in range(H)]
    dirs = [(qf_ref, ff_ref, vf_ref, of_ref, c <= r, C // 2, C - 1),
            (qb_ref, fb_ref, vb_ref, ob_ref, c >= r, C - 1 - C // 2, 0)]

    def prepare(d, n):
        q_ref, f_ref, v_ref, _, tri, mid_row, last_row = dirs[d]
        base = (n if d == 0 else n_chunks - 1 - n) * C
        q = q_ref[base:base + C, :]
        f = f_ref[base:base + C, :]
        vt = v_ref[base:base + C, :].T.astype(BF16)
        k = 1.0 - f
        g = _running_sum(jnp.log2(f), reverse=d == 1)
        g_mid = g[mid_row:mid_row + 1]
        g_last = g[last_row:last_row + 1]
        t = g - g_mid
        qa_f = q * jnp.exp2(t)
        kb_f = k * jnp.exp2(-t)
        qa = qa_f.astype(BF16)
        kb = kb_f.astype(BF16)
        qg = (qa_f * jnp.exp2(g_mid)).astype(BF16)
        kd = (kb_f * jnp.exp2(g_last - g_mid)).astype(BF16)
        a = [jnp.where(tri, _dot_nt(qa[:, ks[h]], kb[:, ks[h]]), 0.0).astype(BF16) for h in range(H)]
        lhs = [jnp.concatenate([qg[:, ks[h]], a[h]], axis=1) for h in range(H)]
        update = [_dot(vt[vs[h], :], kd[:, ks[h]]) for h in range(H)]
        return base, lhs, vt, jnp.exp2(g_last), update

    cur = [prepare(d, 0) for d in range(2)]
    for n in range(n_chunks):
        nxt = [prepare(d, n + 1) for d in range(2)] if n + 1 < n_chunks else None
        for d in range(2):
            base, lhs, vt, decay, update = cur[d]
            st = [st_ref[d, vs[h], :] for h in range(H)]
            outs = [_dot_nt(lhs[h], jnp.concatenate([st[h].astype(BF16), vt[vs[h], :]], axis=1)) for h in range(H)]
            for h in range(H):
                st_ref[d, vs[h], :] = st[h] * decay[:, ks[h]] + update[h]
            dirs[d][3][base:base + C, :] = jnp.concatenate(outs, axis=-1)
        cur = nxt


def _hgrn_out_kernel(of_ref, ob_ref, gate_ref, nw_ref, out_ref):
    DV = HGRN_VAL_DIM
    o = of_ref[...] + ob_ref[...]
    gate = gate_ref[...]
    nw = nw_ref[...]
    ys = [_rmsnorm(o[:, h * DV:(h + 1) * DV], nw) * gate[:, h * DV:(h + 1) * DV] for h in range(HGRN_HEADS)]
    out_ref[...] = jnp.concatenate(ys, axis=-1).astype(BF16)


def _hgrn_mixer(lay, h, mod, gain, w_in, norm_w, lower_bounds, layer):
    D = h.shape[1]
    B, S, C = lay.B, lay.S, lay.C
    KD = HGRN_HEADS * HGRN_KEY_DIM
    tm = lay.tile(512)
    midx = lay.mod_index(tm)
    n_cols = w_in.shape[1] // KD
    proj = pl.pallas_call(
        functools.partial(_hgrn_proj_kernel, layer=layer),
        grid=(lay.n_all // tm,),
        in_specs=[pl.BlockSpec((tm, D), lambda t: (t, 0)),
                  pl.BlockSpec((1, N_MOD, D), lambda t: (midx(t), 0, 0)),
                  pl.BlockSpec((1, D), lambda t: (0, 0)),
                  pl.BlockSpec(lower_bounds.shape, lambda t: (0, 0)),
                  pl.BlockSpec((D, n_cols * KD), lambda t: (0, 0), pipeline_mode=pl.Buffered(1))],
        out_specs=pl.BlockSpec((tm, n_cols * KD), lambda t: (t, 0)),
        out_shape=jax.ShapeDtypeStruct((lay.n_all, n_cols * KD), F32),
        compiler_params=_params("arbitrary"),
        name="hgrn_proj",
    )(h, mod, gain.reshape(1, D), lower_bounds.astype(F32), w_in.astype(BF16))

    ts = C
    n_lat_steps = S // ts
    lat_blocks = lay.n_lat // ts

    def rows(d):
        def index(b, s):
            lat = b * n_lat_steps + (s - 1 if d == 0 else n_lat_steps - s)
            return jnp.where(s == 0, lat_blocks + b, lat)
        return index

    def col(d, j):
        return pl.BlockSpec((ts, KD), lambda b, s: (rows(d)(b, s), j))

    o_f, o_b = pl.pallas_call(
        functools.partial(_hgrn_scan_kernel, n_chunks=ts // HGRN_CHUNK),
        grid=(B, 1 + n_lat_steps),
        in_specs=[col(0, 0), col(0, 1), col(0, 3), col(1, 0), col(1, 2), col(1, 3)],
        out_specs=[col(0, 0), col(1, 0)],
        out_shape=[jax.ShapeDtypeStruct((lay.n_all, KD), F32)] * 2,
        scratch_shapes=[pltpu.VMEM((2, HGRN_HEADS * HGRN_VAL_DIM, HGRN_KEY_DIM), F32)],
        compiler_params=_params("arbitrary", "arbitrary"),
        name="hgrn_scan",
    )(proj, proj, proj, proj, proj, proj)

    tr = lay.tile(1024)
    row = pl.BlockSpec((tr, KD), lambda t: (t, 0))
    return pl.pallas_call(
        _hgrn_out_kernel,
        grid=(lay.n_all // tr,),
        in_specs=[row, row,
                  pl.BlockSpec((tr, KD), lambda t: (t, 4)),
                  pl.BlockSpec((1, HGRN_VAL_DIM), lambda t: (0, 0))],
        out_specs=row,
        out_shape=jax.ShapeDtypeStruct((lay.n_all, KD), BF16),
        compiler_params=_params("arbitrary"),
        name="hgrn_out",
    )(o_f, o_b, proj, norm_w.reshape(1, -1))


MLA_QK_PAD = LANES


def _mla_proj_kernel(h_ref, mod_ref, gain_ref, wd_ref, qn_ref, kvn_ref, wuq_ref, wukv_ref,
                     cos_ref, sa_ref, sb_ref, q_ref, k_ref, v_ref):
    H, P = MLA_HEADS, MLA_QK_PAD
    a = _pre(h_ref[...], gain_ref[...], mod_ref[0], 3).astype(BF16)
    dn = _dot(a, wd_ref[...])
    cos, sa, sb = cos_ref[...], sa_ref[...], sb_ref[...]
    f = MLA_ROPE // 4
    cq = _rmsnorm(dn[:, :MLA_Q_LORA], qn_ref[...]).astype(BF16)
    ckv = _rmsnorm(dn[:, MLA_Q_LORA:MLA_Q_LORA + MLA_KV_LORA], kvn_ref[...]).astype(BF16)
    kr = _rope(dn[:, MLA_Q_LORA + MLA_KV_LORA:], cos, sa, sb, f)
    lane = lax.broadcasted_iota(jnp.int32, (1, P), 1)
    ones_half = jnp.where(lane >= MLA_V_DIM, 1.0, 0.0)

    def emit_q(col, tile):
        for j in range(MXU_COLS // P):
            lo = col + j * P
            q_ref[:, lo:lo + P] = _rope(tile[:, j * P:(j + 1) * P], cos, sa, sb, f).astype(BF16)

    def emit_kv(col, tile):
        for j in range(MXU_COLS // P):
            lo = col + j * P
            x = tile[:, j * P:(j + 1) * P]
            if lo < H * P:
                k_ref[:, lo:lo + P] = (x + kr).astype(BF16)
            else:
                v_ref[:, lo - H * P:lo - H * P + P] = (x + ones_half).astype(BF16)

    _chunked_proj(cq, wuq_ref, emit_q)
    _chunked_proj(ckv, wukv_ref, emit_kv)


def _mla_attn_kernel(q_ref, kl_ref, vl_ref, kx_ref, vx_ref, o_ref, *, nq, with_ctx):
    i = pl.program_id(1)
    P, VD = MLA_QK_PAD, MLA_V_DIM
    c = ((MLA_NOPE + MLA_ROPE) ** -0.5) * LOG2_E

    def run(kv_pairs):
        def scores(hd):
            qh = q_ref[:, hd * P:(hd + 1) * P]
            return [_dot_nt(qh, k_ref[:, hd * P:(hd + 1) * P]) for k_ref, _ in kv_pairs]

        heads = []
        ss = scores(0)
        for hd in range(MLA_HEADS):
            nxt = scores(hd + 1) if hd + 1 < MLA_HEADS else None
            m = functools.reduce(jnp.maximum, [jnp.max(s, axis=-1, keepdims=True) for s in ss])
            mc = m * c
            acc = None
            for s, (_, v_ref) in zip(ss, kv_pairs):
                t = _dot(jnp.exp2(s * c - mc).astype(BF16), v_ref[:, hd * P:(hd + 1) * P])
                acc = t if acc is None else acc + t
            heads.append(acc[:, :VD] * (1.0 / acc[:, VD:]))
            ss = nxt
        o_ref[...] = jnp.concatenate(heads, axis=-1).astype(BF16)

    if not with_ctx:
        run([(kl_ref, vl_ref), (kx_ref, vx_ref)])
        return

    @pl.when(i < nq)
    def _():
        run([(kl_ref, vl_ref), (kx_ref, vx_ref)])

    @pl.when(i >= nq)
    def _():
        run([(kx_ref, vx_ref)])


def _mla_mixer(lay, h, mod, gain, w_down, q_norm_w, kv_norm_w, w_uq, w_ukv, with_ctx_out):
    D = h.shape[1]
    B, S, C = lay.B, lay.S, lay.C
    H, P, VD = MLA_HEADS, MLA_QK_PAD, MLA_V_DIM
    qk = MLA_NOPE + MLA_ROPE
    lora = MLA_Q_LORA + MLA_KV_LORA
    kr_cols = jnp.zeros((D, P), F32).at[:, MLA_NOPE:qk].set(w_down[:, lora:])
    wd = jnp.concatenate([w_down[:, :lora], kr_cols], axis=1).astype(BF16)
    wuq = jnp.pad(w_uq.reshape(MLA_Q_LORA, H, qk), ((0, 0), (0, 0), (0, P - qk))).reshape(MLA_Q_LORA, H * P)
    ukv = w_ukv.reshape(MLA_KV_LORA, H, MLA_NOPE + VD)
    wkn = jnp.pad(ukv[..., :MLA_NOPE], ((0, 0), (0, 0), (0, P - MLA_NOPE))).reshape(MLA_KV_LORA, H * P)
    wv = jnp.pad(ukv[..., MLA_NOPE:], ((0, 0), (0, 0), (0, P - VD))).reshape(MLA_KV_LORA, H * P)
    wukv = jnp.concatenate([wkn, wv], axis=1)

    tm = lay.tile(512)
    midx, pidx = lay.mod_index(tm), lay.pos_index(tm)
    cos, sa, sb = _rope_tables(S, MLA_ROPE, tm, MLA_NOPE)
    tab = pl.BlockSpec((tm, LANES), lambda t: (pidx(t), 0))
    full = lambda shape: pl.BlockSpec(shape, lambda t: (0, 0))
    q, k, v = pl.pallas_call(
        _mla_proj_kernel,
        grid=(lay.n_all // tm,),
        in_specs=[pl.BlockSpec((tm, D), lambda t: (t, 0)),
                  pl.BlockSpec((1, N_MOD, D), lambda t: (midx(t), 0, 0)),
                  full((1, D)), full((D, lora + P)), full((1, MLA_Q_LORA)), full((1, MLA_KV_LORA)),
                  full((MLA_Q_LORA, H * P)), full((MLA_KV_LORA, 2 * H * P)),
                  tab, tab, tab],
        out_specs=[pl.BlockSpec((tm, H * P), lambda t: (t, 0))] * 3,
        out_shape=[jax.ShapeDtypeStruct((lay.n_all, H * P), BF16)] * 3,
        compiler_params=_params("arbitrary"),
        name="mla_proj",
    )(h, mod, gain.reshape(1, D), wd, q_norm_w.reshape(1, -1), kv_norm_w.reshape(1, -1),
      wuq.astype(BF16), wukv.astype(BF16), cos, sa, sb)

    tq = _divisor(math.gcd(S, C), 256)
    q_index, nq, nc, lat, ctx = _attn_specs(lay, tq, [H * P, H * P])
    return pl.pallas_call(
        functools.partial(_mla_attn_kernel, nq=nq, with_ctx=with_ctx_out),
        grid=(B, nq + (nc if with_ctx_out else 0)),
        in_specs=[pl.BlockSpec((tq, H * P), lambda b, i: (q_index(b, i), 0)), *lat, *ctx],
        out_specs=pl.BlockSpec((tq, H * VD), lambda b, i: (q_index(b, i), 0)),
        out_shape=jax.ShapeDtypeStruct((lay.n_all if with_ctx_out else lay.n_lat, H * VD), BF16),
        compiler_params=_params("arbitrary", "arbitrary"),
        name="mla_attn",
    )(q, k, v, k, v)


def kernel(x, c, ctx, c_ctx, ada_w, ada_b, norm_w, final_norm_w, ffn_w_gate, ffn_w_up, ffn_w_down,
           gqa_w_in, gqa_w_out, gqa_sinks, diff_w_in, diff_w_out, diff_lambda, diff_subln_w,
           hgrn_w_in, hgrn_w_out, hgrn_norm_w, hgrn_lower_bounds,
           mla_w_down, mla_q_norm_w, mla_kv_norm_w, mla_w_uq, mla_w_ukv, mla_w_out):
    B, S, D = x.shape
    C = ctx.shape[1]
    lay = _Layout(B, S, C)
    depth = ada_w.shape[0]

    rows = -(-(B + 1) // 8) * 8
    cc = jnp.zeros((rows, D), F32).at[:B].set(c).at[B].set(c_ctx)
    mods = _ada_table(cc, ada_w, ada_b).reshape(depth, rows, N_MOD, D)

    wg, wu, wd = ffn_w_gate.astype(BF16), ffn_w_up.astype(BF16), ffn_w_down.astype(BF16)
    h = (x.reshape(B * S, D), ctx.reshape(B * C, D))
    for i in range(depth):
        kind, j = i % 4, i // 4
        last = i == depth - 1
        mod = mods[i]
        h = _ffn(lay, h, mod, norm_w[i, 0], final_norm_w, wg, wu, wd, i, 0)
        if kind == 0:
            o = _gqa_mixer(lay, h, mod, norm_w[i, 1], gqa_w_in[j], gqa_sinks[j], not last)
            w_out = gqa_w_out[j]
        elif kind == 1:
            lambda_init = 0.8 - 0.6 * math.exp(-0.3 * i)
            o = _diff_mixer(lay, h, mod, norm_w[i, 1], diff_w_in[j], diff_lambda[j], diff_subln_w[j],
                            lambda_init, not last)
            w_out = diff_w_out[j]
        elif kind == 2:
            o = _hgrn_mixer(lay, h, mod, norm_w[i, 1], hgrn_w_in[j], hgrn_norm_w[j], hgrn_lower_bounds, i)
            w_out = hgrn_w_out[j]
        else:
            o = _mla_mixer(lay, h, mod, norm_w[i, 1], mla_w_down[j], mla_q_norm_w[j], mla_kv_norm_w[j],
                           mla_w_uq[j], mla_w_ukv[j], not last)
            w_out = mla_w_out[j]
        h = _ffn(lay, h, mod, norm_w[i, 2], final_norm_w, wg, wu, wd, i, 1, lat_only=last, final_norm=last,
                 mix=(o, w_out))
    return h[:B * S].reshape(B, S, D)
```

```python
import functools
import math

import jax
import jax.numpy as jnp
from jax import lax
from jax.experimental import pallas as pl
from jax.experimental.pallas import tpu as pltpu

F32 = jnp.float32
BF16 = jnp.bfloat16

D_MODEL = 1024
DEPTH = 4
GRID_W = 64
N_MOD = 9
ROPE_BASE = 10000.0
EPS = 1e-6
NEG_INF = -1e30
D_FF = 2816

GQA_HEADS = 16
GQA_KV_HEADS = 4
GQA_HEAD_DIM = 64
GQA_WINDOW = 128
GQA_BLOCK = 128

DIFF_HEADS = 8
DIFF_HEAD_DIM = 64

HGRN_HEADS = 8
HGRN_KEY_DIM = 128
HGRN_VAL_DIM = D_MODEL // HGRN_HEADS
HGRN_CHUNK = 64

MLA_HEADS = 16
MLA_Q_LORA = 256
MLA_KV_LORA = 256
MLA_NOPE = 64
MLA_ROPE = 32
MLA_V_DIM = 64

LOG2_E = 1.4426950408889634
LANES = 128
VMEM_LIMIT = 56 * 1024 * 1024


def _dot(a, b):
    return jnp.dot(a, b, preferred_element_type=F32)


def _dot_nt(a, b):
    return lax.dot_general(a, b, (((1,), (1,)), ((), ())), preferred_element_type=F32)


def _dot_tn(a, b):
    return lax.dot_general(a, b, (((0,), (0,)), ((), ())), preferred_element_type=F32)


def _params(*sem):
    return pltpu.CompilerParams(dimension_semantics=sem, vmem_limit_bytes=VMEM_LIMIT)


def _divisor(n, pref):
    t = min(n, pref)
    while n % t:
        t -= 8
    return t


class _Layout:
    def __init__(self, B, S, C):
        self.B, self.S, self.C = B, S, C
        self.n_lat = B * S
        self.n_all = B * S + B * C

    def tile(self, pref):
        return _divisor(math.gcd(self.S, self.B * self.C), pref)

    def n_tiles(self, tm, lat_only=False):
        return (self.n_lat if lat_only else self.n_all) // tm

    def mod_index(self, tm):
        n_lat_tiles, per_batch, B = self.n_lat // tm, self.S // tm, self.B
        return lambda t: jnp.where(t < n_lat_tiles, t // per_batch, B)

    def pos_index(self, tm):
        n_lat_tiles, per_batch = self.n_lat // tm, self.S // tm
        return lambda t: jnp.where(t < n_lat_tiles, t % per_batch, per_batch)


def _rmsnorm(x, w):
    return (x * lax.rsqrt(jnp.mean(x * x, axis=-1, keepdims=True) + EPS)) * w


def _pre(h, gain, m, off):
    return _rmsnorm(h, gain) * (1.0 + m[off + 1:off + 2]) + m[off:off + 1]


def _silu(x):
    return x * jax.nn.sigmoid(x)


MXU_COLS = 2 * LANES


def _chunked_proj(a, w_ref, emit):
    n = w_ref.shape[1] // MXU_COLS
    cur = _dot(a, w_ref[:, :MXU_COLS])
    for c in range(n):
        nxt = _dot(a, w_ref[:, (c + 1) * MXU_COLS:(c + 2) * MXU_COLS]) if c + 1 < n else None
        emit(c * MXU_COLS, cur)
        cur = nxt


def _rope(x, cos, sa, sb, shift):
    return x * cos + pltpu.roll(x, LANES - shift, 1) * sa + pltpu.roll(x, shift, 1) * sb


def _ada_kernel(c_ref, w_ref, b_ref, o_ref):
    sc = _silu(c_ref[...]).astype(BF16)
    o_ref[0] = _dot(sc, w_ref[0].astype(BF16)) + b_ref[0]


def _ada_table(cc, ada_w, ada_b):
    rows, D = cc.shape
    n_out = ada_w.shape[2]
    tn = n_out // 8
    return pl.pallas_call(
        _ada_kernel,
        grid=(DEPTH, n_out // tn),
        in_specs=[pl.BlockSpec((rows, D), lambda i, j: (0, 0)),
                  pl.BlockSpec((1, D, tn), lambda i, j: (i, 0, j)),
                  pl.BlockSpec((1, 1, tn), lambda i, j: (i, 0, j))],
        out_specs=pl.BlockSpec((1, rows, tn), lambda i, j: (i, 0, j)),
        out_shape=jax.ShapeDtypeStruct((DEPTH, rows, n_out), F32),
        compiler_params=_params("arbitrary", "arbitrary"),
        name="ada_table",
    )(cc, ada_w, ada_b.reshape(DEPTH, 1, n_out))


FFN_CHUNK = 2 * LANES


def _ffn_kernel(*refs, off, final_norm, n_lat_tiles, mixed):
    refs = list(refs)
    if n_lat_tiles is None:
        h = refs.pop(0)[...]
    else:
        hx_ref, hc_ref = refs.pop(0), refs.pop(0)
        h = jnp.where(pl.program_id(0) < n_lat_tiles, hx_ref[...], hc_ref[...])
    if mixed:
        mix_ref, wo_ref = refs.pop(0), refs.pop(0)
    mod_ref, gain_ref, fin_ref, wg_ref, wu_ref, wd_ref, o_ref, a_ref = refs
    m = mod_ref[0]
    if mixed:
        h = h + m[5:6] * _dot(mix_ref[...], wo_ref[...])
    xm = _pre(h, gain_ref[...], m, off).astype(BF16)
    n = D_FF // FFN_CHUNK

    def gate_up(c):
        cols = slice(c * FFN_CHUNK, (c + 1) * FFN_CHUNK)
        return _dot(xm, wg_ref[:, cols]), _dot(xm, wu_ref[:, cols])

    cur = gate_up(0)
    for c in range(n):
        nxt = gate_up(c + 1) if c + 1 < n else None
        g, u = cur
        a_ref[:, c * FFN_CHUNK:(c + 1) * FFN_CHUNK] = (_silu(g) * u).astype(BF16)
        cur = nxt
    out = h + (0.5 * m[off + 2:off + 3]) * _dot(a_ref[...], wd_ref[...])
    if final_norm:
        out = _rmsnorm(out, fin_ref[...])
    o_ref[...] = out


def _ffn(lay, h, mod, gain, fin_w, wg, wu, wd, layer, half, lat_only=False, final_norm=False, mix=None):
    split = isinstance(h, tuple)
    D = wg.shape[2]
    tm = lay.tile(512)
    n_rows = lay.n_lat if lat_only else lay.n_all
    midx = lay.mod_index(tm)
    n_lat_tiles = lay.n_lat // tm
    const = lambda t: (0, 0)
    one = pl.Buffered(1)
    weight = lambda shape: pl.BlockSpec((None, None) + shape, lambda t: (layer, half, 0, 0), pipeline_mode=one)
    if split:
        h_specs = [pl.BlockSpec((tm, D), lambda t: (jnp.minimum(t, n_lat_tiles - 1), 0)),
                   pl.BlockSpec((tm, D), lambda t: (jnp.maximum(t - n_lat_tiles, 0), 0))]
        h_args = list(h)
    else:
        h_specs = [pl.BlockSpec((tm, D), lambda t: (t, 0))]
        h_args = [h]
    if mix is not None:
        o, w_out = mix
        h_specs += [pl.BlockSpec((tm, o.shape[1]), lambda t: (t, 0)),
                    pl.BlockSpec(w_out.shape, const, pipeline_mode=one)]
        h_args += [o, w_out.astype(BF16)]
    return pl.pallas_call(
        functools.partial(_ffn_kernel, off=6 * half, final_norm=final_norm,
                          n_lat_tiles=n_lat_tiles if split else None, mixed=mix is not None),
        grid=(n_rows // tm,),
        in_specs=h_specs + [pl.BlockSpec((1, N_MOD, D), lambda t: (midx(t), 0, 0)),
                            pl.BlockSpec((1, D), const),
                            pl.BlockSpec((1, D), const),
                            weight((D, D_FF)), weight((D, D_FF)), weight((D_FF, D))],
        out_specs=pl.BlockSpec((tm, D), lambda t: (t, 0)),
        out_shape=jax.ShapeDtypeStruct((n_rows, D), F32),
        scratch_shapes=[pltpu.VMEM((tm, D_FF), BF16)],
        compiler_params=_params("arbitrary"),
        name="ffn",
    )(*h_args, mod, gain.reshape(1, D), fin_w.reshape(1, D), wg, wu, wd)


def _rope_tables(S, rot_dim, pad_rows, lane_off):
    rows = S // GRID_W
    row = jnp.repeat(jnp.arange(rows, dtype=F32), GRID_W)
    col = jnp.tile(jnp.arange(GRID_W, dtype=F32), rows)
    axis_dim = rot_dim // 2
    inv_freq = ROPE_BASE ** (-jnp.arange(0, axis_dim, 2, dtype=F32) / axis_dim)
    ang_r = row[:, None] * inv_freq[None, :]
    ang_c = col[:, None] * inv_freq[None, :]
    ang = jnp.concatenate([ang_r, ang_r, ang_c, ang_c], axis=-1)
    cos, sin = jnp.cos(ang), jnp.sin(ang)
    f = rot_dim // 4
    first = (jnp.arange(rot_dim) % (2 * f)) < f
    sa = jnp.where(first[None, :], -sin, 0.0)
    sb = jnp.where(first[None, :], 0.0, sin)
    period = 64 if lane_off + rot_dim <= 64 else LANES

    def widen(t, fill):
        blk = jnp.full((S, period), fill, F32).at[:, lane_off:lane_off + rot_dim].set(t)
        blk = jnp.tile(blk, (1, LANES // period))
        return jnp.concatenate([blk, jnp.full((pad_rows, LANES), fill, F32)], axis=0)

    return widen(cos, 1.0), widen(sa, 0.0), widen(sb, 0.0)


def _attn_specs(lay, tq, widths):
    S, C, B = lay.S, lay.C, lay.B
    nq, nc = S // tq, C // tq
    lat_blocks = lay.n_lat // C

    def q_index(b, i):
        return jnp.where(i < nq, b * nq + i, B * nq + b * nc + (i - nq))

    lat = [pl.BlockSpec((S, w), lambda b, i: (b, 0)) for w in widths]
    ctx = [pl.BlockSpec((C, w), lambda b, i: (lat_blocks + b, 0)) for w in widths]
    return q_index, nq, nc, lat, ctx


def _gqa_proj_kernel(h_ref, mod_ref, gain_ref, w_ref, cos_ref, sa_ref, sb_ref, q_ref, k_ref, v_ref):
    qd, kd = GQA_HEADS * GQA_HEAD_DIM, GQA_KV_HEADS * GQA_HEAD_DIM
    a = _pre(h_ref[...], gain_ref[...], mod_ref[0], 3).astype(BF16)
    cos, sa, sb = cos_ref[...], sa_ref[...], sb_ref[...]
    f = GQA_HEAD_DIM // 4
    scale = GQA_HEAD_DIM ** -0.5
    lane = lax.broadcasted_iota(jnp.int32, (1, MXU_COLS), 1)
    ones_half = jnp.where((lane & (2 * GQA_HEAD_DIM - 1)) >= GQA_HEAD_DIM, 1.0, 0.0)

    def emit(col, tile):
        if col >= qd + kd:
            v_ref[:, col - qd - kd:col - qd - kd + MXU_COLS] = (tile + ones_half).astype(BF16)
            return
        for j in range(MXU_COLS // LANES):
            x = _rope(tile[:, j * LANES:(j + 1) * LANES], cos, sa, sb, f)
            lo = col + j * LANES
            if lo < qd:
                q_ref[:, lo:lo + LANES] = (x * scale).astype(BF16)
            else:
                k_ref[:, lo - qd:lo - qd + LANES] = x.astype(BF16)

    _chunked_proj(a, w_ref, emit)


def _gqa_attn_kernel(sink_ref, q_ref, kp_ref, kc_ref, kn_ref, vp_ref, vc_ref, vn_ref, kx_ref, vx_ref,
                     o_ref, kcat_ref, vcat_ref, *, nq):
    i = pl.program_id(1)
    blk = GQA_BLOCK
    G = GQA_HEADS // GQA_KV_HEADS
    HD = GQA_HEAD_DIM
    VW = 2 * HD
    span = 3 * blk

    def run(k_ref, v_ref, masks):
        kvs = range(GQA_KV_HEADS)
        qs = [jnp.concatenate([q_ref[:, (G * g + n) * HD:(G * g + n + 1) * HD] for n in range(G)], axis=0)
              for g in kvs]
        sinks = [jnp.concatenate([jnp.full((blk, 1), sink_ref[G * g + n], F32) for n in range(G)], axis=0)
                 for g in kvs]
        ss = [_dot_nt(qs[g], k_ref[:, g * HD:(g + 1) * HD]) for g in kvs]
        if masks is not None:
            prev_ok, next_ok = masks
            ss = [jnp.concatenate([jnp.where(prev_ok, s[:, :blk], NEG_INF), s[:, blk:2 * blk],
                                   jnp.where(next_ok, s[:, 2 * blk:span], NEG_INF), s[:, span:]], axis=1)
                  for s in ss]
        ms = [jnp.maximum(jnp.max(ss[g], axis=-1, keepdims=True), sinks[g]) for g in kvs]
        accs = [_dot(jnp.exp(ss[g] - ms[g]).astype(BF16),
                     v_ref[:, (g // 2) * MXU_COLS:(g // 2 + 1) * MXU_COLS])[:, (g % 2) * VW:(g % 2 + 1) * VW]
                for g in kvs]
        outs = [accs[g][:, :HD] * (1.0 / (accs[g][:, HD:] + jnp.exp(sinks[g] - ms[g]))) for g in kvs]
        heads = [outs[g][n * blk:(n + 1) * blk] for g in kvs for n in range(G)]
        o_ref[...] = jnp.concatenate(heads, axis=-1).astype(BF16)

    @pl.when(i < nq)
    def _():
        for n, (k_ref, v_ref) in enumerate([(kp_ref, vp_ref), (kc_ref, vc_ref), (kn_ref, vn_ref)]):
            kcat_ref[n * blk:(n + 1) * blk] = k_ref[...]
            vcat_ref[n * blk:(n + 1) * blk] = v_ref[...]
        kcat_ref[span:] = kx_ref[...]
        vcat_ref[span:] = vx_ref[...]
        row = lax.broadcasted_iota(jnp.int32, (G * blk, blk), 0) & (blk - 1)
        col = lax.broadcasted_iota(jnp.int32, (G * blk, blk), 1)
        prev_ok = (row + blk - col <= GQA_WINDOW) & (i >= 1)
        next_ok = (col + blk - row <= GQA_WINDOW) & (i + 1 < nq)
        run(kcat_ref, vcat_ref, (prev_ok, next_ok))

    @pl.when(i >= nq)
    def _():
        run(kx_ref, vx_ref, None)


def _gqa_mixer(lay, h, mod, gain, w_in, sinks, with_ctx_out):
    D = h.shape[1]
    B, S, C = lay.B, lay.S, lay.C
    qd, kd = GQA_HEADS * GQA_HEAD_DIM, GQA_KV_HEADS * GQA_HEAD_DIM
    w_v = jnp.pad(w_in[:, qd + kd:].reshape(D, GQA_KV_HEADS, GQA_HEAD_DIM), ((0, 0), (0, 0), (0, GQA_HEAD_DIM)))
    w_pad = jnp.concatenate([w_in[:, :qd + kd], w_v.reshape(D, 2 * kd)], axis=1)
    tm = lay.tile(512)
    midx, pidx = lay.mod_index(tm), lay.pos_index(tm)
    cos, sa, sb = _rope_tables(S, GQA_HEAD_DIM, tm, 0)
    tab = pl.BlockSpec((tm, LANES), lambda t: (pidx(t), 0))
    q, k, v = pl.pallas_call(
        _gqa_proj_kernel,
        grid=(lay.n_all // tm,),
        in_specs=[pl.BlockSpec((tm, D), lambda t: (t, 0)),
                  pl.BlockSpec((1, N_MOD, D), lambda t: (midx(t), 0, 0)),
                  pl.BlockSpec((1, D), lambda t: (0, 0)),
                  pl.BlockSpec((D, qd + 3 * kd), lambda t: (0, 0)),
                  tab, tab, tab],
        out_specs=[pl.BlockSpec((tm, qd), lambda t: (t, 0)),
                   pl.BlockSpec((tm, kd), lambda t: (t, 0)),
                   pl.BlockSpec((tm, 2 * kd), lambda t: (t, 0))],
        out_shape=[jax.ShapeDtypeStruct((lay.n_all, qd), BF16),
                   jax.ShapeDtypeStruct((lay.n_all, kd), BF16),
                   jax.ShapeDtypeStruct((lay.n_all, 2 * kd), BF16)],
        compiler_params=_params("arbitrary"),
        name="gqa_proj",
    )(h, mod, gain.reshape(1, D), w_pad.astype(BF16), cos, sa, sb)

    blk = GQA_BLOCK
    assert S % blk == 0 and C % blk == 0
    nq, nc = S // blk, C // blk
    n_i = nq + (nc if with_ctx_out else 0)
    lat_blocks = lay.n_lat // C

    def q_index(b, i):
        return jnp.where(i < nq, b * nq + i, B * nq + b * nc + (i - nq))

    def win(delta):
        return lambda b, i: (b * nq + jnp.clip(i + delta, 0, nq - 1), 0)

    def windows(width):
        return [pl.BlockSpec((blk, width), win(d)) for d in (-1, 0, 1)]

    def ctx_spec(width):
        return pl.BlockSpec((C, width), lambda b, i: (lat_blocks + b, 0))

    row = pl.BlockSpec((blk, qd), lambda b, i: (q_index(b, i), 0))
    return pl.pallas_call(
        functools.partial(_gqa_attn_kernel, nq=nq),
        grid=(B, n_i),
        in_specs=[pl.BlockSpec(memory_space=pltpu.SMEM), row,
                  *windows(kd), *windows(2 * kd), ctx_spec(kd), ctx_spec(2 * kd)],
        out_specs=row,
        out_shape=jax.ShapeDtypeStruct((lay.n_all if with_ctx_out else lay.n_lat, qd), BF16),
        scratch_shapes=[pltpu.VMEM((3 * blk + C, kd), BF16), pltpu.VMEM((3 * blk + C, 2 * kd), BF16)],
        compiler_params=_params("arbitrary", "arbitrary"),
        name="gqa_attn",
    )(sinks.astype(F32), q, k, k, k, v, v, v, k, v)


def _diff_proj_kernel(h_ref, mod_ref, gain_ref, w_ref, cos_ref, sa_ref, sb_ref, q_ref, k_ref, v_ref):
    qk = DIFF_HEADS * 2 * DIFF_HEAD_DIM
    a = _pre(h_ref[...], gain_ref[...], mod_ref[0], 3).astype(BF16)
    cos, sa, sb = cos_ref[...], sa_ref[...], sb_ref[...]
    f = DIFF_HEAD_DIM // 4
    scale = DIFF_HEAD_DIM ** -0.5
    vw = 2 * DIFF_HEAD_DIM
    ones = jnp.ones((a.shape[0], vw), BF16)

    def emit(col, tile):
        for j in range(MXU_COLS // LANES):
            lo = col + j * LANES
            x = tile[:, j * LANES:(j + 1) * LANES]
            if lo < qk:
                q_ref[:, lo:lo + LANES] = (_rope(x, cos, sa, sb, f) * scale).astype(BF16)
            elif lo < 2 * qk:
                k_ref[:, lo - qk:lo - qk + LANES] = _rope(x, cos, sa, sb, f).astype(BF16)
            else:
                hd = (lo - 2 * qk) // vw
                v_ref[:, 2 * hd * vw:(2 * hd + 1) * vw] = x.astype(BF16)
                v_ref[:, (2 * hd + 1) * vw:(2 * hd + 2) * vw] = ones

    _chunked_proj(a, w_ref, emit)


def _diff_attn_kernel(lam_ref, sub_ref, q_ref, kl_ref, vl_ref, kx_ref, vx_ref, o_ref, *, nq, lambda_init):
    i = pl.program_id(1)
    HD = DIFF_HEAD_DIM
    VW = 4 * HD
    lp = lam_ref[...]
    lam = (jnp.exp(jnp.sum(lp[0:1] * lp[1:2], keepdims=True))
           - jnp.exp(jnp.sum(lp[2:3] * lp[3:4], keepdims=True)) + lambda_init)

    def run(kv_pairs):
        def scores(h):
            los = [(2 * h + j) * HD for j in range(2)]
            return [[_dot_nt(q_ref[:, lo:lo + HD], k_ref[:, lo:lo + HD]) for k_ref, _ in kv_pairs] for lo in los]

        heads = []
        ss = scores(0)
        for h in range(DIFF_HEADS):
            nxt = scores(h + 1) if h + 1 < DIFF_HEADS else None
            maps = []
            for parts in ss:
                m = functools.reduce(jnp.maximum, [jnp.max(s, axis=-1, keepdims=True) for s in parts])
                acc = None
                for s, (_, v_ref) in zip(parts, kv_pairs):
                    t = _dot(jnp.exp(s - m).astype(BF16), v_ref[:, h * VW:(h + 1) * VW])
                    acc = t if acc is None else acc + t
                maps.append(acc[:, :VW // 2] * (1.0 / acc[:, VW // 2:]))
            o = maps[0] - lam * maps[1]
            heads.append(_rmsnorm(o, sub_ref[...]) * (1.0 - lambda_init))
            ss = nxt
        o_ref[...] = jnp.concatenate(heads, axis=-1).astype(BF16)

    @pl.when(i < nq)
    def _():
        run([(kl_ref, vl_ref), (kx_ref, vx_ref)])

    @pl.when(i >= nq)
    def _():
        run([(kx_ref, vx_ref)])


def _diff_mixer(lay, h, mod, gain, w_in, lam_params, subln_w, lambda_init, with_ctx_out):
    D = h.shape[1]
    B, S, C = lay.B, lay.S, lay.C
    qk = DIFF_HEADS * 2 * DIFF_HEAD_DIM
    tm = lay.tile(512)
    midx, pidx = lay.mod_index(tm), lay.pos_index(tm)
    cos, sa, sb = _rope_tables(S, DIFF_HEAD_DIM, tm, 0)
    tab = pl.BlockSpec((tm, LANES), lambda t: (pidx(t), 0))
    out = jax.ShapeDtypeStruct((lay.n_all, qk), BF16)
    q, k, v = pl.pallas_call(
        _diff_proj_kernel,
        grid=(lay.n_all // tm,),
        in_specs=[pl.BlockSpec((tm, D), lambda t: (t, 0)),
                  pl.BlockSpec((1, N_MOD, D), lambda t: (midx(t), 0, 0)),
                  pl.BlockSpec((1, D), lambda t: (0, 0)),
                  pl.BlockSpec((D, 3 * qk), lambda t: (0, 0)),
                  tab, tab, tab],
        out_specs=[pl.BlockSpec((tm, qk), lambda t: (t, 0))] * 2 + [pl.BlockSpec((tm, 2 * qk), lambda t: (t, 0))],
        out_shape=[out, out, jax.ShapeDtypeStruct((lay.n_all, 2 * qk), BF16)],
        compiler_params=_params("arbitrary"),
        name="diff_proj",
    )(h, mod, gain.reshape(1, D), w_in.astype(BF16), cos, sa, sb)

    tq = _divisor(math.gcd(S, C), 256)
    q_index, nq, nc, lat, ctx = _attn_specs(lay, tq, [qk, 2 * qk])
    row = pl.BlockSpec((tq, qk), lambda b, i: (q_index(b, i), 0))
    return pl.pallas_call(
        functools.partial(_diff_attn_kernel, nq=nq, lambda_init=lambda_init),
        grid=(B, nq + (nc if with_ctx_out else 0)),
        in_specs=[pl.BlockSpec((4, DIFF_HEAD_DIM), lambda b, i: (0, 0)),
                  pl.BlockSpec((1, 2 * DIFF_HEAD_DIM), lambda b, i: (0, 0)),
                  row, *lat, *ctx],
        out_specs=row,
        out_shape=jax.ShapeDtypeStruct((lay.n_all if with_ctx_out else lay.n_lat, qk), BF16),
        compiler_params=_params("arbitrary", "arbitrary"),
        name="diff_attn",
    )(lam_params.astype(F32), subln_w.reshape(1, -1), q, k, v, k, v)


def _hgrn_proj_kernel(h_ref, mod_ref, gain_ref, lb_ref, w_ref, o_ref, *, layer):
    KD = HGRN_HEADS * HGRN_KEY_DIM
    W = 2 * LANES
    a = _pre(h_ref[...], gain_ref[...], mod_ref[0], 3).astype(BF16)
    raw = lb_ref[...]
    e = jnp.exp(raw - jnp.max(raw, axis=0, keepdims=True))
    soft = e * (1.0 / jnp.sum(e, axis=0, keepdims=True))
    lb = jnp.sum(soft[1:layer + 1], axis=0, keepdims=True) if layer else jnp.zeros_like(soft[0:1])
    n = w_ref.shape[1] // W

    def proj(c):
        return _dot(a, w_ref[:, c * W:(c + 1) * W])

    cur = proj(0)
    for c in range(n):
        nxt = proj(c + 1) if c + 1 < n else None
        group, col = divmod(c * W, KD)
        if group in (0, 4):
            out = _silu(cur)
        elif group in (1, 2):
            lbc = lb[:, col:col + W]
            out = lbc + (1.0 - lbc) * jax.nn.sigmoid(cur)
        else:
            out = cur
        o_ref[:, c * W:(c + 1) * W] = out
        cur = nxt


SUBLANES = 8


def _running_sum(x, reverse):
    n, w = x.shape
    n_groups = n // SUBLANES
    x = x.reshape(n_groups, SUBLANES, w)
    within = lax.broadcasted_iota(jnp.int32, (1, SUBLANES, 1), 1)
    step = 1
    while step < SUBLANES:
        if reverse:
            x = x + jnp.where(within < SUBLANES - step, pltpu.roll(x, SUBLANES - step, 1), 0.0)
        else:
            x = x + jnp.where(within >= step, pltpu.roll(x, step, 1), 0.0)
        step *= 2
    out = [None] * n_groups
    carry = None
    for j in (reversed(range(n_groups)) if reverse else range(n_groups)):
        out[j] = x[j] if carry is None else x[j] + carry
        carry = out[j][0:1] if reverse else out[j][SUBLANES - 1:SUBLANES]
    return jnp.concatenate(out, axis=0)


def _hgrn_scan_kernel(qf_ref, ff_ref, vf_ref, qb_ref, fb_ref, vb_ref, of_ref, ob_ref, st_ref, *, n_chunks):
    step = pl.program_id(1)
    C = HGRN_CHUNK
    DK, DV, H = HGRN_KEY_DIM, HGRN_VAL_DIM, HGRN_HEADS

    @pl.when(step == 0)
    def _():
        st_ref[...] = jnp.zeros_like(st_ref)

    r = lax.broadcasted_iota(jnp.int32, (C, C), 0)
    c = lax.broadcasted_iota(jnp.int32, (C, C), 1)
    ks = [slice(h * DK, (h + 1) * DK) for h in range(H)]
    vs = [slice(h * DV, (h + 1) * DV) for h in range(H)]
    dirs = [(qf_ref, ff_ref, vf_ref, of_ref, c <= r, C // 2, C - 1),
            (qb_ref, fb_ref, vb_ref, ob_ref, c >= r, C - 1 - C // 2, 0)]

    def prepare(d, n):
        q_ref, f_ref, v_ref, _, tri, mid_row, last_row = dirs[d]
        base = (n if d == 0 else n_chunks - 1 - n) * C
        q = q_ref[base:base + C, :]
        f = f_ref[base:base + C, :]
        vt = v_ref[base:base + C, :].T.astype(BF16)
        k = 1.0 - f
        g = _running_sum(jnp.log2(f), reverse=d == 1)
        g_mid = g[mid_row:mid_row + 1]
        g_last = g[last_row:last_row + 1]
        t = g - g_mid
        qa_f = q * jnp.exp2(t)
        kb_f = k * jnp.exp2(-t)
        qa = qa_f.astype(BF16)
        kb = kb_f.astype(BF16)
        qg = (qa_f * jnp.exp2(g_mid)).astype(BF16)
        kd = (kb_f * jnp.exp2(g_last - g_mid)).astype(BF16)
        a = [jnp.where(tri, _dot_nt(qa[:, ks[h]], kb[:, ks[h]]), 0.0).astype(BF16) for h in range(H)]
        lhs = [jnp.concatenate([qg[:, ks[h]], a[h]], axis=1) for h in range(H)]
        update = [_dot(vt[vs[h], :], kd[:, ks[h]]) for h in range(H)]
        return base, lhs, vt, jnp.exp2(g_last), update

    cur = [prepare(d, 0) for d in range(2)]
    for n in range(n_chunks):
        nxt = [prepare(d, n + 1) for d in range(2)] if n + 1 < n_chunks else None
        for d in range(2):
            base, lhs, vt, decay, update = cur[d]
            st = [st_ref[d, vs[h], :] for h in range(H)]
            outs = [_dot_nt(lhs[h], jnp.concatenate([st[h].astype(BF16), vt[vs[h], :]], axis=1)) for h in range(H)]
            for h in range(H):
                st_ref[d, vs[h], :] = st[h] * decay[:, ks[h]] + update[h]
            dirs[d][3][base:base + C, :] = jnp.concatenate(outs, axis=-1)
        cur = nxt


def _hgrn_out_kernel(of_ref, ob_ref, gate_ref, nw_ref, out_ref):
    DV = HGRN_VAL_DIM
    o = of_ref[...] + ob_ref[...]
    gate = gate_ref[...]
    nw = nw_ref[...]
    ys = [_rmsnorm(o[:, h * DV:(h + 1) * DV], nw) * gate[:, h * DV:(h + 1) * DV] for h in range(HGRN_HEADS)]
    out_ref[...] = jnp.concatenate(ys, axis=-1).astype(BF16)


def _hgrn_mixer(lay, h, mod, gain, w_in, norm_w, lower_bounds, layer):
    D = h.shape[1]
    B, S, C = lay.B, lay.S, lay.C
    KD = HGRN_HEADS * HGRN_KEY_DIM
    tm = lay.tile(512)
    midx = lay.mod_index(tm)
    n_cols = w_in.shape[1] // KD
    proj = pl.pallas_call(
        functools.partial(_hgrn_proj_kernel, layer=layer),
        grid=(lay.n_all // tm,),
        in_specs=[pl.BlockSpec((tm, D), lambda t: (t, 0)),
                  pl.BlockSpec((1, N_MOD, D), lambda t: (midx(t), 0, 0)),
                  pl.BlockSpec((1, D), lambda t: (0, 0)),
                  pl.BlockSpec(lower_bounds.shape, lambda t: (0, 0)),
                  pl.BlockSpec((D, n_cols * KD), lambda t: (0, 0), pipeline_mode=pl.Buffered(1))],
        out_specs=pl.BlockSpec((tm, n_cols * KD), lambda t: (t, 0)),
        out_shape=jax.ShapeDtypeStruct((lay.n_all, n_cols * KD), F32),
        compiler_params=_params("arbitrary"),
        name="hgrn_proj",
    )(h, mod, gain.reshape(1, D), lower_bounds.astype(F32), w_in.astype(BF16))

    ts = C
    n_lat_steps = S // ts
    lat_blocks = lay.n_lat // ts

    def rows(d):
        def index(b, s):
            lat = b * n_lat_steps + (s - 1 if d == 0 else n_lat_steps - s)
            return jnp.where(s == 0, lat_blocks + b, lat)
        return index

    def col(d, j):
        return pl.BlockSpec((ts, KD), lambda b, s: (rows(d)(b, s), j))

    o_f, o_b = pl.pallas_call(
        functools.partial(_hgrn_scan_kernel, n_chunks=ts // HGRN_CHUNK),
        grid=(B, 1 + n_lat_steps),
        in_specs=[col(0, 0), col(0, 1), col(0, 3), col(1, 0), col(1, 2), col(1, 3)],
        out_specs=[col(0, 0), col(1, 0)],
        out_shape=[jax.ShapeDtypeStruct((lay.n_all, KD), F32)] * 2,
        scratch_shapes=[pltpu.VMEM((2, HGRN_HEADS * HGRN_VAL_DIM, HGRN_KEY_DIM), F32)],
        compiler_params=_params("arbitrary", "arbitrary"),
        name="hgrn_scan",
    )(proj, proj, proj, proj, proj, proj)

    tr = lay.tile(1024)
    row = pl.BlockSpec((tr, KD), lambda t: (t, 0))
    return pl.pallas_call(
        _hgrn_out_kernel,
        grid=(lay.n_all // tr,),
        in_specs=[row, row,
                  pl.BlockSpec((tr, KD), lambda t: (t, 4)),
                  pl.BlockSpec((1, HGRN_VAL_DIM), lambda t: (0, 0))],
        out_specs=row,
        out_shape=jax.ShapeDtypeStruct((lay.n_all, KD), BF16),
        compiler_params=_params("arbitrary"),
        name="hgrn_out",
    )(o_f, o_b, proj, norm_w.reshape(1, -1))


MLA_QK_PAD = LANES


def _mla_proj_kernel(h_ref, mod_ref, gain_ref, wd_ref, qn_ref, kvn_ref, wuq_ref, wukv_ref,
                     cos_ref, sa_ref, sb_ref, q_ref, k_ref, v_ref):
    H, P = MLA_HEADS, MLA_QK_PAD
    a = _pre(h_ref[...], gain_ref[...], mod_ref[0], 3).astype(BF16)
    dn = _dot(a, wd_ref[...])
    cos, sa, sb = cos_ref[...], sa_ref[...], sb_ref[...]
    f = MLA_ROPE // 4
    cq = _rmsnorm(dn[:, :MLA_Q_LORA], qn_ref[...]).astype(BF16)
    ckv = _rmsnorm(dn[:, MLA_Q_LORA:MLA_Q_LORA + MLA_KV_LORA], kvn_ref[...]).astype(BF16)
    kr = _rope(dn[:, MLA_Q_LORA + MLA_KV_LORA:], cos, sa, sb, f)
    lane = lax.broadcasted_iota(jnp.int32, (1, P), 1)
    ones_half = jnp.where(lane >= MLA_V_DIM, 1.0, 0.0)

    def emit_q(col, tile):
        for j in range(MXU_COLS // P):
            lo = col + j * P
            q_ref[:, lo:lo + P] = _rope(tile[:, j * P:(j + 1) * P], cos, sa, sb, f).astype(BF16)

    def emit_kv(col, tile):
        for j in range(MXU_COLS // P):
            lo = col + j * P
            x = tile[:, j * P:(j + 1) * P]
            if lo < H * P:
                k_ref[:, lo:lo + P] = (x + kr).astype(BF16)
            else:
                v_ref[:, lo - H * P:lo - H * P + P] = (x + ones_half).astype(BF16)

    _chunked_proj(cq, wuq_ref, emit_q)
    _chunked_proj(ckv, wukv_ref, emit_kv)


def _mla_attn_kernel(q_ref, kl_ref, vl_ref, kx_ref, vx_ref, o_ref, *, nq, with_ctx):
    i = pl.program_id(1)
    P, VD = MLA_QK_PAD, MLA_V_DIM
    c = ((MLA_NOPE + MLA_ROPE) ** -0.5) * LOG2_E

    def run(kv_pairs):
        def scores(hd):
            qh = q_ref[:, hd * P:(hd + 1) * P]
            return [_dot_nt(qh, k_ref[:, hd * P:(hd + 1) * P]) for k_ref, _ in kv_pairs]

        heads = []
        ss = scores(0)
        for hd in range(MLA_HEADS):
            nxt = scores(hd + 1) if hd + 1 < MLA_HEADS else None
            m = functools.reduce(jnp.maximum, [jnp.max(s, axis=-1, keepdims=True) for s in ss])
            mc = m * c
            pair = (hd // 2) * MXU_COLS
            acc = None
            for s, (_, v_ref) in zip(ss, kv_pairs):
                t = _dot(jnp.exp2(s * c - mc).astype(BF16), v_ref[:, pair:pair + MXU_COLS])
                acc = t if acc is None else acc + t
            acc = acc[:, (hd % 2) * P:(hd % 2 + 1) * P]
            heads.append(acc[:, :VD] * (1.0 / acc[:, VD:]))
            ss = nxt
        o_ref[...] = jnp.concatenate(heads, axis=-1).astype(BF16)

    if not with_ctx:
        run([(kl_ref, vl_ref), (kx_ref, vx_ref)])
        return

    @pl.when(i < nq)
    def _():
        run([(kl_ref, vl_ref), (kx_ref, vx_ref)])

    @pl.when(i >= nq)
    def _():
        run([(kx_ref, vx_ref)])


def _mla_mixer(lay, h, mod, gain, w_down, q_norm_w, kv_norm_w, w_uq, w_ukv, with_ctx_out):
    D = h.shape[1]
    B, S, C = lay.B, lay.S, lay.C
    H, P, VD = MLA_HEADS, MLA_QK_PAD, MLA_V_DIM
    qk = MLA_NOPE + MLA_ROPE
    lora = MLA_Q_LORA + MLA_KV_LORA
    kr_cols = jnp.zeros((D, P), F32).at[:, MLA_NOPE:qk].set(w_down[:, lora:])
    wd = jnp.concatenate([w_down[:, :lora], kr_cols], axis=1).astype(BF16)
    wuq = jnp.pad(w_uq.reshape(MLA_Q_LORA, H, qk), ((0, 0), (0, 0), (0, P - qk))).reshape(MLA_Q_LORA, H * P)
    ukv = w_ukv.reshape(MLA_KV_LORA, H, MLA_NOPE + VD)
    wkn = jnp.pad(ukv[..., :MLA_NOPE], ((0, 0), (0, 0), (0, P - MLA_NOPE))).reshape(MLA_KV_LORA, H * P)
    wv = jnp.pad(ukv[..., MLA_NOPE:], ((0, 0), (0, 0), (0, P - VD))).reshape(MLA_KV_LORA, H * P)
    wukv = jnp.concatenate([wkn, wv], axis=1)

    tm = lay.tile(512)
    midx, pidx = lay.mod_index(tm), lay.pos_index(tm)
    cos, sa, sb = _rope_tables(S, MLA_ROPE, tm, MLA_NOPE)
    tab = pl.BlockSpec((tm, LANES), lambda t: (pidx(t), 0))
    full = lambda shape: pl.BlockSpec(shape, lambda t: (0, 0))
    q, k, v = pl.pallas_call(
        _mla_proj_kernel,
        grid=(lay.n_all // tm,),
        in_specs=[pl.BlockSpec((tm, D), lambda t: (t, 0)),
                  pl.BlockSpec((1, N_MOD, D), lambda t: (midx(t), 0, 0)),
                  full((1, D)), full((D, lora + P)), full((1, MLA_Q_LORA)), full((1, MLA_KV_LORA)),
                  full((MLA_Q_LORA, H * P)), full((MLA_KV_LORA, 2 * H * P)),
                  tab, tab, tab],
        out_specs=[pl.BlockSpec((tm, H * P), lambda t: (t, 0))] * 3,
        out_shape=[jax.ShapeDtypeStruct((lay.n_all, H * P), BF16)] * 3,
        compiler_params=_params("arbitrary"),
        name="mla_proj",
    )(h, mod, gain.reshape(1, D), wd, q_norm_w.reshape(1, -1), kv_norm_w.reshape(1, -1),
      wuq.astype(BF16), wukv.astype(BF16), cos, sa, sb)

    tq = _divisor(math.gcd(S, C), 256)
    q_index, nq, nc, lat, ctx = _attn_specs(lay, tq, [H * P, H * P])
    return pl.pallas_call(
        functools.partial(_mla_attn_kernel, nq=nq, with_ctx=with_ctx_out),
        grid=(B, nq + (nc if with_ctx_out else 0)),
        in_specs=[pl.BlockSpec((tq, H * P), lambda b, i: (q_index(b, i), 0)), *lat, *ctx],
        out_specs=pl.BlockSpec((tq, H * VD), lambda b, i: (q_index(b, i), 0)),
        out_shape=jax.ShapeDtypeStruct((lay.n_all if with_ctx_out else lay.n_lat, H * VD), BF16),
        compiler_params=_params("arbitrary", "arbitrary"),
        name="mla_attn",
    )(q, k, v, k, v)


def kernel(x, c, ctx, c_ctx, ada_w, ada_b, norm_w, final_norm_w, ffn_w_gate, ffn_w_up, ffn_w_down,
           gqa_w_in, gqa_w_out, gqa_sinks, diff_w_in, diff_w_out, diff_lambda, diff_subln_w,
           hgrn_w_in, hgrn_w_out, hgrn_norm_w, hgrn_lower_bounds,
           mla_w_down, mla_q_norm_w, mla_kv_norm_w, mla_w_uq, mla_w_ukv, mla_w_out):
    B, S, D = x.shape
    C = ctx.shape[1]
    lay = _Layout(B, S, C)
    depth = ada_w.shape[0]

    rows = -(-(B + 1) // 8) * 8
    cc = jnp.zeros((rows, D), F32).at[:B].set(c).at[B].set(c_ctx)
    mods = _ada_table(cc, ada_w, ada_b).reshape(depth, rows, N_MOD, D)

    wg, wu, wd = ffn_w_gate.astype(BF16), ffn_w_up.astype(BF16), ffn_w_down.astype(BF16)
    h = (x.reshape(B * S, D), ctx.reshape(B * C, D))
    for i in range(depth):
        kind, j = i % 4, i // 4
        last = i == depth - 1
        mod = mods[i]
        h = _ffn(lay, h, mod, norm_w[i, 0], final_norm_w, wg, wu, wd, i, 0)
        if kind == 0:
            o = _gqa_mixer(lay, h, mod, norm_w[i, 1], gqa_w_in[j], gqa_sinks[j], not last)
            w_out = gqa_w_out[j]
        elif kind == 1:
            lambda_init = 0.8 - 0.6 * math.exp(-0.3 * i)
            o = _diff_mixer(lay, h, mod, norm_w[i, 1], diff_w_in[j], diff_lambda[j], diff_subln_w[j],
                            lambda_init, not last)
            w_out = diff_w_out[j]
        elif kind == 2:
            o = _hgrn_mixer(lay, h, mod, norm_w[i, 1], hgrn_w_in[j], hgrn_norm_w[j], hgrn_lower_bounds, i)
            w_out = hgrn_w_out[j]
        else:
            o = _mla_mixer(lay, h, mod, norm_w[i, 1], mla_w_down[j], mla_q_norm_w[j], mla_kv_norm_w[j],
                           mla_w_uq[j], mla_w_ukv[j], not last)
            w_out = mla_w_out[j]
        h = _ffn(lay, h, mod, norm_w[i, 2], final_norm_w, wg, wu, wd, i, 1, lat_only=last, final_norm=last,
                 mix=(o, w_out))
    return h[:B * S].reshape(B, S, D)
```

```python
import functools
import math

import jax
import jax.numpy as jnp
from jax import lax
from jax.experimental import pallas as pl
from jax.experimental.pallas import tpu as pltpu

F32 = jnp.float32
BF16 = jnp.bfloat16

D_MODEL = 1024
DEPTH = 4
GRID_W = 64
N_MOD = 9
ROPE_BASE = 10000.0
EPS = 1e-6
NEG_INF = -1e30
D_FF = 2816

GQA_HEADS = 16
GQA_KV_HEADS = 4
GQA_HEAD_DIM = 64
GQA_WINDOW = 128
GQA_BLOCK = 128

DIFF_HEADS = 8
DIFF_HEAD_DIM = 64

HGRN_HEADS = 8
HGRN_KEY_DIM = 128
HGRN_VAL_DIM = D_MODEL // HGRN_HEADS
HGRN_CHUNK = 64

MLA_HEADS = 16
MLA_Q_LORA = 256
MLA_KV_LORA = 256
MLA_NOPE = 64
MLA_ROPE = 32
MLA_V_DIM = 64

LOG2_E = 1.4426950408889634
LANES = 128
SUBLANES = 8
MXU_COLS = 2 * LANES
VMEM_LIMIT = 56 * 1024 * 1024
ROW_TILE = 512
QUERY_TILE = 256


def _dot(a, b):
    return jnp.dot(a, b, preferred_element_type=F32)


def _dot_nt(a, b):
    return lax.dot_general(a, b, (((1,), (1,)), ((), ())), preferred_element_type=F32)


def _params(*sem):
    return pltpu.CompilerParams(dimension_semantics=sem, vmem_limit_bytes=VMEM_LIMIT)


def _divisor(n, pref):
    t = min(n, pref)
    while n % t:
        t -= 8
    return t


class _Layout:
    def __init__(self, B, S, C):
        self.B, self.S, self.C = B, S, C
        self.n_lat = B * S
        self.n_all = B * S + B * C

    def tile(self, pref):
        return _divisor(math.gcd(self.S, self.B * self.C), pref)

    def mod_index(self, tm):
        n_lat_tiles, per_batch, B = self.n_lat // tm, self.S // tm, self.B
        return lambda t: jnp.where(t < n_lat_tiles, t // per_batch, B)

    def pos_index(self, tm):
        n_lat_tiles, per_batch = self.n_lat // tm, self.S // tm
        return lambda t: jnp.where(t < n_lat_tiles, t % per_batch, per_batch)


def _rmsnorm(x, w):
    return (x * lax.rsqrt(jnp.mean(x * x, axis=-1, keepdims=True) + EPS)) * w


def _pre(h, gain, m, off):
    return _rmsnorm(h, gain) * (1.0 + m[off + 1:off + 2]) + m[off:off + 1]


def _silu(x):
    return x * jax.nn.sigmoid(x)


def _chunked_proj(a, w_ref, emit):
    n = w_ref.shape[1] // MXU_COLS
    cur = _dot(a, w_ref[:, :MXU_COLS])
    for c in range(n):
        nxt = _dot(a, w_ref[:, (c + 1) * MXU_COLS:(c + 2) * MXU_COLS]) if c + 1 < n else None
        emit(c * MXU_COLS, cur)
        cur = nxt


def _rope(x, cos, sa, sb, shift):
    return x * cos + pltpu.roll(x, LANES - shift, 1) * sa + pltpu.roll(x, shift, 1) * sb


def _ada_kernel(c_ref, w_ref, b_ref, o_ref):
    sc = _silu(c_ref[...]).astype(BF16)
    o_ref[0] = _dot(sc, w_ref[0].astype(BF16)) + b_ref[0]


def _ada_table(cc, ada_w, ada_b):
    rows, D = cc.shape
    n_out = ada_w.shape[2]
    tn = n_out // 8
    return pl.pallas_call(
        _ada_kernel,
        grid=(DEPTH, n_out // tn),
        in_specs=[pl.BlockSpec((rows, D), lambda i, j: (0, 0)),
                  pl.BlockSpec((1, D, tn), lambda i, j: (i, 0, j)),
                  pl.BlockSpec((1, 1, tn), lambda i, j: (i, 0, j))],
        out_specs=pl.BlockSpec((1, rows, tn), lambda i, j: (i, 0, j)),
        out_shape=jax.ShapeDtypeStruct((DEPTH, rows, n_out), F32),
        compiler_params=_params("arbitrary", "arbitrary"),
        name="ada_table",
    )(cc, ada_w, ada_b.reshape(DEPTH, 1, n_out))


FFN_CHUNK = MXU_COLS


def _ffn_kernel(*refs, off, final_norm, n_lat_tiles, readout, n_mix):
    refs = list(refs)
    if n_lat_tiles is None:
        h = refs.pop(0)[...]
    else:
        hx_ref, hc_ref = refs.pop(0), refs.pop(0)
        h = jnp.where(pl.program_id(0) < n_lat_tiles, hx_ref[...], hc_ref[...])
    mix_refs = [refs.pop(0) for _ in range(n_mix)]
    if readout is not None:
        wo_ref = refs.pop(0)
    mod_ref, gain_ref, fin_ref, wg_ref, wu_ref, wd_ref, o_ref, a_ref = refs
    m = mod_ref[0]
    if readout is not None:
        h = h + m[5:6] * _dot(readout(*mix_refs), wo_ref[...])
    xm = _pre(h, gain_ref[...], m, off).astype(BF16)
    n = D_FF // FFN_CHUNK

    def gate_up(c):
        cols = slice(c * FFN_CHUNK, (c + 1) * FFN_CHUNK)
        return _dot(xm, wg_ref[:, cols]), _dot(xm, wu_ref[:, cols])

    cur = gate_up(0)
    for c in range(n):
        nxt = gate_up(c + 1) if c + 1 < n else None
        g, u = cur
        a_ref[:, c * FFN_CHUNK:(c + 1) * FFN_CHUNK] = (_silu(g) * u).astype(BF16)
        cur = nxt
    out = h + (0.5 * m[off + 2:off + 3]) * _dot(a_ref[...], wd_ref[...])
    if final_norm:
        out = _rmsnorm(out, fin_ref[...])
    o_ref[...] = out


def _ffn(lay, h, mod, gain, fin_w, wg, wu, wd, layer, half, lat_only=False, final_norm=False, mix=None):
    split = isinstance(h, tuple)
    D = wg.shape[2]
    tm = lay.tile(ROW_TILE)
    n_rows = lay.n_lat if lat_only else lay.n_all
    midx = lay.mod_index(tm)
    n_lat_tiles = lay.n_lat // tm
    const = lambda t: (0, 0)
    one = pl.Buffered(1)
    weight = lambda shape: pl.BlockSpec((None, None) + shape, lambda t: (layer, half, 0, 0), pipeline_mode=one)
    if split:
        h_specs = [pl.BlockSpec((tm, D), lambda t: (jnp.minimum(t, n_lat_tiles - 1), 0)),
                   pl.BlockSpec((tm, D), lambda t: (jnp.maximum(t - n_lat_tiles, 0), 0))]
        h_args = list(h)
    else:
        h_specs = [pl.BlockSpec((tm, D), lambda t: (t, 0))]
        h_args = [h]
    readout, operands = None, []
    if mix is not None:
        readout, operands, w_out = mix
        for arr, col in operands:
            if arr.shape[0] == 1:
                h_specs.append(pl.BlockSpec(arr.shape, const))
            else:
                h_specs.append(pl.BlockSpec((tm, D), lambda t, col=col: (t, col)))
            h_args.append(arr)
        h_specs.append(pl.BlockSpec(w_out.shape, const, pipeline_mode=one))
        h_args.append(w_out.astype(BF16))
    return pl.pallas_call(
        functools.partial(_ffn_kernel, off=6 * half, final_norm=final_norm,
                          n_lat_tiles=n_lat_tiles if split else None, readout=readout, n_mix=len(operands)),
        grid=(n_rows // tm,),
        in_specs=h_specs + [pl.BlockSpec((1, N_MOD, D), lambda t: (midx(t), 0, 0)),
                            pl.BlockSpec((1, D), const),
                            pl.BlockSpec((1, D), const),
                            weight((D, D_FF)), weight((D, D_FF)), weight((D_FF, D))],
        out_specs=pl.BlockSpec((tm, D), lambda t: (t, 0)),
        out_shape=jax.ShapeDtypeStruct((n_rows, D), F32),
        scratch_shapes=[pltpu.VMEM((tm, D_FF), BF16)],
        compiler_params=_params("arbitrary"),
        name="ffn",
    )(*h_args, mod, gain.reshape(1, D), fin_w.reshape(1, D), wg, wu, wd)


def _rope_tables(S, rot_dim, pad_rows, lane_off):
    rows = S // GRID_W
    row = jnp.repeat(jnp.arange(rows, dtype=F32), GRID_W)
    col = jnp.tile(jnp.arange(GRID_W, dtype=F32), rows)
    axis_dim = rot_dim // 2
    inv_freq = ROPE_BASE ** (-jnp.arange(0, axis_dim, 2, dtype=F32) / axis_dim)
    ang_r = row[:, None] * inv_freq[None, :]
    ang_c = col[:, None] * inv_freq[None, :]
    ang = jnp.concatenate([ang_r, ang_r, ang_c, ang_c], axis=-1)
    cos, sin = jnp.cos(ang), jnp.sin(ang)
    f = rot_dim // 4
    first = (jnp.arange(rot_dim) % (2 * f)) < f
    sa = jnp.where(first[None, :], -sin, 0.0)
    sb = jnp.where(first[None, :], 0.0, sin)
    period = 64 if lane_off + rot_dim <= 64 else LANES

    def widen(t, fill):
        blk = jnp.full((S, period), fill, F32).at[:, lane_off:lane_off + rot_dim].set(t)
        blk = jnp.tile(blk, (1, LANES // period))
        return jnp.concatenate([blk, jnp.full((pad_rows, LANES), fill, F32)], axis=0)

    return widen(cos, 1.0), widen(sa, 0.0), widen(sb, 0.0)


def _attn_specs(lay, tq, widths):
    S, C, B = lay.S, lay.C, lay.B
    nq, nc = S // tq, C // tq
    lat_blocks = lay.n_lat // C

    def q_index(b, i):
        return jnp.where(i < nq, b * nq + i, B * nq + b * nc + (i - nq))

    lat = [pl.BlockSpec((S, w), lambda b, i: (b, 0)) for w in widths]
    ctx = [pl.BlockSpec((C, w), lambda b, i: (lat_blocks + b, 0)) for w in widths]
    return q_index, nq, nc, lat, ctx


def _gqa_proj_kernel(h_ref, mod_ref, gain_ref, w_ref, cos_ref, sa_ref, sb_ref, q_ref, k_ref, v_ref):
    qd, kd = GQA_HEADS * GQA_HEAD_DIM, GQA_KV_HEADS * GQA_HEAD_DIM
    a = _pre(h_ref[...], gain_ref[...], mod_ref[0], 3).astype(BF16)
    cos, sa, sb = cos_ref[...], sa_ref[...], sb_ref[...]
    f = GQA_HEAD_DIM // 4
    scale = GQA_HEAD_DIM ** -0.5
    lane = lax.broadcasted_iota(jnp.int32, (1, MXU_COLS), 1)
    ones_half = jnp.where((lane & (2 * GQA_HEAD_DIM - 1)) >= GQA_HEAD_DIM, 1.0, 0.0)

    def emit(col, tile):
        if col >= qd + kd:
            v_ref[:, col - qd - kd:col - qd - kd + MXU_COLS] = (tile + ones_half).astype(BF16)
            return
        for j in range(MXU_COLS // LANES):
            x = _rope(tile[:, j * LANES:(j + 1) * LANES], cos, sa, sb, f)
            lo = col + j * LANES
            if lo < qd:
                q_ref[:, lo:lo + LANES] = (x * scale).astype(BF16)
            else:
                k_ref[:, lo - qd:lo - qd + LANES] = x.astype(BF16)

    _chunked_proj(a, w_ref, emit)


def _gqa_attn_kernel(sink_ref, q_ref, kp_ref, kc_ref, kn_ref, vp_ref, vc_ref, vn_ref, kx_ref, vx_ref,
                     o_ref, kcat_ref, vcat_ref, *, nq):
    i = pl.program_id(1)
    blk = GQA_BLOCK
    G = GQA_HEADS // GQA_KV_HEADS
    HD = GQA_HEAD_DIM
    VW = 2 * HD
    span = 3 * blk

    def run(k_ref, v_ref, masks):
        kvs = range(GQA_KV_HEADS)
        qs = [jnp.concatenate([q_ref[:, (G * g + n) * HD:(G * g + n + 1) * HD] for n in range(G)], axis=0)
              for g in kvs]
        sinks = [jnp.concatenate([jnp.full((blk, 1), sink_ref[G * g + n], F32) for n in range(G)], axis=0)
                 for g in kvs]
        ss = [_dot_nt(qs[g], k_ref[:, g * HD:(g + 1) * HD]) for g in kvs]
        if masks is not None:
            prev_ok, next_ok = masks
            ss = [jnp.concatenate([jnp.where(prev_ok, s[:, :blk], NEG_INF), s[:, blk:2 * blk],
                                   jnp.where(next_ok, s[:, 2 * blk:span], NEG_INF), s[:, span:]], axis=1)
                  for s in ss]
        ms = [jnp.maximum(jnp.max(ss[g], axis=-1, keepdims=True), sinks[g]) for g in kvs]
        accs = [_dot(jnp.exp(ss[g] - ms[g]).astype(BF16),
                     v_ref[:, (g // 2) * MXU_COLS:(g // 2 + 1) * MXU_COLS])[:, (g % 2) * VW:(g % 2 + 1) * VW]
                for g in kvs]
        outs = [accs[g][:, :HD] * (1.0 / (accs[g][:, HD:] + jnp.exp(sinks[g] - ms[g]))) for g in kvs]
        heads = [outs[g][n * blk:(n + 1) * blk] for g in kvs for n in range(G)]
        o_ref[...] = jnp.concatenate(heads, axis=-1).astype(BF16)

    @pl.when(i < nq)
    def _():
        for n, (k_ref, v_ref) in enumerate([(kp_ref, vp_ref), (kc_ref, vc_ref), (kn_ref, vn_ref)]):
            kcat_ref[n * blk:(n + 1) * blk] = k_ref[...]
            vcat_ref[n * blk:(n + 1) * blk] = v_ref[...]
        kcat_ref[span:] = kx_ref[...]
        vcat_ref[span:] = vx_ref[...]
        row = lax.broadcasted_iota(jnp.int32, (G * blk, blk), 0) & (blk - 1)
        col = lax.broadcasted_iota(jnp.int32, (G * blk, blk), 1)
        prev_ok = (row + blk - col <= GQA_WINDOW) & (i >= 1)
        next_ok = (col + blk - row <= GQA_WINDOW) & (i + 1 < nq)
        run(kcat_ref, vcat_ref, (prev_ok, next_ok))

    @pl.when(i >= nq)
    def _():
        run(kx_ref, vx_ref, None)


def _gqa_mixer(lay, h, mod, gain, w_in, sinks, with_ctx_out):
    D = h.shape[1]
    B, S, C = lay.B, lay.S, lay.C
    qd, kd = GQA_HEADS * GQA_HEAD_DIM, GQA_KV_HEADS * GQA_HEAD_DIM
    w_v = jnp.pad(w_in[:, qd + kd:].reshape(D, GQA_KV_HEADS, GQA_HEAD_DIM), ((0, 0), (0, 0), (0, GQA_HEAD_DIM)))
    w_pad = jnp.concatenate([w_in[:, :qd + kd], w_v.reshape(D, 2 * kd)], axis=1)
    tm = lay.tile(ROW_TILE)
    midx, pidx = lay.mod_index(tm), lay.pos_index(tm)
    cos, sa, sb = _rope_tables(S, GQA_HEAD_DIM, tm, 0)
    tab = pl.BlockSpec((tm, LANES), lambda t: (pidx(t), 0))
    q, k, v = pl.pallas_call(
        _gqa_proj_kernel,
        grid=(lay.n_all // tm,),
        in_specs=[pl.BlockSpec((tm, D), lambda t: (t, 0)),
                  pl.BlockSpec((1, N_MOD, D), lambda t: (midx(t), 0, 0)),
                  pl.BlockSpec((1, D), lambda t: (0, 0)),
                  pl.BlockSpec((D, qd + 3 * kd), lambda t: (0, 0)),
                  tab, tab, tab],
        out_specs=[pl.BlockSpec((tm, qd), lambda t: (t, 0)),
                   pl.BlockSpec((tm, kd), lambda t: (t, 0)),
                   pl.BlockSpec((tm, 2 * kd), lambda t: (t, 0))],
        out_shape=[jax.ShapeDtypeStruct((lay.n_all, qd), BF16),
                   jax.ShapeDtypeStruct((lay.n_all, kd), BF16),
                   jax.ShapeDtypeStruct((lay.n_all, 2 * kd), BF16)],
        compiler_params=_params("arbitrary"),
        name="gqa_proj",
    )(h, mod, gain.reshape(1, D), w_pad.astype(BF16), cos, sa, sb)

    blk = GQA_BLOCK
    assert S % blk == 0 and C % blk == 0
    nq, nc = S // blk, C // blk
    n_i = nq + (nc if with_ctx_out else 0)
    lat_blocks = lay.n_lat // C

    def q_index(b, i):
        return jnp.where(i < nq, b * nq + i, B * nq + b * nc + (i - nq))

    def win(delta):
        return lambda b, i: (b * nq + jnp.clip(i + delta, 0, nq - 1), 0)

    def windows(width):
        return [pl.BlockSpec((blk, width), win(d)) for d in (-1, 0, 1)]

    def ctx_spec(width):
        return pl.BlockSpec((C, width), lambda b, i: (lat_blocks + b, 0))

    row = pl.BlockSpec((blk, qd), lambda b, i: (q_index(b, i), 0))
    return pl.pallas_call(
        functools.partial(_gqa_attn_kernel, nq=nq),
        grid=(B, n_i),
        in_specs=[pl.BlockSpec(memory_space=pltpu.SMEM), row,
                  *windows(kd), *windows(2 * kd), ctx_spec(kd), ctx_spec(2 * kd)],
        out_specs=row,
        out_shape=jax.ShapeDtypeStruct((lay.n_all if with_ctx_out else lay.n_lat, qd), BF16),
        scratch_shapes=[pltpu.VMEM((3 * blk + C, kd), BF16), pltpu.VMEM((3 * blk + C, 2 * kd), BF16)],
        compiler_params=_params("arbitrary", "arbitrary"),
        name="gqa_attn",
    )(sinks.astype(F32), q, k, k, k, v, v, v, k, v)


def _diff_proj_kernel(h_ref, mod_ref, gain_ref, w_ref, cos_ref, sa_ref, sb_ref, q_ref, k_ref, v_ref):
    qk = DIFF_HEADS * 2 * DIFF_HEAD_DIM
    a = _pre(h_ref[...], gain_ref[...], mod_ref[0], 3).astype(BF16)
    cos, sa, sb = cos_ref[...], sa_ref[...], sb_ref[...]
    f = DIFF_HEAD_DIM // 4
    scale = DIFF_HEAD_DIM ** -0.5
    vw = 2 * DIFF_HEAD_DIM
    ones = jnp.ones((a.shape[0], vw), BF16)

    def emit(col, tile):
        for j in range(MXU_COLS // LANES):
            lo = col + j * LANES
            x = tile[:, j * LANES:(j + 1) * LANES]
            if lo < qk:
                q_ref[:, lo:lo + LANES] = (_rope(x, cos, sa, sb, f) * scale).astype(BF16)
            elif lo < 2 * qk:
                k_ref[:, lo - qk:lo - qk + LANES] = _rope(x, cos, sa, sb, f).astype(BF16)
            else:
                hd = (lo - 2 * qk) // vw
                v_ref[:, 2 * hd * vw:(2 * hd + 1) * vw] = x.astype(BF16)
                v_ref[:, (2 * hd + 1) * vw:(2 * hd + 2) * vw] = ones

    _chunked_proj(a, w_ref, emit)


def _diff_attn_kernel(lam_ref, sub_ref, q_ref, kl_ref, vl_ref, kx_ref, vx_ref, o_ref, *, nq, lambda_init):
    i = pl.program_id(1)
    HD = DIFF_HEAD_DIM
    VW = 4 * HD
    lp = lam_ref[...]
    lam = (jnp.exp(jnp.sum(lp[0:1] * lp[1:2], keepdims=True))
           - jnp.exp(jnp.sum(lp[2:3] * lp[3:4], keepdims=True)) + lambda_init)

    def run(kv_pairs):
        def scores(h):
            los = [(2 * h + j) * HD for j in range(2)]
            return [[_dot_nt(q_ref[:, lo:lo + HD], k_ref[:, lo:lo + HD]) for k_ref, _ in kv_pairs] for lo in los]

        heads = []
        ss = scores(0)
        for h in range(DIFF_HEADS):
            nxt = scores(h + 1) if h + 1 < DIFF_HEADS else None
            maps = []
            for parts in ss:
                m = functools.reduce(jnp.maximum, [jnp.max(s, axis=-1, keepdims=True) for s in parts])
                acc = None
                for s, (_, v_ref) in zip(parts, kv_pairs):
                    t = _dot(jnp.exp(s - m).astype(BF16), v_ref[:, h * VW:(h + 1) * VW])
                    acc = t if acc is None else acc + t
                maps.append(acc[:, :VW // 2] * (1.0 / acc[:, VW // 2:]))
            o = maps[0] - lam * maps[1]
            heads.append(_rmsnorm(o, sub_ref[...]) * (1.0 - lambda_init))
            ss = nxt
        o_ref[...] = jnp.concatenate(heads, axis=-1).astype(BF16)

    @pl.when(i < nq)
    def _():
        run([(kl_ref, vl_ref), (kx_ref, vx_ref)])

    @pl.when(i >= nq)
    def _():
        run([(kx_ref, vx_ref)])


def _diff_mixer(lay, h, mod, gain, w_in, lam_params, subln_w, lambda_init, with_ctx_out):
    D = h.shape[1]
    B, S, C = lay.B, lay.S, lay.C
    qk = DIFF_HEADS * 2 * DIFF_HEAD_DIM
    tm = lay.tile(ROW_TILE)
    midx, pidx = lay.mod_index(tm), lay.pos_index(tm)
    cos, sa, sb = _rope_tables(S, DIFF_HEAD_DIM, tm, 0)
    tab = pl.BlockSpec((tm, LANES), lambda t: (pidx(t), 0))
    out = jax.ShapeDtypeStruct((lay.n_all, qk), BF16)
    q, k, v = pl.pallas_call(
        _diff_proj_kernel,
        grid=(lay.n_all // tm,),
        in_specs=[pl.BlockSpec((tm, D), lambda t: (t, 0)),
                  pl.BlockSpec((1, N_MOD, D), lambda t: (midx(t), 0, 0)),
                  pl.BlockSpec((1, D), lambda t: (0, 0)),
                  pl.BlockSpec((D, 3 * qk), lambda t: (0, 0)),
                  tab, tab, tab],
        out_specs=[pl.BlockSpec((tm, qk), lambda t: (t, 0))] * 2 + [pl.BlockSpec((tm, 2 * qk), lambda t: (t, 0))],
        out_shape=[out, out, jax.ShapeDtypeStruct((lay.n_all, 2 * qk), BF16)],
        compiler_params=_params("arbitrary"),
        name="diff_proj",
    )(h, mod, gain.reshape(1, D), w_in.astype(BF16), cos, sa, sb)

    tq = _divisor(math.gcd(S, C), QUERY_TILE)
    q_index, nq, nc, lat, ctx = _attn_specs(lay, tq, [qk, 2 * qk])
    row = pl.BlockSpec((tq, qk), lambda b, i: (q_index(b, i), 0))
    return pl.pallas_call(
        functools.partial(_diff_attn_kernel, nq=nq, lambda_init=lambda_init),
        grid=(B, nq + (nc if with_ctx_out else 0)),
        in_specs=[pl.BlockSpec((4, DIFF_HEAD_DIM), lambda b, i: (0, 0)),
                  pl.BlockSpec((1, 2 * DIFF_HEAD_DIM), lambda b, i: (0, 0)),
                  row, *lat, *ctx],
        out_specs=row,
        out_shape=jax.ShapeDtypeStruct((lay.n_all if with_ctx_out else lay.n_lat, qk), BF16),
        compiler_params=_params("arbitrary", "arbitrary"),
        name="diff_attn",
    )(lam_params.astype(F32), subln_w.reshape(1, -1), q, k, v, k, v)


def _hgrn_proj_kernel(h_ref, mod_ref, gain_ref, lb_ref, w_ref, o_ref, *, layer):
    KD = HGRN_HEADS * HGRN_KEY_DIM
    W = 2 * LANES
    a = _pre(h_ref[...], gain_ref[...], mod_ref[0], 3).astype(BF16)
    raw = lb_ref[...]
    e = jnp.exp(raw - jnp.max(raw, axis=0, keepdims=True))
    soft = e * (1.0 / jnp.sum(e, axis=0, keepdims=True))
    lb = jnp.sum(soft[1:layer + 1], axis=0, keepdims=True) if layer else jnp.zeros_like(soft[0:1])
    n = w_ref.shape[1] // W

    def proj(c):
        return _dot(a, w_ref[:, c * W:(c + 1) * W])

    cur = proj(0)
    for c in range(n):
        nxt = proj(c + 1) if c + 1 < n else None
        group, col = divmod(c * W, KD)
        if group in (0, 4):
            out = _silu(cur)
        elif group in (1, 2):
            lbc = lb[:, col:col + W]
            out = lbc + (1.0 - lbc) * jax.nn.sigmoid(cur)
        else:
            out = cur
        o_ref[:, c * W:(c + 1) * W] = out
        cur = nxt


def _running_sum(x, reverse):
    n, w = x.shape
    n_groups = n // SUBLANES
    x = x.reshape(n_groups, SUBLANES, w)
    within = lax.broadcasted_iota(jnp.int32, (1, SUBLANES, 1), 1)
    step = 1
    while step < SUBLANES:
        if reverse:
            x = x + jnp.where(within < SUBLANES - step, pltpu.roll(x, SUBLANES - step, 1), 0.0)
        else:
            x = x + jnp.where(within >= step, pltpu.roll(x, step, 1), 0.0)
        step *= 2
    out = [None] * n_groups
    carry = None
    for j in (reversed(range(n_groups)) if reverse else range(n_groups)):
        out[j] = x[j] if carry is None else x[j] + carry
        carry = out[j][0:1] if reverse else out[j][SUBLANES - 1:SUBLANES]
    return jnp.concatenate(out, axis=0)


def _hgrn_scan_kernel(qf_ref, ff_ref, vf_ref, qb_ref, fb_ref, vb_ref, of_ref, ob_ref, st_ref, *, n_chunks):
    step = pl.program_id(1)
    C = HGRN_CHUNK
    DK, DV, H = HGRN_KEY_DIM, HGRN_VAL_DIM, HGRN_HEADS

    @pl.when(step == 0)
    def _():
        st_ref[...] = jnp.zeros_like(st_ref)

    r = lax.broadcasted_iota(jnp.int32, (C, C), 0)
    c = lax.broadcasted_iota(jnp.int32, (C, C), 1)
    ks = [slice(h * DK, (h + 1) * DK) for h in range(H)]
    vs = [slice(h * DV, (h + 1) * DV) for h in range(H)]
    dirs = [(qf_ref, ff_ref, vf_ref, of_ref, c <= r, C // 2, C - 1),
            (qb_ref, fb_ref, vb_ref, ob_ref, c >= r, C - 1 - C // 2, 0)]

    def prepare(d, n):
        q_ref, f_ref, v_ref, _, tri, mid_row, last_row = dirs[d]
        base = (n if d == 0 else n_chunks - 1 - n) * C
        q = q_ref[base:base + C, :]
        f = f_ref[base:base + C, :]
        vt = v_ref[base:base + C, :].T.astype(BF16)
        k = 1.0 - f
        g = _running_sum(jnp.log2(f), reverse=d == 1)
        g_mid = g[mid_row:mid_row + 1]
        g_last = g[last_row:last_row + 1]
        t = g - g_mid
        qa_f = q * jnp.exp2(t)
        kb_f = k * jnp.exp2(-t)
        qa = qa_f.astype(BF16)
        kb = kb_f.astype(BF16)
        qg = (qa_f * jnp.exp2(g_mid)).astype(BF16)
        kd = (kb_f * jnp.exp2(g_last - g_mid)).astype(BF16)
        a = [jnp.where(tri, _dot_nt(qa[:, ks[h]], kb[:, ks[h]]), 0.0).astype(BF16) for h in range(H)]
        lhs = [jnp.concatenate([qg[:, ks[h]], a[h]], axis=1) for h in range(H)]
        update = [_dot(vt[vs[h], :], kd[:, ks[h]]) for h in range(H)]
        return base, lhs, vt, jnp.exp2(g_last), update

    cur = [prepare(d, 0) for d in range(2)]
    for n in range(n_chunks):
        nxt = [prepare(d, n + 1) for d in range(2)] if n + 1 < n_chunks else None
        for d in range(2):
            base, lhs, vt, decay, update = cur[d]
            st = [st_ref[d, vs[h], :] for h in range(H)]
            outs = [_dot_nt(lhs[h], jnp.concatenate([st[h].astype(BF16), vt[vs[h], :]], axis=1)) for h in range(H)]
            for h in range(H):
                st_ref[d, vs[h], :] = st[h] * decay[:, ks[h]] + update[h]
            dirs[d][3][base:base + C, :] = jnp.concatenate(outs, axis=-1)
        cur = nxt


def _hgrn_readout(of_ref, ob_ref, gate_ref, nw_ref):
    DV = HGRN_VAL_DIM
    o = of_ref[...] + ob_ref[...]
    gate = gate_ref[...]
    nw = nw_ref[...]
    ys = [_rmsnorm(o[:, h * DV:(h + 1) * DV], nw) * gate[:, h * DV:(h + 1) * DV] for h in range(HGRN_HEADS)]
    return jnp.concatenate(ys, axis=-1).astype(BF16)


def _plain_readout(o_ref):
    return o_ref[...]


def _hgrn_mixer(lay, h, mod, gain, w_in, norm_w, lower_bounds, layer):
    D = h.shape[1]
    B, S, C = lay.B, lay.S, lay.C
    KD = HGRN_HEADS * HGRN_KEY_DIM
    tm = lay.tile(ROW_TILE)
    midx = lay.mod_index(tm)
    n_cols = w_in.shape[1] // KD
    proj = pl.pallas_call(
        functools.partial(_hgrn_proj_kernel, layer=layer),
        grid=(lay.n_all // tm,),
        in_specs=[pl.BlockSpec((tm, D), lambda t: (t, 0)),
                  pl.BlockSpec((1, N_MOD, D), lambda t: (midx(t), 0, 0)),
                  pl.BlockSpec((1, D), lambda t: (0, 0)),
                  pl.BlockSpec(lower_bounds.shape, lambda t: (0, 0)),
                  pl.BlockSpec((D, n_cols * KD), lambda t: (0, 0), pipeline_mode=pl.Buffered(1))],
        out_specs=pl.BlockSpec((tm, n_cols * KD), lambda t: (t, 0)),
        out_shape=jax.ShapeDtypeStruct((lay.n_all, n_cols * KD), F32),
        compiler_params=_params("arbitrary"),
        name="hgrn_proj",
    )(h, mod, gain.reshape(1, D), lower_bounds.astype(F32), w_in.astype(BF16))

    ts = C
    n_lat_steps = S // ts
    lat_blocks = lay.n_lat // ts

    def rows(d):
        def index(b, s):
            lat = b * n_lat_steps + (s - 1 if d == 0 else n_lat_steps - s)
            return jnp.where(s == 0, lat_blocks + b, lat)
        return index

    def col(d, j):
        return pl.BlockSpec((ts, KD), lambda b, s: (rows(d)(b, s), j))

    o_f, o_b = pl.pallas_call(
        functools.partial(_hgrn_scan_kernel, n_chunks=ts // HGRN_CHUNK),
        grid=(B, 1 + n_lat_steps),
        in_specs=[col(0, 0), col(0, 1), col(0, 3), col(1, 0), col(1, 2), col(1, 3)],
        out_specs=[col(0, 0), col(1, 0)],
        out_shape=[jax.ShapeDtypeStruct((lay.n_all, KD), F32)] * 2,
        scratch_shapes=[pltpu.VMEM((2, HGRN_HEADS * HGRN_VAL_DIM, HGRN_KEY_DIM), F32)],
        compiler_params=_params("arbitrary", "arbitrary"),
        name="hgrn_scan",
    )(proj, proj, proj, proj, proj, proj)

    return _hgrn_readout, [(o_f, 0), (o_b, 0), (proj, 4), (norm_w.reshape(1, -1), 0)]


MLA_QK_PAD = LANES


def _mla_proj_kernel(h_ref, mod_ref, gain_ref, wd_ref, qn_ref, kvn_ref, wuq_ref, wukv_ref,
                     cos_ref, sa_ref, sb_ref, q_ref, k_ref, v_ref):
    H, P = MLA_HEADS, MLA_QK_PAD
    a = _pre(h_ref[...], gain_ref[...], mod_ref[0], 3).astype(BF16)
    dn = _dot(a, wd_ref[...])
    cos, sa, sb = cos_ref[...], sa_ref[...], sb_ref[...]
    f = MLA_ROPE // 4
    cq = _rmsnorm(dn[:, :MLA_Q_LORA], qn_ref[...]).astype(BF16)
    ckv = _rmsnorm(dn[:, MLA_Q_LORA:MLA_Q_LORA + MLA_KV_LORA], kvn_ref[...]).astype(BF16)
    kr = _rope(dn[:, MLA_Q_LORA + MLA_KV_LORA:], cos, sa, sb, f)
    lane = lax.broadcasted_iota(jnp.int32, (1, P), 1)
    ones_half = jnp.where(lane >= MLA_V_DIM, 1.0, 0.0)

    def emit_q(col, tile):
        for j in range(MXU_COLS // P):
            lo = col + j * P
            q_ref[:, lo:lo + P] = _rope(tile[:, j * P:(j + 1) * P], cos, sa, sb, f).astype(BF16)

    def emit_kv(col, tile):
        for j in range(MXU_COLS // P):
            lo = col + j * P
            x = tile[:, j * P:(j + 1) * P]
            if lo < H * P:
                k_ref[:, lo:lo + P] = (x + kr).astype(BF16)
            else:
                v_ref[:, lo - H * P:lo - H * P + P] = (x + ones_half).astype(BF16)

    _chunked_proj(cq, wuq_ref, emit_q)
    _chunked_proj(ckv, wukv_ref, emit_kv)


def _mla_attn_kernel(q_ref, kl_ref, vl_ref, kx_ref, vx_ref, o_ref, *, nq, with_ctx):
    i = pl.program_id(1)
    P, VD = MLA_QK_PAD, MLA_V_DIM
    c = ((MLA_NOPE + MLA_ROPE) ** -0.5) * LOG2_E

    def run(kv_pairs):
        def scores(hd):
            qh = q_ref[:, hd * P:(hd + 1) * P]
            return [_dot_nt(qh, k_ref[:, hd * P:(hd + 1) * P]) for k_ref, _ in kv_pairs]

        heads = []
        ss = scores(0)
        for hd in range(MLA_HEADS):
            nxt = scores(hd + 1) if hd + 1 < MLA_HEADS else None
            m = functools.reduce(jnp.maximum, [jnp.max(s, axis=-1, keepdims=True) for s in ss])
            mc = m * c
            pair = (hd // 2) * MXU_COLS
            acc = None
            for s, (_, v_ref) in zip(ss, kv_pairs):
                t = _dot(jnp.exp2(s * c - mc).astype(BF16), v_ref[:, pair:pair + MXU_COLS])
                acc = t if acc is None else acc + t
            acc = acc[:, (hd % 2) * P:(hd % 2 + 1) * P]
            heads.append(acc[:, :VD] * (1.0 / acc[:, VD:]))
            ss = nxt
        o_ref[...] = jnp.concatenate(heads, axis=-1).astype(BF16)

    if not with_ctx:
        run([(kl_ref, vl_ref), (kx_ref, vx_ref)])
        return

    @pl.when(i < nq)
    def _():
        run([(kl_ref, vl_ref), (kx_ref, vx_ref)])

    @pl.when(i >= nq)
    def _():
        run([(kx_ref, vx_ref)])


def _mla_mixer(lay, h, mod, gain, w_down, q_norm_w, kv_norm_w, w_uq, w_ukv, with_ctx_out):
    D = h.shape[1]
    B, S, C = lay.B, lay.S, lay.C
    H, P, VD = MLA_HEADS, MLA_QK_PAD, MLA_V_DIM
    qk = MLA_NOPE + MLA_ROPE
    lora = MLA_Q_LORA + MLA_KV_LORA
    kr_cols = jnp.zeros((D, P), F32).at[:, MLA_NOPE:qk].set(w_down[:, lora:])
    wd = jnp.concatenate([w_down[:, :lora], kr_cols], axis=1).astype(BF16)
    wuq = jnp.pad(w_uq.reshape(MLA_Q_LORA, H, qk), ((0, 0), (0, 0), (0, P - qk))).reshape(MLA_Q_LORA, H * P)
    ukv = w_ukv.reshape(MLA_KV_LORA, H, MLA_NOPE + VD)
    wkn = jnp.pad(ukv[..., :MLA_NOPE], ((0, 0), (0, 0), (0, P - MLA_NOPE))).reshape(MLA_KV_LORA, H * P)
    wv = jnp.pad(ukv[..., MLA_NOPE:], ((0, 0), (0, 0), (0, P - VD))).reshape(MLA_KV_LORA, H * P)
    wukv = jnp.concatenate([wkn, wv], axis=1)

    tm = lay.tile(ROW_TILE)
    midx, pidx = lay.mod_index(tm), lay.pos_index(tm)
    cos, sa, sb = _rope_tables(S, MLA_ROPE, tm, MLA_NOPE)
    tab = pl.BlockSpec((tm, LANES), lambda t: (pidx(t), 0))
    full = lambda shape: pl.BlockSpec(shape, lambda t: (0, 0))
    q, k, v = pl.pallas_call(
        _mla_proj_kernel,
        grid=(lay.n_all // tm,),
        in_specs=[pl.BlockSpec((tm, D), lambda t: (t, 0)),
                  pl.BlockSpec((1, N_MOD, D), lambda t: (midx(t), 0, 0)),
                  full((1, D)), full((D, lora + P)), full((1, MLA_Q_LORA)), full((1, MLA_KV_LORA)),
                  full((MLA_Q_LORA, H * P)), full((MLA_KV_LORA, 2 * H * P)),
                  tab, tab, tab],
        out_specs=[pl.BlockSpec((tm, H * P), lambda t: (t, 0))] * 3,
        out_shape=[jax.ShapeDtypeStruct((lay.n_all, H * P), BF16)] * 3,
        compiler_params=_params("arbitrary"),
        name="mla_proj",
    )(h, mod, gain.reshape(1, D), wd, q_norm_w.reshape(1, -1), kv_norm_w.reshape(1, -1),
      wuq.astype(BF16), wukv.astype(BF16), cos, sa, sb)

    tq = _divisor(math.gcd(S, C), QUERY_TILE)
    q_index, nq, nc, lat, ctx = _attn_specs(lay, tq, [H * P, H * P])
    return pl.pallas_call(
        functools.partial(_mla_attn_kernel, nq=nq, with_ctx=with_ctx_out),
        grid=(B, nq + (nc if with_ctx_out else 0)),
        in_specs=[pl.BlockSpec((tq, H * P), lambda b, i: (q_index(b, i), 0)), *lat, *ctx],
        out_specs=pl.BlockSpec((tq, H * VD), lambda b, i: (q_index(b, i), 0)),
        out_shape=jax.ShapeDtypeStruct((lay.n_all if with_ctx_out else lay.n_lat, H * VD), BF16),
        compiler_params=_params("arbitrary", "arbitrary"),
        name="mla_attn",
    )(q, k, v, k, v)


def kernel(x, c, ctx, c_ctx, ada_w, ada_b, norm_w, final_norm_w, ffn_w_gate, ffn_w_up, ffn_w_down,
           gqa_w_in, gqa_w_out, gqa_sinks, diff_w_in, diff_w_out, diff_lambda, diff_subln_w,
           hgrn_w_in, hgrn_w_out, hgrn_norm_w, hgrn_lower_bounds,
           mla_w_down, mla_q_norm_w, mla_kv_norm_w, mla_w_uq, mla_w_ukv, mla_w_out):
    B, S, D = x.shape
    C = ctx.shape[1]
    lay = _Layout(B, S, C)
    depth = ada_w.shape[0]

    rows = -(-(B + 1) // 8) * 8
    cc = jnp.zeros((rows, D), F32).at[:B].set(c).at[B].set(c_ctx)
    mods = _ada_table(cc, ada_w, ada_b).reshape(depth, rows, N_MOD, D)

    wg, wu, wd = ffn_w_gate.astype(BF16), ffn_w_up.astype(BF16), ffn_w_down.astype(BF16)
    h = (x.reshape(B * S, D), ctx.reshape(B * C, D))
    for i in range(depth):
        kind, j = i % 4, i // 4
        last = i == depth - 1
        mod = mods[i]
        h = _ffn(lay, h, mod, norm_w[i, 0], final_norm_w, wg, wu, wd, i, 0)
        if kind == 0:
            o = _gqa_mixer(lay, h, mod, norm_w[i, 1], gqa_w_in[j], gqa_sinks[j], not last)
            mix = (_plain_readout, [(o, 0)], gqa_w_out[j])
        elif kind == 1:
            lambda_init = 0.8 - 0.6 * math.exp(-0.3 * i)
            o = _diff_mixer(lay, h, mod, norm_w[i, 1], diff_w_in[j], diff_lambda[j], diff_subln_w[j],
                            lambda_init, not last)
            mix = (_plain_readout, [(o, 0)], diff_w_out[j])
        elif kind == 2:
            readout, operands = _hgrn_mixer(lay, h, mod, norm_w[i, 1], hgrn_w_in[j], hgrn_norm_w[j],
                                            hgrn_lower_bounds, i)
            mix = (readout, operands, hgrn_w_out[j])
        else:
            o = _mla_mixer(lay, h, mod, norm_w[i, 1], mla_w_down[j], mla_q_norm_w[j], mla_kv_norm_w[j],
                           mla_w_uq[j], mla_w_ukv[j], not last)
            mix = (_plain_readout, [(o, 0)], mla_w_out[j])
        h = _ffn(lay, h, mod, norm_w[i, 2], final_norm_w, wg, wu, wd, i, 1, lat_only=last, final_norm=last,
                 mix=mix)
    return h[:B * S].reshape(B, S, D)
```

```python
import functools
import math

import jax
import jax.numpy as jnp
from jax import lax
from jax.experimental import pallas as pl
from jax.experimental.pallas import tpu as pltpu

F32 = jnp.float32
BF16 = jnp.bfloat16

D_MODEL = 1024
DEPTH = 4
GRID_W = 64
N_MOD = 9
ROPE_BASE = 10000.0
EPS = 1e-6
NEG_INF = -1e30
D_FF = 2816

GQA_HEADS = 16
GQA_KV_HEADS = 4
GQA_HEAD_DIM = 64
GQA_WINDOW = 128
GQA_BLOCK = 128

DIFF_HEADS = 8
DIFF_HEAD_DIM = 64

HGRN_HEADS = 8
HGRN_KEY_DIM = 128
HGRN_VAL_DIM = D_MODEL // HGRN_HEADS
HGRN_CHUNK = 64

MLA_HEADS = 16
MLA_Q_LORA = 256
MLA_KV_LORA = 256
MLA_NOPE = 64
MLA_ROPE = 32
MLA_V_DIM = 64

LOG2_E = 1.4426950408889634
LANES = 128
SUBLANES = 8
MXU_COLS = 2 * LANES
VMEM_LIMIT = 56 * 1024 * 1024
ROW_TILE = 512
QUERY_TILE = 256


def _dot(a, b):
    return jnp.dot(a, b, preferred_element_type=F32)


def _dot_nt(a, b):
    return lax.dot_general(a, b, (((1,), (1,)), ((), ())), preferred_element_type=F32)


def _params(*sem):
    return pltpu.CompilerParams(dimension_semantics=sem, vmem_limit_bytes=VMEM_LIMIT)


def _divisor(n, pref):
    t = min(n, pref)
    while n % t:
        t -= 8
    return t


class _Layout:
    def __init__(self, B, S, C):
        self.B, self.S, self.C = B, S, C
        self.n_lat = B * S
        self.n_all = B * S + B * C

    def tile(self, pref):
        return _divisor(math.gcd(self.S, self.B * self.C), pref)

    def mod_index(self, tm):
        n_lat_tiles, per_batch, B = self.n_lat // tm, self.S // tm, self.B
        return lambda t: jnp.where(t < n_lat_tiles, t // per_batch, B)

    def pos_index(self, tm):
        n_lat_tiles, per_batch = self.n_lat // tm, self.S // tm
        return lambda t: jnp.where(t < n_lat_tiles, t % per_batch, per_batch)


def _rmsnorm(x, w):
    return (x * lax.rsqrt(jnp.mean(x * x, axis=-1, keepdims=True) + EPS)) * w


def _pre(h, gain, m, off):
    return _rmsnorm(h, gain) * (1.0 + m[off + 1:off + 2]) + m[off:off + 1]


def _silu(x):
    return x * jax.nn.sigmoid(x)


def _chunked_proj(a, w_ref, emit):
    n = w_ref.shape[1] // MXU_COLS
    cur = _dot(a, w_ref[:, :MXU_COLS])
    for c in range(n):
        nxt = _dot(a, w_ref[:, (c + 1) * MXU_COLS:(c + 2) * MXU_COLS]) if c + 1 < n else None
        emit(c * MXU_COLS, cur)
        cur = nxt


def _rope(x, cos, sa, sb, shift):
    return x * cos + pltpu.roll(x, LANES - shift, 1) * sa + pltpu.roll(x, shift, 1) * sb


def _ada_kernel(c_ref, w_ref, b_ref, o_ref):
    sc = _silu(c_ref[...]).astype(BF16)
    o_ref[0] = _dot(sc, w_ref[0].astype(BF16)) + b_ref[0]


def _ada_table(cc, ada_w, ada_b):
    rows, D = cc.shape
    n_out = ada_w.shape[2]
    tn = n_out // 8
    return pl.pallas_call(
        _ada_kernel,
        grid=(DEPTH, n_out // tn),
        in_specs=[pl.BlockSpec((rows, D), lambda i, j: (0, 0)),
                  pl.BlockSpec((1, D, tn), lambda i, j: (i, 0, j)),
                  pl.BlockSpec((1, 1, tn), lambda i, j: (i, 0, j))],
        out_specs=pl.BlockSpec((1, rows, tn), lambda i, j: (i, 0, j)),
        out_shape=jax.ShapeDtypeStruct((DEPTH, rows, n_out), F32),
        compiler_params=_params("arbitrary", "arbitrary"),
        name="ada_table",
    )(cc, ada_w, ada_b.reshape(DEPTH, 1, n_out))


FFN_CHUNK = MXU_COLS


def _ffn_kernel(*refs, off, final_norm, n_lat_tiles, readout, n_mix, cast_next):
    refs = list(refs)
    if n_lat_tiles is None:
        h = refs.pop(0)[...]
    else:
        hx_ref, hc_ref = refs.pop(0), refs.pop(0)
        h = jnp.where(pl.program_id(0) < n_lat_tiles, hx_ref[...], hc_ref[...])
    mix_refs = [refs.pop(0) for _ in range(n_mix)]
    if readout is not None:
        wo_ref = refs.pop(0)
    mod_ref, gain_ref, fin_ref, wg_ref, wu_ref, wd_ref = refs[:6]
    if cast_next:
        for src, dst in zip(refs[6:9], refs[10:13]):
            dst[...] = src[...].astype(BF16)
    o_ref, a_ref = refs[9 if cast_next else 6], refs[-1]
    m = mod_ref[0]
    if readout is not None:
        h = h + m[5:6] * _dot(readout(*mix_refs), wo_ref[...])
    xm = _pre(h, gain_ref[...], m, off).astype(BF16)
    n = D_FF // FFN_CHUNK

    def gate_up(c):
        cols = slice(c * FFN_CHUNK, (c + 1) * FFN_CHUNK)
        return _dot(xm, wg_ref[:, cols]), _dot(xm, wu_ref[:, cols])

    cur = gate_up(0)
    for c in range(n):
        nxt = gate_up(c + 1) if c + 1 < n else None
        g, u = cur
        a_ref[:, c * FFN_CHUNK:(c + 1) * FFN_CHUNK] = (_silu(g) * u).astype(BF16)
        cur = nxt
    out = h + (0.5 * m[off + 2:off + 3]) * _dot(a_ref[...], wd_ref[...])
    if final_norm:
        out = _rmsnorm(out, fin_ref[...])
    o_ref[...] = out


def _slabs(rows, n_steps):
    n = n_steps
    while rows % n or (rows // n) % (2 * SUBLANES):
        n -= 1
    return n


def _ffn(lay, h, mod, gain, fin_w, weights, half, lat_only=False, final_norm=False, mix=None, cast_next=None):
    split = isinstance(h, tuple)
    wg, wu, wd = weights
    D = wg.shape[0]
    tm = lay.tile(ROW_TILE)
    n_rows = lay.n_lat if lat_only else lay.n_all
    n_steps = n_rows // tm
    midx = lay.mod_index(tm)
    n_lat_tiles = lay.n_lat // tm
    const = lambda t: (0, 0)
    one = pl.Buffered(1)
    weight = lambda shape: pl.BlockSpec(shape, const, pipeline_mode=one)
    if split:
        h_specs = [pl.BlockSpec((tm, D), lambda t: (jnp.minimum(t, n_lat_tiles - 1), 0)),
                   pl.BlockSpec((tm, D), lambda t: (jnp.maximum(t - n_lat_tiles, 0), 0))]
        h_args = list(h)
    else:
        h_specs = [pl.BlockSpec((tm, D), lambda t: (t, 0))]
        h_args = [h]
    readout, operands = None, []
    if mix is not None:
        readout, operands, w_out = mix
        for arr, col in operands:
            if arr.shape[0] == 1:
                h_specs.append(pl.BlockSpec(arr.shape, const))
            else:
                h_specs.append(pl.BlockSpec((tm, D), lambda t, col=col: (t, col)))
            h_args.append(arr)
        h_specs.append(pl.BlockSpec(w_out.shape, const, pipeline_mode=one))
        h_args.append(w_out.astype(BF16))
    cast_specs, cast_args, cast_out_specs, cast_out_shapes = [], [], [], []
    if cast_next is not None:
        *stacks, nl, nh = cast_next
        for w32 in stacks:
            rows, cols = w32.shape[2:]
            n = _slabs(rows, n_steps)
            slab = rows // n
            cast_specs.append(pl.BlockSpec((None, None, slab, cols),
                                           lambda t, n=n: (nl, nh, jnp.minimum(t, n - 1), 0)))
            cast_args.append(w32)
            cast_out_specs.append(pl.BlockSpec((slab, cols), lambda t, n=n: (jnp.minimum(t, n - 1), 0)))
            cast_out_shapes.append(jax.ShapeDtypeStruct((rows, cols), BF16))
    outs = pl.pallas_call(
        functools.partial(_ffn_kernel, off=6 * half, final_norm=final_norm,
                          n_lat_tiles=n_lat_tiles if split else None, readout=readout, n_mix=len(operands),
                          cast_next=cast_next is not None),
        grid=(n_steps,),
        in_specs=h_specs + [pl.BlockSpec((1, N_MOD, D), lambda t: (midx(t), 0, 0)),
                            pl.BlockSpec((1, D), const),
                            pl.BlockSpec((1, D), const),
                            weight((D, D_FF)), weight((D, D_FF)), weight((D_FF, D))] + cast_specs,
        out_specs=[pl.BlockSpec((tm, D), lambda t: (t, 0))] + cast_out_specs,
        out_shape=[jax.ShapeDtypeStruct((n_rows, D), F32)] + cast_out_shapes,
        scratch_shapes=[pltpu.VMEM((tm, D_FF), BF16)],
        compiler_params=_params("arbitrary"),
        name="ffn",
    )(*h_args, mod, gain.reshape(1, D), fin_w.reshape(1, D), wg, wu, wd, *cast_args)
    return outs[0], (tuple(outs[1:]) if cast_next is not None else None)


def _rope_tables(S, rot_dim, pad_rows, lane_off):
    rows = S // GRID_W
    row = jnp.repeat(jnp.arange(rows, dtype=F32), GRID_W)
    col = jnp.tile(jnp.arange(GRID_W, dtype=F32), rows)
    axis_dim = rot_dim // 2
    inv_freq = ROPE_BASE ** (-jnp.arange(0, axis_dim, 2, dtype=F32) / axis_dim)
    ang_r = row[:, None] * inv_freq[None, :]
    ang_c = col[:, None] * inv_freq[None, :]
    ang = jnp.concatenate([ang_r, ang_r, ang_c, ang_c], axis=-1)
    cos, sin = jnp.cos(ang), jnp.sin(ang)
    f = rot_dim // 4
    first = (jnp.arange(rot_dim) % (2 * f)) < f
    sa = jnp.where(first[None, :], -sin, 0.0)
    sb = jnp.where(first[None, :], 0.0, sin)
    period = 64 if lane_off + rot_dim <= 64 else LANES

    def widen(t, fill):
        blk = jnp.full((S, period), fill, F32).at[:, lane_off:lane_off + rot_dim].set(t)
        blk = jnp.tile(blk, (1, LANES // period))
        return jnp.concatenate([blk, jnp.full((pad_rows, LANES), fill, F32)], axis=0)

    return widen(cos, 1.0), widen(sa, 0.0), widen(sb, 0.0)


def _attn_specs(lay, tq, widths):
    S, C, B = lay.S, lay.C, lay.B
    nq, nc = S // tq, C // tq
    lat_blocks = lay.n_lat // C

    def q_index(b, i):
        return jnp.where(i < nq, b * nq + i, B * nq + b * nc + (i - nq))

    lat = [pl.BlockSpec((S, w), lambda b, i: (b, 0)) for w in widths]
    ctx = [pl.BlockSpec((C, w), lambda b, i: (lat_blocks + b, 0)) for w in widths]
    return q_index, nq, nc, lat, ctx


def _gqa_proj_kernel(h_ref, mod_ref, gain_ref, w_ref, cos_ref, sa_ref, sb_ref, q_ref, k_ref, v_ref):
    qd, kd = GQA_HEADS * GQA_HEAD_DIM, GQA_KV_HEADS * GQA_HEAD_DIM
    a = _pre(h_ref[...], gain_ref[...], mod_ref[0], 3).astype(BF16)
    cos, sa, sb = cos_ref[...], sa_ref[...], sb_ref[...]
    f = GQA_HEAD_DIM // 4
    scale = GQA_HEAD_DIM ** -0.5
    lane = lax.broadcasted_iota(jnp.int32, (1, MXU_COLS), 1)
    ones_half = jnp.where((lane & (2 * GQA_HEAD_DIM - 1)) >= GQA_HEAD_DIM, 1.0, 0.0)

    def emit(col, tile):
        if col >= qd + kd:
            v_ref[:, col - qd - kd:col - qd - kd + MXU_COLS] = (tile + ones_half).astype(BF16)
            return
        for j in range(MXU_COLS // LANES):
            x = _rope(tile[:, j * LANES:(j + 1) * LANES], cos, sa, sb, f)
            lo = col + j * LANES
            if lo < qd:
                q_ref[:, lo:lo + LANES] = (x * scale).astype(BF16)
            else:
                k_ref[:, lo - qd:lo - qd + LANES] = x.astype(BF16)

    _chunked_proj(a, w_ref, emit)


def _gqa_attn_kernel(sink_ref, q_ref, kp_ref, kc_ref, kn_ref, vp_ref, vc_ref, vn_ref, kx_ref, vx_ref,
                     o_ref, kcat_ref, vcat_ref, *, nq):
    i = pl.program_id(1)
    blk = GQA_BLOCK
    G = GQA_HEADS // GQA_KV_HEADS
    HD = GQA_HEAD_DIM
    VW = 2 * HD
    span = 3 * blk

    def run(k_ref, v_ref, masks):
        kvs = range(GQA_KV_HEADS)
        qs = [jnp.concatenate([q_ref[:, (G * g + n) * HD:(G * g + n + 1) * HD] for n in range(G)], axis=0)
              for g in kvs]
        sinks = [jnp.concatenate([jnp.full((blk, 1), sink_ref[G * g + n], F32) for n in range(G)], axis=0)
                 for g in kvs]
        ss = [_dot_nt(qs[g], k_ref[:, g * HD:(g + 1) * HD]) for g in kvs]
        if masks is not None:
            prev_ok, next_ok = masks
            ss = [jnp.concatenate([jnp.where(prev_ok, s[:, :blk], NEG_INF), s[:, blk:2 * blk],
                                   jnp.where(next_ok, s[:, 2 * blk:span], NEG_INF), s[:, span:]], axis=1)
                  for s in ss]
        ms = [jnp.maximum(jnp.max(ss[g], axis=-1, keepdims=True), sinks[g]) for g in kvs]
        accs = [_dot(jnp.exp(ss[g] - ms[g]).astype(BF16),
                     v_ref[:, (g // 2) * MXU_COLS:(g // 2 + 1) * MXU_COLS])[:, (g % 2) * VW:(g % 2 + 1) * VW]
                for g in kvs]
        outs = [accs[g][:, :HD] * (1.0 / (accs[g][:, HD:] + jnp.exp(sinks[g] - ms[g]))) for g in kvs]
        heads = [outs[g][n * blk:(n + 1) * blk] for g in kvs for n in range(G)]
        o_ref[...] = jnp.concatenate(heads, axis=-1).astype(BF16)

    @pl.when(i < nq)
    def _():
        for n, (k_ref, v_ref) in enumerate([(kp_ref, vp_ref), (kc_ref, vc_ref), (kn_ref, vn_ref)]):
            kcat_ref[n * blk:(n + 1) * blk] = k_ref[...]
            vcat_ref[n * blk:(n + 1) * blk] = v_ref[...]
        kcat_ref[span:] = kx_ref[...]
        vcat_ref[span:] = vx_ref[...]
        row = lax.broadcasted_iota(jnp.int32, (G * blk, blk), 0) & (blk - 1)
        col = lax.broadcasted_iota(jnp.int32, (G * blk, blk), 1)
        prev_ok = (row + blk - col <= GQA_WINDOW) & (i >= 1)
        next_ok = (col + blk - row <= GQA_WINDOW) & (i + 1 < nq)
        run(kcat_ref, vcat_ref, (prev_ok, next_ok))

    @pl.when(i >= nq)
    def _():
        run(kx_ref, vx_ref, None)


def _gqa_mixer(lay, h, mod, gain, w_in, sinks, with_ctx_out):
    D = h.shape[1]
    B, S, C = lay.B, lay.S, lay.C
    qd, kd = GQA_HEADS * GQA_HEAD_DIM, GQA_KV_HEADS * GQA_HEAD_DIM
    w_v = jnp.pad(w_in[:, qd + kd:].reshape(D, GQA_KV_HEADS, GQA_HEAD_DIM), ((0, 0), (0, 0), (0, GQA_HEAD_DIM)))
    w_pad = jnp.concatenate([w_in[:, :qd + kd], w_v.reshape(D, 2 * kd)], axis=1)
    tm = lay.tile(ROW_TILE)
    midx, pidx = lay.mod_index(tm), lay.pos_index(tm)
    cos, sa, sb = _rope_tables(S, GQA_HEAD_DIM, tm, 0)
    tab = pl.BlockSpec((tm, LANES), lambda t: (pidx(t), 0))
    q, k, v = pl.pallas_call(
        _gqa_proj_kernel,
        grid=(lay.n_all // tm,),
        in_specs=[pl.BlockSpec((tm, D), lambda t: (t, 0)),
                  pl.BlockSpec((1, N_MOD, D), lambda t: (midx(t), 0, 0)),
                  pl.BlockSpec((1, D), lambda t: (0, 0)),
                  pl.BlockSpec((D, qd + 3 * kd), lambda t: (0, 0)),
                  tab, tab, tab],
        out_specs=[pl.BlockSpec((tm, qd), lambda t: (t, 0)),
                   pl.BlockSpec((tm, kd), lambda t: (t, 0)),
                   pl.BlockSpec((tm, 2 * kd), lambda t: (t, 0))],
        out_shape=[jax.ShapeDtypeStruct((lay.n_all, qd), BF16),
                   jax.ShapeDtypeStruct((lay.n_all, kd), BF16),
                   jax.ShapeDtypeStruct((lay.n_all, 2 * kd), BF16)],
        compiler_params=_params("arbitrary"),
        name="gqa_proj",
    )(h, mod, gain.reshape(1, D), w_pad.astype(BF16), cos, sa, sb)

    blk = GQA_BLOCK
    assert S % blk == 0 and C % blk == 0
    nq, nc = S // blk, C // blk
    n_i = nq + (nc if with_ctx_out else 0)
    lat_blocks = lay.n_lat // C

    def q_index(b, i):
        return jnp.where(i < nq, b * nq + i, B * nq + b * nc + (i - nq))

    def win(delta):
        return lambda b, i: (b * nq + jnp.clip(i + delta, 0, nq - 1), 0)

    def windows(width):
        return [pl.BlockSpec((blk, width), win(d)) for d in (-1, 0, 1)]

    def ctx_spec(width):
        return pl.BlockSpec((C, width), lambda b, i: (lat_blocks + b, 0))

    row = pl.BlockSpec((blk, qd), lambda b, i: (q_index(b, i), 0))
    return pl.pallas_call(
        functools.partial(_gqa_attn_kernel, nq=nq),
        grid=(B, n_i),
        in_specs=[pl.BlockSpec(memory_space=pltpu.SMEM), row,
                  *windows(kd), *windows(2 * kd), ctx_spec(kd), ctx_spec(2 * kd)],
        out_specs=row,
        out_shape=jax.ShapeDtypeStruct((lay.n_all if with_ctx_out else lay.n_lat, qd), BF16),
        scratch_shapes=[pltpu.VMEM((3 * blk + C, kd), BF16), pltpu.VMEM((3 * blk + C, 2 * kd), BF16)],
        compiler_params=_params("arbitrary", "arbitrary"),
        name="gqa_attn",
    )(sinks.astype(F32), q, k, k, k, v, v, v, k, v)


def _diff_proj_kernel(h_ref, mod_ref, gain_ref, w_ref, cos_ref, sa_ref, sb_ref, q_ref, k_ref, v_ref):
    qk = DIFF_HEADS * 2 * DIFF_HEAD_DIM
    a = _pre(h_ref[...], gain_ref[...], mod_ref[0], 3).astype(BF16)
    cos, sa, sb = cos_ref[...], sa_ref[...], sb_ref[...]
    f = DIFF_HEAD_DIM // 4
    scale = DIFF_HEAD_DIM ** -0.5
    vw = 2 * DIFF_HEAD_DIM
    ones = jnp.ones((a.shape[0], vw), BF16)

    def emit(col, tile):
        for j in range(MXU_COLS // LANES):
            lo = col + j * LANES
            x = tile[:, j * LANES:(j + 1) * LANES]
            if lo < qk:
                q_ref[:, lo:lo + LANES] = (_rope(x, cos, sa, sb, f) * scale).astype(BF16)
            elif lo < 2 * qk:
                k_ref[:, lo - qk:lo - qk + LANES] = _rope(x, cos, sa, sb, f).astype(BF16)
            else:
                hd = (lo - 2 * qk) // vw
                v_ref[:, 2 * hd * vw:(2 * hd + 1) * vw] = x.astype(BF16)
                v_ref[:, (2 * hd + 1) * vw:(2 * hd + 2) * vw] = ones

    _chunked_proj(a, w_ref, emit)


def _diff_attn_kernel(lam_ref, sub_ref, q_ref, kl_ref, vl_ref, kx_ref, vx_ref, o_ref, *, nq, lambda_init):
    i = pl.program_id(1)
    HD = DIFF_HEAD_DIM
    VW = 4 * HD
    lp = lam_ref[...]
    lam = (jnp.exp(jnp.sum(lp[0:1] * lp[1:2], keepdims=True))
           - jnp.exp(jnp.sum(lp[2:3] * lp[3:4], keepdims=True)) + lambda_init)

    def run(kv_pairs):
        def scores(h):
            los = [(2 * h + j) * HD for j in range(2)]
            return [[_dot_nt(q_ref[:, lo:lo + HD], k_ref[:, lo:lo + HD]) for k_ref, _ in kv_pairs] for lo in los]

        heads = []
        ss = scores(0)
        for h in range(DIFF_HEADS):
            nxt = scores(h + 1) if h + 1 < DIFF_HEADS else None
            maps = []
            for parts in ss:
                m = functools.reduce(jnp.maximum, [jnp.max(s, axis=-1, keepdims=True) for s in parts])
                acc = None
                for s, (_, v_ref) in zip(parts, kv_pairs):
                    t = _dot(jnp.exp(s - m).astype(BF16), v_ref[:, h * VW:(h + 1) * VW])
                    acc = t if acc is None else acc + t
                maps.append(acc[:, :VW // 2] * (1.0 / acc[:, VW // 2:]))
            o = maps[0] - lam * maps[1]
            heads.append(_rmsnorm(o, sub_ref[...]) * (1.0 - lambda_init))
            ss = nxt
        o_ref[...] = jnp.concatenate(heads, axis=-1).astype(BF16)

    @pl.when(i < nq)
    def _():
        run([(kl_ref, vl_ref), (kx_ref, vx_ref)])

    @pl.when(i >= nq)
    def _():
        run([(kx_ref, vx_ref)])


def _diff_mixer(lay, h, mod, gain, w_in, lam_params, subln_w, lambda_init, with_ctx_out):
    D = h.shape[1]
    B, S, C = lay.B, lay.S, lay.C
    qk = DIFF_HEADS * 2 * DIFF_HEAD_DIM
    tm = lay.tile(ROW_TILE)
    midx, pidx = lay.mod_index(tm), lay.pos_index(tm)
    cos, sa, sb = _rope_tables(S, DIFF_HEAD_DIM, tm, 0)
    tab = pl.BlockSpec((tm, LANES), lambda t: (pidx(t), 0))
    out = jax.ShapeDtypeStruct((lay.n_all, qk), BF16)
    q, k, v = pl.pallas_call(
        _diff_proj_kernel,
        grid=(lay.n_all // tm,),
        in_specs=[pl.BlockSpec((tm, D), lambda t: (t, 0)),
                  pl.BlockSpec((1, N_MOD, D), lambda t: (midx(t), 0, 0)),
                  pl.BlockSpec((1, D), lambda t: (0, 0)),
                  pl.BlockSpec((D, 3 * qk), lambda t: (0, 0)),
                  tab, tab, tab],
        out_specs=[pl.BlockSpec((tm, qk), lambda t: (t, 0))] * 2 + [pl.BlockSpec((tm, 2 * qk), lambda t: (t, 0))],
        out_shape=[out, out, jax.ShapeDtypeStruct((lay.n_all, 2 * qk), BF16)],
        compiler_params=_params("arbitrary"),
        name="diff_proj",
    )(h, mod, gain.reshape(1, D), w_in.astype(BF16), cos, sa, sb)

    tq = _divisor(math.gcd(S, C), QUERY_TILE)
    q_index, nq, nc, lat, ctx = _attn_specs(lay, tq, [qk, 2 * qk])
    row = pl.BlockSpec((tq, qk), lambda b, i: (q_index(b, i), 0))
    return pl.pallas_call(
        functools.partial(_diff_attn_kernel, nq=nq, lambda_init=lambda_init),
        grid=(B, nq + (nc if with_ctx_out else 0)),
        in_specs=[pl.BlockSpec((4, DIFF_HEAD_DIM), lambda b, i: (0, 0)),
                  pl.BlockSpec((1, 2 * DIFF_HEAD_DIM), lambda b, i: (0, 0)),
                  row, *lat, *ctx],
        out_specs=row,
        out_shape=jax.ShapeDtypeStruct((lay.n_all if with_ctx_out else lay.n_lat, qk), BF16),
        compiler_params=_params("arbitrary", "arbitrary"),
        name="diff_attn",
    )(lam_params.astype(F32), subln_w.reshape(1, -1), q, k, v, k, v)


def _hgrn_proj_kernel(h_ref, mod_ref, gain_ref, lb_ref, w_ref, o_ref, *, layer):
    KD = HGRN_HEADS * HGRN_KEY_DIM
    W = 2 * LANES
    a = _pre(h_ref[...], gain_ref[...], mod_ref[0], 3).astype(BF16)
    raw = lb_ref[...]
    e = jnp.exp(raw - jnp.max(raw, axis=0, keepdims=True))
    soft = e * (1.0 / jnp.sum(e, axis=0, keepdims=True))
    lb = jnp.sum(soft[1:layer + 1], axis=0, keepdims=True) if layer else jnp.zeros_like(soft[0:1])
    n = w_ref.shape[1] // W

    def proj(c):
        return _dot(a, w_ref[:, c * W:(c + 1) * W])

    cur = proj(0)
    for c in range(n):
        nxt = proj(c + 1) if c + 1 < n else None
        group, col = divmod(c * W, KD)
        if group in (0, 4):
            out = _silu(cur)
        elif group in (1, 2):
            lbc = lb[:, col:col + W]
            out = lbc + (1.0 - lbc) * jax.nn.sigmoid(cur)
        else:
            out = cur
        o_ref[:, c * W:(c + 1) * W] = out
        cur = nxt


def _running_sum(x, reverse):
    n, w = x.shape
    n_groups = n // SUBLANES
    x = x.reshape(n_groups, SUBLANES, w)
    within = lax.broadcasted_iota(jnp.int32, (1, SUBLANES, 1), 1)
    step = 1
    while step < SUBLANES:
        if reverse:
            x = x + jnp.where(within < SUBLANES - step, pltpu.roll(x, SUBLANES - step, 1), 0.0)
        else:
            x = x + jnp.where(within >= step, pltpu.roll(x, step, 1), 0.0)
        step *= 2
    out = [None] * n_groups
    carry = None
    for j in (reversed(range(n_groups)) if reverse else range(n_groups)):
        out[j] = x[j] if carry is None else x[j] + carry
        carry = out[j][0:1] if reverse else out[j][SUBLANES - 1:SUBLANES]
    return jnp.concatenate(out, axis=0)


def _hgrn_scan_kernel(qf_ref, ff_ref, vf_ref, qb_ref, fb_ref, vb_ref, of_ref, ob_ref, st_ref, *, n_chunks):
    step = pl.program_id(1)
    C = HGRN_CHUNK
    DK, DV, H = HGRN_KEY_DIM, HGRN_VAL_DIM, HGRN_HEADS

    @pl.when(step == 0)
    def _():
        st_ref[...] = jnp.zeros_like(st_ref)

    r = lax.broadcasted_iota(jnp.int32, (C, C), 0)
    c = lax.broadcasted_iota(jnp.int32, (C, C), 1)
    ks = [slice(h * DK, (h + 1) * DK) for h in range(H)]
    vs = [slice(h * DV, (h + 1) * DV) for h in range(H)]
    dirs = [(qf_ref, ff_ref, vf_ref, of_ref, c <= r, C // 2, C - 1),
            (qb_ref, fb_ref, vb_ref, ob_ref, c >= r, C - 1 - C // 2, 0)]

    def prepare(d, n):
        q_ref, f_ref, v_ref, _, tri, mid_row, last_row = dirs[d]
        base = (n if d == 0 else n_chunks - 1 - n) * C
        q = q_ref[base:base + C, :]
        f = f_ref[base:base + C, :]
        vt = v_ref[base:base + C, :].T.astype(BF16)
        k = 1.0 - f
        g = _running_sum(jnp.log2(f), reverse=d == 1)
        g_mid = g[mid_row:mid_row + 1]
        g_last = g[last_row:last_row + 1]
        t = g - g_mid
        qa_f = q * jnp.exp2(t)
        kb_f = k * jnp.exp2(-t)
        qa = qa_f.astype(BF16)
        kb = kb_f.astype(BF16)
        qg = (qa_f * jnp.exp2(g_mid)).astype(BF16)
        kd = (kb_f * jnp.exp2(g_last - g_mid)).astype(BF16)
        a = [jnp.where(tri, _dot_nt(qa[:, ks[h]], kb[:, ks[h]]), 0.0).astype(BF16) for h in range(H)]
        lhs = [jnp.concatenate([qg[:, ks[h]], a[h]], axis=1) for h in range(H)]
        update = [_dot(vt[vs[h], :], kd[:, ks[h]]) for h in range(H)]
        return base, lhs, vt, jnp.exp2(g_last), update

    cur = [prepare(d, 0) for d in range(2)]
    for n in range(n_chunks):
        nxt = [prepare(d, n + 1) for d in range(2)] if n + 1 < n_chunks else None
        for d in range(2):
            base, lhs, vt, decay, update = cur[d]
            st = [st_ref[d, vs[h], :] for h in range(H)]
            outs = [_dot_nt(lhs[h], jnp.concatenate([st[h].astype(BF16), vt[vs[h], :]], axis=1)) for h in range(H)]
            for h in range(H):
                st_ref[d, vs[h], :] = st[h] * decay[:, ks[h]] + update[h]
            dirs[d][3][base:base + C, :] = jnp.concatenate(outs, axis=-1)
        cur = nxt


def _hgrn_readout(of_ref, ob_ref, gate_ref, nw_ref):
    DV = HGRN_VAL_DIM
    o = of_ref[...] + ob_ref[...]
    gate = gate_ref[...]
    nw = nw_ref[...]
    ys = [_rmsnorm(o[:, h * DV:(h + 1) * DV], nw) * gate[:, h * DV:(h + 1) * DV] for h in range(HGRN_HEADS)]
    return jnp.concatenate(ys, axis=-1).astype(BF16)


def _plain_readout(o_ref):
    return o_ref[...]


def _hgrn_mixer(lay, h, mod, gain, w_in, norm_w, lower_bounds, layer):
    D = h.shape[1]
    B, S, C = lay.B, lay.S, lay.C
    KD = HGRN_HEADS * HGRN_KEY_DIM
    tm = lay.tile(ROW_TILE)
    midx = lay.mod_index(tm)
    n_cols = w_in.shape[1] // KD
    proj = pl.pallas_call(
        functools.partial(_hgrn_proj_kernel, layer=layer),
        grid=(lay.n_all // tm,),
        in_specs=[pl.BlockSpec((tm, D), lambda t: (t, 0)),
                  pl.BlockSpec((1, N_MOD, D), lambda t: (midx(t), 0, 0)),
                  pl.BlockSpec((1, D), lambda t: (0, 0)),
                  pl.BlockSpec(lower_bounds.shape, lambda t: (0, 0)),
                  pl.BlockSpec((D, n_cols * KD), lambda t: (0, 0), pipeline_mode=pl.Buffered(1))],
        out_specs=pl.BlockSpec((tm, n_cols * KD), lambda t: (t, 0)),
        out_shape=jax.ShapeDtypeStruct((lay.n_all, n_cols * KD), F32),
        compiler_params=_params("arbitrary"),
        name="hgrn_proj",
    )(h, mod, gain.reshape(1, D), lower_bounds.astype(F32), w_in.astype(BF16))

    ts = C
    n_lat_steps = S // ts
    lat_blocks = lay.n_lat // ts

    def rows(d):
        def index(b, s):
            lat = b * n_lat_steps + (s - 1 if d == 0 else n_lat_steps - s)
            return jnp.where(s == 0, lat_blocks + b, lat)
        return index

    def col(d, j):
        return pl.BlockSpec((ts, KD), lambda b, s: (rows(d)(b, s), j))

    o_f, o_b = pl.pallas_call(
        functools.partial(_hgrn_scan_kernel, n_chunks=ts // HGRN_CHUNK),
        grid=(B, 1 + n_lat_steps),
        in_specs=[col(0, 0), col(0, 1), col(0, 3), col(1, 0), col(1, 2), col(1, 3)],
        out_specs=[col(0, 0), col(1, 0)],
        out_shape=[jax.ShapeDtypeStruct((lay.n_all, KD), F32)] * 2,
        scratch_shapes=[pltpu.VMEM((2, HGRN_HEADS * HGRN_VAL_DIM, HGRN_KEY_DIM), F32)],
        compiler_params=_params("arbitrary", "arbitrary"),
        name="hgrn_scan",
    )(proj, proj, proj, proj, proj, proj)

    return _hgrn_readout, [(o_f, 0), (o_b, 0), (proj, 4), (norm_w.reshape(1, -1), 0)]


MLA_QK_PAD = LANES


def _mla_proj_kernel(h_ref, mod_ref, gain_ref, wd_ref, qn_ref, kvn_ref, wuq_ref, wukv_ref,
                     cos_ref, sa_ref, sb_ref, q_ref, k_ref, v_ref):
    H, P = MLA_HEADS, MLA_QK_PAD
    a = _pre(h_ref[...], gain_ref[...], mod_ref[0], 3).astype(BF16)
    dn = _dot(a, wd_ref[...])
    cos, sa, sb = cos_ref[...], sa_ref[...], sb_ref[...]
    f = MLA_ROPE // 4
    cq = _rmsnorm(dn[:, :MLA_Q_LORA], qn_ref[...]).astype(BF16)
    ckv = _rmsnorm(dn[:, MLA_Q_LORA:MLA_Q_LORA + MLA_KV_LORA], kvn_ref[...]).astype(BF16)
    kr = _rope(dn[:, MLA_Q_LORA + MLA_KV_LORA:], cos, sa, sb, f)
    lane = lax.broadcasted_iota(jnp.int32, (1, P), 1)
    ones_half = jnp.where(lane >= MLA_V_DIM, 1.0, 0.0)

    def emit_q(col, tile):
        for j in range(MXU_COLS // P):
            lo = col + j * P
            q_ref[:, lo:lo + P] = _rope(tile[:, j * P:(j + 1) * P], cos, sa, sb, f).astype(BF16)

    def emit_kv(col, tile):
        for j in range(MXU_COLS // P):
            lo = col + j * P
            x = tile[:, j * P:(j + 1) * P]
            if lo < H * P:
                k_ref[:, lo:lo + P] = (x + kr).astype(BF16)
            else:
                v_ref[:, lo - H * P:lo - H * P + P] = (x + ones_half).astype(BF16)

    _chunked_proj(cq, wuq_ref, emit_q)
    _chunked_proj(ckv, wukv_ref, emit_kv)


def _mla_attn_kernel(q_ref, kl_ref, vl_ref, kx_ref, vx_ref, o_ref, *, nq, with_ctx):
    i = pl.program_id(1)
    P, VD = MLA_QK_PAD, MLA_V_DIM
    c = ((MLA_NOPE + MLA_ROPE) ** -0.5) * LOG2_E

    def run(kv_pairs):
        def scores(hd):
            qh = q_ref[:, hd * P:(hd + 1) * P]
            return [_dot_nt(qh, k_ref[:, hd * P:(hd + 1) * P]) for k_ref, _ in kv_pairs]

        heads = []
        ss = scores(0)
        for hd in range(MLA_HEADS):
            nxt = scores(hd + 1) if hd + 1 < MLA_HEADS else None
            m = functools.reduce(jnp.maximum, [jnp.max(s, axis=-1, keepdims=True) for s in ss])
            mc = m * c
            pair = (hd // 2) * MXU_COLS
            acc = None
            for s, (_, v_ref) in zip(ss, kv_pairs):
                t = _dot(jnp.exp2(s * c - mc).astype(BF16), v_ref[:, pair:pair + MXU_COLS])
                acc = t if acc is None else acc + t
            acc = acc[:, (hd % 2) * P:(hd % 2 + 1) * P]
            heads.append(acc[:, :VD] * (1.0 / acc[:, VD:]))
            ss = nxt
        o_ref[...] = jnp.concatenate(heads, axis=-1).astype(BF16)

    if not with_ctx:
        run([(kl_ref, vl_ref), (kx_ref, vx_ref)])
        return

    @pl.when(i < nq)
    def _():
        run([(kl_ref, vl_ref), (kx_ref, vx_ref)])

    @pl.when(i >= nq)
    def _():
        run([(kx_ref, vx_ref)])


def _mla_mixer(lay, h, mod, gain, w_down, q_norm_w, kv_norm_w, w_uq, w_ukv, with_ctx_out):
    D = h.shape[1]
    B, S, C = lay.B, lay.S, lay.C
    H, P, VD = MLA_HEADS, MLA_QK_PAD, MLA_V_DIM
    qk = MLA_NOPE + MLA_ROPE
    lora = MLA_Q_LORA + MLA_KV_LORA
    kr_cols = jnp.zeros((D, P), F32).at[:, MLA_NOPE:qk].set(w_down[:, lora:])
    wd = jnp.concatenate([w_down[:, :lora], kr_cols], axis=1).astype(BF16)
    wuq = jnp.pad(w_uq.reshape(MLA_Q_LORA, H, qk), ((0, 0), (0, 0), (0, P - qk))).reshape(MLA_Q_LORA, H * P)
    ukv = w_ukv.reshape(MLA_KV_LORA, H, MLA_NOPE + VD)
    wkn = jnp.pad(ukv[..., :MLA_NOPE], ((0, 0), (0, 0), (0, P - MLA_NOPE))).reshape(MLA_KV_LORA, H * P)
    wv = jnp.pad(ukv[..., MLA_NOPE:], ((0, 0), (0, 0), (0, P - VD))).reshape(MLA_KV_LORA, H * P)
    wukv = jnp.concatenate([wkn, wv], axis=1)

    tm = lay.tile(ROW_TILE)
    midx, pidx = lay.mod_index(tm), lay.pos_index(tm)
    cos, sa, sb = _rope_tables(S, MLA_ROPE, tm, MLA_NOPE)
    tab = pl.BlockSpec((tm, LANES), lambda t: (pidx(t), 0))
    full = lambda shape: pl.BlockSpec(shape, lambda t: (0, 0))
    q, k, v = pl.pallas_call(
        _mla_proj_kernel,
        grid=(lay.n_all // tm,),
        in_specs=[pl.BlockSpec((tm, D), lambda t: (t, 0)),
                  pl.BlockSpec((1, N_MOD, D), lambda t: (midx(t), 0, 0)),
                  full((1, D)), full((D, lora + P)), full((1, MLA_Q_LORA)), full((1, MLA_KV_LORA)),
                  full((MLA_Q_LORA, H * P)), full((MLA_KV_LORA, 2 * H * P)),
                  tab, tab, tab],
        out_specs=[pl.BlockSpec((tm, H * P), lambda t: (t, 0))] * 3,
        out_shape=[jax.ShapeDtypeStruct((lay.n_all, H * P), BF16)] * 3,
        compiler_params=_params("arbitrary"),
        name="mla_proj",
    )(h, mod, gain.reshape(1, D), wd, q_norm_w.reshape(1, -1), kv_norm_w.reshape(1, -1),
      wuq.astype(BF16), wukv.astype(BF16), cos, sa, sb)

    tq = _divisor(math.gcd(S, C), QUERY_TILE)
    q_index, nq, nc, lat, ctx = _attn_specs(lay, tq, [H * P, H * P])
    return pl.pallas_call(
        functools.partial(_mla_attn_kernel, nq=nq, with_ctx=with_ctx_out),
        grid=(B, nq + (nc if with_ctx_out else 0)),
        in_specs=[pl.BlockSpec((tq, H * P), lambda b, i: (q_index(b, i), 0)), *lat, *ctx],
        out_specs=pl.BlockSpec((tq, H * VD), lambda b, i: (q_index(b, i), 0)),
        out_shape=jax.ShapeDtypeStruct((lay.n_all if with_ctx_out else lay.n_lat, H * VD), BF16),
        compiler_params=_params("arbitrary", "arbitrary"),
        name="mla_attn",
    )(q, k, v, k, v)


def kernel(x, c, ctx, c_ctx, ada_w, ada_b, norm_w, final_norm_w, ffn_w_gate, ffn_w_up, ffn_w_down,
           gqa_w_in, gqa_w_out, gqa_sinks, diff_w_in, diff_w_out, diff_lambda, diff_subln_w,
           hgrn_w_in, hgrn_w_out, hgrn_norm_w, hgrn_lower_bounds,
           mla_w_down, mla_q_norm_w, mla_kv_norm_w, mla_w_uq, mla_w_ukv, mla_w_out):
    B, S, D = x.shape
    C = ctx.shape[1]
    lay = _Layout(B, S, C)
    depth = ada_w.shape[0]

    rows = -(-(B + 1) // 8) * 8
    cc = jnp.zeros((rows, D), F32).at[:B].set(c).at[B].set(c_ctx)
    mods = _ada_table(cc, ada_w, ada_b).reshape(depth, rows, N_MOD, D)

    stacks = (ffn_w_gate, ffn_w_up, ffn_w_down)
    weights = tuple(w[0, 0].astype(BF16) for w in stacks)
    h = (x.reshape(B * S, D), ctx.reshape(B * C, D))
    for i in range(depth):
        kind, j = i % 4, i // 4
        last = i == depth - 1
        mod = mods[i]
        h, weights = _ffn(lay, h, mod, norm_w[i, 0], final_norm_w, weights, 0, cast_next=(*stacks, i, 1))
        if kind == 0:
            o = _gqa_mixer(lay, h, mod, norm_w[i, 1], gqa_w_in[j], gqa_sinks[j], not last)
            mix = (_plain_readout, [(o, 0)], gqa_w_out[j])
        elif kind == 1:
            lambda_init = 0.8 - 0.6 * math.exp(-0.3 * i)
            o = _diff_mixer(lay, h, mod, norm_w[i, 1], diff_w_in[j], diff_lambda[j], diff_subln_w[j],
                            lambda_init, not last)
            mix = (_plain_readout, [(o, 0)], diff_w_out[j])
        elif kind == 2:
            readout, operands = _hgrn_mixer(lay, h, mod, norm_w[i, 1], hgrn_w_in[j], hgrn_norm_w[j],
                                            hgrn_lower_bounds, i)
            mix = (readout, operands, hgrn_w_out[j])
        else:
            o = _mla_mixer(lay, h, mod, norm_w[i, 1], mla_w_down[j], mla_q_norm_w[j], mla_kv_norm_w[j],
                           mla_w_uq[j], mla_w_ukv[j], not last)
            mix = (_plain_readout, [(o, 0)], mla_w_out[j])
        h, weights = _ffn(lay, h, mod, norm_w[i, 2], final_norm_w, weights, 1, lat_only=last, final_norm=last,
                          mix=mix, cast_next=None if last else (*stacks, i + 1, 0))
    return h[:B * S].reshape(B, S, D)
```

```python
import functools
import math

import jax
import jax.numpy as jnp
from jax import lax
from jax.experimental import pallas as pl
from jax.experimental.pallas import tpu as pltpu

F32 = jnp.float32
BF16 = jnp.bfloat16

D_MODEL = 1024
GRID_W = 64
N_MOD = 9
ROPE_BASE = 10000.0
EPS = 1e-6
NEG_INF = -1e30
D_FF = 2816

GQA_HEADS = 16
GQA_KV_HEADS = 4
GQA_HEAD_DIM = 64
GQA_WINDOW = 128
GQA_BLOCK = 128

DIFF_HEADS = 8
DIFF_HEAD_DIM = 64

HGRN_HEADS = 8
HGRN_KEY_DIM = 128
HGRN_VAL_DIM = D_MODEL // HGRN_HEADS
HGRN_CHUNK = 64

MLA_HEADS = 16
MLA_Q_LORA = 256
MLA_KV_LORA = 256
MLA_NOPE = 64
MLA_ROPE = 32
MLA_V_DIM = 64

LOG2_E = 1.4426950408889634
LANES = 128
SUBLANES = 8
MXU_COLS = 2 * LANES
VMEM_LIMIT = 56 * 1024 * 1024
ROW_TILE = 512
QUERY_TILE = 256


def _dot(a, b):
    return jnp.dot(a, b, preferred_element_type=F32)


def _dot_nt(a, b):
    return lax.dot_general(a, b, (((1,), (1,)), ((), ())), preferred_element_type=F32)


def _params(*sem):
    return pltpu.CompilerParams(dimension_semantics=sem, vmem_limit_bytes=VMEM_LIMIT)


def _divisor(n, pref):
    t = min(n, pref)
    while n % t:
        t -= 8
    return t


class _Layout:
    def __init__(self, B, S, C):
        self.B, self.S, self.C = B, S, C
        self.n_lat = B * S
        self.n_all = B * S + B * C

    def tile(self, pref):
        return _divisor(math.gcd(self.S, self.B * self.C), pref)

    def mod_index(self, tm):
        n_lat_tiles, per_batch, B = self.n_lat // tm, self.S // tm, self.B
        return lambda t: jnp.where(t < n_lat_tiles, t // per_batch, B)

    def pos_index(self, tm):
        n_lat_tiles, per_batch = self.n_lat // tm, self.S // tm
        return lambda t: jnp.where(t < n_lat_tiles, t % per_batch, per_batch)


def _rmsnorm(x, w):
    return (x * lax.rsqrt(jnp.mean(x * x, axis=-1, keepdims=True) + EPS)) * w


def _pre(h, gain, m, off):
    return _rmsnorm(h, gain) * (1.0 + m[off + 1:off + 2]) + m[off:off + 1]


def _silu(x):
    return x * jax.nn.sigmoid(x)


def _chunked_proj(a, w_ref, emit):
    n = w_ref.shape[1] // MXU_COLS
    cur = _dot(a, w_ref[:, :MXU_COLS])
    for c in range(n):
        nxt = _dot(a, w_ref[:, (c + 1) * MXU_COLS:(c + 2) * MXU_COLS]) if c + 1 < n else None
        emit(c * MXU_COLS, cur)
        cur = nxt


def _rope(x, cos, sa, sb, shift):
    return x * cos + pltpu.roll(x, LANES - shift, 1) * sa + pltpu.roll(x, shift, 1) * sb


def _ada_kernel(c_ref, w_ref, b_ref, o_ref):
    sc = _silu(c_ref[...]).astype(BF16)
    o_ref[0] = _dot(sc, w_ref[0].astype(BF16)) + b_ref[0]


ADA_COLS = N_MOD * LANES


def _ada_table(cc, ada_w, ada_b):
    rows, D = cc.shape
    depth, _, n_out = ada_w.shape
    tn = ADA_COLS
    return pl.pallas_call(
        _ada_kernel,
        grid=(depth, n_out // tn),
        in_specs=[pl.BlockSpec((rows, D), lambda i, j: (0, 0)),
                  pl.BlockSpec((1, D, tn), lambda i, j: (i, 0, j)),
                  pl.BlockSpec((1, 1, tn), lambda i, j: (i, 0, j))],
        out_specs=pl.BlockSpec((1, rows, tn), lambda i, j: (i, 0, j)),
        out_shape=jax.ShapeDtypeStruct((depth, rows, n_out), F32),
        compiler_params=_params("arbitrary", "arbitrary"),
        name="ada_table",
    )(cc, ada_w, ada_b.reshape(depth, 1, n_out))


FFN_CHUNK = MXU_COLS


def _ffn_kernel(*refs, off, final_norm, n_lat_tiles, readout, n_mix, cast_next):
    refs = list(refs)
    if n_lat_tiles is None:
        h = refs.pop(0)[...]
    else:
        hx_ref, hc_ref = refs.pop(0), refs.pop(0)
        h = jnp.where(pl.program_id(0) < n_lat_tiles, hx_ref[...], hc_ref[...])
    mix_refs = [refs.pop(0) for _ in range(n_mix)]
    if readout is not None:
        wo_ref = refs.pop(0)
    mod_ref, gain_ref, fin_ref, wg_ref, wu_ref, wd_ref = refs[:6]
    if cast_next:
        for src, dst in zip(refs[6:9], refs[10:13]):
            dst[...] = src[...].astype(BF16)
    o_ref, a_ref = refs[9 if cast_next else 6], refs[-1]
    m = mod_ref[0]
    if readout is not None:
        h = h + m[5:6] * _dot(readout(*mix_refs), wo_ref[...])
    xm = _pre(h, gain_ref[...], m, off).astype(BF16)
    n = D_FF // FFN_CHUNK

    def gate_up(c):
        cols = slice(c * FFN_CHUNK, (c + 1) * FFN_CHUNK)
        return _dot(xm, wg_ref[:, cols]), _dot(xm, wu_ref[:, cols])

    cur = gate_up(0)
    for c in range(n):
        nxt = gate_up(c + 1) if c + 1 < n else None
        g, u = cur
        a_ref[:, c * FFN_CHUNK:(c + 1) * FFN_CHUNK] = (_silu(g) * u).astype(BF16)
        cur = nxt
    out = h + (0.5 * m[off + 2:off + 3]) * _dot(a_ref[...], wd_ref[...])
    if final_norm:
        out = _rmsnorm(out, fin_ref[...])
    o_ref[...] = out


def _slabs(rows, n_steps):
    n = n_steps
    while rows % n or (rows // n) % (2 * SUBLANES):
        n -= 1
    return n


def _ffn(lay, h, mod, gain, fin_w, weights, half, lat_only=False, final_norm=False, mix=None, cast_next=None):
    split = isinstance(h, tuple)
    wg, wu, wd = weights
    D = wg.shape[0]
    tm = lay.tile(ROW_TILE)
    n_rows = lay.n_lat if lat_only else lay.n_all
    n_steps = n_rows // tm
    midx = lay.mod_index(tm)
    n_lat_tiles = lay.n_lat // tm
    const = lambda t: (0, 0)
    one = pl.Buffered(1)
    weight = lambda shape: pl.BlockSpec(shape, const, pipeline_mode=one)
    if split:
        h_specs = [pl.BlockSpec((tm, D), lambda t: (jnp.minimum(t, n_lat_tiles - 1), 0)),
                   pl.BlockSpec((tm, D), lambda t: (jnp.maximum(t - n_lat_tiles, 0), 0))]
        h_args = list(h)
    else:
        h_specs = [pl.BlockSpec((tm, D), lambda t: (t, 0))]
        h_args = [h]
    readout, operands = None, []
    if mix is not None:
        readout, operands, w_out = mix
        for arr, col in operands:
            if arr.shape[0] == 1:
                h_specs.append(pl.BlockSpec(arr.shape, const))
            else:
                h_specs.append(pl.BlockSpec((tm, D), lambda t, col=col: (t, col)))
            h_args.append(arr)
        h_specs.append(pl.BlockSpec(w_out.shape, const, pipeline_mode=one))
        h_args.append(w_out.astype(BF16))
    cast_specs, cast_args, cast_out_specs, cast_out_shapes = [], [], [], []
    if cast_next is not None:
        *stacks, nl, nh = cast_next
        for w32 in stacks:
            rows, cols = w32.shape[2:]
            n = _slabs(rows, n_steps)
            slab = rows // n
            cast_specs.append(pl.BlockSpec((None, None, slab, cols),
                                           lambda t, n=n: (nl, nh, jnp.minimum(t, n - 1), 0)))
            cast_args.append(w32)
            cast_out_specs.append(pl.BlockSpec((slab, cols), lambda t, n=n: (jnp.minimum(t, n - 1), 0)))
            cast_out_shapes.append(jax.ShapeDtypeStruct((rows, cols), BF16))
    outs = pl.pallas_call(
        functools.partial(_ffn_kernel, off=6 * half, final_norm=final_norm,
                          n_lat_tiles=n_lat_tiles if split else None, readout=readout, n_mix=len(operands),
                          cast_next=cast_next is not None),
        grid=(n_steps,),
        in_specs=h_specs + [pl.BlockSpec((1, N_MOD, D), lambda t: (midx(t), 0, 0)),
                            pl.BlockSpec((1, D), const),
                            pl.BlockSpec((1, D), const),
                            weight((D, D_FF)), weight((D, D_FF)), weight((D_FF, D))] + cast_specs,
        out_specs=[pl.BlockSpec((tm, D), lambda t: (t, 0))] + cast_out_specs,
        out_shape=[jax.ShapeDtypeStruct((n_rows, D), F32)] + cast_out_shapes,
        scratch_shapes=[pltpu.VMEM((tm, D_FF), BF16)],
        compiler_params=_params("arbitrary"),
        name="ffn",
    )(*h_args, mod, gain.reshape(1, D), fin_w.reshape(1, D), wg, wu, wd, *cast_args)
    return outs[0], (tuple(outs[1:]) if cast_next is not None else None)


def _rope_tables(S, rot_dim, pad_rows, lane_off):
    rows = S // GRID_W
    row = jnp.repeat(jnp.arange(rows, dtype=F32), GRID_W)
    col = jnp.tile(jnp.arange(GRID_W, dtype=F32), rows)
    axis_dim = rot_dim // 2
    inv_freq = ROPE_BASE ** (-jnp.arange(0, axis_dim, 2, dtype=F32) / axis_dim)
    ang_r = row[:, None] * inv_freq[None, :]
    ang_c = col[:, None] * inv_freq[None, :]
    ang = jnp.concatenate([ang_r, ang_r, ang_c, ang_c], axis=-1)
    cos, sin = jnp.cos(ang), jnp.sin(ang)
    f = rot_dim // 4
    first = (jnp.arange(rot_dim) % (2 * f)) < f
    sa = jnp.where(first[None, :], -sin, 0.0)
    sb = jnp.where(first[None, :], 0.0, sin)
    period = 64 if lane_off + rot_dim <= 64 else LANES

    def widen(t, fill):
        blk = jnp.full((S, period), fill, F32).at[:, lane_off:lane_off + rot_dim].set(t)
        blk = jnp.tile(blk, (1, LANES // period))
        return jnp.concatenate([blk, jnp.full((pad_rows, LANES), fill, F32)], axis=0)

    return widen(cos, 1.0), widen(sa, 0.0), widen(sb, 0.0)


def _attn_specs(lay, tq, widths):
    S, C, B = lay.S, lay.C, lay.B
    nq, nc = S // tq, C // tq
    lat_blocks = lay.n_lat // C

    def q_index(b, i):
        return jnp.where(i < nq, b * nq + i, B * nq + b * nc + (i - nq))

    lat = [pl.BlockSpec((S, w), lambda b, i: (b, 0)) for w in widths]
    ctx = [pl.BlockSpec((C, w), lambda b, i: (lat_blocks + b, 0)) for w in widths]
    return q_index, nq, nc, lat, ctx


def _gqa_proj_kernel(h_ref, mod_ref, gain_ref, w_ref, cos_ref, sa_ref, sb_ref, q_ref, k_ref, v_ref):
    qd, kd = GQA_HEADS * GQA_HEAD_DIM, GQA_KV_HEADS * GQA_HEAD_DIM
    a = _pre(h_ref[...], gain_ref[...], mod_ref[0], 3).astype(BF16)
    cos, sa, sb = cos_ref[...], sa_ref[...], sb_ref[...]
    f = GQA_HEAD_DIM // 4
    scale = GQA_HEAD_DIM ** -0.5
    lane = lax.broadcasted_iota(jnp.int32, (1, MXU_COLS), 1)
    ones_half = jnp.where((lane & (2 * GQA_HEAD_DIM - 1)) >= GQA_HEAD_DIM, 1.0, 0.0)

    def emit(col, tile):
        if col >= qd + kd:
            v_ref[:, col - qd - kd:col - qd - kd + MXU_COLS] = (tile + ones_half).astype(BF16)
            return
        for j in range(MXU_COLS // LANES):
            x = _rope(tile[:, j * LANES:(j + 1) * LANES], cos, sa, sb, f)
            lo = col + j * LANES
            if lo < qd:
                q_ref[:, lo:lo + LANES] = (x * scale).astype(BF16)
            else:
                k_ref[:, lo - qd:lo - qd + LANES] = x.astype(BF16)

    _chunked_proj(a, w_ref, emit)


def _gqa_attn_kernel(sink_ref, q_ref, kp_ref, kc_ref, kn_ref, vp_ref, vc_ref, vn_ref, kx_ref, vx_ref,
                     o_ref, kcat_ref, vcat_ref, *, nq):
    i = pl.program_id(1)
    blk = GQA_BLOCK
    G = GQA_HEADS // GQA_KV_HEADS
    HD = GQA_HEAD_DIM
    VW = 2 * HD
    span = 3 * blk

    def run(k_ref, v_ref, masks):
        kvs = range(GQA_KV_HEADS)
        qs = [jnp.concatenate([q_ref[:, (G * g + n) * HD:(G * g + n + 1) * HD] for n in range(G)], axis=0)
              for g in kvs]
        sinks = [jnp.concatenate([jnp.full((blk, 1), sink_ref[G * g + n], F32) for n in range(G)], axis=0)
                 for g in kvs]
        ss = [_dot_nt(qs[g], k_ref[:, g * HD:(g + 1) * HD]) for g in kvs]
        if masks is not None:
            prev_ok, next_ok = masks
            ss = [jnp.concatenate([jnp.where(prev_ok, s[:, :blk], NEG_INF), s[:, blk:2 * blk],
                                   jnp.where(next_ok, s[:, 2 * blk:span], NEG_INF), s[:, span:]], axis=1)
                  for s in ss]
        ms = [jnp.maximum(jnp.max(ss[g], axis=-1, keepdims=True), sinks[g]) for g in kvs]
        accs = [_dot(jnp.exp(ss[g] - ms[g]).astype(BF16),
                     v_ref[:, (g // 2) * MXU_COLS:(g // 2 + 1) * MXU_COLS])[:, (g % 2) * VW:(g % 2 + 1) * VW]
                for g in kvs]
        outs = [accs[g][:, :HD] * (1.0 / (accs[g][:, HD:] + jnp.exp(sinks[g] - ms[g]))) for g in kvs]
        heads = [outs[g][n * blk:(n + 1) * blk] for g in kvs for n in range(G)]
        o_ref[...] = jnp.concatenate(heads, axis=-1).astype(BF16)

    @pl.when(i < nq)
    def _():
        for n, (k_ref, v_ref) in enumerate([(kp_ref, vp_ref), (kc_ref, vc_ref), (kn_ref, vn_ref)]):
            kcat_ref[n * blk:(n + 1) * blk] = k_ref[...]
            vcat_ref[n * blk:(n + 1) * blk] = v_ref[...]
        kcat_ref[span:] = kx_ref[...]
        vcat_ref[span:] = vx_ref[...]
        row = lax.broadcasted_iota(jnp.int32, (G * blk, blk), 0) & (blk - 1)
        col = lax.broadcasted_iota(jnp.int32, (G * blk, blk), 1)
        prev_ok = (row + blk - col <= GQA_WINDOW) & (i >= 1)
        next_ok = (col + blk - row <= GQA_WINDOW) & (i + 1 < nq)
        run(kcat_ref, vcat_ref, (prev_ok, next_ok))

    @pl.when(i >= nq)
    def _():
        run(kx_ref, vx_ref, None)


def _gqa_mixer(lay, h, mod, gain, w_in, sinks, with_ctx_out):
    D = h.shape[1]
    B, S, C = lay.B, lay.S, lay.C
    qd, kd = GQA_HEADS * GQA_HEAD_DIM, GQA_KV_HEADS * GQA_HEAD_DIM
    w_v = jnp.pad(w_in[:, qd + kd:].reshape(D, GQA_KV_HEADS, GQA_HEAD_DIM), ((0, 0), (0, 0), (0, GQA_HEAD_DIM)))
    w_pad = jnp.concatenate([w_in[:, :qd + kd], w_v.reshape(D, 2 * kd)], axis=1)
    tm = lay.tile(ROW_TILE)
    midx, pidx = lay.mod_index(tm), lay.pos_index(tm)
    cos, sa, sb = _rope_tables(S, GQA_HEAD_DIM, tm, 0)
    tab = pl.BlockSpec((tm, LANES), lambda t: (pidx(t), 0))
    q, k, v = pl.pallas_call(
        _gqa_proj_kernel,
        grid=(lay.n_all // tm,),
        in_specs=[pl.BlockSpec((tm, D), lambda t: (t, 0)),
                  pl.BlockSpec((1, N_MOD, D), lambda t: (midx(t), 0, 0)),
                  pl.BlockSpec((1, D), lambda t: (0, 0)),
                  pl.BlockSpec((D, qd + 3 * kd), lambda t: (0, 0)),
                  tab, tab, tab],
        out_specs=[pl.BlockSpec((tm, qd), lambda t: (t, 0)),
                   pl.BlockSpec((tm, kd), lambda t: (t, 0)),
                   pl.BlockSpec((tm, 2 * kd), lambda t: (t, 0))],
        out_shape=[jax.ShapeDtypeStruct((lay.n_all, qd), BF16),
                   jax.ShapeDtypeStruct((lay.n_all, kd), BF16),
                   jax.ShapeDtypeStruct((lay.n_all, 2 * kd), BF16)],
        compiler_params=_params("arbitrary"),
        name="gqa_proj",
    )(h, mod, gain.reshape(1, D), w_pad.astype(BF16), cos, sa, sb)

    blk = GQA_BLOCK
    assert S % blk == 0 and C % blk == 0
    nq, nc = S // blk, C // blk
    n_i = nq + (nc if with_ctx_out else 0)
    lat_blocks = lay.n_lat // C

    def q_index(b, i):
        return jnp.where(i < nq, b * nq + i, B * nq + b * nc + (i - nq))

    def win(delta):
        return lambda b, i: (b * nq + jnp.clip(i + delta, 0, nq - 1), 0)

    def windows(width):
        return [pl.BlockSpec((blk, width), win(d)) for d in (-1, 0, 1)]

    def ctx_spec(width):
        return pl.BlockSpec((C, width), lambda b, i: (lat_blocks + b, 0))

    row = pl.BlockSpec((blk, qd), lambda b, i: (q_index(b, i), 0))
    return pl.pallas_call(
        functools.partial(_gqa_attn_kernel, nq=nq),
        grid=(B, n_i),
        in_specs=[pl.BlockSpec(memory_space=pltpu.SMEM), row,
                  *windows(kd), *windows(2 * kd), ctx_spec(kd), ctx_spec(2 * kd)],
        out_specs=row,
        out_shape=jax.ShapeDtypeStruct((lay.n_all if with_ctx_out else lay.n_lat, qd), BF16),
        scratch_shapes=[pltpu.VMEM((3 * blk + C, kd), BF16), pltpu.VMEM((3 * blk + C, 2 * kd), BF16)],
        compiler_params=_params("arbitrary", "arbitrary"),
        name="gqa_attn",
    )(sinks.astype(F32), q, k, k, k, v, v, v, k, v)


def _diff_proj_kernel(h_ref, mod_ref, gain_ref, w_ref, cos_ref, sa_ref, sb_ref, q_ref, k_ref, v_ref):
    qk = DIFF_HEADS * 2 * DIFF_HEAD_DIM
    a = _pre(h_ref[...], gain_ref[...], mod_ref[0], 3).astype(BF16)
    cos, sa, sb = cos_ref[...], sa_ref[...], sb_ref[...]
    f = DIFF_HEAD_DIM // 4
    scale = DIFF_HEAD_DIM ** -0.5
    vw = 2 * DIFF_HEAD_DIM
    ones = jnp.ones((a.shape[0], vw), BF16)

    def emit(col, tile):
        for j in range(MXU_COLS // LANES):
            lo = col + j * LANES
            x = tile[:, j * LANES:(j + 1) * LANES]
            if lo < qk:
                q_ref[:, lo:lo + LANES] = (_rope(x, cos, sa, sb, f) * scale).astype(BF16)
            elif lo < 2 * qk:
                k_ref[:, lo - qk:lo - qk + LANES] = _rope(x, cos, sa, sb, f).astype(BF16)
            else:
                hd = (lo - 2 * qk) // vw
                v_ref[:, 2 * hd * vw:(2 * hd + 1) * vw] = x.astype(BF16)
                v_ref[:, (2 * hd + 1) * vw:(2 * hd + 2) * vw] = ones

    _chunked_proj(a, w_ref, emit)


def _diff_attn_kernel(lam_ref, sub_ref, q_ref, kl_ref, vl_ref, kx_ref, vx_ref, o_ref, *, nq, lambda_init):
    i = pl.program_id(1)
    HD = DIFF_HEAD_DIM
    VW = 4 * HD
    lp = lam_ref[...]
    lam = (jnp.exp(jnp.sum(lp[0:1] * lp[1:2], keepdims=True))
           - jnp.exp(jnp.sum(lp[2:3] * lp[3:4], keepdims=True)) + lambda_init)

    def run(kv_pairs):
        def scores(h):
            los = [(2 * h + j) * HD for j in range(2)]
            return [[_dot_nt(q_ref[:, lo:lo + HD], k_ref[:, lo:lo + HD]) for k_ref, _ in kv_pairs] for lo in los]

        heads = []
        ss = scores(0)
        for h in range(DIFF_HEADS):
            nxt = scores(h + 1) if h + 1 < DIFF_HEADS else None
            maps = []
            for parts in ss:
                m = functools.reduce(jnp.maximum, [jnp.max(s, axis=-1, keepdims=True) for s in parts])
                acc = None
                for s, (_, v_ref) in zip(parts, kv_pairs):
                    t = _dot(jnp.exp(s - m).astype(BF16), v_ref[:, h * VW:(h + 1) * VW])
                    acc = t if acc is None else acc + t
                maps.append(acc[:, :VW // 2] * (1.0 / acc[:, VW // 2:]))
            o = maps[0] - lam * maps[1]
            heads.append(_rmsnorm(o, sub_ref[...]) * (1.0 - lambda_init))
            ss = nxt
        o_ref[...] = jnp.concatenate(heads, axis=-1).astype(BF16)

    @pl.when(i < nq)
    def _():
        run([(kl_ref, vl_ref), (kx_ref, vx_ref)])

    @pl.when(i >= nq)
    def _():
        run([(kx_ref, vx_ref)])


def _diff_mixer(lay, h, mod, gain, w_in, lam_params, subln_w, lambda_init, with_ctx_out):
    D = h.shape[1]
    B, S, C = lay.B, lay.S, lay.C
    qk = DIFF_HEADS * 2 * DIFF_HEAD_DIM
    tm = lay.tile(ROW_TILE)
    midx, pidx = lay.mod_index(tm), lay.pos_index(tm)
    cos, sa, sb = _rope_tables(S, DIFF_HEAD_DIM, tm, 0)
    tab = pl.BlockSpec((tm, LANES), lambda t: (pidx(t), 0))
    out = jax.ShapeDtypeStruct((lay.n_all, qk), BF16)
    q, k, v = pl.pallas_call(
        _diff_proj_kernel,
        grid=(lay.n_all // tm,),
        in_specs=[pl.BlockSpec((tm, D), lambda t: (t, 0)),
                  pl.BlockSpec((1, N_MOD, D), lambda t: (midx(t), 0, 0)),
                  pl.BlockSpec((1, D), lambda t: (0, 0)),
                  pl.BlockSpec((D, 3 * qk), lambda t: (0, 0)),
                  tab, tab, tab],
        out_specs=[pl.BlockSpec((tm, qk), lambda t: (t, 0))] * 2 + [pl.BlockSpec((tm, 2 * qk), lambda t: (t, 0))],
        out_shape=[out, out, jax.ShapeDtypeStruct((lay.n_all, 2 * qk), BF16)],
        compiler_params=_params("arbitrary"),
        name="diff_proj",
    )(h, mod, gain.reshape(1, D), w_in.astype(BF16), cos, sa, sb)

    tq = _divisor(math.gcd(S, C), QUERY_TILE)
    q_index, nq, nc, lat, ctx = _attn_specs(lay, tq, [qk, 2 * qk])
    row = pl.BlockSpec((tq, qk), lambda b, i: (q_index(b, i), 0))
    return pl.pallas_call(
        functools.partial(_diff_attn_kernel, nq=nq, lambda_init=lambda_init),
        grid=(B, nq + (nc if with_ctx_out else 0)),
        in_specs=[pl.BlockSpec((4, DIFF_HEAD_DIM), lambda b, i: (0, 0)),
                  pl.BlockSpec((1, 2 * DIFF_HEAD_DIM), lambda b, i: (0, 0)),
                  row, *lat, *ctx],
        out_specs=row,
        out_shape=jax.ShapeDtypeStruct((lay.n_all if with_ctx_out else lay.n_lat, qk), BF16),
        compiler_params=_params("arbitrary", "arbitrary"),
        name="diff_attn",
    )(lam_params.astype(F32), subln_w.reshape(1, -1), q, k, v, k, v)


def _hgrn_proj_kernel(h_ref, mod_ref, gain_ref, lb_ref, w_ref, o_ref, *, layer):
    KD = HGRN_HEADS * HGRN_KEY_DIM
    W = 2 * LANES
    a = _pre(h_ref[...], gain_ref[...], mod_ref[0], 3).astype(BF16)
    raw = lb_ref[...]
    e = jnp.exp(raw - jnp.max(raw, axis=0, keepdims=True))
    soft = e * (1.0 / jnp.sum(e, axis=0, keepdims=True))
    lb = jnp.sum(soft[1:layer + 1], axis=0, keepdims=True) if layer else jnp.zeros_like(soft[0:1])
    n = w_ref.shape[1] // W

    def proj(c):
        return _dot(a, w_ref[:, c * W:(c + 1) * W])

    cur = proj(0)
    for c in range(n):
        nxt = proj(c + 1) if c + 1 < n else None
        group, col = divmod(c * W, KD)
        if group in (0, 4):
            out = _silu(cur)
        elif group in (1, 2):
            lbc = lb[:, col:col + W]
            out = lbc + (1.0 - lbc) * jax.nn.sigmoid(cur)
        else:
            out = cur
        o_ref[:, c * W:(c + 1) * W] = out
        cur = nxt


def _running_sum(x, reverse):
    n, w = x.shape
    n_groups = n // SUBLANES
    x = x.reshape(n_groups, SUBLANES, w)
    within = lax.broadcasted_iota(jnp.int32, (1, SUBLANES, 1), 1)
    step = 1
    while step < SUBLANES:
        if reverse:
            x = x + jnp.where(within < SUBLANES - step, pltpu.roll(x, SUBLANES - step, 1), 0.0)
        else:
            x = x + jnp.where(within >= step, pltpu.roll(x, step, 1), 0.0)
        step *= 2
    out = [None] * n_groups
    carry = None
    for j in (reversed(range(n_groups)) if reverse else range(n_groups)):
        out[j] = x[j] if carry is None else x[j] + carry
        carry = out[j][0:1] if reverse else out[j][SUBLANES - 1:SUBLANES]
    return jnp.concatenate(out, axis=0)


def _hgrn_scan_kernel(qf_ref, ff_ref, vf_ref, qb_ref, fb_ref, vb_ref, of_ref, ob_ref, st_ref, *, n_chunks):
    step = pl.program_id(1)
    C = HGRN_CHUNK
    DK, DV, H = HGRN_KEY_DIM, HGRN_VAL_DIM, HGRN_HEADS

    @pl.when(step == 0)
    def _():
        st_ref[...] = jnp.zeros_like(st_ref)

    r = lax.broadcasted_iota(jnp.int32, (C, C), 0)
    c = lax.broadcasted_iota(jnp.int32, (C, C), 1)
    ks = [slice(h * DK, (h + 1) * DK) for h in range(H)]
    vs = [slice(h * DV, (h + 1) * DV) for h in range(H)]
    dirs = [(qf_ref, ff_ref, vf_ref, of_ref, c <= r, C // 2, C - 1),
            (qb_ref, fb_ref, vb_ref, ob_ref, c >= r, C - 1 - C // 2, 0)]

    def prepare(d, n):
        q_ref, f_ref, v_ref, _, tri, mid_row, last_row = dirs[d]
        base = (n if d == 0 else n_chunks - 1 - n) * C
        q = q_ref[base:base + C, :]
        f = f_ref[base:base + C, :]
        vt = v_ref[base:base + C, :].T.astype(BF16)
        k = 1.0 - f
        g = _running_sum(jnp.log2(f), reverse=d == 1)
        g_mid = g[mid_row:mid_row + 1]
        g_last = g[last_row:last_row + 1]
        t = g - g_mid
        qa_f = q * jnp.exp2(t)
        kb_f = k * jnp.exp2(-t)
        qa = qa_f.astype(BF16)
        kb = kb_f.astype(BF16)
        qg = (qa_f * jnp.exp2(g_mid)).astype(BF16)
        kd = (kb_f * jnp.exp2(g_last - g_mid)).astype(BF16)
        a = [jnp.where(tri, _dot_nt(qa[:, ks[h]], kb[:, ks[h]]), 0.0).astype(BF16) for h in range(H)]
        lhs = [jnp.concatenate([qg[:, ks[h]], a[h]], axis=1) for h in range(H)]
        update = [_dot(vt[vs[h], :], kd[:, ks[h]]) for h in range(H)]
        return base, lhs, vt, jnp.exp2(g_last), update

    cur = [prepare(d, 0) for d in range(2)]
    for n in range(n_chunks):
        nxt = [prepare(d, n + 1) for d in range(2)] if n + 1 < n_chunks else None
        for d in range(2):
            base, lhs, vt, decay, update = cur[d]
            st = [st_ref[d, vs[h], :] for h in range(H)]
            outs = [_dot_nt(lhs[h], jnp.concatenate([st[h].astype(BF16), vt[vs[h], :]], axis=1)) for h in range(H)]
            for h in range(H):
                st_ref[d, vs[h], :] = st[h] * decay[:, ks[h]] + update[h]
            dirs[d][3][base:base + C, :] = jnp.concatenate(outs, axis=-1)
        cur = nxt


def _hgrn_readout(of_ref, ob_ref, gate_ref, nw_ref):
    DV = HGRN_VAL_DIM
    o = of_ref[...] + ob_ref[...]
    gate = gate_ref[...]
    nw = nw_ref[...]
    ys = [_rmsnorm(o[:, h * DV:(h + 1) * DV], nw) * gate[:, h * DV:(h + 1) * DV] for h in range(HGRN_HEADS)]
    return jnp.concatenate(ys, axis=-1).astype(BF16)


def _plain_readout(o_ref):
    return o_ref[...]


def _hgrn_mixer(lay, h, mod, gain, w_in, norm_w, lower_bounds, layer):
    D = h.shape[1]
    B, S, C = lay.B, lay.S, lay.C
    KD = HGRN_HEADS * HGRN_KEY_DIM
    tm = lay.tile(ROW_TILE)
    midx = lay.mod_index(tm)
    n_cols = w_in.shape[1] // KD
    proj = pl.pallas_call(
        functools.partial(_hgrn_proj_kernel, layer=layer),
        grid=(lay.n_all // tm,),
        in_specs=[pl.BlockSpec((tm, D), lambda t: (t, 0)),
                  pl.BlockSpec((1, N_MOD, D), lambda t: (midx(t), 0, 0)),
                  pl.BlockSpec((1, D), lambda t: (0, 0)),
                  pl.BlockSpec(lower_bounds.shape, lambda t: (0, 0)),
                  pl.BlockSpec((D, n_cols * KD), lambda t: (0, 0), pipeline_mode=pl.Buffered(1))],
        out_specs=pl.BlockSpec((tm, n_cols * KD), lambda t: (t, 0)),
        out_shape=jax.ShapeDtypeStruct((lay.n_all, n_cols * KD), F32),
        compiler_params=_params("arbitrary"),
        name="hgrn_proj",
    )(h, mod, gain.reshape(1, D), lower_bounds.astype(F32), w_in.astype(BF16))

    ts = C
    n_lat_steps = S // ts
    lat_blocks = lay.n_lat // ts

    def rows(d):
        def index(b, s):
            lat = b * n_lat_steps + (s - 1 if d == 0 else n_lat_steps - s)
            return jnp.where(s == 0, lat_blocks + b, lat)
        return index

    def col(d, j):
        return pl.BlockSpec((ts, KD), lambda b, s: (rows(d)(b, s), j))

    o_f, o_b = pl.pallas_call(
        functools.partial(_hgrn_scan_kernel, n_chunks=ts // HGRN_CHUNK),
        grid=(B, 1 + n_lat_steps),
        in_specs=[col(0, 0), col(0, 1), col(0, 3), col(1, 0), col(1, 2), col(1, 3)],
        out_specs=[col(0, 0), col(1, 0)],
        out_shape=[jax.ShapeDtypeStruct((lay.n_all, KD), F32)] * 2,
        scratch_shapes=[pltpu.VMEM((2, HGRN_HEADS * HGRN_VAL_DIM, HGRN_KEY_DIM), F32)],
        compiler_params=_params("arbitrary", "arbitrary"),
        name="hgrn_scan",
    )(proj, proj, proj, proj, proj, proj)

    return _hgrn_readout, [(o_f, 0), (o_b, 0), (proj, 4), (norm_w.reshape(1, -1), 0)]


MLA_QK_PAD = LANES


def _mla_proj_kernel(h_ref, mod_ref, gain_ref, wd_ref, qn_ref, kvn_ref, wuq_ref, wukv_ref,
                     cos_ref, sa_ref, sb_ref, q_ref, k_ref, v_ref):
    H, P = MLA_HEADS, MLA_QK_PAD
    a = _pre(h_ref[...], gain_ref[...], mod_ref[0], 3).astype(BF16)
    dn = _dot(a, wd_ref[...])
    cos, sa, sb = cos_ref[...], sa_ref[...], sb_ref[...]
    f = MLA_ROPE // 4
    cq = _rmsnorm(dn[:, :MLA_Q_LORA], qn_ref[...]).astype(BF16)
    ckv = _rmsnorm(dn[:, MLA_Q_LORA:MLA_Q_LORA + MLA_KV_LORA], kvn_ref[...]).astype(BF16)
    kr = _rope(dn[:, MLA_Q_LORA + MLA_KV_LORA:MLA_Q_LORA + MLA_KV_LORA + P], cos, sa, sb, f)
    lane = lax.broadcasted_iota(jnp.int32, (1, P), 1)
    ones_half = jnp.where(lane >= MLA_V_DIM, 1.0, 0.0)

    def emit_q(col, tile):
        for j in range(MXU_COLS // P):
            lo = col + j * P
            q_ref[:, lo:lo + P] = _rope(tile[:, j * P:(j + 1) * P], cos, sa, sb, f).astype(BF16)

    def emit_kv(col, tile):
        for j in range(MXU_COLS // P):
            lo = col + j * P
            x = tile[:, j * P:(j + 1) * P]
            if lo < H * P:
                k_ref[:, lo:lo + P] = (x + kr).astype(BF16)
            else:
                v_ref[:, lo - H * P:lo - H * P + P] = (x + ones_half).astype(BF16)

    _chunked_proj(cq, wuq_ref, emit_q)
    _chunked_proj(ckv, wukv_ref, emit_kv)


def _mla_attn_kernel(q_ref, kl_ref, vl_ref, kx_ref, vx_ref, o_ref, *, nq, with_ctx):
    i = pl.program_id(1)
    P, VD = MLA_QK_PAD, MLA_V_DIM
    c = ((MLA_NOPE + MLA_ROPE) ** -0.5) * LOG2_E

    def run(kv_pairs):
        def scores(hd):
            qh = q_ref[:, hd * P:(hd + 1) * P]
            return [_dot_nt(qh, k_ref[:, hd * P:(hd + 1) * P]) for k_ref, _ in kv_pairs]

        heads = []
        ss = scores(0)
        for hd in range(MLA_HEADS):
            nxt = scores(hd + 1) if hd + 1 < MLA_HEADS else None
            m = functools.reduce(jnp.maximum, [jnp.max(s, axis=-1, keepdims=True) for s in ss])
            mc = m * c
            pair = (hd // 2) * MXU_COLS
            acc = None
            for s, (_, v_ref) in zip(ss, kv_pairs):
                t = _dot(jnp.exp2(s * c - mc).astype(BF16), v_ref[:, pair:pair + MXU_COLS])
                acc = t if acc is None else acc + t
            acc = acc[:, (hd % 2) * P:(hd % 2 + 1) * P]
            heads.append(acc[:, :VD] * (1.0 / acc[:, VD:]))
            ss = nxt
        o_ref[...] = jnp.concatenate(heads, axis=-1).astype(BF16)

    if not with_ctx:
        run([(kl_ref, vl_ref), (kx_ref, vx_ref)])
        return

    @pl.when(i < nq)
    def _():
        run([(kl_ref, vl_ref), (kx_ref, vx_ref)])

    @pl.when(i >= nq)
    def _():
        run([(kx_ref, vx_ref)])


def _mla_mixer(lay, h, mod, gain, w_down, q_norm_w, kv_norm_w, w_uq, w_ukv, with_ctx_out):
    D = h.shape[1]
    B, S, C = lay.B, lay.S, lay.C
    H, P, VD = MLA_HEADS, MLA_QK_PAD, MLA_V_DIM
    qk = MLA_NOPE + MLA_ROPE
    lora = MLA_Q_LORA + MLA_KV_LORA
    kr_cols = jnp.zeros((D, MXU_COLS), F32).at[:, MLA_NOPE:qk].set(w_down[:, lora:])
    wd = jnp.concatenate([w_down[:, :lora], kr_cols], axis=1).astype(BF16)
    wuq = jnp.pad(w_uq.reshape(MLA_Q_LORA, H, qk), ((0, 0), (0, 0), (0, P - qk))).reshape(MLA_Q_LORA, H * P)
    ukv = w_ukv.reshape(MLA_KV_LORA, H, MLA_NOPE + VD)
    wkn = jnp.pad(ukv[..., :MLA_NOPE], ((0, 0), (0, 0), (0, P - MLA_NOPE))).reshape(MLA_KV_LORA, H * P)
    wv = jnp.pad(ukv[..., MLA_NOPE:], ((0, 0), (0, 0), (0, P - VD))).reshape(MLA_KV_LORA, H * P)
    wukv = jnp.concatenate([wkn, wv], axis=1)

    tm = lay.tile(ROW_TILE)
    midx, pidx = lay.mod_index(tm), lay.pos_index(tm)
    cos, sa, sb = _rope_tables(S, MLA_ROPE, tm, MLA_NOPE)
    tab = pl.BlockSpec((tm, LANES), lambda t: (pidx(t), 0))
    full = lambda shape: pl.BlockSpec(shape, lambda t: (0, 0))
    q, k, v = pl.pallas_call(
        _mla_proj_kernel,
        grid=(lay.n_all // tm,),
        in_specs=[pl.BlockSpec((tm, D), lambda t: (t, 0)),
                  pl.BlockSpec((1, N_MOD, D), lambda t: (midx(t), 0, 0)),
                  full((1, D)), full((D, lora + MXU_COLS)), full((1, MLA_Q_LORA)), full((1, MLA_KV_LORA)),
                  full((MLA_Q_LORA, H * P)), full((MLA_KV_LORA, 2 * H * P)),
                  tab, tab, tab],
        out_specs=[pl.BlockSpec((tm, H * P), lambda t: (t, 0))] * 3,
        out_shape=[jax.ShapeDtypeStruct((lay.n_all, H * P), BF16)] * 3,
        compiler_params=_params("arbitrary"),
        name="mla_proj",
    )(h, mod, gain.reshape(1, D), wd, q_norm_w.reshape(1, -1), kv_norm_w.reshape(1, -1),
      wuq.astype(BF16), wukv.astype(BF16), cos, sa, sb)

    tq = _divisor(math.gcd(S, C), QUERY_TILE)
    q_index, nq, nc, lat, ctx = _attn_specs(lay, tq, [H * P, H * P])
    return pl.pallas_call(
        functools.partial(_mla_attn_kernel, nq=nq, with_ctx=with_ctx_out),
        grid=(B, nq + (nc if with_ctx_out else 0)),
        in_specs=[pl.BlockSpec((tq, H * P), lambda b, i: (q_index(b, i), 0)), *lat, *ctx],
        out_specs=pl.BlockSpec((tq, H * VD), lambda b, i: (q_index(b, i), 0)),
        out_shape=jax.ShapeDtypeStruct((lay.n_all if with_ctx_out else lay.n_lat, H * VD), BF16),
        compiler_params=_params("arbitrary", "arbitrary"),
        name="mla_attn",
    )(q, k, v, k, v)


def kernel(x, c, ctx, c_ctx, ada_w, ada_b, norm_w, final_norm_w, ffn_w_gate, ffn_w_up, ffn_w_down,
           gqa_w_in, gqa_w_out, gqa_sinks, diff_w_in, diff_w_out, diff_lambda, diff_subln_w,
           hgrn_w_in, hgrn_w_out, hgrn_norm_w, hgrn_lower_bounds,
           mla_w_down, mla_q_norm_w, mla_kv_norm_w, mla_w_uq, mla_w_ukv, mla_w_out):
    B, S, D = x.shape
    C = ctx.shape[1]
    lay = _Layout(B, S, C)
    depth = ada_w.shape[0]

    rows = -(-(B + 1) // 8) * 8
    cc = jnp.zeros((rows, D), F32).at[:B].set(c).at[B].set(c_ctx)
    mods = _ada_table(cc, ada_w, ada_b).reshape(depth, rows, N_MOD, D)

    stacks = (ffn_w_gate, ffn_w_up, ffn_w_down)
    weights = tuple(w[0, 0].astype(BF16) for w in stacks)
    h = (x.reshape(B * S, D), ctx.reshape(B * C, D))
    for i in range(depth):
        kind, j = i % 4, i // 4
        last = i == depth - 1
        mod = mods[i]
        h, weights = _ffn(lay, h, mod, norm_w[i, 0], final_norm_w, weights, 0, cast_next=(*stacks, i, 1))
        if kind == 0:
            o = _gqa_mixer(lay, h, mod, norm_w[i, 1], gqa_w_in[j], gqa_sinks[j], not last)
            mix = (_plain_readout, [(o, 0)], gqa_w_out[j])
        elif kind == 1:
            lambda_init = 0.8 - 0.6 * math.exp(-0.3 * i)
            o = _diff_mixer(lay, h, mod, norm_w[i, 1], diff_w_in[j], diff_lambda[j], diff_subln_w[j],
                            lambda_init, not last)
            mix = (_plain_readout, [(o, 0)], diff_w_out[j])
        elif kind == 2:
            readout, operands = _hgrn_mixer(lay, h, mod, norm_w[i, 1], hgrn_w_in[j], hgrn_norm_w[j],
                                            hgrn_lower_bounds, i)
            mix = (readout, operands, hgrn_w_out[j])
        else:
            o = _mla_mixer(lay, h, mod, norm_w[i, 1], mla_w_down[j], mla_q_norm_w[j], mla_kv_norm_w[j],
                           mla_w_uq[j], mla_w_ukv[j], not last)
            mix = (_plain_readout, [(o, 0)], mla_w_out[j])
        h, weights = _ffn(lay, h, mod, norm_w[i, 2], final_norm_w, weights, 1, lat_only=last, final_norm=last,
                          mix=mix, cast_next=None if last else (*stacks, i + 1, 0))
    return h[:B * S].reshape(B, S, D)
```

```python
import functools
import math

import jax
import jax.numpy as jnp
from jax import lax
from jax.experimental import pallas as pl
from jax.experimental.pallas import tpu as pltpu

F32 = jnp.float32
BF16 = jnp.bfloat16

D_MODEL = 1024
GRID_W = 64
N_MOD = 9
ROPE_BASE = 10000.0
EPS = 1e-6
NEG_INF = -1e30
D_FF = 2816

GQA_HEADS = 16
GQA_KV_HEADS = 4
GQA_HEAD_DIM = 64
GQA_WINDOW = 128
GQA_BLOCK = 128

DIFF_HEADS = 8
DIFF_HEAD_DIM = 64

HGRN_HEADS = 8
HGRN_KEY_DIM = 128
HGRN_VAL_DIM = D_MODEL // HGRN_HEADS
HGRN_CHUNK = 64

MLA_HEADS = 16
MLA_Q_LORA = 256
MLA_KV_LORA = 256
MLA_NOPE = 64
MLA_ROPE = 32
MLA_V_DIM = 64

LOG2_E = 1.4426950408889634
LANES = 128
SUBLANES = 8
MXU_COLS = 2 * LANES
VMEM_LIMIT = 56 * 1024 * 1024
ROW_TILE = 512
QUERY_TILE = 256


def _dot(a, b):
    return jnp.dot(a, b, preferred_element_type=F32)


def _dot_nt(a, b):
    return lax.dot_general(a, b, (((1,), (1,)), ((), ())), preferred_element_type=F32)


def _params(*sem):
    return pltpu.CompilerParams(dimension_semantics=sem, vmem_limit_bytes=VMEM_LIMIT)


def _divisor(n, pref):
    t = min(n, pref)
    while n % t:
        t -= 8
    return t


class _Layout:
    def __init__(self, B, S, C):
        self.B, self.S, self.C = B, S, C
        self.n_lat = B * S
        self.n_all = B * S + B * C

    def tile(self, pref):
        return _divisor(math.gcd(self.S, self.B * self.C), pref)

    def mod_index(self, tm):
        n_lat_tiles, per_batch, B = self.n_lat // tm, self.S // tm, self.B
        return lambda t: jnp.where(t < n_lat_tiles, t // per_batch, B)

    def pos_index(self, tm):
        n_lat_tiles, per_batch = self.n_lat // tm, self.S // tm
        return lambda t: jnp.where(t < n_lat_tiles, t % per_batch, per_batch)


def _rmsnorm(x, w):
    return (x * lax.rsqrt(jnp.mean(x * x, axis=-1, keepdims=True) + EPS)) * w


def _pre(h, gain, m, off):
    inv = lax.rsqrt(jnp.mean(h * h, axis=-1, keepdims=True) + EPS)
    return (h * inv) * (gain * (1.0 + m[off + 1:off + 2])) + m[off:off + 1]


def _silu(x):
    return x * jax.nn.sigmoid(x)


def _chunked_proj(a, w_ref, emit):
    n = w_ref.shape[1] // MXU_COLS
    cur = _dot(a, w_ref[:, :MXU_COLS])
    for c in range(n):
        nxt = _dot(a, w_ref[:, (c + 1) * MXU_COLS:(c + 2) * MXU_COLS]) if c + 1 < n else None
        emit(c * MXU_COLS, cur)
        cur = nxt


def _rope(x, cos, sa, sb, shift):
    return x * cos + pltpu.roll(x, LANES - shift, 1) * sa + pltpu.roll(x, shift, 1) * sb


def _ada_kernel(c_ref, w_ref, b_ref, o_ref):
    sc = _silu(c_ref[...]).astype(BF16)
    o_ref[0] = _dot(sc, w_ref[0].astype(BF16)) + b_ref[0]


ADA_COLS = N_MOD * LANES


def _ada_table(cc, ada_w, ada_b):
    rows, D = cc.shape
    depth, _, n_out = ada_w.shape
    tn = ADA_COLS
    return pl.pallas_call(
        _ada_kernel,
        grid=(depth, n_out // tn),
        in_specs=[pl.BlockSpec((rows, D), lambda i, j: (0, 0)),
                  pl.BlockSpec((1, D, tn), lambda i, j: (i, 0, j)),
                  pl.BlockSpec((1, 1, tn), lambda i, j: (i, 0, j))],
        out_specs=pl.BlockSpec((1, rows, tn), lambda i, j: (i, 0, j)),
        out_shape=jax.ShapeDtypeStruct((depth, rows, n_out), F32),
        compiler_params=_params("arbitrary", "arbitrary"),
        name="ada_table",
    )(cc, ada_w, ada_b.reshape(depth, 1, n_out))


FFN_CHUNK = MXU_COLS


def _ffn_kernel(*refs, off, final_norm, n_lat_tiles, readout, n_mix, cast_next):
    refs = list(refs)
    if n_lat_tiles is None:
        h = refs.pop(0)[...]
    else:
        hx_ref, hc_ref = refs.pop(0), refs.pop(0)
        h = jnp.where(pl.program_id(0) < n_lat_tiles, hx_ref[...], hc_ref[...])
    mix_refs = [refs.pop(0) for _ in range(n_mix)]
    if readout is not None:
        wo_ref = refs.pop(0)
    mod_ref, gain_ref, fin_ref, wg_ref, wu_ref, wd_ref = refs[:6]
    if cast_next:
        for src, dst in zip(refs[6:9], refs[10:13]):
            dst[...] = src[...].astype(BF16)
    o_ref, a_ref = refs[9 if cast_next else 6], refs[-1]
    m = mod_ref[0]
    if readout is not None:
        h = h + m[5:6] * _dot(readout(*mix_refs), wo_ref[...])
    xm = _pre(h, gain_ref[...], m, off).astype(BF16)
    n = D_FF // FFN_CHUNK

    def gate_up(c):
        cols = slice(c * FFN_CHUNK, (c + 1) * FFN_CHUNK)
        return _dot(xm, wg_ref[:, cols]), _dot(xm, wu_ref[:, cols])

    cur = gate_up(0)
    for c in range(n):
        nxt = gate_up(c + 1) if c + 1 < n else None
        g, u = cur
        a_ref[:, c * FFN_CHUNK:(c + 1) * FFN_CHUNK] = (_silu(g) * u).astype(BF16)
        cur = nxt
    out = h + (0.5 * m[off + 2:off + 3]) * _dot(a_ref[...], wd_ref[...])
    if final_norm:
        out = _rmsnorm(out, fin_ref[...])
    o_ref[...] = out


def _slabs(rows, n_steps):
    n = n_steps
    while rows % n or (rows // n) % (2 * SUBLANES):
        n -= 1
    return n


def _ffn(lay, h, mod, gain, fin_w, weights, half, lat_only=False, final_norm=False, mix=None, cast_next=None):
    split = isinstance(h, tuple)
    wg, wu, wd = weights
    D = wg.shape[0]
    tm = lay.tile(ROW_TILE)
    n_rows = lay.n_lat if lat_only else lay.n_all
    n_steps = n_rows // tm
    midx = lay.mod_index(tm)
    n_lat_tiles = lay.n_lat // tm
    const = lambda t: (0, 0)
    one = pl.Buffered(1)
    weight = lambda shape: pl.BlockSpec(shape, const, pipeline_mode=one)
    if split:
        h_specs = [pl.BlockSpec((tm, D), lambda t: (jnp.minimum(t, n_lat_tiles - 1), 0)),
                   pl.BlockSpec((tm, D), lambda t: (jnp.maximum(t - n_lat_tiles, 0), 0))]
        h_args = list(h)
    else:
        h_specs = [pl.BlockSpec((tm, D), lambda t: (t, 0))]
        h_args = [h]
    readout, operands = None, []
    if mix is not None:
        readout, operands, w_out = mix
        for arr, col in operands:
            if arr.shape[0] == 1:
                h_specs.append(pl.BlockSpec(arr.shape, const))
            else:
                h_specs.append(pl.BlockSpec((tm, D), lambda t, col=col: (t, col)))
            h_args.append(arr)
        h_specs.append(pl.BlockSpec(w_out.shape, const, pipeline_mode=one))
        h_args.append(w_out.astype(BF16))
    cast_specs, cast_args, cast_out_specs, cast_out_shapes = [], [], [], []
    if cast_next is not None:
        *stacks, nl, nh = cast_next
        for w32 in stacks:
            rows, cols = w32.shape[2:]
            n = _slabs(rows, n_steps)
            slab = rows // n
            cast_specs.append(pl.BlockSpec((None, None, slab, cols),
                                           lambda t, n=n: (nl, nh, jnp.minimum(t, n - 1), 0)))
            cast_args.append(w32)
            cast_out_specs.append(pl.BlockSpec((slab, cols), lambda t, n=n: (jnp.minimum(t, n - 1), 0)))
            cast_out_shapes.append(jax.ShapeDtypeStruct((rows, cols), BF16))
    outs = pl.pallas_call(
        functools.partial(_ffn_kernel, off=6 * half, final_norm=final_norm,
                          n_lat_tiles=n_lat_tiles if split else None, readout=readout, n_mix=len(operands),
                          cast_next=cast_next is not None),
        grid=(n_steps,),
        in_specs=h_specs + [pl.BlockSpec((1, N_MOD, D), lambda t: (midx(t), 0, 0)),
                            pl.BlockSpec((1, D), const),
                            pl.BlockSpec((1, D), const),
                            weight((D, D_FF)), weight((D, D_FF)), weight((D_FF, D))] + cast_specs,
        out_specs=[pl.BlockSpec((tm, D), lambda t: (t, 0))] + cast_out_specs,
        out_shape=[jax.ShapeDtypeStruct((n_rows, D), F32)] + cast_out_shapes,
        scratch_shapes=[pltpu.VMEM((tm, D_FF), BF16)],
        compiler_params=_params("arbitrary"),
        name="ffn",
    )(*h_args, mod, gain.reshape(1, D), fin_w.reshape(1, D), wg, wu, wd, *cast_args)
    return outs[0], (tuple(outs[1:]) if cast_next is not None else None)


def _rope_tables(S, rot_dim, pad_rows, lane_off):
    rows = S // GRID_W
    row = jnp.repeat(jnp.arange(rows, dtype=F32), GRID_W)
    col = jnp.tile(jnp.arange(GRID_W, dtype=F32), rows)
    axis_dim = rot_dim // 2
    inv_freq = ROPE_BASE ** (-jnp.arange(0, axis_dim, 2, dtype=F32) / axis_dim)
    ang_r = row[:, None] * inv_freq[None, :]
    ang_c = col[:, None] * inv_freq[None, :]
    ang = jnp.concatenate([ang_r, ang_r, ang_c, ang_c], axis=-1)
    cos, sin = jnp.cos(ang), jnp.sin(ang)
    f = rot_dim // 4
    first = (jnp.arange(rot_dim) % (2 * f)) < f
    sa = jnp.where(first[None, :], -sin, 0.0)
    sb = jnp.where(first[None, :], 0.0, sin)
    period = 64 if lane_off + rot_dim <= 64 else LANES

    def widen(t, fill):
        blk = jnp.full((S, period), fill, F32).at[:, lane_off:lane_off + rot_dim].set(t)
        blk = jnp.tile(blk, (1, LANES // period))
        return jnp.concatenate([blk, jnp.full((pad_rows, LANES), fill, F32)], axis=0)

    return widen(cos, 1.0), widen(sa, 0.0), widen(sb, 0.0)


def _attn_specs(lay, tq, widths):
    S, C, B = lay.S, lay.C, lay.B
    nq, nc = S // tq, C // tq
    lat_blocks = lay.n_lat // C

    def q_index(b, i):
        return jnp.where(i < nq, b * nq + i, B * nq + b * nc + (i - nq))

    lat = [pl.BlockSpec((S, w), lambda b, i: (b, 0)) for w in widths]
    ctx = [pl.BlockSpec((C, w), lambda b, i: (lat_blocks + b, 0)) for w in widths]
    return q_index, nq, nc, lat, ctx


def _gqa_proj_kernel(h_ref, mod_ref, gain_ref, w_ref, cos_ref, sa_ref, sb_ref, q_ref, k_ref, v_ref):
    qd, kd = GQA_HEADS * GQA_HEAD_DIM, GQA_KV_HEADS * GQA_HEAD_DIM
    a = _pre(h_ref[...], gain_ref[...], mod_ref[0], 3).astype(BF16)
    cos, sa, sb = cos_ref[...], sa_ref[...], sb_ref[...]
    f = GQA_HEAD_DIM // 4
    scale = GQA_HEAD_DIM ** -0.5 * LOG2_E
    lane = lax.broadcasted_iota(jnp.int32, (1, MXU_COLS), 1)
    ones_half = jnp.where((lane & (2 * GQA_HEAD_DIM - 1)) >= GQA_HEAD_DIM, 1.0, 0.0)

    def emit(col, tile):
        if col >= qd + kd:
            v_ref[:, col - qd - kd:col - qd - kd + MXU_COLS] = (tile + ones_half).astype(BF16)
            return
        for j in range(MXU_COLS // LANES):
            x = _rope(tile[:, j * LANES:(j + 1) * LANES], cos, sa, sb, f)
            lo = col + j * LANES
            if lo < qd:
                q_ref[:, lo:lo + LANES] = (x * scale).astype(BF16)
            else:
                k_ref[:, lo - qd:lo - qd + LANES] = x.astype(BF16)

    _chunked_proj(a, w_ref, emit)


def _gqa_attn_kernel(sink_ref, q_ref, kp_ref, kc_ref, kn_ref, vp_ref, vc_ref, vn_ref, kx_ref, vx_ref,
                     o_ref, kcat_ref, vcat_ref, *, nq):
    i = pl.program_id(1)
    blk = GQA_BLOCK
    G = GQA_HEADS // GQA_KV_HEADS
    HD = GQA_HEAD_DIM
    VW = 2 * HD
    span = 3 * blk

    def run(k_ref, v_ref, masks):
        kvs = range(GQA_KV_HEADS)
        qs = [jnp.concatenate([q_ref[:, (G * g + n) * HD:(G * g + n + 1) * HD] for n in range(G)], axis=0)
              for g in kvs]
        sinks = [jnp.concatenate([jnp.full((blk, 1), sink_ref[G * g + n] * LOG2_E, F32) for n in range(G)], axis=0)
                 for g in kvs]
        ss = [_dot_nt(qs[g], k_ref[:, g * HD:(g + 1) * HD]) for g in kvs]
        if masks is not None:
            prev_ok, next_ok = masks
            ss = [jnp.concatenate([jnp.where(prev_ok, s[:, :blk], NEG_INF), s[:, blk:2 * blk],
                                   jnp.where(next_ok, s[:, 2 * blk:span], NEG_INF), s[:, span:]], axis=1)
                  for s in ss]
        ms = [jnp.maximum(jnp.max(ss[g], axis=-1, keepdims=True), sinks[g]) for g in kvs]
        accs = [_dot(jnp.exp2(ss[g] - ms[g]).astype(BF16),
                     v_ref[:, (g // 2) * MXU_COLS:(g // 2 + 1) * MXU_COLS])[:, (g % 2) * VW:(g % 2 + 1) * VW]
                for g in kvs]
        outs = [accs[g][:, :HD] * (1.0 / (accs[g][:, HD:] + jnp.exp2(sinks[g] - ms[g]))) for g in kvs]
        heads = [outs[g][n * blk:(n + 1) * blk] for g in kvs for n in range(G)]
        o_ref[...] = jnp.concatenate(heads, axis=-1).astype(BF16)

    @pl.when(i < nq)
    def _():
        for n, (k_ref, v_ref) in enumerate([(kp_ref, vp_ref), (kc_ref, vc_ref), (kn_ref, vn_ref)]):
            kcat_ref[n * blk:(n + 1) * blk] = k_ref[...]
            vcat_ref[n * blk:(n + 1) * blk] = v_ref[...]
        kcat_ref[span:] = kx_ref[...]
        vcat_ref[span:] = vx_ref[...]
        row = lax.broadcasted_iota(jnp.int32, (G * blk, blk), 0) & (blk - 1)
        col = lax.broadcasted_iota(jnp.int32, (G * blk, blk), 1)
        prev_ok = (row + blk - col <= GQA_WINDOW) & (i >= 1)
        next_ok = (col + blk - row <= GQA_WINDOW) & (i + 1 < nq)
        run(kcat_ref, vcat_ref, (prev_ok, next_ok))

    @pl.when(i >= nq)
    def _():
        run(kx_ref, vx_ref, None)


def _gqa_mixer(lay, h, mod, gain, w_in, sinks, with_ctx_out):
    D = h.shape[1]
    B, S, C = lay.B, lay.S, lay.C
    qd, kd = GQA_HEADS * GQA_HEAD_DIM, GQA_KV_HEADS * GQA_HEAD_DIM
    w_v = jnp.pad(w_in[:, qd + kd:].reshape(D, GQA_KV_HEADS, GQA_HEAD_DIM), ((0, 0), (0, 0), (0, GQA_HEAD_DIM)))
    w_pad = jnp.concatenate([w_in[:, :qd + kd], w_v.reshape(D, 2 * kd)], axis=1)
    tm = lay.tile(ROW_TILE)
    midx, pidx = lay.mod_index(tm), lay.pos_index(tm)
    cos, sa, sb = _rope_tables(S, GQA_HEAD_DIM, tm, 0)
    tab = pl.BlockSpec((tm, LANES), lambda t: (pidx(t), 0))
    q, k, v = pl.pallas_call(
        _gqa_proj_kernel,
        grid=(lay.n_all // tm,),
        in_specs=[pl.BlockSpec((tm, D), lambda t: (t, 0)),
                  pl.BlockSpec((1, N_MOD, D), lambda t: (midx(t), 0, 0)),
                  pl.BlockSpec((1, D), lambda t: (0, 0)),
                  pl.BlockSpec((D, qd + 3 * kd), lambda t: (0, 0)),
                  tab, tab, tab],
        out_specs=[pl.BlockSpec((tm, qd), lambda t: (t, 0)),
                   pl.BlockSpec((tm, kd), lambda t: (t, 0)),
                   pl.BlockSpec((tm, 2 * kd), lambda t: (t, 0))],
        out_shape=[jax.ShapeDtypeStruct((lay.n_all, qd), BF16),
                   jax.ShapeDtypeStruct((lay.n_all, kd), BF16),
                   jax.ShapeDtypeStruct((lay.n_all, 2 * kd), BF16)],
        compiler_params=_params("arbitrary"),
        name="gqa_proj",
    )(h, mod, gain.reshape(1, D), w_pad.astype(BF16), cos, sa, sb)

    blk = GQA_BLOCK
    assert S % blk == 0 and C % blk == 0
    nq, nc = S // blk, C // blk
    n_i = nq + (nc if with_ctx_out else 0)
    lat_blocks = lay.n_lat // C

    def q_index(b, i):
        return jnp.where(i < nq, b * nq + i, B * nq + b * nc + (i - nq))

    def win(delta):
        return lambda b, i: (b * nq + jnp.clip(i + delta, 0, nq - 1), 0)

    def windows(width):
        return [pl.BlockSpec((blk, width), win(d)) for d in (-1, 0, 1)]

    def ctx_spec(width):
        return pl.BlockSpec((C, width), lambda b, i: (lat_blocks + b, 0))

    row = pl.BlockSpec((blk, qd), lambda b, i: (q_index(b, i), 0))
    return pl.pallas_call(
        functools.partial(_gqa_attn_kernel, nq=nq),
        grid=(B, n_i),
        in_specs=[pl.BlockSpec(memory_space=pltpu.SMEM), row,
                  *windows(kd), *windows(2 * kd), ctx_spec(kd), ctx_spec(2 * kd)],
        out_specs=row,
        out_shape=jax.ShapeDtypeStruct((lay.n_all if with_ctx_out else lay.n_lat, qd), BF16),
        scratch_shapes=[pltpu.VMEM((3 * blk + C, kd), BF16), pltpu.VMEM((3 * blk + C, 2 * kd), BF16)],
        compiler_params=_params("arbitrary", "arbitrary"),
        name="gqa_attn",
    )(sinks.astype(F32), q, k, k, k, v, v, v, k, v)


def _diff_proj_kernel(h_ref, mod_ref, gain_ref, w_ref, cos_ref, sa_ref, sb_ref, q_ref, k_ref, v_ref):
    qk = DIFF_HEADS * 2 * DIFF_HEAD_DIM
    a = _pre(h_ref[...], gain_ref[...], mod_ref[0], 3).astype(BF16)
    cos, sa, sb = cos_ref[...], sa_ref[...], sb_ref[...]
    f = DIFF_HEAD_DIM // 4
    scale = DIFF_HEAD_DIM ** -0.5 * LOG2_E
    vw = 2 * DIFF_HEAD_DIM
    ones = jnp.ones((a.shape[0], vw), BF16)

    def emit(col, tile):
        for j in range(MXU_COLS // LANES):
            lo = col + j * LANES
            x = tile[:, j * LANES:(j + 1) * LANES]
            if lo < qk:
                q_ref[:, lo:lo + LANES] = (_rope(x, cos, sa, sb, f) * scale).astype(BF16)
            elif lo < 2 * qk:
                k_ref[:, lo - qk:lo - qk + LANES] = _rope(x, cos, sa, sb, f).astype(BF16)
            else:
                hd = (lo - 2 * qk) // vw
                v_ref[:, 2 * hd * vw:(2 * hd + 1) * vw] = x.astype(BF16)
                v_ref[:, (2 * hd + 1) * vw:(2 * hd + 2) * vw] = ones

    _chunked_proj(a, w_ref, emit)


def _diff_attn_kernel(lam_ref, sub_ref, q_ref, kl_ref, vl_ref, kx_ref, vx_ref, o_ref, *, nq, lambda_init):
    i = pl.program_id(1)
    HD = DIFF_HEAD_DIM
    VW = 4 * HD
    lp = lam_ref[...]
    lam = (jnp.exp(jnp.sum(lp[0:1] * lp[1:2], keepdims=True))
           - jnp.exp(jnp.sum(lp[2:3] * lp[3:4], keepdims=True)) + lambda_init)

    def run(kv_pairs):
        def scores(h):
            los = [(2 * h + j) * HD for j in range(2)]
            return [[_dot_nt(q_ref[:, lo:lo + HD], k_ref[:, lo:lo + HD]) for k_ref, _ in kv_pairs] for lo in los]

        heads = []
        ss = scores(0)
        for h in range(DIFF_HEADS):
            nxt = scores(h + 1) if h + 1 < DIFF_HEADS else None
            maps = []
            for parts in ss:
                m = functools.reduce(jnp.maximum, [jnp.max(s, axis=-1, keepdims=True) for s in parts])
                acc = None
                for s, (_, v_ref) in zip(parts, kv_pairs):
                    t = _dot(jnp.exp2(s - m).astype(BF16), v_ref[:, h * VW:(h + 1) * VW])
                    acc = t if acc is None else acc + t
                maps.append(acc[:, :VW // 2] * (1.0 / acc[:, VW // 2:]))
            o = maps[0] - lam * maps[1]
            heads.append(_rmsnorm(o, sub_ref[...]) * (1.0 - lambda_init))
            ss = nxt
        o_ref[...] = jnp.concatenate(heads, axis=-1).astype(BF16)

    @pl.when(i < nq)
    def _():
        run([(kl_ref, vl_ref), (kx_ref, vx_ref)])

    @pl.when(i >= nq)
    def _():
        run([(kx_ref, vx_ref)])


def _diff_mixer(lay, h, mod, gain, w_in, lam_params, subln_w, lambda_init, with_ctx_out):
    D = h.shape[1]
    B, S, C = lay.B, lay.S, lay.C
    qk = DIFF_HEADS * 2 * DIFF_HEAD_DIM
    tm = lay.tile(ROW_TILE)
    midx, pidx = lay.mod_index(tm), lay.pos_index(tm)
    cos, sa, sb = _rope_tables(S, DIFF_HEAD_DIM, tm, 0)
    tab = pl.BlockSpec((tm, LANES), lambda t: (pidx(t), 0))
    out = jax.ShapeDtypeStruct((lay.n_all, qk), BF16)
    q, k, v = pl.pallas_call(
        _diff_proj_kernel,
        grid=(lay.n_all // tm,),
        in_specs=[pl.BlockSpec((tm, D), lambda t: (t, 0)),
                  pl.BlockSpec((1, N_MOD, D), lambda t: (midx(t), 0, 0)),
                  pl.BlockSpec((1, D), lambda t: (0, 0)),
                  pl.BlockSpec((D, 3 * qk), lambda t: (0, 0)),
                  tab, tab, tab],
        out_specs=[pl.BlockSpec((tm, qk), lambda t: (t, 0))] * 2 + [pl.BlockSpec((tm, 2 * qk), lambda t: (t, 0))],
        out_shape=[out, out, jax.ShapeDtypeStruct((lay.n_all, 2 * qk), BF16)],
        compiler_params=_params("arbitrary"),
        name="diff_proj",
    )(h, mod, gain.reshape(1, D), w_in.astype(BF16), cos, sa, sb)

    tq = _divisor(math.gcd(S, C), QUERY_TILE)
    q_index, nq, nc, lat, ctx = _attn_specs(lay, tq, [qk, 2 * qk])
    row = pl.BlockSpec((tq, qk), lambda b, i: (q_index(b, i), 0))
    return pl.pallas_call(
        functools.partial(_diff_attn_kernel, nq=nq, lambda_init=lambda_init),
        grid=(B, nq + (nc if with_ctx_out else 0)),
        in_specs=[pl.BlockSpec((4, DIFF_HEAD_DIM), lambda b, i: (0, 0)),
                  pl.BlockSpec((1, 2 * DIFF_HEAD_DIM), lambda b, i: (0, 0)),
                  row, *lat, *ctx],
        out_specs=row,
        out_shape=jax.ShapeDtypeStruct((lay.n_all if with_ctx_out else lay.n_lat, qk), BF16),
        compiler_params=_params("arbitrary", "arbitrary"),
        name="diff_attn",
    )(lam_params.astype(F32), subln_w.reshape(1, -1), q, k, v, k, v)


def _hgrn_proj_kernel(h_ref, mod_ref, gain_ref, lb_ref, w_ref, o_ref, *, layer):
    KD = HGRN_HEADS * HGRN_KEY_DIM
    W = 2 * LANES
    a = _pre(h_ref[...], gain_ref[...], mod_ref[0], 3).astype(BF16)
    raw = lb_ref[...]
    e = jnp.exp(raw - jnp.max(raw, axis=0, keepdims=True))
    soft = e * (1.0 / jnp.sum(e, axis=0, keepdims=True))
    lb = jnp.sum(soft[1:layer + 1], axis=0, keepdims=True) if layer else jnp.zeros_like(soft[0:1])
    n = w_ref.shape[1] // W

    def proj(c):
        return _dot(a, w_ref[:, c * W:(c + 1) * W])

    cur = proj(0)
    for c in range(n):
        nxt = proj(c + 1) if c + 1 < n else None
        group, col = divmod(c * W, KD)
        if group in (0, 4):
            out = _silu(cur)
        elif group in (1, 2):
            lbc = lb[:, col:col + W]
            out = lbc + (1.0 - lbc) * jax.nn.sigmoid(cur)
        else:
            out = cur
        o_ref[:, c * W:(c + 1) * W] = out
        cur = nxt


def _running_sum(x, reverse):
    n, w = x.shape
    n_groups = n // SUBLANES
    x = x.reshape(n_groups, SUBLANES, w)
    within = lax.broadcasted_iota(jnp.int32, (1, SUBLANES, 1), 1)
    step = 1
    while step < SUBLANES:
        if reverse:
            x = x + jnp.where(within < SUBLANES - step, pltpu.roll(x, SUBLANES - step, 1), 0.0)
        else:
            x = x + jnp.where(within >= step, pltpu.roll(x, step, 1), 0.0)
        step *= 2
    out = [None] * n_groups
    carry = None
    for j in (reversed(range(n_groups)) if reverse else range(n_groups)):
        out[j] = x[j] if carry is None else x[j] + carry
        carry = out[j][0:1] if reverse else out[j][SUBLANES - 1:SUBLANES]
    return jnp.concatenate(out, axis=0)


def _hgrn_scan_kernel(qf_ref, ff_ref, vf_ref, qb_ref, fb_ref, vb_ref, of_ref, ob_ref, st_ref, *, n_chunks):
    step = pl.program_id(1)
    C = HGRN_CHUNK
    DK, DV, H = HGRN_KEY_DIM, HGRN_VAL_DIM, HGRN_HEADS

    @pl.when(step == 0)
    def _():
        st_ref[...] = jnp.zeros_like(st_ref)

    r = lax.broadcasted_iota(jnp.int32, (C, C), 0)
    c = lax.broadcasted_iota(jnp.int32, (C, C), 1)
    ks = [slice(h * DK, (h + 1) * DK) for h in range(H)]
    vs = [slice(h * DV, (h + 1) * DV) for h in range(H)]
    dirs = [(qf_ref, ff_ref, vf_ref, of_ref, c <= r, C // 2, C - 1),
            (qb_ref, fb_ref, vb_ref, ob_ref, c >= r, C - 1 - C // 2, 0)]

    def prepare(d, n):
        q_ref, f_ref, v_ref, _, tri, mid_row, last_row = dirs[d]
        base = (n if d == 0 else n_chunks - 1 - n) * C
        q = q_ref[base:base + C, :]
        f = f_ref[base:base + C, :]
        vt = v_ref[base:base + C, :].T.astype(BF16)
        k = 1.0 - f
        g = _running_sum(jnp.log2(f), reverse=d == 1)
        g_mid = g[mid_row:mid_row + 1]
        g_last = g[last_row:last_row + 1]
        t = g - g_mid
        qa_f = q * jnp.exp2(t)
        kb_f = k * jnp.exp2(-t)
        qa = qa_f.astype(BF16)
        kb = kb_f.astype(BF16)
        qg = (qa_f * jnp.exp2(g_mid)).astype(BF16)
        kd = (kb_f * jnp.exp2(g_last - g_mid)).astype(BF16)
        a = [jnp.where(tri, _dot_nt(qa[:, ks[h]], kb[:, ks[h]]), 0.0).astype(BF16) for h in range(H)]
        lhs = [jnp.concatenate([qg[:, ks[h]], a[h]], axis=1) for h in range(H)]
        update = [_dot(vt[vs[h], :], kd[:, ks[h]]) for h in range(H)]
        return base, lhs, vt, jnp.exp2(g_last), update

    cur = [prepare(d, 0) for d in range(2)]
    for n in range(n_chunks):
        nxt = [prepare(d, n + 1) for d in range(2)] if n + 1 < n_chunks else None
        for d in range(2):
            base, lhs, vt, decay, update = cur[d]
            st = [st_ref[d, vs[h], :] for h in range(H)]
            outs = [_dot_nt(lhs[h], jnp.concatenate([st[h].astype(BF16), vt[vs[h], :]], axis=1)) for h in range(H)]
            for h in range(H):
                st_ref[d, vs[h], :] = st[h] * decay[:, ks[h]] + update[h]
            dirs[d][3][base:base + C, :] = jnp.concatenate(outs, axis=-1)
        cur = nxt


def _hgrn_readout(of_ref, ob_ref, gate_ref, nw_ref):
    DV = HGRN_VAL_DIM
    o = of_ref[...] + ob_ref[...]
    gate = gate_ref[...]
    nw = nw_ref[...]
    ys = [_rmsnorm(o[:, h * DV:(h + 1) * DV], nw) * gate[:, h * DV:(h + 1) * DV] for h in range(HGRN_HEADS)]
    return jnp.concatenate(ys, axis=-1).astype(BF16)


def _plain_readout(o_ref):
    return o_ref[...]


def _hgrn_mixer(lay, h, mod, gain, w_in, norm_w, lower_bounds, layer):
    D = h.shape[1]
    B, S, C = lay.B, lay.S, lay.C
    KD = HGRN_HEADS * HGRN_KEY_DIM
    tm = lay.tile(ROW_TILE)
    midx = lay.mod_index(tm)
    n_cols = w_in.shape[1] // KD
    proj = pl.pallas_call(
        functools.partial(_hgrn_proj_kernel, layer=layer),
        grid=(lay.n_all // tm,),
        in_specs=[pl.BlockSpec((tm, D), lambda t: (t, 0)),
                  pl.BlockSpec((1, N_MOD, D), lambda t: (midx(t), 0, 0)),
                  pl.BlockSpec((1, D), lambda t: (0, 0)),
                  pl.BlockSpec(lower_bounds.shape, lambda t: (0, 0)),
                  pl.BlockSpec((D, n_cols * KD), lambda t: (0, 0), pipeline_mode=pl.Buffered(1))],
        out_specs=pl.BlockSpec((tm, n_cols * KD), lambda t: (t, 0)),
        out_shape=jax.ShapeDtypeStruct((lay.n_all, n_cols * KD), F32),
        compiler_params=_params("arbitrary"),
        name="hgrn_proj",
    )(h, mod, gain.reshape(1, D), lower_bounds.astype(F32), w_in.astype(BF16))

    ts = C
    n_lat_steps = S // ts
    lat_blocks = lay.n_lat // ts

    def rows(d):
        def index(b, s):
            lat = b * n_lat_steps + (s - 1 if d == 0 else n_lat_steps - s)
            return jnp.where(s == 0, lat_blocks + b, lat)
        return index

    def col(d, j):
        return pl.BlockSpec((ts, KD), lambda b, s: (rows(d)(b, s), j))

    o_f, o_b = pl.pallas_call(
        functools.partial(_hgrn_scan_kernel, n_chunks=ts // HGRN_CHUNK),
        grid=(B, 1 + n_lat_steps),
        in_specs=[col(0, 0), col(0, 1), col(0, 3), col(1, 0), col(1, 2), col(1, 3)],
        out_specs=[col(0, 0), col(1, 0)],
        out_shape=[jax.ShapeDtypeStruct((lay.n_all, KD), F32)] * 2,
        scratch_shapes=[pltpu.VMEM((2, HGRN_HEADS * HGRN_VAL_DIM, HGRN_KEY_DIM), F32)],
        compiler_params=_params("arbitrary", "arbitrary"),
        name="hgrn_scan",
    )(proj, proj, proj, proj, proj, proj)

    return _hgrn_readout, [(o_f, 0), (o_b, 0), (proj, 4), (norm_w.reshape(1, -1), 0)]


MLA_QK_PAD = LANES


def _mla_proj_kernel(h_ref, mod_ref, gain_ref, wd_ref, qn_ref, kvn_ref, wuq_ref, wukv_ref,
                     cos_ref, sa_ref, sb_ref, q_ref, k_ref, v_ref):
    H, P = MLA_HEADS, MLA_QK_PAD
    a = _pre(h_ref[...], gain_ref[...], mod_ref[0], 3).astype(BF16)
    dn = _dot(a, wd_ref[...])
    cos, sa, sb = cos_ref[...], sa_ref[...], sb_ref[...]
    f = MLA_ROPE // 4
    cq = _rmsnorm(dn[:, :MLA_Q_LORA], qn_ref[...]).astype(BF16)
    ckv = _rmsnorm(dn[:, MLA_Q_LORA:MLA_Q_LORA + MLA_KV_LORA], kvn_ref[...]).astype(BF16)
    kr = _rope(dn[:, MLA_Q_LORA + MLA_KV_LORA:MLA_Q_LORA + MLA_KV_LORA + P], cos, sa, sb, f)
    lane = lax.broadcasted_iota(jnp.int32, (1, P), 1)
    ones_half = jnp.where(lane >= MLA_V_DIM, 1.0, 0.0)

    scale = (MLA_NOPE + MLA_ROPE) ** -0.5 * LOG2_E

    def emit_q(col, tile):
        for j in range(MXU_COLS // P):
            lo = col + j * P
            q_ref[:, lo:lo + P] = (_rope(tile[:, j * P:(j + 1) * P], cos, sa, sb, f) * scale).astype(BF16)

    def emit_kv(col, tile):
        for j in range(MXU_COLS // P):
            lo = col + j * P
            x = tile[:, j * P:(j + 1) * P]
            if lo < H * P:
                k_ref[:, lo:lo + P] = (x + kr).astype(BF16)
            else:
                v_ref[:, lo - H * P:lo - H * P + P] = (x + ones_half).astype(BF16)

    _chunked_proj(cq, wuq_ref, emit_q)
    _chunked_proj(ckv, wukv_ref, emit_kv)


def _mla_attn_kernel(q_ref, kl_ref, vl_ref, kx_ref, vx_ref, o_ref, *, nq, with_ctx):
    i = pl.program_id(1)
    P, VD = MLA_QK_PAD, MLA_V_DIM

    def run(kv_pairs):
        def scores(hd):
            qh = q_ref[:, hd * P:(hd + 1) * P]
            return [_dot_nt(qh, k_ref[:, hd * P:(hd + 1) * P]) for k_ref, _ in kv_pairs]

        heads = []
        ss = scores(0)
        for hd in range(MLA_HEADS):
            nxt = scores(hd + 1) if hd + 1 < MLA_HEADS else None
            m = functools.reduce(jnp.maximum, [jnp.max(s, axis=-1, keepdims=True) for s in ss])
            pair = (hd // 2) * MXU_COLS
            acc = None
            for s, (_, v_ref) in zip(ss, kv_pairs):
                t = _dot(jnp.exp2(s - m).astype(BF16), v_ref[:, pair:pair + MXU_COLS])
                acc = t if acc is None else acc + t
            acc = acc[:, (hd % 2) * P:(hd % 2 + 1) * P]
            heads.append(acc[:, :VD] * (1.0 / acc[:, VD:]))
            ss = nxt
        o_ref[...] = jnp.concatenate(heads, axis=-1).astype(BF16)

    if not with_ctx:
        run([(kl_ref, vl_ref), (kx_ref, vx_ref)])
        return

    @pl.when(i < nq)
    def _():
        run([(kl_ref, vl_ref), (kx_ref, vx_ref)])

    @pl.when(i >= nq)
    def _():
        run([(kx_ref, vx_ref)])


def _mla_mixer(lay, h, mod, gain, w_down, q_norm_w, kv_norm_w, w_uq, w_ukv, with_ctx_out):
    D = h.shape[1]
    B, S, C = lay.B, lay.S, lay.C
    H, P, VD = MLA_HEADS, MLA_QK_PAD, MLA_V_DIM
    qk = MLA_NOPE + MLA_ROPE
    lora = MLA_Q_LORA + MLA_KV_LORA
    kr_cols = jnp.zeros((D, MXU_COLS), F32).at[:, MLA_NOPE:qk].set(w_down[:, lora:])
    wd = jnp.concatenate([w_down[:, :lora], kr_cols], axis=1).astype(BF16)
    wuq = jnp.pad(w_uq.reshape(MLA_Q_LORA, H, qk), ((0, 0), (0, 0), (0, P - qk))).reshape(MLA_Q_LORA, H * P)
    ukv = w_ukv.reshape(MLA_KV_LORA, H, MLA_NOPE + VD)
    wkn = jnp.pad(ukv[..., :MLA_NOPE], ((0, 0), (0, 0), (0, P - MLA_NOPE))).reshape(MLA_KV_LORA, H * P)
    wv = jnp.pad(ukv[..., MLA_NOPE:], ((0, 0), (0, 0), (0, P - VD))).reshape(MLA_KV_LORA, H * P)
    wukv = jnp.concatenate([wkn, wv], axis=1)

    tm = lay.tile(ROW_TILE)
    midx, pidx = lay.mod_index(tm), lay.pos_index(tm)
    cos, sa, sb = _rope_tables(S, MLA_ROPE, tm, MLA_NOPE)
    tab = pl.BlockSpec((tm, LANES), lambda t: (pidx(t), 0))
    full = lambda shape: pl.BlockSpec(shape, lambda t: (0, 0))
    q, k, v = pl.pallas_call(
        _mla_proj_kernel,
        grid=(lay.n_all // tm,),
        in_specs=[pl.BlockSpec((tm, D), lambda t: (t, 0)),
                  pl.BlockSpec((1, N_MOD, D), lambda t: (midx(t), 0, 0)),
                  full((1, D)), full((D, lora + MXU_COLS)), full((1, MLA_Q_LORA)), full((1, MLA_KV_LORA)),
                  full((MLA_Q_LORA, H * P)), full((MLA_KV_LORA, 2 * H * P)),
                  tab, tab, tab],
        out_specs=[pl.BlockSpec((tm, H * P), lambda t: (t, 0))] * 3,
        out_shape=[jax.ShapeDtypeStruct((lay.n_all, H * P), BF16)] * 3,
        compiler_params=_params("arbitrary"),
        name="mla_proj",
    )(h, mod, gain.reshape(1, D), wd, q_norm_w.reshape(1, -1), kv_norm_w.reshape(1, -1),
      wuq.astype(BF16), wukv.astype(BF16), cos, sa, sb)

    tq = _divisor(math.gcd(S, C), QUERY_TILE)
    q_index, nq, nc, lat, ctx = _attn_specs(lay, tq, [H * P, H * P])
    return pl.pallas_call(
        functools.partial(_mla_attn_kernel, nq=nq, with_ctx=with_ctx_out),
        grid=(B, nq + (nc if with_ctx_out else 0)),
        in_specs=[pl.BlockSpec((tq, H * P), lambda b, i: (q_index(b, i), 0)), *lat, *ctx],
        out_specs=pl.BlockSpec((tq, H * VD), lambda b, i: (q_index(b, i), 0)),
        out_shape=jax.ShapeDtypeStruct((lay.n_all if with_ctx_out else lay.n_lat, H * VD), BF16),
        compiler_params=_params("arbitrary", "arbitrary"),
        name="mla_attn",
    )(q, k, v, k, v)


def kernel(x, c, ctx, c_ctx, ada_w, ada_b, norm_w, final_norm_w, ffn_w_gate, ffn_w_up, ffn_w_down,
           gqa_w_in, gqa_w_out, gqa_sinks, diff_w_in, diff_w_out, diff_lambda, diff_subln_w,
           hgrn_w_in, hgrn_w_out, hgrn_norm_w, hgrn_lower_bounds,
           mla_w_down, mla_q_norm_w, mla_kv_norm_w, mla_w_uq, mla_w_ukv, mla_w_out):
    B, S, D = x.shape
    C = ctx.shape[1]
    lay = _Layout(B, S, C)
    depth = ada_w.shape[0]

    rows = -(-(B + 1) // 8) * 8
    cc = jnp.zeros((rows, D), F32).at[:B].set(c).at[B].set(c_ctx)
    mods = _ada_table(cc, ada_w, ada_b).reshape(depth, rows, N_MOD, D)

    stacks = (ffn_w_gate, ffn_w_up, ffn_w_down)
    weights = tuple(w[0, 0].astype(BF16) for w in stacks)
    h = (x.reshape(B * S, D), ctx.reshape(B * C, D))
    for i in range(depth):
        kind, j = i % 4, i // 4
        last = i == depth - 1
        mod = mods[i]
        h, weights = _ffn(lay, h, mod, norm_w[i, 0], final_norm_w, weights, 0, cast_next=(*stacks, i, 1))
        if kind == 0:
            o = _gqa_mixer(lay, h, mod, norm_w[i, 1], gqa_w_in[j], gqa_sinks[j], not last)
            mix = (_plain_readout, [(o, 0)], gqa_w_out[j])
        elif kind == 1:
            lambda_init = 0.8 - 0.6 * math.exp(-0.3 * i)
            o = _diff_mixer(lay, h, mod, norm_w[i, 1], diff_w_in[j], diff_lambda[j], diff_subln_w[j],
                            lambda_init, not last)
            mix = (_plain_readout, [(o, 0)], diff_w_out[j])
        elif kind == 2:
            readout, operands = _hgrn_mixer(lay, h, mod, norm_w[i, 1], hgrn_w_in[j], hgrn_norm_w[j],
                                            hgrn_lower_bounds, i)
            mix = (readout, operands, hgrn_w_out[j])
        else:
            o = _mla_mixer(lay, h, mod, norm_w[i, 1], mla_w_down[j], mla_q_norm_w[j], mla_kv_norm_w[j],
                           mla_w_uq[j], mla_w_ukv[j], not last)
            mix = (_plain_readout, [(o, 0)], mla_w_out[j])
        h, weights = _ffn(lay, h, mod, norm_w[i, 2], final_norm_w, weights, 1, lat_only=last, final_norm=last,
                          mix=mix, cast_next=None if last else (*stacks, i + 1, 0))
    return h[:B * S].reshape(B, S, D)
```

```python
import functools
import math

import jax
import jax.numpy as jnp
from jax import lax
from jax.experimental import pallas as pl
from jax.experimental.pallas import tpu as pltpu

F32 = jnp.float32
BF16 = jnp.bfloat16

D_MODEL = 1024
GRID_W = 64
N_MOD = 9
ROPE_BASE = 10000.0
EPS = 1e-6
NEG_INF = -1e30
D_FF = 2816

GQA_HEADS = 16
GQA_KV_HEADS = 4
GQA_HEAD_DIM = 64
GQA_WINDOW = 128
GQA_BLOCK = 128

DIFF_HEADS = 8
DIFF_HEAD_DIM = 64

HGRN_HEADS = 8
HGRN_KEY_DIM = 128
HGRN_VAL_DIM = D_MODEL // HGRN_HEADS
HGRN_CHUNK = 64

MLA_HEADS = 16
MLA_Q_LORA = 256
MLA_KV_LORA = 256
MLA_NOPE = 64
MLA_ROPE = 32
MLA_V_DIM = 64

LOG2_E = 1.4426950408889634
LANES = 128
SUBLANES = 8
MXU_COLS = 2 * LANES
VMEM_LIMIT = 56 * 1024 * 1024
ROW_TILE = 512
FFN_ROW_TILE = 1024
QUERY_TILE = 256


def _dot(a, b):
    return jnp.dot(a, b, preferred_element_type=F32)


def _dot_nt(a, b):
    return lax.dot_general(a, b, (((1,), (1,)), ((), ())), preferred_element_type=F32)


def _params(*sem):
    return pltpu.CompilerParams(dimension_semantics=sem, vmem_limit_bytes=VMEM_LIMIT)


def _divisor(n, pref):
    t = min(n, pref)
    while n % t:
        t -= 8
    return t


class _Layout:
    def __init__(self, B, S, C):
        self.B, self.S, self.C = B, S, C
        self.n_lat = B * S
        self.n_all = B * S + B * C

    def tile(self, pref):
        return _divisor(math.gcd(self.S, self.B * self.C), pref)

    def mod_index(self, tm):
        n_lat_tiles, per_batch, B = self.n_lat // tm, self.S // tm, self.B
        return lambda t: jnp.where(t < n_lat_tiles, t // per_batch, B)

    def pos_index(self, tm):
        n_lat_tiles, per_batch = self.n_lat // tm, self.S // tm
        return lambda t: jnp.where(t < n_lat_tiles, t % per_batch, per_batch)


def _rmsnorm(x, w):
    return (x * lax.rsqrt(jnp.mean(x * x, axis=-1, keepdims=True) + EPS)) * w


def _pre(h, gain, m, off):
    inv = lax.rsqrt(jnp.mean(h * h, axis=-1, keepdims=True) + EPS)
    return (h * inv) * (gain * (1.0 + m[off + 1:off + 2])) + m[off:off + 1]


def _silu(x):
    return x * jax.nn.sigmoid(x)


def _chunked_proj(a, w_ref, emit):
    n = w_ref.shape[1] // MXU_COLS
    cur = _dot(a, w_ref[:, :MXU_COLS])
    for c in range(n):
        nxt = _dot(a, w_ref[:, (c + 1) * MXU_COLS:(c + 2) * MXU_COLS]) if c + 1 < n else None
        emit(c * MXU_COLS, cur)
        cur = nxt


def _rope(x, cos, sa, sb, shift):
    return x * cos + pltpu.roll(x, LANES - shift, 1) * sa + pltpu.roll(x, shift, 1) * sb


def _ada_kernel(c_ref, w_ref, b_ref, o_ref):
    sc = _silu(c_ref[...]).astype(BF16)
    o_ref[0] = _dot(sc, w_ref[0].astype(BF16)) + b_ref[0]


ADA_COLS = N_MOD * LANES


def _ada_table(cc, ada_w, ada_b):
    rows, D = cc.shape
    depth, _, n_out = ada_w.shape
    tn = ADA_COLS
    return pl.pallas_call(
        _ada_kernel,
        grid=(depth, n_out // tn),
        in_specs=[pl.BlockSpec((rows, D), lambda i, j: (0, 0)),
                  pl.BlockSpec((1, D, tn), lambda i, j: (i, 0, j)),
                  pl.BlockSpec((1, 1, tn), lambda i, j: (i, 0, j))],
        out_specs=pl.BlockSpec((1, rows, tn), lambda i, j: (i, 0, j)),
        out_shape=jax.ShapeDtypeStruct((depth, rows, n_out), F32),
        compiler_params=_params("arbitrary", "arbitrary"),
        name="ada_table",
    )(cc, ada_w, ada_b.reshape(depth, 1, n_out))


FFN_CHUNK = MXU_COLS


def _ffn_kernel(*refs, off, final_norm, n_lat_tiles, readout, n_mix, cast_next):
    refs = list(refs)
    if n_lat_tiles is None:
        h = refs.pop(0)[...]
    else:
        hx_ref, hc_ref = refs.pop(0), refs.pop(0)
        h = jnp.where(pl.program_id(0) < n_lat_tiles, hx_ref[...], hc_ref[...])
    mix_refs = [refs.pop(0) for _ in range(n_mix)]
    if readout is not None:
        wo_ref = refs.pop(0)
    mod_ref, gain_ref, fin_ref, wg_ref, wu_ref, wd_ref = refs[:6]
    if cast_next:
        for src, dst in zip(refs[6:9], refs[10:13]):
            dst[...] = src[...].astype(BF16)
    o_ref, a_ref = refs[9 if cast_next else 6], refs[-1]
    m = mod_ref[0]
    if readout is not None:
        h = h + m[5:6] * _dot(readout(*mix_refs), wo_ref[...])
    xm = _pre(h, gain_ref[...], m, off).astype(BF16)
    n = D_FF // FFN_CHUNK

    def gate_up(c):
        cols = slice(c * FFN_CHUNK, (c + 1) * FFN_CHUNK)
        return _dot(xm, wg_ref[:, cols]), _dot(xm, wu_ref[:, cols])

    cur = gate_up(0)
    for c in range(n):
        nxt = gate_up(c + 1) if c + 1 < n else None
        g, u = cur
        a_ref[:, c * FFN_CHUNK:(c + 1) * FFN_CHUNK] = (_silu(g) * u).astype(BF16)
        cur = nxt
    out = h + (0.5 * m[off + 2:off + 3]) * _dot(a_ref[...], wd_ref[...])
    if final_norm:
        out = _rmsnorm(out, fin_ref[...])
    o_ref[...] = out


def _slabs(rows, n_steps):
    n = n_steps
    while rows % n or (rows // n) % (2 * SUBLANES):
        n -= 1
    return n


def _ffn(lay, h, mod, gain, fin_w, weights, half, lat_only=False, final_norm=False, mix=None, cast_next=None):
    split = isinstance(h, tuple)
    wg, wu, wd = weights
    D = wg.shape[0]
    wide_operands = mix is not None and any(a.dtype == F32 and a.shape[0] > 1 for a, _ in mix[1])
    tm = lay.tile(ROW_TILE if (split or wide_operands) else FFN_ROW_TILE)
    n_rows = lay.n_lat if lat_only else lay.n_all
    n_steps = n_rows // tm
    midx = lay.mod_index(tm)
    n_lat_tiles = lay.n_lat // tm
    const = lambda t: (0, 0)
    one = pl.Buffered(1)
    weight = lambda shape: pl.BlockSpec(shape, const, pipeline_mode=one)
    if split:
        h_specs = [pl.BlockSpec((tm, D), lambda t: (jnp.minimum(t, n_lat_tiles - 1), 0)),
                   pl.BlockSpec((tm, D), lambda t: (jnp.maximum(t - n_lat_tiles, 0), 0))]
        h_args = list(h)
    else:
        h_specs = [pl.BlockSpec((tm, D), lambda t: (t, 0))]
        h_args = [h]
    readout, operands = None, []
    if mix is not None:
        readout, operands, w_out = mix
        for arr, col in operands:
            if arr.shape[0] == 1:
                h_specs.append(pl.BlockSpec(arr.shape, const))
            else:
                h_specs.append(pl.BlockSpec((tm, D), lambda t, col=col: (t, col)))
            h_args.append(arr)
        h_specs.append(pl.BlockSpec(w_out.shape, const, pipeline_mode=one))
        h_args.append(w_out.astype(BF16))
    cast_specs, cast_args, cast_out_specs, cast_out_shapes = [], [], [], []
    if cast_next is not None:
        *stacks, nl, nh = cast_next
        for w32 in stacks:
            rows, cols = w32.shape[2:]
            n = _slabs(rows, n_steps)
            slab = rows // n
            cast_specs.append(pl.BlockSpec((None, None, slab, cols),
                                           lambda t, n=n: (nl, nh, jnp.minimum(t, n - 1), 0)))
            cast_args.append(w32)
            cast_out_specs.append(pl.BlockSpec((slab, cols), lambda t, n=n: (jnp.minimum(t, n - 1), 0)))
            cast_out_shapes.append(jax.ShapeDtypeStruct((rows, cols), BF16))
    outs = pl.pallas_call(
        functools.partial(_ffn_kernel, off=6 * half, final_norm=final_norm,
                          n_lat_tiles=n_lat_tiles if split else None, readout=readout, n_mix=len(operands),
                          cast_next=cast_next is not None),
        grid=(n_steps,),
        in_specs=h_specs + [pl.BlockSpec((1, N_MOD, D), lambda t: (midx(t), 0, 0)),
                            pl.BlockSpec((1, D), const),
                            pl.BlockSpec((1, D), const),
                            weight((D, D_FF)), weight((D, D_FF)), weight((D_FF, D))] + cast_specs,
        out_specs=[pl.BlockSpec((tm, D), lambda t: (t, 0))] + cast_out_specs,
        out_shape=[jax.ShapeDtypeStruct((n_rows, D), F32)] + cast_out_shapes,
        scratch_shapes=[pltpu.VMEM((tm, D_FF), BF16)],
        compiler_params=_params("arbitrary"),
        name="ffn",
    )(*h_args, mod, gain.reshape(1, D), fin_w.reshape(1, D), wg, wu, wd, *cast_args)
    return outs[0], (tuple(outs[1:]) if cast_next is not None else None)


def _rope_tables(S, rot_dim, pad_rows, lane_off):
    rows = S // GRID_W
    row = jnp.repeat(jnp.arange(rows, dtype=F32), GRID_W)
    col = jnp.tile(jnp.arange(GRID_W, dtype=F32), rows)
    axis_dim = rot_dim // 2
    inv_freq = ROPE_BASE ** (-jnp.arange(0, axis_dim, 2, dtype=F32) / axis_dim)
    ang_r = row[:, None] * inv_freq[None, :]
    ang_c = col[:, None] * inv_freq[None, :]
    ang = jnp.concatenate([ang_r, ang_r, ang_c, ang_c], axis=-1)
    cos, sin = jnp.cos(ang), jnp.sin(ang)
    f = rot_dim // 4
    first = (jnp.arange(rot_dim) % (2 * f)) < f
    sa = jnp.where(first[None, :], -sin, 0.0)
    sb = jnp.where(first[None, :], 0.0, sin)
    period = 64 if lane_off + rot_dim <= 64 else LANES

    def widen(t, fill):
        blk = jnp.full((S, period), fill, F32).at[:, lane_off:lane_off + rot_dim].set(t)
        blk = jnp.tile(blk, (1, LANES // period))
        return jnp.concatenate([blk, jnp.full((pad_rows, LANES), fill, F32)], axis=0)

    return widen(cos, 1.0), widen(sa, 0.0), widen(sb, 0.0)


def _attn_specs(lay, tq, widths):
    S, C, B = lay.S, lay.C, lay.B
    nq, nc = S // tq, C // tq
    lat_blocks = lay.n_lat // C

    def q_index(b, i):
        return jnp.where(i < nq, b * nq + i, B * nq + b * nc + (i - nq))

    lat = [pl.BlockSpec((S, w), lambda b, i: (b, 0)) for w in widths]
    ctx = [pl.BlockSpec((C, w), lambda b, i: (lat_blocks + b, 0)) for w in widths]
    return q_index, nq, nc, lat, ctx


def _gqa_proj_kernel(h_ref, mod_ref, gain_ref, w_ref, cos_ref, sa_ref, sb_ref, q_ref, k_ref, v_ref):
    qd, kd = GQA_HEADS * GQA_HEAD_DIM, GQA_KV_HEADS * GQA_HEAD_DIM
    a = _pre(h_ref[...], gain_ref[...], mod_ref[0], 3).astype(BF16)
    cos, sa, sb = cos_ref[...], sa_ref[...], sb_ref[...]
    f = GQA_HEAD_DIM // 4
    scale = GQA_HEAD_DIM ** -0.5 * LOG2_E
    lane = lax.broadcasted_iota(jnp.int32, (1, MXU_COLS), 1)
    ones_half = jnp.where((lane & (2 * GQA_HEAD_DIM - 1)) >= GQA_HEAD_DIM, 1.0, 0.0)

    def emit(col, tile):
        if col >= qd + kd:
            v_ref[:, col - qd - kd:col - qd - kd + MXU_COLS] = (tile + ones_half).astype(BF16)
            return
        for j in range(MXU_COLS // LANES):
            x = _rope(tile[:, j * LANES:(j + 1) * LANES], cos, sa, sb, f)
            lo = col + j * LANES
            if lo < qd:
                q_ref[:, lo:lo + LANES] = (x * scale).astype(BF16)
            else:
                k_ref[:, lo - qd:lo - qd + LANES] = x.astype(BF16)

    _chunked_proj(a, w_ref, emit)


def _gqa_attn_kernel(sink_ref, q_ref, kp_ref, kc_ref, kn_ref, vp_ref, vc_ref, vn_ref, kx_ref, vx_ref,
                     o_ref, kcat_ref, vcat_ref, *, nq):
    i = pl.program_id(1)
    blk = GQA_BLOCK
    G = GQA_HEADS // GQA_KV_HEADS
    HD = GQA_HEAD_DIM
    VW = 2 * HD
    span = 3 * blk

    def run(k_ref, v_ref, masks):
        kvs = range(GQA_KV_HEADS)
        qs = [jnp.concatenate([q_ref[:, (G * g + n) * HD:(G * g + n + 1) * HD] for n in range(G)], axis=0)
              for g in kvs]
        sinks = [jnp.concatenate([jnp.full((blk, 1), sink_ref[G * g + n] * LOG2_E, F32) for n in range(G)], axis=0)
                 for g in kvs]
        ss = [_dot_nt(qs[g], k_ref[:, g * HD:(g + 1) * HD]) for g in kvs]
        if masks is not None:
            prev_ok, next_ok = masks
            ss = [jnp.concatenate([jnp.where(prev_ok, s[:, :blk], NEG_INF), s[:, blk:2 * blk],
                                   jnp.where(next_ok, s[:, 2 * blk:span], NEG_INF), s[:, span:]], axis=1)
                  for s in ss]
        ms = [jnp.maximum(jnp.max(ss[g], axis=-1, keepdims=True), sinks[g]) for g in kvs]
        accs = [_dot(jnp.exp2(ss[g] - ms[g]).astype(BF16),
                     v_ref[:, (g // 2) * MXU_COLS:(g // 2 + 1) * MXU_COLS])[:, (g % 2) * VW:(g % 2 + 1) * VW]
                for g in kvs]
        outs = [accs[g][:, :HD] * (1.0 / (accs[g][:, HD:] + jnp.exp2(sinks[g] - ms[g]))) for g in kvs]
        heads = [outs[g][n * blk:(n + 1) * blk] for g in kvs for n in range(G)]
        o_ref[...] = jnp.concatenate(heads, axis=-1).astype(BF16)

    @pl.when(i < nq)
    def _():
        for n, (k_ref, v_ref) in enumerate([(kp_ref, vp_ref), (kc_ref, vc_ref), (kn_ref, vn_ref)]):
            kcat_ref[n * blk:(n + 1) * blk] = k_ref[...]
            vcat_ref[n * blk:(n + 1) * blk] = v_ref[...]
        kcat_ref[span:] = kx_ref[...]
        vcat_ref[span:] = vx_ref[...]
        row = lax.broadcasted_iota(jnp.int32, (G * blk, blk), 0) & (blk - 1)
        col = lax.broadcasted_iota(jnp.int32, (G * blk, blk), 1)
        prev_ok = (row + blk - col <= GQA_WINDOW) & (i >= 1)
        next_ok = (col + blk - row <= GQA_WINDOW) & (i + 1 < nq)
        run(kcat_ref, vcat_ref, (prev_ok, next_ok))

    @pl.when(i >= nq)
    def _():
        run(kx_ref, vx_ref, None)


def _gqa_mixer(lay, h, mod, gain, w_in, sinks, with_ctx_out):
    D = h.shape[1]
    B, S, C = lay.B, lay.S, lay.C
    qd, kd = GQA_HEADS * GQA_HEAD_DIM, GQA_KV_HEADS * GQA_HEAD_DIM
    w_v = jnp.pad(w_in[:, qd + kd:].reshape(D, GQA_KV_HEADS, GQA_HEAD_DIM), ((0, 0), (0, 0), (0, GQA_HEAD_DIM)))
    w_pad = jnp.concatenate([w_in[:, :qd + kd], w_v.reshape(D, 2 * kd)], axis=1)
    tm = lay.tile(ROW_TILE)
    midx, pidx = lay.mod_index(tm), lay.pos_index(tm)
    cos, sa, sb = _rope_tables(S, GQA_HEAD_DIM, tm, 0)
    tab = pl.BlockSpec((tm, LANES), lambda t: (pidx(t), 0))
    q, k, v = pl.pallas_call(
        _gqa_proj_kernel,
        grid=(lay.n_all // tm,),
        in_specs=[pl.BlockSpec((tm, D), lambda t: (t, 0)),
                  pl.BlockSpec((1, N_MOD, D), lambda t: (midx(t), 0, 0)),
                  pl.BlockSpec((1, D), lambda t: (0, 0)),
                  pl.BlockSpec((D, qd + 3 * kd), lambda t: (0, 0)),
                  tab, tab, tab],
        out_specs=[pl.BlockSpec((tm, qd), lambda t: (t, 0)),
                   pl.BlockSpec((tm, kd), lambda t: (t, 0)),
                   pl.BlockSpec((tm, 2 * kd), lambda t: (t, 0))],
        out_shape=[jax.ShapeDtypeStruct((lay.n_all, qd), BF16),
                   jax.ShapeDtypeStruct((lay.n_all, kd), BF16),
                   jax.ShapeDtypeStruct((lay.n_all, 2 * kd), BF16)],
        compiler_params=_params("arbitrary"),
        name="gqa_proj",
    )(h, mod, gain.reshape(1, D), w_pad.astype(BF16), cos, sa, sb)

    blk = GQA_BLOCK
    assert S % blk == 0 and C % blk == 0
    nq, nc = S // blk, C // blk
    n_i = nq + (nc if with_ctx_out else 0)
    lat_blocks = lay.n_lat // C

    def q_index(b, i):
        return jnp.where(i < nq, b * nq + i, B * nq + b * nc + (i - nq))

    def win(delta):
        return lambda b, i: (b * nq + jnp.clip(i + delta, 0, nq - 1), 0)

    def windows(width):
        return [pl.BlockSpec((blk, width), win(d)) for d in (-1, 0, 1)]

    def ctx_spec(width):
        return pl.BlockSpec((C, width), lambda b, i: (lat_blocks + b, 0))

    row = pl.BlockSpec((blk, qd), lambda b, i: (q_index(b, i), 0))
    return pl.pallas_call(
        functools.partial(_gqa_attn_kernel, nq=nq),
        grid=(B, n_i),
        in_specs=[pl.BlockSpec(memory_space=pltpu.SMEM), row,
                  *windows(kd), *windows(2 * kd), ctx_spec(kd), ctx_spec(2 * kd)],
        out_specs=row,
        out_shape=jax.ShapeDtypeStruct((lay.n_all if with_ctx_out else lay.n_lat, qd), BF16),
        scratch_shapes=[pltpu.VMEM((3 * blk + C, kd), BF16), pltpu.VMEM((3 * blk + C, 2 * kd), BF16)],
        compiler_params=_params("arbitrary", "arbitrary"),
        name="gqa_attn",
    )(sinks.astype(F32), q, k, k, k, v, v, v, k, v)


def _diff_proj_kernel(h_ref, mod_ref, gain_ref, w_ref, cos_ref, sa_ref, sb_ref, q_ref, k_ref, v_ref):
    qk = DIFF_HEADS * 2 * DIFF_HEAD_DIM
    a = _pre(h_ref[...], gain_ref[...], mod_ref[0], 3).astype(BF16)
    cos, sa, sb = cos_ref[...], sa_ref[...], sb_ref[...]
    f = DIFF_HEAD_DIM // 4
    scale = DIFF_HEAD_DIM ** -0.5 * LOG2_E
    vw = 2 * DIFF_HEAD_DIM
    ones = jnp.ones((a.shape[0], vw), BF16)

    def emit(col, tile):
        for j in range(MXU_COLS // LANES):
            lo = col + j * LANES
            x = tile[:, j * LANES:(j + 1) * LANES]
            if lo < qk:
                q_ref[:, lo:lo + LANES] = (_rope(x, cos, sa, sb, f) * scale).astype(BF16)
            elif lo < 2 * qk:
                k_ref[:, lo - qk:lo - qk + LANES] = _rope(x, cos, sa, sb, f).astype(BF16)
            else:
                hd = (lo - 2 * qk) // vw
                v_ref[:, 2 * hd * vw:(2 * hd + 1) * vw] = x.astype(BF16)
                v_ref[:, (2 * hd + 1) * vw:(2 * hd + 2) * vw] = ones

    _chunked_proj(a, w_ref, emit)


def _diff_attn_kernel(lam_ref, sub_ref, q_ref, kl_ref, vl_ref, kx_ref, vx_ref, o_ref, *, nq, lambda_init):
    i = pl.program_id(1)
    HD = DIFF_HEAD_DIM
    VW = 4 * HD
    lp = lam_ref[...]
    lam = (jnp.exp(jnp.sum(lp[0:1] * lp[1:2], keepdims=True))
           - jnp.exp(jnp.sum(lp[2:3] * lp[3:4], keepdims=True)) + lambda_init)

    def run(kv_pairs):
        def scores(h):
            los = [(2 * h + j) * HD for j in range(2)]
            return [[_dot_nt(q_ref[:, lo:lo + HD], k_ref[:, lo:lo + HD]) for k_ref, _ in kv_pairs] for lo in los]

        heads = []
        ss = scores(0)
        for h in range(DIFF_HEADS):
            nxt = scores(h + 1) if h + 1 < DIFF_HEADS else None
            maps = []
            for parts in ss:
                m = functools.reduce(jnp.maximum, [jnp.max(s, axis=-1, keepdims=True) for s in parts])
                acc = None
                for s, (_, v_ref) in zip(parts, kv_pairs):
                    t = _dot(jnp.exp2(s - m).astype(BF16), v_ref[:, h * VW:(h + 1) * VW])
                    acc = t if acc is None else acc + t
                maps.append(acc[:, :VW // 2] * (1.0 / acc[:, VW // 2:]))
            o = maps[0] - lam * maps[1]
            heads.append(_rmsnorm(o, sub_ref[...]) * (1.0 - lambda_init))
            ss = nxt
        o_ref[...] = jnp.concatenate(heads, axis=-1).astype(BF16)

    @pl.when(i < nq)
    def _():
        run([(kl_ref, vl_ref), (kx_ref, vx_ref)])

    @pl.when(i >= nq)
    def _():
        run([(kx_ref, vx_ref)])


def _diff_mixer(lay, h, mod, gain, w_in, lam_params, subln_w, lambda_init, with_ctx_out):
    D = h.shape[1]
    B, S, C = lay.B, lay.S, lay.C
    qk = DIFF_HEADS * 2 * DIFF_HEAD_DIM
    tm = lay.tile(ROW_TILE)
    midx, pidx = lay.mod_index(tm), lay.pos_index(tm)
    cos, sa, sb = _rope_tables(S, DIFF_HEAD_DIM, tm, 0)
    tab = pl.BlockSpec((tm, LANES), lambda t: (pidx(t), 0))
    out = jax.ShapeDtypeStruct((lay.n_all, qk), BF16)
    q, k, v = pl.pallas_call(
        _diff_proj_kernel,
        grid=(lay.n_all // tm,),
        in_specs=[pl.BlockSpec((tm, D), lambda t: (t, 0)),
                  pl.BlockSpec((1, N_MOD, D), lambda t: (midx(t), 0, 0)),
                  pl.BlockSpec((1, D), lambda t: (0, 0)),
                  pl.BlockSpec((D, 3 * qk), lambda t: (0, 0)),
                  tab, tab, tab],
        out_specs=[pl.BlockSpec((tm, qk), lambda t: (t, 0))] * 2 + [pl.BlockSpec((tm, 2 * qk), lambda t: (t, 0))],
        out_shape=[out, out, jax.ShapeDtypeStruct((lay.n_all, 2 * qk), BF16)],
        compiler_params=_params("arbitrary"),
        name="diff_proj",
    )(h, mod, gain.reshape(1, D), w_in.astype(BF16), cos, sa, sb)

    tq = _divisor(math.gcd(S, C), QUERY_TILE)
    q_index, nq, nc, lat, ctx = _attn_specs(lay, tq, [qk, 2 * qk])
    row = pl.BlockSpec((tq, qk), lambda b, i: (q_index(b, i), 0))
    return pl.pallas_call(
        functools.partial(_diff_attn_kernel, nq=nq, lambda_init=lambda_init),
        grid=(B, nq + (nc if with_ctx_out else 0)),
        in_specs=[pl.BlockSpec((4, DIFF_HEAD_DIM), lambda b, i: (0, 0)),
                  pl.BlockSpec((1, 2 * DIFF_HEAD_DIM), lambda b, i: (0, 0)),
                  row, *lat, *ctx],
        out_specs=row,
        out_shape=jax.ShapeDtypeStruct((lay.n_all if with_ctx_out else lay.n_lat, qk), BF16),
        compiler_params=_params("arbitrary", "arbitrary"),
        name="diff_attn",
    )(lam_params.astype(F32), subln_w.reshape(1, -1), q, k, v, k, v)


def _hgrn_proj_kernel(h_ref, mod_ref, gain_ref, lb_ref, w_ref, o_ref, *, layer):
    KD = HGRN_HEADS * HGRN_KEY_DIM
    W = 2 * LANES
    a = _pre(h_ref[...], gain_ref[...], mod_ref[0], 3).astype(BF16)
    raw = lb_ref[...]
    e = jnp.exp(raw - jnp.max(raw, axis=0, keepdims=True))
    soft = e * (1.0 / jnp.sum(e, axis=0, keepdims=True))
    lb = jnp.sum(soft[1:layer + 1], axis=0, keepdims=True) if layer else jnp.zeros_like(soft[0:1])
    n = w_ref.shape[1] // W

    def proj(c):
        return _dot(a, w_ref[:, c * W:(c + 1) * W])

    cur = proj(0)
    for c in range(n):
        nxt = proj(c + 1) if c + 1 < n else None
        group, col = divmod(c * W, KD)
        if group in (0, 4):
            out = _silu(cur)
        elif group in (1, 2):
            lbc = lb[:, col:col + W]
            out = lbc + (1.0 - lbc) * jax.nn.sigmoid(cur)
        else:
            out = cur
        o_ref[:, c * W:(c + 1) * W] = out
        cur = nxt


def _running_sum(x, reverse):
    n, w = x.shape
    n_groups = n // SUBLANES
    x = x.reshape(n_groups, SUBLANES, w)
    within = lax.broadcasted_iota(jnp.int32, (1, SUBLANES, 1), 1)
    step = 1
    while step < SUBLANES:
        if reverse:
            x = x + jnp.where(within < SUBLANES - step, pltpu.roll(x, SUBLANES - step, 1), 0.0)
        else:
            x = x + jnp.where(within >= step, pltpu.roll(x, step, 1), 0.0)
        step *= 2
    out = [None] * n_groups
    carry = None
    for j in (reversed(range(n_groups)) if reverse else range(n_groups)):
        out[j] = x[j] if carry is None else x[j] + carry
        carry = out[j][0:1] if reverse else out[j][SUBLANES - 1:SUBLANES]
    return jnp.concatenate(out, axis=0)


def _hgrn_scan_kernel(qf_ref, ff_ref, vf_ref, qb_ref, fb_ref, vb_ref, of_ref, ob_ref, st_ref, *, n_chunks):
    step = pl.program_id(1)
    C = HGRN_CHUNK
    DK, DV, H = HGRN_KEY_DIM, HGRN_VAL_DIM, HGRN_HEADS

    @pl.when(step == 0)
    def _():
        st_ref[...] = jnp.zeros_like(st_ref)

    r = lax.broadcasted_iota(jnp.int32, (C, C), 0)
    c = lax.broadcasted_iota(jnp.int32, (C, C), 1)
    ks = [slice(h * DK, (h + 1) * DK) for h in range(H)]
    vs = [slice(h * DV, (h + 1) * DV) for h in range(H)]
    dirs = [(qf_ref, ff_ref, vf_ref, of_ref, c <= r, C // 2, C - 1),
            (qb_ref, fb_ref, vb_ref, ob_ref, c >= r, C - 1 - C // 2, 0)]

    def prepare(d, n):
        q_ref, f_ref, v_ref, _, tri, mid_row, last_row = dirs[d]
        base = (n if d == 0 else n_chunks - 1 - n) * C
        q = q_ref[base:base + C, :]
        f = f_ref[base:base + C, :]
        vt = v_ref[base:base + C, :].T.astype(BF16)
        k = 1.0 - f
        g = _running_sum(jnp.log2(f), reverse=d == 1)
        g_mid = g[mid_row:mid_row + 1]
        g_last = g[last_row:last_row + 1]
        t = g - g_mid
        qa_f = q * jnp.exp2(t)
        kb_f = k * jnp.exp2(-t)
        qa = qa_f.astype(BF16)
        kb = kb_f.astype(BF16)
        qg = (qa_f * jnp.exp2(g_mid)).astype(BF16)
        kd = (kb_f * jnp.exp2(g_last - g_mid)).astype(BF16)
        a = [jnp.where(tri, _dot_nt(qa[:, ks[h]], kb[:, ks[h]]), 0.0).astype(BF16) for h in range(H)]
        lhs = [jnp.concatenate([qg[:, ks[h]], a[h]], axis=1) for h in range(H)]
        update = [_dot(vt[vs[h], :], kd[:, ks[h]]) for h in range(H)]
        return base, lhs, vt, jnp.exp2(g_last), update

    cur = [prepare(d, 0) for d in range(2)]
    for n in range(n_chunks):
        nxt = [prepare(d, n + 1) for d in range(2)] if n + 1 < n_chunks else None
        for d in range(2):
            base, lhs, vt, decay, update = cur[d]
            st = [st_ref[d, vs[h], :] for h in range(H)]
            outs = [_dot_nt(lhs[h], jnp.concatenate([st[h].astype(BF16), vt[vs[h], :]], axis=1)) for h in range(H)]
            for h in range(H):
                st_ref[d, vs[h], :] = st[h] * decay[:, ks[h]] + update[h]
            dirs[d][3][base:base + C, :] = jnp.concatenate(outs, axis=-1)
        cur = nxt


def _hgrn_readout(of_ref, ob_ref, gate_ref, nw_ref):
    DV = HGRN_VAL_DIM
    o = of_ref[...] + ob_ref[...]
    gate = gate_ref[...]
    nw = nw_ref[...]
    ys = [_rmsnorm(o[:, h * DV:(h + 1) * DV], nw) * gate[:, h * DV:(h + 1) * DV] for h in range(HGRN_HEADS)]
    return jnp.concatenate(ys, axis=-1).astype(BF16)


def _plain_readout(o_ref):
    return o_ref[...]


def _hgrn_mixer(lay, h, mod, gain, w_in, norm_w, lower_bounds, layer):
    D = h.shape[1]
    B, S, C = lay.B, lay.S, lay.C
    KD = HGRN_HEADS * HGRN_KEY_DIM
    tm = lay.tile(ROW_TILE)
    midx = lay.mod_index(tm)
    n_cols = w_in.shape[1] // KD
    proj = pl.pallas_call(
        functools.partial(_hgrn_proj_kernel, layer=layer),
        grid=(lay.n_all // tm,),
        in_specs=[pl.BlockSpec((tm, D), lambda t: (t, 0)),
                  pl.BlockSpec((1, N_MOD, D), lambda t: (midx(t), 0, 0)),
                  pl.BlockSpec((1, D), lambda t: (0, 0)),
                  pl.BlockSpec(lower_bounds.shape, lambda t: (0, 0)),
                  pl.BlockSpec((D, n_cols * KD), lambda t: (0, 0), pipeline_mode=pl.Buffered(1))],
        out_specs=pl.BlockSpec((tm, n_cols * KD), lambda t: (t, 0)),
        out_shape=jax.ShapeDtypeStruct((lay.n_all, n_cols * KD), F32),
        compiler_params=_params("arbitrary"),
        name="hgrn_proj",
    )(h, mod, gain.reshape(1, D), lower_bounds.astype(F32), w_in.astype(BF16))

    ts = C
    n_lat_steps = S // ts
    lat_blocks = lay.n_lat // ts

    def rows(d):
        def index(b, s):
            lat = b * n_lat_steps + (s - 1 if d == 0 else n_lat_steps - s)
            return jnp.where(s == 0, lat_blocks + b, lat)
        return index

    def col(d, j):
        return pl.BlockSpec((ts, KD), lambda b, s: (rows(d)(b, s), j))

    o_f, o_b = pl.pallas_call(
        functools.partial(_hgrn_scan_kernel, n_chunks=ts // HGRN_CHUNK),
        grid=(B, 1 + n_lat_steps),
        in_specs=[col(0, 0), col(0, 1), col(0, 3), col(1, 0), col(1, 2), col(1, 3)],
        out_specs=[col(0, 0), col(1, 0)],
        out_shape=[jax.ShapeDtypeStruct((lay.n_all, KD), F32)] * 2,
        scratch_shapes=[pltpu.VMEM((2, HGRN_HEADS * HGRN_VAL_DIM, HGRN_KEY_DIM), F32)],
        compiler_params=_params("arbitrary", "arbitrary"),
        name="hgrn_scan",
    )(proj, proj, proj, proj, proj, proj)

    return _hgrn_readout, [(o_f, 0), (o_b, 0), (proj, 4), (norm_w.reshape(1, -1), 0)]


MLA_QK_PAD = LANES


def _mla_proj_kernel(h_ref, mod_ref, gain_ref, wd_ref, qn_ref, kvn_ref, wuq_ref, wukv_ref,
                     cos_ref, sa_ref, sb_ref, q_ref, k_ref, v_ref):
    H, P = MLA_HEADS, MLA_QK_PAD
    a = _pre(h_ref[...], gain_ref[...], mod_ref[0], 3).astype(BF16)
    dn = _dot(a, wd_ref[...])
    cos, sa, sb = cos_ref[...], sa_ref[...], sb_ref[...]
    f = MLA_ROPE // 4
    cq = _rmsnorm(dn[:, :MLA_Q_LORA], qn_ref[...]).astype(BF16)
    ckv = _rmsnorm(dn[:, MLA_Q_LORA:MLA_Q_LORA + MLA_KV_LORA], kvn_ref[...]).astype(BF16)
    kr = _rope(dn[:, MLA_Q_LORA + MLA_KV_LORA:MLA_Q_LORA + MLA_KV_LORA + P], cos, sa, sb, f)
    lane = lax.broadcasted_iota(jnp.int32, (1, P), 1)
    ones_half = jnp.where(lane >= MLA_V_DIM, 1.0, 0.0)

    scale = (MLA_NOPE + MLA_ROPE) ** -0.5 * LOG2_E

    def emit_q(col, tile):
        for j in range(MXU_COLS // P):
            lo = col + j * P
            q_ref[:, lo:lo + P] = (_rope(tile[:, j * P:(j + 1) * P], cos, sa, sb, f) * scale).astype(BF16)

    def emit_kv(col, tile):
        for j in range(MXU_COLS // P):
            lo = col + j * P
            x = tile[:, j * P:(j + 1) * P]
            if lo < H * P:
                k_ref[:, lo:lo + P] = (x + kr).astype(BF16)
            else:
                v_ref[:, lo - H * P:lo - H * P + P] = (x + ones_half).astype(BF16)

    _chunked_proj(cq, wuq_ref, emit_q)
    _chunked_proj(ckv, wukv_ref, emit_kv)


def _mla_attn_kernel(q_ref, kl_ref, vl_ref, kx_ref, vx_ref, o_ref, *, nq, with_ctx):
    i = pl.program_id(1)
    P, VD = MLA_QK_PAD, MLA_V_DIM

    def run(kv_pairs):
        def scores(hd):
            qh = q_ref[:, hd * P:(hd + 1) * P]
            return [_dot_nt(qh, k_ref[:, hd * P:(hd + 1) * P]) for k_ref, _ in kv_pairs]

        heads = []
        ss = scores(0)
        for hd in range(MLA_HEADS):
            nxt = scores(hd + 1) if hd + 1 < MLA_HEADS else None
            m = functools.reduce(jnp.maximum, [jnp.max(s, axis=-1, keepdims=True) for s in ss])
            pair = (hd // 2) * MXU_COLS
            acc = None
            for s, (_, v_ref) in zip(ss, kv_pairs):
                t = _dot(jnp.exp2(s - m).astype(BF16), v_ref[:, pair:pair + MXU_COLS])
                acc = t if acc is None else acc + t
            acc = acc[:, (hd % 2) * P:(hd % 2 + 1) * P]
            heads.append(acc[:, :VD] * (1.0 / acc[:, VD:]))
            ss = nxt
        o_ref[...] = jnp.concatenate(heads, axis=-1).astype(BF16)

    if not with_ctx:
        run([(kl_ref, vl_ref), (kx_ref, vx_ref)])
        return

    @pl.when(i < nq)
    def _():
        run([(kl_ref, vl_ref), (kx_ref, vx_ref)])

    @pl.when(i >= nq)
    def _():
        run([(kx_ref, vx_ref)])


def _mla_mixer(lay, h, mod, gain, w_down, q_norm_w, kv_norm_w, w_uq, w_ukv, with_ctx_out):
    D = h.shape[1]
    B, S, C = lay.B, lay.S, lay.C
    H, P, VD = MLA_HEADS, MLA_QK_PAD, MLA_V_DIM
    qk = MLA_NOPE + MLA_ROPE
    lora = MLA_Q_LORA + MLA_KV_LORA
    kr_cols = jnp.zeros((D, MXU_COLS), F32).at[:, MLA_NOPE:qk].set(w_down[:, lora:])
    wd = jnp.concatenate([w_down[:, :lora], kr_cols], axis=1).astype(BF16)
    wuq = jnp.pad(w_uq.reshape(MLA_Q_LORA, H, qk), ((0, 0), (0, 0), (0, P - qk))).reshape(MLA_Q_LORA, H * P)
    ukv = w_ukv.reshape(MLA_KV_LORA, H, MLA_NOPE + VD)
    wkn = jnp.pad(ukv[..., :MLA_NOPE], ((0, 0), (0, 0), (0, P - MLA_NOPE))).reshape(MLA_KV_LORA, H * P)
    wv = jnp.pad(ukv[..., MLA_NOPE:], ((0, 0), (0, 0), (0, P - VD))).reshape(MLA_KV_LORA, H * P)
    wukv = jnp.concatenate([wkn, wv], axis=1)

    tm = lay.tile(ROW_TILE)
    midx, pidx = lay.mod_index(tm), lay.pos_index(tm)
    cos, sa, sb = _rope_tables(S, MLA_ROPE, tm, MLA_NOPE)
    tab = pl.BlockSpec((tm, LANES), lambda t: (pidx(t), 0))
    full = lambda shape: pl.BlockSpec(shape, lambda t: (0, 0))
    q, k, v = pl.pallas_call(
        _mla_proj_kernel,
        grid=(lay.n_all // tm,),
        in_specs=[pl.BlockSpec((tm, D), lambda t: (t, 0)),
                  pl.BlockSpec((1, N_MOD, D), lambda t: (midx(t), 0, 0)),
                  full((1, D)), full((D, lora + MXU_COLS)), full((1, MLA_Q_LORA)), full((1, MLA_KV_LORA)),
                  full((MLA_Q_LORA, H * P)), full((MLA_KV_LORA, 2 * H * P)),
                  tab, tab, tab],
        out_specs=[pl.BlockSpec((tm, H * P), lambda t: (t, 0))] * 3,
        out_shape=[jax.ShapeDtypeStruct((lay.n_all, H * P), BF16)] * 3,
        compiler_params=_params("arbitrary"),
        name="mla_proj",
    )(h, mod, gain.reshape(1, D), wd, q_norm_w.reshape(1, -1), kv_norm_w.reshape(1, -1),
      wuq.astype(BF16), wukv.astype(BF16), cos, sa, sb)

    tq = _divisor(math.gcd(S, C), QUERY_TILE)
    q_index, nq, nc, lat, ctx = _attn_specs(lay, tq, [H * P, H * P])
    return pl.pallas_call(
        functools.partial(_mla_attn_kernel, nq=nq, with_ctx=with_ctx_out),
        grid=(B, nq + (nc if with_ctx_out else 0)),
        in_specs=[pl.BlockSpec((tq, H * P), lambda b, i: (q_index(b, i), 0)), *lat, *ctx],
        out_specs=pl.BlockSpec((tq, H * VD), lambda b, i: (q_index(b, i), 0)),
        out_shape=jax.ShapeDtypeStruct((lay.n_all if with_ctx_out else lay.n_lat, H * VD), BF16),
        compiler_params=_params("arbitrary", "arbitrary"),
        name="mla_attn",
    )(q, k, v, k, v)


def kernel(x, c, ctx, c_ctx, ada_w, ada_b, norm_w, final_norm_w, ffn_w_gate, ffn_w_up, ffn_w_down,
           gqa_w_in, gqa_w_out, gqa_sinks, diff_w_in, diff_w_out, diff_lambda, diff_subln_w,
           hgrn_w_in, hgrn_w_out, hgrn_norm_w, hgrn_lower_bounds,
           mla_w_down, mla_q_norm_w, mla_kv_norm_w, mla_w_uq, mla_w_ukv, mla_w_out):
    B, S, D = x.shape
    C = ctx.shape[1]
    lay = _Layout(B, S, C)
    depth = ada_w.shape[0]

    rows = -(-(B + 1) // 8) * 8
    cc = jnp.zeros((rows, D), F32).at[:B].set(c).at[B].set(c_ctx)
    mods = _ada_table(cc, ada_w, ada_b).reshape(depth, rows, N_MOD, D)

    stacks = (ffn_w_gate, ffn_w_up, ffn_w_down)
    weights = tuple(w[0, 0].astype(BF16) for w in stacks)
    h = (x.reshape(B * S, D), ctx.reshape(B * C, D))
    for i in range(depth):
        kind, j = i % 4, i // 4
        last = i == depth - 1
        mod = mods[i]
        h, weights = _ffn(lay, h, mod, norm_w[i, 0], final_norm_w, weights, 0, cast_next=(*stacks, i, 1))
        if kind == 0:
            o = _gqa_mixer(lay, h, mod, norm_w[i, 1], gqa_w_in[j], gqa_sinks[j], not last)
            mix = (_plain_readout, [(o, 0)], gqa_w_out[j])
        elif kind == 1:
            lambda_init = 0.8 - 0.6 * math.exp(-0.3 * i)
            o = _diff_mixer(lay, h, mod, norm_w[i, 1], diff_w_in[j], diff_lambda[j], diff_subln_w[j],
                            lambda_init, not last)
            mix = (_plain_readout, [(o, 0)], diff_w_out[j])
        elif kind == 2:
            readout, operands = _hgrn_mixer(lay, h, mod, norm_w[i, 1], hgrn_w_in[j], hgrn_norm_w[j],
                                            hgrn_lower_bounds, i)
            mix = (readout, operands, hgrn_w_out[j])
        else:
            o = _mla_mixer(lay, h, mod, norm_w[i, 1], mla_w_down[j], mla_q_norm_w[j], mla_kv_norm_w[j],
                           mla_w_uq[j], mla_w_ukv[j], not last)
            mix = (_plain_readout, [(o, 0)], mla_w_out[j])
        h, weights = _ffn(lay, h, mod, norm_w[i, 2], final_norm_w, weights, 1, lat_only=last, final_norm=last,
                          mix=mix, cast_next=None if last else (*stacks, i + 1, 0))
    return h[:B * S].reshape(B, S, D)
```

```python
import functools
import math

import jax
import jax.numpy as jnp
from jax import lax
from jax.experimental import pallas as pl
from jax.experimental.pallas import tpu as pltpu

F32 = jnp.float32
BF16 = jnp.bfloat16

D_MODEL = 1024
GRID_W = 64
N_MOD = 9
ROPE_BASE = 10000.0
EPS = 1e-6
NEG_INF = -1e30
D_FF = 2816

GQA_HEADS = 16
GQA_KV_HEADS = 4
GQA_HEAD_DIM = 64
GQA_WINDOW = 128
GQA_BLOCK = 128

DIFF_HEADS = 8
DIFF_HEAD_DIM = 64

HGRN_HEADS = 8
HGRN_KEY_DIM = 128
HGRN_VAL_DIM = D_MODEL // HGRN_HEADS
HGRN_CHUNK = 64

MLA_HEADS = 16
MLA_Q_LORA = 256
MLA_KV_LORA = 256
MLA_NOPE = 64
MLA_ROPE = 32
MLA_V_DIM = 64

LOG2_E = 1.4426950408889634
LANES = 128
SUBLANES = 8
MXU_COLS = 2 * LANES
VMEM_LIMIT = 56 * 1024 * 1024
ROW_TILE = 1024
NARROW_ROW_TILE = 512
QUERY_TILE = 256


def _dot(a, b):
    return jnp.dot(a, b, preferred_element_type=F32)


def _dot_nt(a, b):
    return lax.dot_general(a, b, (((1,), (1,)), ((), ())), preferred_element_type=F32)


def _params(*sem):
    return pltpu.CompilerParams(dimension_semantics=sem, vmem_limit_bytes=VMEM_LIMIT)


def _divisor(n, pref):
    t = min(n, pref)
    while n % t:
        t -= 8
    return t


class _Layout:
    def __init__(self, B, S, C):
        self.B, self.S, self.C = B, S, C
        self.n_lat = B * S
        self.n_all = B * S + B * C

    def tile(self, pref):
        return _divisor(math.gcd(self.S, self.B * self.C), pref)

    def mod_index(self, tm):
        n_lat_tiles, per_batch, B = self.n_lat // tm, self.S // tm, self.B
        return lambda t: jnp.where(t < n_lat_tiles, t // per_batch, B)

    def pos_index(self, tm):
        n_lat_tiles, per_batch = self.n_lat // tm, self.S // tm
        return lambda t: jnp.where(t < n_lat_tiles, t % per_batch, per_batch)


def _rmsnorm(x, w):
    return (x * lax.rsqrt(jnp.mean(x * x, axis=-1, keepdims=True) + EPS)) * w


def _pre(h, gain, m, off):
    inv = lax.rsqrt(jnp.mean(h * h, axis=-1, keepdims=True) + EPS)
    return (h * inv) * (gain * (1.0 + m[off + 1:off + 2])) + m[off:off + 1]


def _silu(x):
    return x * jax.nn.sigmoid(x)


def _chunked_proj(a, w_ref, emit):
    n = w_ref.shape[1] // MXU_COLS
    cur = _dot(a, w_ref[:, :MXU_COLS])
    for c in range(n):
        nxt = _dot(a, w_ref[:, (c + 1) * MXU_COLS:(c + 2) * MXU_COLS]) if c + 1 < n else None
        emit(c * MXU_COLS, cur)
        cur = nxt


def _rope(x, cos, sa, sb, shift):
    return x * cos + pltpu.roll(x, LANES - shift, 1) * sa + pltpu.roll(x, shift, 1) * sb


def _ada_kernel(c_ref, w_ref, b_ref, o_ref):
    sc = _silu(c_ref[...]).astype(BF16)
    o_ref[0] = _dot(sc, w_ref[0].astype(BF16)) + b_ref[0]


ADA_COLS = N_MOD * LANES


def _ada_table(cc, ada_w, ada_b):
    rows, D = cc.shape
    depth, _, n_out = ada_w.shape
    tn = ADA_COLS
    return pl.pallas_call(
        _ada_kernel,
        grid=(depth, n_out // tn),
        in_specs=[pl.BlockSpec((rows, D), lambda i, j: (0, 0)),
                  pl.BlockSpec((1, D, tn), lambda i, j: (i, 0, j)),
                  pl.BlockSpec((1, 1, tn), lambda i, j: (i, 0, j))],
        out_specs=pl.BlockSpec((1, rows, tn), lambda i, j: (i, 0, j)),
        out_shape=jax.ShapeDtypeStruct((depth, rows, n_out), F32),
        compiler_params=_params("arbitrary", "arbitrary"),
        name="ada_table",
    )(cc, ada_w, ada_b.reshape(depth, 1, n_out))


FFN_CHUNK = MXU_COLS


def _ffn_kernel(*refs, off, final_norm, n_lat_tiles, readout, n_mix, cast_next):
    refs = list(refs)
    if n_lat_tiles is None:
        h = refs.pop(0)[...]
    else:
        hx_ref, hc_ref = refs.pop(0), refs.pop(0)
        h = jnp.where(pl.program_id(0) < n_lat_tiles, hx_ref[...], hc_ref[...])
    mix_refs = [refs.pop(0) for _ in range(n_mix)]
    if readout is not None:
        wo_ref = refs.pop(0)
    mod_ref, gain_ref, fin_ref, wg_ref, wu_ref, wd_ref = refs[:6]
    if cast_next:
        for src, dst in zip(refs[6:9], refs[10:13]):
            dst[...] = src[...].astype(BF16)
    o_ref, a_ref = refs[9 if cast_next else 6], refs[-1]
    m = mod_ref[0]
    if readout is not None:
        h = h + m[5:6] * _dot(readout(*mix_refs), wo_ref[...])
    xm = _pre(h, gain_ref[...], m, off).astype(BF16)
    n = D_FF // FFN_CHUNK

    def gate_up(c):
        cols = slice(c * FFN_CHUNK, (c + 1) * FFN_CHUNK)
        return _dot(xm, wg_ref[:, cols]), _dot(xm, wu_ref[:, cols])

    cur = gate_up(0)
    for c in range(n):
        nxt = gate_up(c + 1) if c + 1 < n else None
        g, u = cur
        a_ref[:, c * FFN_CHUNK:(c + 1) * FFN_CHUNK] = (_silu(g) * u).astype(BF16)
        cur = nxt
    out = h + (0.5 * m[off + 2:off + 3]) * _dot(a_ref[...], wd_ref[...])
    if final_norm:
        out = _rmsnorm(out, fin_ref[...])
    o_ref[...] = out


def _slabs(rows, n_steps):
    n = n_steps
    while rows % n or (rows // n) % (2 * SUBLANES):
        n -= 1
    return n


def _ffn(lay, h, mod, gain, fin_w, weights, half, lat_only=False, final_norm=False, mix=None, cast_next=None):
    split = isinstance(h, tuple)
    wg, wu, wd = weights
    D = wg.shape[0]
    wide_operands = mix is not None and any(a.dtype == F32 and a.shape[0] > 1 for a, _ in mix[1])
    tm = lay.tile(NARROW_ROW_TILE if (split or wide_operands) else ROW_TILE)
    n_rows = lay.n_lat if lat_only else lay.n_all
    n_steps = n_rows // tm
    midx = lay.mod_index(tm)
    n_lat_tiles = lay.n_lat // tm
    const = lambda t: (0, 0)
    one = pl.Buffered(1)
    weight = lambda shape: pl.BlockSpec(shape, const, pipeline_mode=one)
    if split:
        h_specs = [pl.BlockSpec((tm, D), lambda t: (jnp.minimum(t, n_lat_tiles - 1), 0)),
                   pl.BlockSpec((tm, D), lambda t: (jnp.maximum(t - n_lat_tiles, 0), 0))]
        h_args = list(h)
    else:
        h_specs = [pl.BlockSpec((tm, D), lambda t: (t, 0))]
        h_args = [h]
    readout, operands = None, []
    if mix is not None:
        readout, operands, w_out = mix
        for arr, col in operands:
            if arr.shape[0] == 1:
                h_specs.append(pl.BlockSpec(arr.shape, const))
            else:
                h_specs.append(pl.BlockSpec((tm, D), lambda t, col=col: (t, col)))
            h_args.append(arr)
        h_specs.append(pl.BlockSpec(w_out.shape, const, pipeline_mode=one))
        h_args.append(w_out.astype(BF16))
    cast_specs, cast_args, cast_out_specs, cast_out_shapes = [], [], [], []
    if cast_next is not None:
        *stacks, nl, nh = cast_next
        for w32 in stacks:
            rows, cols = w32.shape[2:]
            n = _slabs(rows, n_steps)
            slab = rows // n
            cast_specs.append(pl.BlockSpec((None, None, slab, cols),
                                           lambda t, n=n: (nl, nh, jnp.minimum(t, n - 1), 0)))
            cast_args.append(w32)
            cast_out_specs.append(pl.BlockSpec((slab, cols), lambda t, n=n: (jnp.minimum(t, n - 1), 0)))
            cast_out_shapes.append(jax.ShapeDtypeStruct((rows, cols), BF16))
    outs = pl.pallas_call(
        functools.partial(_ffn_kernel, off=6 * half, final_norm=final_norm,
                          n_lat_tiles=n_lat_tiles if split else None, readout=readout, n_mix=len(operands),
                          cast_next=cast_next is not None),
        grid=(n_steps,),
        in_specs=h_specs + [pl.BlockSpec((1, N_MOD, D), lambda t: (midx(t), 0, 0)),
                            pl.BlockSpec((1, D), const),
                            pl.BlockSpec((1, D), const),
                            weight((D, D_FF)), weight((D, D_FF)), weight((D_FF, D))] + cast_specs,
        out_specs=[pl.BlockSpec((tm, D), lambda t: (t, 0))] + cast_out_specs,
        out_shape=[jax.ShapeDtypeStruct((n_rows, D), F32)] + cast_out_shapes,
        scratch_shapes=[pltpu.VMEM((tm, D_FF), BF16)],
        compiler_params=_params("arbitrary"),
        name="ffn",
    )(*h_args, mod, gain.reshape(1, D), fin_w.reshape(1, D), wg, wu, wd, *cast_args)
    return outs[0], (tuple(outs[1:]) if cast_next is not None else None)


def _rope_tables(S, rot_dim, pad_rows, lane_off):
    rows = S // GRID_W
    row = jnp.repeat(jnp.arange(rows, dtype=F32), GRID_W)
    col = jnp.tile(jnp.arange(GRID_W, dtype=F32), rows)
    axis_dim = rot_dim // 2
    inv_freq = ROPE_BASE ** (-jnp.arange(0, axis_dim, 2, dtype=F32) / axis_dim)
    ang_r = row[:, None] * inv_freq[None, :]
    ang_c = col[:, None] * inv_freq[None, :]
    ang = jnp.concatenate([ang_r, ang_r, ang_c, ang_c], axis=-1)
    cos, sin = jnp.cos(ang), jnp.sin(ang)
    f = rot_dim // 4
    first = (jnp.arange(rot_dim) % (2 * f)) < f
    sa = jnp.where(first[None, :], -sin, 0.0)
    sb = jnp.where(first[None, :], 0.0, sin)
    period = 64 if lane_off + rot_dim <= 64 else LANES

    def widen(t, fill):
        blk = jnp.full((S, period), fill, F32).at[:, lane_off:lane_off + rot_dim].set(t)
        blk = jnp.tile(blk, (1, LANES // period))
        return jnp.concatenate([blk, jnp.full((pad_rows, LANES), fill, F32)], axis=0)

    return widen(cos, 1.0), widen(sa, 0.0), widen(sb, 0.0)


def _attn_specs(lay, tq, widths):
    S, C, B = lay.S, lay.C, lay.B
    nq, nc = S // tq, C // tq
    lat_blocks = lay.n_lat // C

    def q_index(b, i):
        return jnp.where(i < nq, b * nq + i, B * nq + b * nc + (i - nq))

    lat = [pl.BlockSpec((S, w), lambda b, i: (b, 0)) for w in widths]
    ctx = [pl.BlockSpec((C, w), lambda b, i: (lat_blocks + b, 0)) for w in widths]
    return q_index, nq, nc, lat, ctx


def _gqa_proj_kernel(h_ref, mod_ref, gain_ref, w_ref, cos_ref, sa_ref, sb_ref, q_ref, k_ref, v_ref):
    qd, kd = GQA_HEADS * GQA_HEAD_DIM, GQA_KV_HEADS * GQA_HEAD_DIM
    a = _pre(h_ref[...], gain_ref[...], mod_ref[0], 3).astype(BF16)
    cos, sa, sb = cos_ref[...], sa_ref[...], sb_ref[...]
    f = GQA_HEAD_DIM // 4
    scale = GQA_HEAD_DIM ** -0.5 * LOG2_E
    lane = lax.broadcasted_iota(jnp.int32, (1, MXU_COLS), 1)
    ones_half = jnp.where((lane & (2 * GQA_HEAD_DIM - 1)) >= GQA_HEAD_DIM, 1.0, 0.0)

    def emit(col, tile):
        if col >= qd + kd:
            v_ref[:, col - qd - kd:col - qd - kd + MXU_COLS] = (tile + ones_half).astype(BF16)
            return
        for j in range(MXU_COLS // LANES):
            x = _rope(tile[:, j * LANES:(j + 1) * LANES], cos, sa, sb, f)
            lo = col + j * LANES
            if lo < qd:
                q_ref[:, lo:lo + LANES] = (x * scale).astype(BF16)
            else:
                k_ref[:, lo - qd:lo - qd + LANES] = x.astype(BF16)

    _chunked_proj(a, w_ref, emit)


def _gqa_attn_kernel(sink_ref, q_ref, kp_ref, kc_ref, kn_ref, vp_ref, vc_ref, vn_ref, kx_ref, vx_ref,
                     o_ref, kcat_ref, vcat_ref, *, nq):
    i = pl.program_id(1)
    blk = GQA_BLOCK
    G = GQA_HEADS // GQA_KV_HEADS
    HD = GQA_HEAD_DIM
    VW = 2 * HD
    span = 3 * blk

    def run(k_ref, v_ref, masks):
        kvs = range(GQA_KV_HEADS)
        qs = [jnp.concatenate([q_ref[:, (G * g + n) * HD:(G * g + n + 1) * HD] for n in range(G)], axis=0)
              for g in kvs]
        sinks = [jnp.concatenate([jnp.full((blk, 1), sink_ref[G * g + n] * LOG2_E, F32) for n in range(G)], axis=0)
                 for g in kvs]
        ss = [_dot_nt(qs[g], k_ref[:, g * HD:(g + 1) * HD]) for g in kvs]
        if masks is not None:
            prev_ok, next_ok = masks
            ss = [jnp.concatenate([jnp.where(prev_ok, s[:, :blk], NEG_INF), s[:, blk:2 * blk],
                                   jnp.where(next_ok, s[:, 2 * blk:span], NEG_INF), s[:, span:]], axis=1)
                  for s in ss]
        ms = [jnp.maximum(jnp.max(ss[g], axis=-1, keepdims=True), sinks[g]) for g in kvs]
        accs = [_dot(jnp.exp2(ss[g] - ms[g]).astype(BF16),
                     v_ref[:, (g // 2) * MXU_COLS:(g // 2 + 1) * MXU_COLS])[:, (g % 2) * VW:(g % 2 + 1) * VW]
                for g in kvs]
        outs = [accs[g][:, :HD] * (1.0 / (accs[g][:, HD:] + jnp.exp2(sinks[g] - ms[g]))) for g in kvs]
        heads = [outs[g][n * blk:(n + 1) * blk] for g in kvs for n in range(G)]
        o_ref[...] = jnp.concatenate(heads, axis=-1).astype(BF16)

    @pl.when(i < nq)
    def _():
        for n, (k_ref, v_ref) in enumerate([(kp_ref, vp_ref), (kc_ref, vc_ref), (kn_ref, vn_ref)]):
            kcat_ref[n * blk:(n + 1) * blk] = k_ref[...]
            vcat_ref[n * blk:(n + 1) * blk] = v_ref[...]
        kcat_ref[span:] = kx_ref[...]
        vcat_ref[span:] = vx_ref[...]
        row = lax.broadcasted_iota(jnp.int32, (G * blk, blk), 0) & (blk - 1)
        col = lax.broadcasted_iota(jnp.int32, (G * blk, blk), 1)
        prev_ok = (row + blk - col <= GQA_WINDOW) & (i >= 1)
        next_ok = (col + blk - row <= GQA_WINDOW) & (i + 1 < nq)
        run(kcat_ref, vcat_ref, (prev_ok, next_ok))

    @pl.when(i >= nq)
    def _():
        run(kx_ref, vx_ref, None)


def _gqa_mixer(lay, h, mod, gain, w_in, sinks, with_ctx_out):
    D = h.shape[1]
    B, S, C = lay.B, lay.S, lay.C
    qd, kd = GQA_HEADS * GQA_HEAD_DIM, GQA_KV_HEADS * GQA_HEAD_DIM
    w_v = jnp.pad(w_in[:, qd + kd:].reshape(D, GQA_KV_HEADS, GQA_HEAD_DIM), ((0, 0), (0, 0), (0, GQA_HEAD_DIM)))
    w_pad = jnp.concatenate([w_in[:, :qd + kd], w_v.reshape(D, 2 * kd)], axis=1)
    tm = lay.tile(ROW_TILE)
    midx, pidx = lay.mod_index(tm), lay.pos_index(tm)
    cos, sa, sb = _rope_tables(S, GQA_HEAD_DIM, tm, 0)
    tab = pl.BlockSpec((tm, LANES), lambda t: (pidx(t), 0))
    q, k, v = pl.pallas_call(
        _gqa_proj_kernel,
        grid=(lay.n_all // tm,),
        in_specs=[pl.BlockSpec((tm, D), lambda t: (t, 0)),
                  pl.BlockSpec((1, N_MOD, D), lambda t: (midx(t), 0, 0)),
                  pl.BlockSpec((1, D), lambda t: (0, 0)),
                  pl.BlockSpec((D, qd + 3 * kd), lambda t: (0, 0)),
                  tab, tab, tab],
        out_specs=[pl.BlockSpec((tm, qd), lambda t: (t, 0)),
                   pl.BlockSpec((tm, kd), lambda t: (t, 0)),
                   pl.BlockSpec((tm, 2 * kd), lambda t: (t, 0))],
        out_shape=[jax.ShapeDtypeStruct((lay.n_all, qd), BF16),
                   jax.ShapeDtypeStruct((lay.n_all, kd), BF16),
                   jax.ShapeDtypeStruct((lay.n_all, 2 * kd), BF16)],
        compiler_params=_params("arbitrary"),
        name="gqa_proj",
    )(h, mod, gain.reshape(1, D), w_pad.astype(BF16), cos, sa, sb)

    blk = GQA_BLOCK
    assert S % blk == 0 and C % blk == 0
    nq, nc = S // blk, C // blk
    n_i = nq + (nc if with_ctx_out else 0)
    lat_blocks = lay.n_lat // C

    def q_index(b, i):
        return jnp.where(i < nq, b * nq + i, B * nq + b * nc + (i - nq))

    def win(delta):
        return lambda b, i: (b * nq + jnp.clip(i + delta, 0, nq - 1), 0)

    def windows(width):
        return [pl.BlockSpec((blk, width), win(d)) for d in (-1, 0, 1)]

    def ctx_spec(width):
        return pl.BlockSpec((C, width), lambda b, i: (lat_blocks + b, 0))

    row = pl.BlockSpec((blk, qd), lambda b, i: (q_index(b, i), 0))
    return pl.pallas_call(
        functools.partial(_gqa_attn_kernel, nq=nq),
        grid=(B, n_i),
        in_specs=[pl.BlockSpec(memory_space=pltpu.SMEM), row,
                  *windows(kd), *windows(2 * kd), ctx_spec(kd), ctx_spec(2 * kd)],
        out_specs=row,
        out_shape=jax.ShapeDtypeStruct((lay.n_all if with_ctx_out else lay.n_lat, qd), BF16),
        scratch_shapes=[pltpu.VMEM((3 * blk + C, kd), BF16), pltpu.VMEM((3 * blk + C, 2 * kd), BF16)],
        compiler_params=_params("arbitrary", "arbitrary"),
        name="gqa_attn",
    )(sinks.astype(F32), q, k, k, k, v, v, v, k, v)


def _diff_proj_kernel(h_ref, mod_ref, gain_ref, w_ref, cos_ref, sa_ref, sb_ref, q_ref, k_ref, v_ref):
    qk = DIFF_HEADS * 2 * DIFF_HEAD_DIM
    a = _pre(h_ref[...], gain_ref[...], mod_ref[0], 3).astype(BF16)
    cos, sa, sb = cos_ref[...], sa_ref[...], sb_ref[...]
    f = DIFF_HEAD_DIM // 4
    scale = DIFF_HEAD_DIM ** -0.5 * LOG2_E
    vw = 2 * DIFF_HEAD_DIM
    ones = jnp.ones((a.shape[0], vw), BF16)

    def emit(col, tile):
        for j in range(MXU_COLS // LANES):
            lo = col + j * LANES
            x = tile[:, j * LANES:(j + 1) * LANES]
            if lo < qk:
                q_ref[:, lo:lo + LANES] = (_rope(x, cos, sa, sb, f) * scale).astype(BF16)
            elif lo < 2 * qk:
                k_ref[:, lo - qk:lo - qk + LANES] = _rope(x, cos, sa, sb, f).astype(BF16)
            else:
                hd = (lo - 2 * qk) // vw
                v_ref[:, 2 * hd * vw:(2 * hd + 1) * vw] = x.astype(BF16)
                v_ref[:, (2 * hd + 1) * vw:(2 * hd + 2) * vw] = ones

    _chunked_proj(a, w_ref, emit)


def _diff_attn_kernel(lam_ref, sub_ref, q_ref, kl_ref, vl_ref, kx_ref, vx_ref, o_ref, *, nq, lambda_init):
    i = pl.program_id(1)
    HD = DIFF_HEAD_DIM
    VW = 4 * HD
    lp = lam_ref[...]
    lam = (jnp.exp(jnp.sum(lp[0:1] * lp[1:2], keepdims=True))
           - jnp.exp(jnp.sum(lp[2:3] * lp[3:4], keepdims=True)) + lambda_init)

    def run(kv_pairs):
        def scores(h):
            los = [(2 * h + j) * HD for j in range(2)]
            return [[_dot_nt(q_ref[:, lo:lo + HD], k_ref[:, lo:lo + HD]) for k_ref, _ in kv_pairs] for lo in los]

        heads = []
        ss = scores(0)
        for h in range(DIFF_HEADS):
            nxt = scores(h + 1) if h + 1 < DIFF_HEADS else None
            maps = []
            for parts in ss:
                m = functools.reduce(jnp.maximum, [jnp.max(s, axis=-1, keepdims=True) for s in parts])
                acc = None
                for s, (_, v_ref) in zip(parts, kv_pairs):
                    t = _dot(jnp.exp2(s - m).astype(BF16), v_ref[:, h * VW:(h + 1) * VW])
                    acc = t if acc is None else acc + t
                maps.append(acc[:, :VW // 2] * (1.0 / acc[:, VW // 2:]))
            o = maps[0] - lam * maps[1]
            heads.append(_rmsnorm(o, sub_ref[...]) * (1.0 - lambda_init))
            ss = nxt
        o_ref[...] = jnp.concatenate(heads, axis=-1).astype(BF16)

    @pl.when(i < nq)
    def _():
        run([(kl_ref, vl_ref), (kx_ref, vx_ref)])

    @pl.when(i >= nq)
    def _():
        run([(kx_ref, vx_ref)])


def _diff_mixer(lay, h, mod, gain, w_in, lam_params, subln_w, lambda_init, with_ctx_out):
    D = h.shape[1]
    B, S, C = lay.B, lay.S, lay.C
    qk = DIFF_HEADS * 2 * DIFF_HEAD_DIM
    tm = lay.tile(ROW_TILE)
    midx, pidx = lay.mod_index(tm), lay.pos_index(tm)
    cos, sa, sb = _rope_tables(S, DIFF_HEAD_DIM, tm, 0)
    tab = pl.BlockSpec((tm, LANES), lambda t: (pidx(t), 0))
    out = jax.ShapeDtypeStruct((lay.n_all, qk), BF16)
    q, k, v = pl.pallas_call(
        _diff_proj_kernel,
        grid=(lay.n_all // tm,),
        in_specs=[pl.BlockSpec((tm, D), lambda t: (t, 0)),
                  pl.BlockSpec((1, N_MOD, D), lambda t: (midx(t), 0, 0)),
                  pl.BlockSpec((1, D), lambda t: (0, 0)),
                  pl.BlockSpec((D, 3 * qk), lambda t: (0, 0)),
                  tab, tab, tab],
        out_specs=[pl.BlockSpec((tm, qk), lambda t: (t, 0))] * 2 + [pl.BlockSpec((tm, 2 * qk), lambda t: (t, 0))],
        out_shape=[out, out, jax.ShapeDtypeStruct((lay.n_all, 2 * qk), BF16)],
        compiler_params=_params("arbitrary"),
        name="diff_proj",
    )(h, mod, gain.reshape(1, D), w_in.astype(BF16), cos, sa, sb)

    tq = _divisor(math.gcd(S, C), QUERY_TILE)
    q_index, nq, nc, lat, ctx = _attn_specs(lay, tq, [qk, 2 * qk])
    row = pl.BlockSpec((tq, qk), lambda b, i: (q_index(b, i), 0))
    return pl.pallas_call(
        functools.partial(_diff_attn_kernel, nq=nq, lambda_init=lambda_init),
        grid=(B, nq + (nc if with_ctx_out else 0)),
        in_specs=[pl.BlockSpec((4, DIFF_HEAD_DIM), lambda b, i: (0, 0)),
                  pl.BlockSpec((1, 2 * DIFF_HEAD_DIM), lambda b, i: (0, 0)),
                  row, *lat, *ctx],
        out_specs=row,
        out_shape=jax.ShapeDtypeStruct((lay.n_all if with_ctx_out else lay.n_lat, qk), BF16),
        compiler_params=_params("arbitrary", "arbitrary"),
        name="diff_attn",
    )(lam_params.astype(F32), subln_w.reshape(1, -1), q, k, v, k, v)


def _hgrn_proj_kernel(h_ref, mod_ref, gain_ref, lb_ref, w_ref, o_ref, *, layer):
    KD = HGRN_HEADS * HGRN_KEY_DIM
    W = 2 * LANES
    a = _pre(h_ref[...], gain_ref[...], mod_ref[0], 3).astype(BF16)
    raw = lb_ref[...]
    e = jnp.exp(raw - jnp.max(raw, axis=0, keepdims=True))
    soft = e * (1.0 / jnp.sum(e, axis=0, keepdims=True))
    lb = jnp.sum(soft[1:layer + 1], axis=0, keepdims=True) if layer else jnp.zeros_like(soft[0:1])
    n = w_ref.shape[1] // W

    def proj(c):
        return _dot(a, w_ref[:, c * W:(c + 1) * W])

    cur = proj(0)
    for c in range(n):
        nxt = proj(c + 1) if c + 1 < n else None
        group, col = divmod(c * W, KD)
        if group in (0, 4):
            out = _silu(cur)
        elif group in (1, 2):
            lbc = lb[:, col:col + W]
            out = lbc + (1.0 - lbc) * jax.nn.sigmoid(cur)
        else:
            out = cur
        o_ref[:, c * W:(c + 1) * W] = out
        cur = nxt


def _running_sum(x, reverse):
    n, w = x.shape
    n_groups = n // SUBLANES
    x = x.reshape(n_groups, SUBLANES, w)
    within = lax.broadcasted_iota(jnp.int32, (1, SUBLANES, 1), 1)
    step = 1
    while step < SUBLANES:
        if reverse:
            x = x + jnp.where(within < SUBLANES - step, pltpu.roll(x, SUBLANES - step, 1), 0.0)
        else:
            x = x + jnp.where(within >= step, pltpu.roll(x, step, 1), 0.0)
        step *= 2
    out = [None] * n_groups
    carry = None
    for j in (reversed(range(n_groups)) if reverse else range(n_groups)):
        out[j] = x[j] if carry is None else x[j] + carry
        carry = out[j][0:1] if reverse else out[j][SUBLANES - 1:SUBLANES]
    return jnp.concatenate(out, axis=0)


def _hgrn_scan_kernel(qf_ref, ff_ref, vf_ref, qb_ref, fb_ref, vb_ref, of_ref, ob_ref, st_ref, *, n_chunks):
    step = pl.program_id(1)
    C = HGRN_CHUNK
    DK, DV, H = HGRN_KEY_DIM, HGRN_VAL_DIM, HGRN_HEADS

    @pl.when(step == 0)
    def _():
        st_ref[...] = jnp.zeros_like(st_ref)

    r = lax.broadcasted_iota(jnp.int32, (C, C), 0)
    c = lax.broadcasted_iota(jnp.int32, (C, C), 1)
    ks = [slice(h * DK, (h + 1) * DK) for h in range(H)]
    vs = [slice(h * DV, (h + 1) * DV) for h in range(H)]
    dirs = [(qf_ref, ff_ref, vf_ref, of_ref, c <= r, C // 2, C - 1),
            (qb_ref, fb_ref, vb_ref, ob_ref, c >= r, C - 1 - C // 2, 0)]

    def prepare(d, n):
        q_ref, f_ref, v_ref, _, tri, mid_row, last_row = dirs[d]
        base = (n if d == 0 else n_chunks - 1 - n) * C
        q = q_ref[base:base + C, :]
        f = f_ref[base:base + C, :]
        vt = v_ref[base:base + C, :].T.astype(BF16)
        k = 1.0 - f
        g = _running_sum(jnp.log2(f), reverse=d == 1)
        g_mid = g[mid_row:mid_row + 1]
        g_last = g[last_row:last_row + 1]
        t = g - g_mid
        qa_f = q * jnp.exp2(t)
        kb_f = k * jnp.exp2(-t)
        qa = qa_f.astype(BF16)
        kb = kb_f.astype(BF16)
        qg = (qa_f * jnp.exp2(g_mid)).astype(BF16)
        kd = (kb_f * jnp.exp2(g_last - g_mid)).astype(BF16)
        a = [jnp.where(tri, _dot_nt(qa[:, ks[h]], kb[:, ks[h]]), 0.0).astype(BF16) for h in range(H)]
        lhs = [jnp.concatenate([qg[:, ks[h]], a[h]], axis=1) for h in range(H)]
        update = [_dot(vt[vs[h], :], kd[:, ks[h]]) for h in range(H)]
        return base, lhs, vt, jnp.exp2(g_last), update

    cur = [prepare(d, 0) for d in range(2)]
    for n in range(n_chunks):
        nxt = [prepare(d, n + 1) for d in range(2)] if n + 1 < n_chunks else None
        for d in range(2):
            base, lhs, vt, decay, update = cur[d]
            st = [st_ref[d, vs[h], :] for h in range(H)]
            outs = [_dot_nt(lhs[h], jnp.concatenate([st[h].astype(BF16), vt[vs[h], :]], axis=1)) for h in range(H)]
            for h in range(H):
                st_ref[d, vs[h], :] = st[h] * decay[:, ks[h]] + update[h]
            dirs[d][3][base:base + C, :] = jnp.concatenate(outs, axis=-1)
        cur = nxt


def _hgrn_readout(of_ref, ob_ref, gate_ref, nw_ref):
    DV = HGRN_VAL_DIM
    o = of_ref[...] + ob_ref[...]
    gate = gate_ref[...]
    nw = nw_ref[...]
    ys = [_rmsnorm(o[:, h * DV:(h + 1) * DV], nw) * gate[:, h * DV:(h + 1) * DV] for h in range(HGRN_HEADS)]
    return jnp.concatenate(ys, axis=-1).astype(BF16)


def _plain_readout(o_ref):
    return o_ref[...]


def _hgrn_mixer(lay, h, mod, gain, w_in, norm_w, lower_bounds, layer):
    D = h.shape[1]
    B, S, C = lay.B, lay.S, lay.C
    KD = HGRN_HEADS * HGRN_KEY_DIM
    tm = lay.tile(NARROW_ROW_TILE)
    midx = lay.mod_index(tm)
    n_cols = w_in.shape[1] // KD
    proj = pl.pallas_call(
        functools.partial(_hgrn_proj_kernel, layer=layer),
        grid=(lay.n_all // tm,),
        in_specs=[pl.BlockSpec((tm, D), lambda t: (t, 0)),
                  pl.BlockSpec((1, N_MOD, D), lambda t: (midx(t), 0, 0)),
                  pl.BlockSpec((1, D), lambda t: (0, 0)),
                  pl.BlockSpec(lower_bounds.shape, lambda t: (0, 0)),
                  pl.BlockSpec((D, n_cols * KD), lambda t: (0, 0), pipeline_mode=pl.Buffered(1))],
        out_specs=pl.BlockSpec((tm, n_cols * KD), lambda t: (t, 0)),
        out_shape=jax.ShapeDtypeStruct((lay.n_all, n_cols * KD), F32),
        compiler_params=_params("arbitrary"),
        name="hgrn_proj",
    )(h, mod, gain.reshape(1, D), lower_bounds.astype(F32), w_in.astype(BF16))

    ts = C
    n_lat_steps = S // ts
    lat_blocks = lay.n_lat // ts

    def rows(d):
        def index(b, s):
            lat = b * n_lat_steps + (s - 1 if d == 0 else n_lat_steps - s)
            return jnp.where(s == 0, lat_blocks + b, lat)
        return index

    def col(d, j):
        return pl.BlockSpec((ts, KD), lambda b, s: (rows(d)(b, s), j))

    o_f, o_b = pl.pallas_call(
        functools.partial(_hgrn_scan_kernel, n_chunks=ts // HGRN_CHUNK),
        grid=(B, 1 + n_lat_steps),
        in_specs=[col(0, 0), col(0, 1), col(0, 3), col(1, 0), col(1, 2), col(1, 3)],
        out_specs=[col(0, 0), col(1, 0)],
        out_shape=[jax.ShapeDtypeStruct((lay.n_all, KD), F32)] * 2,
        scratch_shapes=[pltpu.VMEM((2, HGRN_HEADS * HGRN_VAL_DIM, HGRN_KEY_DIM), F32)],
        compiler_params=_params("arbitrary", "arbitrary"),
        name="hgrn_scan",
    )(proj, proj, proj, proj, proj, proj)

    return _hgrn_readout, [(o_f, 0), (o_b, 0), (proj, 4), (norm_w.reshape(1, -1), 0)]


MLA_QK_PAD = LANES


def _mla_proj_kernel(h_ref, mod_ref, gain_ref, wd_ref, qn_ref, kvn_ref, wuq_ref, wukv_ref,
                     cos_ref, sa_ref, sb_ref, q_ref, k_ref, v_ref):
    H, P = MLA_HEADS, MLA_QK_PAD
    a = _pre(h_ref[...], gain_ref[...], mod_ref[0], 3).astype(BF16)
    dn = _dot(a, wd_ref[...])
    cos, sa, sb = cos_ref[...], sa_ref[...], sb_ref[...]
    f = MLA_ROPE // 4
    cq = _rmsnorm(dn[:, :MLA_Q_LORA], qn_ref[...]).astype(BF16)
    ckv = _rmsnorm(dn[:, MLA_Q_LORA:MLA_Q_LORA + MLA_KV_LORA], kvn_ref[...]).astype(BF16)
    kr = _rope(dn[:, MLA_Q_LORA + MLA_KV_LORA:MLA_Q_LORA + MLA_KV_LORA + P], cos, sa, sb, f)
    lane = lax.broadcasted_iota(jnp.int32, (1, P), 1)
    ones_half = jnp.where(lane >= MLA_V_DIM, 1.0, 0.0)

    scale = (MLA_NOPE + MLA_ROPE) ** -0.5 * LOG2_E

    def emit_q(col, tile):
        for j in range(MXU_COLS // P):
            lo = col + j * P
            q_ref[:, lo:lo + P] = (_rope(tile[:, j * P:(j + 1) * P], cos, sa, sb, f) * scale).astype(BF16)

    def emit_kv(col, tile):
        for j in range(MXU_COLS // P):
            lo = col + j * P
            x = tile[:, j * P:(j + 1) * P]
            if lo < H * P:
                k_ref[:, lo:lo + P] = (x + kr).astype(BF16)
            else:
                v_ref[:, lo - H * P:lo - H * P + P] = (x + ones_half).astype(BF16)

    _chunked_proj(cq, wuq_ref, emit_q)
    _chunked_proj(ckv, wukv_ref, emit_kv)


def _mla_attn_kernel(q_ref, kl_ref, vl_ref, kx_ref, vx_ref, o_ref, *, nq, with_ctx):
    i = pl.program_id(1)
    P, VD = MLA_QK_PAD, MLA_V_DIM

    def run(kv_pairs):
        def scores(hd):
            qh = q_ref[:, hd * P:(hd + 1) * P]
            return [_dot_nt(qh, k_ref[:, hd * P:(hd + 1) * P]) for k_ref, _ in kv_pairs]

        heads = []
        ss = scores(0)
        for hd in range(MLA_HEADS):
            nxt = scores(hd + 1) if hd + 1 < MLA_HEADS else None
            m = functools.reduce(jnp.maximum, [jnp.max(s, axis=-1, keepdims=True) for s in ss])
            pair = (hd // 2) * MXU_COLS
            acc = None
            for s, (_, v_ref) in zip(ss, kv_pairs):
                t = _dot(jnp.exp2(s - m).astype(BF16), v_ref[:, pair:pair + MXU_COLS])
                acc = t if acc is None else acc + t
            acc = acc[:, (hd % 2) * P:(hd % 2 + 1) * P]
            heads.append(acc[:, :VD] * (1.0 / acc[:, VD:]))
            ss = nxt
        o_ref[...] = jnp.concatenate(heads, axis=-1).astype(BF16)

    if not with_ctx:
        run([(kl_ref, vl_ref), (kx_ref, vx_ref)])
        return

    @pl.when(i < nq)
    def _():
        run([(kl_ref, vl_ref), (kx_ref, vx_ref)])

    @pl.when(i >= nq)
    def _():
        run([(kx_ref, vx_ref)])


def _mla_mixer(lay, h, mod, gain, w_down, q_norm_w, kv_norm_w, w_uq, w_ukv, with_ctx_out):
    D = h.shape[1]
    B, S, C = lay.B, lay.S, lay.C
    H, P, VD = MLA_HEADS, MLA_QK_PAD, MLA_V_DIM
    qk = MLA_NOPE + MLA_ROPE
    lora = MLA_Q_LORA + MLA_KV_LORA
    kr_cols = jnp.zeros((D, MXU_COLS), F32).at[:, MLA_NOPE:qk].set(w_down[:, lora:])
    wd = jnp.concatenate([w_down[:, :lora], kr_cols], axis=1).astype(BF16)
    wuq = jnp.pad(w_uq.reshape(MLA_Q_LORA, H, qk), ((0, 0), (0, 0), (0, P - qk))).reshape(MLA_Q_LORA, H * P)
    ukv = w_ukv.reshape(MLA_KV_LORA, H, MLA_NOPE + VD)
    wkn = jnp.pad(ukv[..., :MLA_NOPE], ((0, 0), (0, 0), (0, P - MLA_NOPE))).reshape(MLA_KV_LORA, H * P)
    wv = jnp.pad(ukv[..., MLA_NOPE:], ((0, 0), (0, 0), (0, P - VD))).reshape(MLA_KV_LORA, H * P)
    wukv = jnp.concatenate([wkn, wv], axis=1)

    tm = lay.tile(ROW_TILE)
    midx, pidx = lay.mod_index(tm), lay.pos_index(tm)
    cos, sa, sb = _rope_tables(S, MLA_ROPE, tm, MLA_NOPE)
    tab = pl.BlockSpec((tm, LANES), lambda t: (pidx(t), 0))
    full = lambda shape: pl.BlockSpec(shape, lambda t: (0, 0))
    q, k, v = pl.pallas_call(
        _mla_proj_kernel,
        grid=(lay.n_all // tm,),
        in_specs=[pl.BlockSpec((tm, D), lambda t: (t, 0)),
                  pl.BlockSpec((1, N_MOD, D), lambda t: (midx(t), 0, 0)),
                  full((1, D)), full((D, lora + MXU_COLS)), full((1, MLA_Q_LORA)), full((1, MLA_KV_LORA)),
                  full((MLA_Q_LORA, H * P)), full((MLA_KV_LORA, 2 * H * P)),
                  tab, tab, tab],
        out_specs=[pl.BlockSpec((tm, H * P), lambda t: (t, 0))] * 3,
        out_shape=[jax.ShapeDtypeStruct((lay.n_all, H * P), BF16)] * 3,
        compiler_params=_params("arbitrary"),
        name="mla_proj",
    )(h, mod, gain.reshape(1, D), wd, q_norm_w.reshape(1, -1), kv_norm_w.reshape(1, -1),
      wuq.astype(BF16), wukv.astype(BF16), cos, sa, sb)

    tq = _divisor(math.gcd(S, C), QUERY_TILE)
    q_index, nq, nc, lat, ctx = _attn_specs(lay, tq, [H * P, H * P])
    return pl.pallas_call(
        functools.partial(_mla_attn_kernel, nq=nq, with_ctx=with_ctx_out),
        grid=(B, nq + (nc if with_ctx_out else 0)),
        in_specs=[pl.BlockSpec((tq, H * P), lambda b, i: (q_index(b, i), 0)), *lat, *ctx],
        out_specs=pl.BlockSpec((tq, H * VD), lambda b, i: (q_index(b, i), 0)),
        out_shape=jax.ShapeDtypeStruct((lay.n_all if with_ctx_out else lay.n_lat, H * VD), BF16),
        compiler_params=_params("arbitrary", "arbitrary"),
        name="mla_attn",
    )(q, k, v, k, v)


def kernel(x, c, ctx, c_ctx, ada_w, ada_b, norm_w, final_norm_w, ffn_w_gate, ffn_w_up, ffn_w_down,
           gqa_w_in, gqa_w_out, gqa_sinks, diff_w_in, diff_w_out, diff_lambda, diff_subln_w,
           hgrn_w_in, hgrn_w_out, hgrn_norm_w, hgrn_lower_bounds,
           mla_w_down, mla_q_norm_w, mla_kv_norm_w, mla_w_uq, mla_w_ukv, mla_w_out):
    B, S, D = x.shape
    C = ctx.shape[1]
    lay = _Layout(B, S, C)
    depth = ada_w.shape[0]

    rows = -(-(B + 1) // 8) * 8
    cc = jnp.zeros((rows, D), F32).at[:B].set(c).at[B].set(c_ctx)
    mods = _ada_table(cc, ada_w, ada_b).reshape(depth, rows, N_MOD, D)

    stacks = (ffn_w_gate, ffn_w_up, ffn_w_down)
    weights = tuple(w[0, 0].astype(BF16) for w in stacks)
    h = (x.reshape(B * S, D), ctx.reshape(B * C, D))
    for i in range(depth):
        kind, j = i % 4, i // 4
        last = i == depth - 1
        mod = mods[i]
        h, weights = _ffn(lay, h, mod, norm_w[i, 0], final_norm_w, weights, 0, cast_next=(*stacks, i, 1))
        if kind == 0:
            o = _gqa_mixer(lay, h, mod, norm_w[i, 1], gqa_w_in[j], gqa_sinks[j], not last)
            mix = (_plain_readout, [(o, 0)], gqa_w_out[j])
        elif kind == 1:
            lambda_init = 0.8 - 0.6 * math.exp(-0.3 * i)
            o = _diff_mixer(lay, h, mod, norm_w[i, 1], diff_w_in[j], diff_lambda[j], diff_subln_w[j],
                            lambda_init, not last)
            mix = (_plain_readout, [(o, 0)], diff_w_out[j])
        elif kind == 2:
            readout, operands = _hgrn_mixer(lay, h, mod, norm_w[i, 1], hgrn_w_in[j], hgrn_norm_w[j],
                                            hgrn_lower_bounds, i)
            mix = (readout, operands, hgrn_w_out[j])
        else:
            o = _mla_mixer(lay, h, mod, norm_w[i, 1], mla_w_down[j], mla_q_norm_w[j], mla_kv_norm_w[j],
                           mla_w_uq[j], mla_w_ukv[j], not last)
            mix = (_plain_readout, [(o, 0)], mla_w_out[j])
        h, weights = _ffn(lay, h, mod, norm_w[i, 2], final_norm_w, weights, 1, lat_only=last, final_norm=last,
                          mix=mix, cast_next=None if last else (*stacks, i + 1, 0))
    return h[:B * S].reshape(B, S, D)
```

```python
import functools
import math

import jax
import jax.numpy as jnp
from jax import lax
from jax.experimental import pallas as pl
from jax.experimental.pallas import tpu as pltpu

F32 = jnp.float32
BF16 = jnp.bfloat16

D_MODEL = 1024
GRID_W = 64
N_MOD = 9
ROPE_BASE = 10000.0
EPS = 1e-6
NEG_INF = -1e30
D_FF = 2816

GQA_HEADS = 16
GQA_KV_HEADS = 4
GQA_HEAD_DIM = 64
GQA_WINDOW = 128
GQA_BLOCK = 128

DIFF_HEADS = 8
DIFF_HEAD_DIM = 64

HGRN_HEADS = 8
HGRN_KEY_DIM = 128
HGRN_VAL_DIM = D_MODEL // HGRN_HEADS
HGRN_CHUNK = 64

MLA_HEADS = 16
MLA_Q_LORA = 256
MLA_KV_LORA = 256
MLA_NOPE = 64
MLA_ROPE = 32
MLA_V_DIM = 64

LOG2_E = 1.4426950408889634
LANES = 128
SUBLANES = 8
MXU_COLS = 2 * LANES
VMEM_LIMIT = 56 * 1024 * 1024
ROW_TILE = 1024
NARROW_ROW_TILE = 512
QUERY_TILE = 256


def _dot(a, b):
    return jnp.dot(a, b, preferred_element_type=F32)


def _dot_nt(a, b):
    return lax.dot_general(a, b, (((1,), (1,)), ((), ())), preferred_element_type=F32)


def _params(*sem):
    return pltpu.CompilerParams(dimension_semantics=sem, vmem_limit_bytes=VMEM_LIMIT)


def _divisor(n, pref):
    t = min(n, pref)
    while n % t:
        t -= 8
    return t


class _Layout:
    def __init__(self, B, S, C):
        self.B, self.S, self.C = B, S, C
        self.n_lat = B * S
        self.n_all = B * S + B * C

    def tile(self, pref):
        return _divisor(math.gcd(self.S, self.B * self.C), pref)

    def mod_index(self, tm):
        n_lat_tiles, per_batch, B = self.n_lat // tm, self.S // tm, self.B
        return lambda t: jnp.where(t < n_lat_tiles, t // per_batch, B)

    def pos_index(self, tm):
        n_lat_tiles, per_batch = self.n_lat // tm, self.S // tm
        return lambda t: jnp.where(t < n_lat_tiles, t % per_batch, per_batch)


def _rmsnorm(x, w):
    return (x * lax.rsqrt(jnp.mean(x * x, axis=-1, keepdims=True) + EPS)) * w


def _pre(h, gain, m, off):
    inv = lax.rsqrt(jnp.mean(h * h, axis=-1, keepdims=True) + EPS)
    return (h * inv) * (gain * (1.0 + m[off + 1:off + 2])) + m[off:off + 1]


def _silu(x):
    return x * jax.nn.sigmoid(x)


def _chunked_proj(a, w_ref, emit):
    n = w_ref.shape[1] // MXU_COLS
    cur = _dot(a, w_ref[:, :MXU_COLS])
    for c in range(n):
        nxt = _dot(a, w_ref[:, (c + 1) * MXU_COLS:(c + 2) * MXU_COLS]) if c + 1 < n else None
        emit(c * MXU_COLS, cur)
        cur = nxt


def _rope(x, cos, sin, shift):
    lane = lax.broadcasted_iota(jnp.int32, (1, LANES), 1)
    first = (lane & (2 * shift - 1)) < shift
    partner = jnp.where(first, pltpu.roll(x, LANES - shift, 1), pltpu.roll(x, shift, 1))
    return x * cos + partner * sin


def _ada_kernel(c_ref, w_ref, b_ref, o_ref):
    sc = _silu(c_ref[...]).astype(BF16)
    o_ref[0] = _dot(sc, w_ref[0].astype(BF16)) + b_ref[0]


ADA_COLS = N_MOD * LANES


def _ada_table(cc, ada_w, ada_b):
    rows, D = cc.shape
    depth, _, n_out = ada_w.shape
    tn = ADA_COLS
    return pl.pallas_call(
        _ada_kernel,
        grid=(depth, n_out // tn),
        in_specs=[pl.BlockSpec((rows, D), lambda i, j: (0, 0)),
                  pl.BlockSpec((1, D, tn), lambda i, j: (i, 0, j)),
                  pl.BlockSpec((1, 1, tn), lambda i, j: (i, 0, j))],
        out_specs=pl.BlockSpec((1, rows, tn), lambda i, j: (i, 0, j)),
        out_shape=jax.ShapeDtypeStruct((depth, rows, n_out), F32),
        compiler_params=_params("arbitrary", "arbitrary"),
        name="ada_table",
    )(cc, ada_w, ada_b.reshape(depth, 1, n_out))


FFN_CHUNK = MXU_COLS


def _ffn_kernel(*refs, off, final_norm, n_lat_tiles, readout, n_mix, cast_next):
    refs = list(refs)
    if n_lat_tiles is None:
        h = refs.pop(0)[...]
    else:
        hx_ref, hc_ref = refs.pop(0), refs.pop(0)
        h = jnp.where(pl.program_id(0) < n_lat_tiles, hx_ref[...], hc_ref[...])
    mix_refs = [refs.pop(0) for _ in range(n_mix)]
    if readout is not None:
        wo_ref = refs.pop(0)
    mod_ref, gain_ref, fin_ref, wg_ref, wu_ref, wd_ref = refs[:6]
    if cast_next:
        for src, dst in zip(refs[6:9], refs[10:13]):
            dst[...] = src[...].astype(BF16)
    o_ref, a_ref = refs[9 if cast_next else 6], refs[-1]
    m = mod_ref[0]
    if readout is not None:
        h = h + m[5:6] * _dot(readout(*mix_refs), wo_ref[...])
    xm = _pre(h, gain_ref[...], m, off).astype(BF16)
    n = D_FF // FFN_CHUNK

    def gate_up(c):
        cols = slice(c * FFN_CHUNK, (c + 1) * FFN_CHUNK)
        return _dot(xm, wg_ref[:, cols]), _dot(xm, wu_ref[:, cols])

    cur = gate_up(0)
    for c in range(n):
        nxt = gate_up(c + 1) if c + 1 < n else None
        g, u = cur
        a_ref[:, c * FFN_CHUNK:(c + 1) * FFN_CHUNK] = (_silu(g) * u).astype(BF16)
        cur = nxt
    out = h + (0.5 * m[off + 2:off + 3]) * _dot(a_ref[...], wd_ref[...])
    if final_norm:
        out = _rmsnorm(out, fin_ref[...])
    o_ref[...] = out


def _slabs(rows, n_steps):
    n = n_steps
    while rows % n or (rows // n) % (2 * SUBLANES):
        n -= 1
    return n


def _ffn(lay, h, mod, gain, fin_w, weights, half, lat_only=False, final_norm=False, mix=None, cast_next=None):
    split = isinstance(h, tuple)
    wg, wu, wd = weights
    D = wg.shape[0]
    wide_operands = mix is not None and any(a.dtype == F32 and a.shape[0] > 1 for a, _ in mix[1])
    tm = lay.tile(NARROW_ROW_TILE if (split or wide_operands) else ROW_TILE)
    n_rows = lay.n_lat if lat_only else lay.n_all
    n_steps = n_rows // tm
    midx = lay.mod_index(tm)
    n_lat_tiles = lay.n_lat // tm
    const = lambda t: (0, 0)
    one = pl.Buffered(1)
    weight = lambda shape: pl.BlockSpec(shape, const, pipeline_mode=one)
    if split:
        h_specs = [pl.BlockSpec((tm, D), lambda t: (jnp.minimum(t, n_lat_tiles - 1), 0)),
                   pl.BlockSpec((tm, D), lambda t: (jnp.maximum(t - n_lat_tiles, 0), 0))]
        h_args = list(h)
    else:
        h_specs = [pl.BlockSpec((tm, D), lambda t: (t, 0))]
        h_args = [h]
    readout, operands = None, []
    if mix is not None:
        readout, operands, w_out = mix
        for arr, col in operands:
            if arr.shape[0] == 1:
                h_specs.append(pl.BlockSpec(arr.shape, const))
            else:
                h_specs.append(pl.BlockSpec((tm, D), lambda t, col=col: (t, col)))
            h_args.append(arr)
        h_specs.append(pl.BlockSpec(w_out.shape, const, pipeline_mode=one))
        h_args.append(w_out.astype(BF16))
    cast_specs, cast_args, cast_out_specs, cast_out_shapes = [], [], [], []
    if cast_next is not None:
        *stacks, nl, nh = cast_next
        for w32 in stacks:
            rows, cols = w32.shape[2:]
            n = _slabs(rows, n_steps)
            slab = rows // n
            cast_specs.append(pl.BlockSpec((None, None, slab, cols),
                                           lambda t, n=n: (nl, nh, jnp.minimum(t, n - 1), 0)))
            cast_args.append(w32)
            cast_out_specs.append(pl.BlockSpec((slab, cols), lambda t, n=n: (jnp.minimum(t, n - 1), 0)))
            cast_out_shapes.append(jax.ShapeDtypeStruct((rows, cols), BF16))
    outs = pl.pallas_call(
        functools.partial(_ffn_kernel, off=6 * half, final_norm=final_norm,
                          n_lat_tiles=n_lat_tiles if split else None, readout=readout, n_mix=len(operands),
                          cast_next=cast_next is not None),
        grid=(n_steps,),
        in_specs=h_specs + [pl.BlockSpec((1, N_MOD, D), lambda t: (midx(t), 0, 0)),
                            pl.BlockSpec((1, D), const),
                            pl.BlockSpec((1, D), const),
                            weight((D, D_FF)), weight((D, D_FF)), weight((D_FF, D))] + cast_specs,
        out_specs=[pl.BlockSpec((tm, D), lambda t: (t, 0))] + cast_out_specs,
        out_shape=[jax.ShapeDtypeStruct((n_rows, D), F32)] + cast_out_shapes,
        scratch_shapes=[pltpu.VMEM((tm, D_FF), BF16)],
        compiler_params=_params("arbitrary"),
        name="ffn",
    )(*h_args, mod, gain.reshape(1, D), fin_w.reshape(1, D), wg, wu, wd, *cast_args)
    return outs[0], (tuple(outs[1:]) if cast_next is not None else None)


def _rope_tables(S, rot_dim, pad_rows, lane_off):
    rows = S // GRID_W
    row = jnp.repeat(jnp.arange(rows, dtype=F32), GRID_W)
    col = jnp.tile(jnp.arange(GRID_W, dtype=F32), rows)
    axis_dim = rot_dim // 2
    inv_freq = ROPE_BASE ** (-jnp.arange(0, axis_dim, 2, dtype=F32) / axis_dim)
    ang_r = row[:, None] * inv_freq[None, :]
    ang_c = col[:, None] * inv_freq[None, :]
    ang = jnp.concatenate([ang_r, ang_r, ang_c, ang_c], axis=-1)
    cos, sin = jnp.cos(ang), jnp.sin(ang)
    f = rot_dim // 4
    first = (jnp.arange(rot_dim) % (2 * f)) < f
    sa = jnp.where(first[None, :], -sin, 0.0)
    sb = jnp.where(first[None, :], 0.0, sin)
    period = 64 if lane_off + rot_dim <= 64 else LANES

    def widen(t, fill):
        blk = jnp.full((S, period), fill, F32).at[:, lane_off:lane_off + rot_dim].set(t)
        blk = jnp.tile(blk, (1, LANES // period))
        return jnp.concatenate([blk, jnp.full((pad_rows, LANES), fill, F32)], axis=0)

    return widen(cos, 1.0), widen(sa, 0.0), widen(sb, 0.0)


def _attn_specs(lay, tq, widths):
    S, C, B = lay.S, lay.C, lay.B
    nq, nc = S // tq, C // tq
    lat_blocks = lay.n_lat // C

    def q_index(b, i):
        return jnp.where(i < nq, b * nq + i, B * nq + b * nc + (i - nq))

    lat = [pl.BlockSpec((S, w), lambda b, i: (b, 0)) for w in widths]
    ctx = [pl.BlockSpec((C, w), lambda b, i: (lat_blocks + b, 0)) for w in widths]
    return q_index, nq, nc, lat, ctx


def _gqa_proj_kernel(h_ref, mod_ref, gain_ref, w_ref, cos_ref, sa_ref, sb_ref, q_ref, k_ref, v_ref):
    qd, kd = GQA_HEADS * GQA_HEAD_DIM, GQA_KV_HEADS * GQA_HEAD_DIM
    a = _pre(h_ref[...], gain_ref[...], mod_ref[0], 3).astype(BF16)
    cos, sin = cos_ref[...], sa_ref[...] + sb_ref[...]
    f = GQA_HEAD_DIM // 4
    scale = GQA_HEAD_DIM ** -0.5 * LOG2_E
    lane = lax.broadcasted_iota(jnp.int32, (1, MXU_COLS), 1)
    ones_half = jnp.where((lane & (2 * GQA_HEAD_DIM - 1)) >= GQA_HEAD_DIM, 1.0, 0.0)

    def emit(col, tile):
        if col >= qd + kd:
            v_ref[:, col - qd - kd:col - qd - kd + MXU_COLS] = (tile + ones_half).astype(BF16)
            return
        for j in range(MXU_COLS // LANES):
            x = _rope(tile[:, j * LANES:(j + 1) * LANES], cos, sin, f)
            lo = col + j * LANES
            if lo < qd:
                q_ref[:, lo:lo + LANES] = (x * scale).astype(BF16)
            else:
                k_ref[:, lo - qd:lo - qd + LANES] = x.astype(BF16)

    _chunked_proj(a, w_ref, emit)


def _gqa_attn_kernel(sink_ref, q_ref, kp_ref, kc_ref, kn_ref, vp_ref, vc_ref, vn_ref, kx_ref, vx_ref,
                     o_ref, kcat_ref, vcat_ref, *, nq):
    i = pl.program_id(1)
    blk = GQA_BLOCK
    G = GQA_HEADS // GQA_KV_HEADS
    HD = GQA_HEAD_DIM
    VW = 2 * HD
    span = 3 * blk

    def run(k_ref, v_ref, masks):
        kvs = range(GQA_KV_HEADS)
        qs = [jnp.concatenate([q_ref[:, (G * g + n) * HD:(G * g + n + 1) * HD] for n in range(G)], axis=0)
              for g in kvs]
        sinks = [jnp.concatenate([jnp.full((blk, 1), sink_ref[G * g + n] * LOG2_E, F32) for n in range(G)], axis=0)
                 for g in kvs]
        ss = [_dot_nt(qs[g], k_ref[:, g * HD:(g + 1) * HD]) for g in kvs]
        if masks is not None:
            prev_ok, next_ok = masks
            ss = [jnp.concatenate([jnp.where(prev_ok, s[:, :blk], NEG_INF), s[:, blk:2 * blk],
                                   jnp.where(next_ok, s[:, 2 * blk:span], NEG_INF), s[:, span:]], axis=1)
                  for s in ss]
        ms = [jnp.maximum(jnp.max(ss[g], axis=-1, keepdims=True), sinks[g]) for g in kvs]
        accs = [_dot(jnp.exp2(ss[g] - ms[g]).astype(BF16),
                     v_ref[:, (g // 2) * MXU_COLS:(g // 2 + 1) * MXU_COLS])[:, (g % 2) * VW:(g % 2 + 1) * VW]
                for g in kvs]
        outs = [accs[g][:, :HD] * (1.0 / (accs[g][:, HD:] + jnp.exp2(sinks[g] - ms[g]))) for g in kvs]
        heads = [outs[g][n * blk:(n + 1) * blk] for g in kvs for n in range(G)]
        o_ref[...] = jnp.concatenate(heads, axis=-1).astype(BF16)

    @pl.when(i < nq)
    def _():
        for n, (k_ref, v_ref) in enumerate([(kp_ref, vp_ref), (kc_ref, vc_ref), (kn_ref, vn_ref)]):
            kcat_ref[n * blk:(n + 1) * blk] = k_ref[...]
            vcat_ref[n * blk:(n + 1) * blk] = v_ref[...]
        kcat_ref[span:] = kx_ref[...]
        vcat_ref[span:] = vx_ref[...]
        row = lax.broadcasted_iota(jnp.int32, (G * blk, blk), 0) & (blk - 1)
        col = lax.broadcasted_iota(jnp.int32, (G * blk, blk), 1)
        prev_ok = (row + blk - col <= GQA_WINDOW) & (i >= 1)
        next_ok = (col + blk - row <= GQA_WINDOW) & (i + 1 < nq)
        run(kcat_ref, vcat_ref, (prev_ok, next_ok))

    @pl.when(i >= nq)
    def _():
        run(kx_ref, vx_ref, None)


def _gqa_mixer(lay, h, mod, gain, w_in, sinks, with_ctx_out):
    D = h.shape[1]
    B, S, C = lay.B, lay.S, lay.C
    qd, kd = GQA_HEADS * GQA_HEAD_DIM, GQA_KV_HEADS * GQA_HEAD_DIM
    w_v = jnp.pad(w_in[:, qd + kd:].reshape(D, GQA_KV_HEADS, GQA_HEAD_DIM), ((0, 0), (0, 0), (0, GQA_HEAD_DIM)))
    w_pad = jnp.concatenate([w_in[:, :qd + kd], w_v.reshape(D, 2 * kd)], axis=1)
    tm = lay.tile(ROW_TILE)
    midx, pidx = lay.mod_index(tm), lay.pos_index(tm)
    cos, sa, sb = _rope_tables(S, GQA_HEAD_DIM, tm, 0)
    tab = pl.BlockSpec((tm, LANES), lambda t: (pidx(t), 0))
    q, k, v = pl.pallas_call(
        _gqa_proj_kernel,
        grid=(lay.n_all // tm,),
        in_specs=[pl.BlockSpec((tm, D), lambda t: (t, 0)),
                  pl.BlockSpec((1, N_MOD, D), lambda t: (midx(t), 0, 0)),
                  pl.BlockSpec((1, D), lambda t: (0, 0)),
                  pl.BlockSpec((D, qd + 3 * kd), lambda t: (0, 0)),
                  tab, tab, tab],
        out_specs=[pl.BlockSpec((tm, qd), lambda t: (t, 0)),
                   pl.BlockSpec((tm, kd), lambda t: (t, 0)),
                   pl.BlockSpec((tm, 2 * kd), lambda t: (t, 0))],
        out_shape=[jax.ShapeDtypeStruct((lay.n_all, qd), BF16),
                   jax.ShapeDtypeStruct((lay.n_all, kd), BF16),
                   jax.ShapeDtypeStruct((lay.n_all, 2 * kd), BF16)],
        compiler_params=_params("arbitrary"),
        name="gqa_proj",
    )(h, mod, gain.reshape(1, D), w_pad.astype(BF16), cos, sa, sb)

    blk = GQA_BLOCK
    assert S % blk == 0 and C % blk == 0
    nq, nc = S // blk, C // blk
    n_i = nq + (nc if with_ctx_out else 0)
    lat_blocks = lay.n_lat // C

    def q_index(b, i):
        return jnp.where(i < nq, b * nq + i, B * nq + b * nc + (i - nq))

    def win(delta):
        return lambda b, i: (b * nq + jnp.clip(i + delta, 0, nq - 1), 0)

    def windows(width):
        return [pl.BlockSpec((blk, width), win(d)) for d in (-1, 0, 1)]

    def ctx_spec(width):
        return pl.BlockSpec((C, width), lambda b, i: (lat_blocks + b, 0))

    row = pl.BlockSpec((blk, qd), lambda b, i: (q_index(b, i), 0))
    return pl.pallas_call(
        functools.partial(_gqa_attn_kernel, nq=nq),
        grid=(B, n_i),
        in_specs=[pl.BlockSpec(memory_space=pltpu.SMEM), row,
                  *windows(kd), *windows(2 * kd), ctx_spec(kd), ctx_spec(2 * kd)],
        out_specs=row,
        out_shape=jax.ShapeDtypeStruct((lay.n_all if with_ctx_out else lay.n_lat, qd), BF16),
        scratch_shapes=[pltpu.VMEM((3 * blk + C, kd), BF16), pltpu.VMEM((3 * blk + C, 2 * kd), BF16)],
        compiler_params=_params("arbitrary", "arbitrary"),
        name="gqa_attn",
    )(sinks.astype(F32), q, k, k, k, v, v, v, k, v)


def _diff_proj_kernel(h_ref, mod_ref, gain_ref, w_ref, cos_ref, sa_ref, sb_ref, q_ref, k_ref, v_ref):
    qk = DIFF_HEADS * 2 * DIFF_HEAD_DIM
    a = _pre(h_ref[...], gain_ref[...], mod_ref[0], 3).astype(BF16)
    cos, sin = cos_ref[...], sa_ref[...] + sb_ref[...]
    f = DIFF_HEAD_DIM // 4
    scale = DIFF_HEAD_DIM ** -0.5 * LOG2_E
    vw = 2 * DIFF_HEAD_DIM
    ones = jnp.ones((a.shape[0], vw), BF16)

    def emit(col, tile):
        for j in range(MXU_COLS // LANES):
            lo = col + j * LANES
            x = tile[:, j * LANES:(j + 1) * LANES]
            if lo < qk:
                q_ref[:, lo:lo + LANES] = (_rope(x, cos, sin, f) * scale).astype(BF16)
            elif lo < 2 * qk:
                k_ref[:, lo - qk:lo - qk + LANES] = _rope(x, cos, sin, f).astype(BF16)
            else:
                hd = (lo - 2 * qk) // vw
                v_ref[:, 2 * hd * vw:(2 * hd + 1) * vw] = x.astype(BF16)
                v_ref[:, (2 * hd + 1) * vw:(2 * hd + 2) * vw] = ones

    _chunked_proj(a, w_ref, emit)


def _diff_attn_kernel(lam_ref, sub_ref, q_ref, kl_ref, vl_ref, kx_ref, vx_ref, o_ref, *, nq, lambda_init):
    i = pl.program_id(1)
    HD = DIFF_HEAD_DIM
    VW = 4 * HD
    lp = lam_ref[...]
    lam = (jnp.exp(jnp.sum(lp[0:1] * lp[1:2], keepdims=True))
           - jnp.exp(jnp.sum(lp[2:3] * lp[3:4], keepdims=True)) + lambda_init)

    def run(kv_pairs):
        def scores(h):
            los = [(2 * h + j) * HD for j in range(2)]
            return [[_dot_nt(q_ref[:, lo:lo + HD], k_ref[:, lo:lo + HD]) for k_ref, _ in kv_pairs] for lo in los]

        heads = []
        ss = scores(0)
        for h in range(DIFF_HEADS):
            nxt = scores(h + 1) if h + 1 < DIFF_HEADS else None
            maps = []
            for parts in ss:
                m = functools.reduce(jnp.maximum, [jnp.max(s, axis=-1, keepdims=True) for s in parts])
                acc = None
                for s, (_, v_ref) in zip(parts, kv_pairs):
                    t = _dot(jnp.exp2(s - m).astype(BF16), v_ref[:, h * VW:(h + 1) * VW])
                    acc = t if acc is None else acc + t
                maps.append(acc[:, :VW // 2] * (1.0 / acc[:, VW // 2:]))
            o = maps[0] - lam * maps[1]
            heads.append(_rmsnorm(o, sub_ref[...]) * (1.0 - lambda_init))
            ss = nxt
        o_ref[...] = jnp.concatenate(heads, axis=-1).astype(BF16)

    @pl.when(i < nq)
    def _():
        run([(kl_ref, vl_ref), (kx_ref, vx_ref)])

    @pl.when(i >= nq)
    def _():
        run([(kx_ref, vx_ref)])


def _diff_mixer(lay, h, mod, gain, w_in, lam_params, subln_w, lambda_init, with_ctx_out):
    D = h.shape[1]
    B, S, C = lay.B, lay.S, lay.C
    qk = DIFF_HEADS * 2 * DIFF_HEAD_DIM
    tm = lay.tile(ROW_TILE)
    midx, pidx = lay.mod_index(tm), lay.pos_index(tm)
    cos, sa, sb = _rope_tables(S, DIFF_HEAD_DIM, tm, 0)
    tab = pl.BlockSpec((tm, LANES), lambda t: (pidx(t), 0))
    out = jax.ShapeDtypeStruct((lay.n_all, qk), BF16)
    q, k, v = pl.pallas_call(
        _diff_proj_kernel,
        grid=(lay.n_all // tm,),
        in_specs=[pl.BlockSpec((tm, D), lambda t: (t, 0)),
                  pl.BlockSpec((1, N_MOD, D), lambda t: (midx(t), 0, 0)),
                  pl.BlockSpec((1, D), lambda t: (0, 0)),
                  pl.BlockSpec((D, 3 * qk), lambda t: (0, 0)),
                  tab, tab, tab],
        out_specs=[pl.BlockSpec((tm, qk), lambda t: (t, 0))] * 2 + [pl.BlockSpec((tm, 2 * qk), lambda t: (t, 0))],
        out_shape=[out, out, jax.ShapeDtypeStruct((lay.n_all, 2 * qk), BF16)],
        compiler_params=_params("arbitrary"),
        name="diff_proj",
    )(h, mod, gain.reshape(1, D), w_in.astype(BF16), cos, sa, sb)

    tq = _divisor(math.gcd(S, C), QUERY_TILE)
    q_index, nq, nc, lat, ctx = _attn_specs(lay, tq, [qk, 2 * qk])
    row = pl.BlockSpec((tq, qk), lambda b, i: (q_index(b, i), 0))
    return pl.pallas_call(
        functools.partial(_diff_attn_kernel, nq=nq, lambda_init=lambda_init),
        grid=(B, nq + (nc if with_ctx_out else 0)),
        in_specs=[pl.BlockSpec((4, DIFF_HEAD_DIM), lambda b, i: (0, 0)),
                  pl.BlockSpec((1, 2 * DIFF_HEAD_DIM), lambda b, i: (0, 0)),
                  row, *lat, *ctx],
        out_specs=row,
        out_shape=jax.ShapeDtypeStruct((lay.n_all if with_ctx_out else lay.n_lat, qk), BF16),
        compiler_params=_params("arbitrary", "arbitrary"),
        name="diff_attn",
    )(lam_params.astype(F32), subln_w.reshape(1, -1), q, k, v, k, v)


def _hgrn_proj_kernel(h_ref, mod_ref, gain_ref, lb_ref, w_ref, o_ref, *, layer):
    KD = HGRN_HEADS * HGRN_KEY_DIM
    W = 2 * LANES
    a = _pre(h_ref[...], gain_ref[...], mod_ref[0], 3).astype(BF16)
    raw = lb_ref[...]
    e = jnp.exp(raw - jnp.max(raw, axis=0, keepdims=True))
    soft = e * (1.0 / jnp.sum(e, axis=0, keepdims=True))
    lb = jnp.sum(soft[1:layer + 1], axis=0, keepdims=True) if layer else jnp.zeros_like(soft[0:1])
    n = w_ref.shape[1] // W

    def proj(c):
        return _dot(a, w_ref[:, c * W:(c + 1) * W])

    cur = proj(0)
    for c in range(n):
        nxt = proj(c + 1) if c + 1 < n else None
        group, col = divmod(c * W, KD)
        if group in (0, 4):
            out = _silu(cur)
        elif group in (1, 2):
            lbc = lb[:, col:col + W]
            out = lbc + (1.0 - lbc) * jax.nn.sigmoid(cur)
        else:
            out = cur
        o_ref[:, c * W:(c + 1) * W] = out
        cur = nxt


def _running_sum(x, reverse):
    n, w = x.shape
    n_groups = n // SUBLANES
    x = x.reshape(n_groups, SUBLANES, w)
    within = lax.broadcasted_iota(jnp.int32, (1, SUBLANES, 1), 1)
    step = 1
    while step < SUBLANES:
        if reverse:
            x = x + jnp.where(within < SUBLANES - step, pltpu.roll(x, SUBLANES - step, 1), 0.0)
        else:
            x = x + jnp.where(within >= step, pltpu.roll(x, step, 1), 0.0)
        step *= 2
    out = [None] * n_groups
    carry = None
    for j in (reversed(range(n_groups)) if reverse else range(n_groups)):
        out[j] = x[j] if carry is None else x[j] + carry
        carry = out[j][0:1] if reverse else out[j][SUBLANES - 1:SUBLANES]
    return jnp.concatenate(out, axis=0)


def _hgrn_scan_kernel(qf_ref, ff_ref, vf_ref, qb_ref, fb_ref, vb_ref, of_ref, ob_ref, st_ref, *, n_chunks):
    step = pl.program_id(1)
    C = HGRN_CHUNK
    DK, DV, H = HGRN_KEY_DIM, HGRN_VAL_DIM, HGRN_HEADS

    @pl.when(step == 0)
    def _():
        st_ref[...] = jnp.zeros_like(st_ref)

    r = lax.broadcasted_iota(jnp.int32, (C, C), 0)
    c = lax.broadcasted_iota(jnp.int32, (C, C), 1)
    ks = [slice(h * DK, (h + 1) * DK) for h in range(H)]
    vs = [slice(h * DV, (h + 1) * DV) for h in range(H)]
    dirs = [(qf_ref, ff_ref, vf_ref, of_ref, c <= r, C // 2, C - 1),
            (qb_ref, fb_ref, vb_ref, ob_ref, c >= r, C - 1 - C // 2, 0)]

    def prepare(d, n):
        q_ref, f_ref, v_ref, _, tri, mid_row, last_row = dirs[d]
        base = (n if d == 0 else n_chunks - 1 - n) * C
        q = q_ref[base:base + C, :]
        f = f_ref[base:base + C, :]
        vt = v_ref[base:base + C, :].T.astype(BF16)
        k = 1.0 - f
        g = _running_sum(jnp.log2(f), reverse=d == 1)
        g_mid = g[mid_row:mid_row + 1]
        g_last = g[last_row:last_row + 1]
        t = g - g_mid
        qa_f = q * jnp.exp2(t)
        kb_f = k * jnp.exp2(-t)
        qa = qa_f.astype(BF16)
        kb = kb_f.astype(BF16)
        qg = (qa_f * jnp.exp2(g_mid)).astype(BF16)
        kd = (kb_f * jnp.exp2(g_last - g_mid)).astype(BF16)
        a = [jnp.where(tri, _dot_nt(qa[:, ks[h]], kb[:, ks[h]]), 0.0).astype(BF16) for h in range(H)]
        lhs = [jnp.concatenate([qg[:, ks[h]], a[h]], axis=1) for h in range(H)]
        update = [_dot(vt[vs[h], :], kd[:, ks[h]]) for h in range(H)]
        return base, lhs, vt, jnp.exp2(g_last), update

    cur = [prepare(d, 0) for d in range(2)]
    for n in range(n_chunks):
        nxt = [prepare(d, n + 1) for d in range(2)] if n + 1 < n_chunks else None
        for d in range(2):
            base, lhs, vt, decay, update = cur[d]
            st = [st_ref[d, vs[h], :] for h in range(H)]
            outs = [_dot_nt(lhs[h], jnp.concatenate([st[h].astype(BF16), vt[vs[h], :]], axis=1)) for h in range(H)]
            for h in range(H):
                st_ref[d, vs[h], :] = st[h] * decay[:, ks[h]] + update[h]
            dirs[d][3][base:base + C, :] = jnp.concatenate(outs, axis=-1)
        cur = nxt


def _hgrn_readout(of_ref, ob_ref, gate_ref, nw_ref):
    DV = HGRN_VAL_DIM
    o = of_ref[...] + ob_ref[...]
    gate = gate_ref[...]
    nw = nw_ref[...]
    ys = [_rmsnorm(o[:, h * DV:(h + 1) * DV], nw) * gate[:, h * DV:(h + 1) * DV] for h in range(HGRN_HEADS)]
    return jnp.concatenate(ys, axis=-1).astype(BF16)


def _plain_readout(o_ref):
    return o_ref[...]


def _hgrn_mixer(lay, h, mod, gain, w_in, norm_w, lower_bounds, layer):
    D = h.shape[1]
    B, S, C = lay.B, lay.S, lay.C
    KD = HGRN_HEADS * HGRN_KEY_DIM
    tm = lay.tile(NARROW_ROW_TILE)
    midx = lay.mod_index(tm)
    n_cols = w_in.shape[1] // KD
    proj = pl.pallas_call(
        functools.partial(_hgrn_proj_kernel, layer=layer),
        grid=(lay.n_all // tm,),
        in_specs=[pl.BlockSpec((tm, D), lambda t: (t, 0)),
                  pl.BlockSpec((1, N_MOD, D), lambda t: (midx(t), 0, 0)),
                  pl.BlockSpec((1, D), lambda t: (0, 0)),
                  pl.BlockSpec(lower_bounds.shape, lambda t: (0, 0)),
                  pl.BlockSpec((D, n_cols * KD), lambda t: (0, 0), pipeline_mode=pl.Buffered(1))],
        out_specs=pl.BlockSpec((tm, n_cols * KD), lambda t: (t, 0)),
        out_shape=jax.ShapeDtypeStruct((lay.n_all, n_cols * KD), F32),
        compiler_params=_params("arbitrary"),
        name="hgrn_proj",
    )(h, mod, gain.reshape(1, D), lower_bounds.astype(F32), w_in.astype(BF16))

    ts = C
    n_lat_steps = S // ts
    lat_blocks = lay.n_lat // ts

    def rows(d):
        def index(b, s):
            lat = b * n_lat_steps + (s - 1 if d == 0 else n_lat_steps - s)
            return jnp.where(s == 0, lat_blocks + b, lat)
        return index

    def col(d, j):
        return pl.BlockSpec((ts, KD), lambda b, s: (rows(d)(b, s), j))

    o_f, o_b = pl.pallas_call(
        functools.partial(_hgrn_scan_kernel, n_chunks=ts // HGRN_CHUNK),
        grid=(B, 1 + n_lat_steps),
        in_specs=[col(0, 0), col(0, 1), col(0, 3), col(1, 0), col(1, 2), col(1, 3)],
        out_specs=[col(0, 0), col(1, 0)],
        out_shape=[jax.ShapeDtypeStruct((lay.n_all, KD), F32)] * 2,
        scratch_shapes=[pltpu.VMEM((2, HGRN_HEADS * HGRN_VAL_DIM, HGRN_KEY_DIM), F32)],
        compiler_params=_params("arbitrary", "arbitrary"),
        name="hgrn_scan",
    )(proj, proj, proj, proj, proj, proj)

    return _hgrn_readout, [(o_f, 0), (o_b, 0), (proj, 4), (norm_w.reshape(1, -1), 0)]


MLA_QK_PAD = LANES


def _mla_proj_kernel(h_ref, mod_ref, gain_ref, wd_ref, qn_ref, kvn_ref, wuq_ref, wukv_ref,
                     cos_ref, sa_ref, sb_ref, q_ref, k_ref, v_ref):
    H, P = MLA_HEADS, MLA_QK_PAD
    a = _pre(h_ref[...], gain_ref[...], mod_ref[0], 3).astype(BF16)
    dn = _dot(a, wd_ref[...])
    cos, sin = cos_ref[...], sa_ref[...] + sb_ref[...]
    f = MLA_ROPE // 4
    cq = _rmsnorm(dn[:, :MLA_Q_LORA], qn_ref[...]).astype(BF16)
    ckv = _rmsnorm(dn[:, MLA_Q_LORA:MLA_Q_LORA + MLA_KV_LORA], kvn_ref[...]).astype(BF16)
    kr = _rope(dn[:, MLA_Q_LORA + MLA_KV_LORA:MLA_Q_LORA + MLA_KV_LORA + P], cos, sin, f)
    lane = lax.broadcasted_iota(jnp.int32, (1, P), 1)
    ones_half = jnp.where(lane >= MLA_V_DIM, 1.0, 0.0)

    scale = (MLA_NOPE + MLA_ROPE) ** -0.5 * LOG2_E

    def emit_q(col, tile):
        for j in range(MXU_COLS // P):
            lo = col + j * P
            q_ref[:, lo:lo + P] = (_rope(tile[:, j * P:(j + 1) * P], cos, sin, f) * scale).astype(BF16)

    def emit_kv(col, tile):
        for j in range(MXU_COLS // P):
            lo = col + j * P
            x = tile[:, j * P:(j + 1) * P]
            if lo < H * P:
                k_ref[:, lo:lo + P] = (x + kr).astype(BF16)
            else:
                v_ref[:, lo - H * P:lo - H * P + P] = (x + ones_half).astype(BF16)

    _chunked_proj(cq, wuq_ref, emit_q)
    _chunked_proj(ckv, wukv_ref, emit_kv)


def _mla_attn_kernel(q_ref, kl_ref, vl_ref, kx_ref, vx_ref, o_ref, *, nq, with_ctx):
    i = pl.program_id(1)
    P, VD = MLA_QK_PAD, MLA_V_DIM

    def run(kv_pairs):
        def scores(hd):
            qh = q_ref[:, hd * P:(hd + 1) * P]
            return [_dot_nt(qh, k_ref[:, hd * P:(hd + 1) * P]) for k_ref, _ in kv_pairs]

        heads = []
        ss = scores(0)
        for hd in range(MLA_HEADS):
            nxt = scores(hd + 1) if hd + 1 < MLA_HEADS else None
            m = functools.reduce(jnp.maximum, [jnp.max(s, axis=-1, keepdims=True) for s in ss])
            pair = (hd // 2) * MXU_COLS
            acc = None
            for s, (_, v_ref) in zip(ss, kv_pairs):
                t = _dot(jnp.exp2(s - m).astype(BF16), v_ref[:, pair:pair + MXU_COLS])
                acc = t if acc is None else acc + t
            acc = acc[:, (hd % 2) * P:(hd % 2 + 1) * P]
            heads.append(acc[:, :VD] * (1.0 / acc[:, VD:]))
            ss = nxt
        o_ref[...] = jnp.concatenate(heads, axis=-1).astype(BF16)

    if not with_ctx:
        run([(kl_ref, vl_ref), (kx_ref, vx_ref)])
        return

    @pl.when(i < nq)
    def _():
        run([(kl_ref, vl_ref), (kx_ref, vx_ref)])

    @pl.when(i >= nq)
    def _():
        run([(kx_ref, vx_ref)])


def _mla_mixer(lay, h, mod, gain, w_down, q_norm_w, kv_norm_w, w_uq, w_ukv, with_ctx_out):
    D = h.shape[1]
    B, S, C = lay.B, lay.S, lay.C
    H, P, VD = MLA_HEADS, MLA_QK_PAD, MLA_V_DIM
    qk = MLA_NOPE + MLA_ROPE
    lora = MLA_Q_LORA + MLA_KV_LORA
    kr_cols = jnp.zeros((D, MXU_COLS), F32).at[:, MLA_NOPE:qk].set(w_down[:, lora:])
    wd = jnp.concatenate([w_down[:, :lora], kr_cols], axis=1).astype(BF16)
    wuq = jnp.pad(w_uq.reshape(MLA_Q_LORA, H, qk), ((0, 0), (0, 0), (0, P - qk))).reshape(MLA_Q_LORA, H * P)
    ukv = w_ukv.reshape(MLA_KV_LORA, H, MLA_NOPE + VD)
    wkn = jnp.pad(ukv[..., :MLA_NOPE], ((0, 0), (0, 0), (0, P - MLA_NOPE))).reshape(MLA_KV_LORA, H * P)
    wv = jnp.pad(ukv[..., MLA_NOPE:], ((0, 0), (0, 0), (0, P - VD))).reshape(MLA_KV_LORA, H * P)
    wukv = jnp.concatenate([wkn, wv], axis=1)

    tm = lay.tile(ROW_TILE)
    midx, pidx = lay.mod_index(tm), lay.pos_index(tm)
    cos, sa, sb = _rope_tables(S, MLA_ROPE, tm, MLA_NOPE)
    tab = pl.BlockSpec((tm, LANES), lambda t: (pidx(t), 0))
    full = lambda shape: pl.BlockSpec(shape, lambda t: (0, 0))
    q, k, v = pl.pallas_call(
        _mla_proj_kernel,
        grid=(lay.n_all // tm,),
        in_specs=[pl.BlockSpec((tm, D), lambda t: (t, 0)),
                  pl.BlockSpec((1, N_MOD, D), lambda t: (midx(t), 0, 0)),
                  full((1, D)), full((D, lora + MXU_COLS)), full((1, MLA_Q_LORA)), full((1, MLA_KV_LORA)),
                  full((MLA_Q_LORA, H * P)), full((MLA_KV_LORA, 2 * H * P)),
                  tab, tab, tab],
        out_specs=[pl.BlockSpec((tm, H * P), lambda t: (t, 0))] * 3,
        out_shape=[jax.ShapeDtypeStruct((lay.n_all, H * P), BF16)] * 3,
        compiler_params=_params("arbitrary"),
        name="mla_proj",
    )(h, mod, gain.reshape(1, D), wd, q_norm_w.reshape(1, -1), kv_norm_w.reshape(1, -1),
      wuq.astype(BF16), wukv.astype(BF16), cos, sa, sb)

    tq = _divisor(math.gcd(S, C), QUERY_TILE)
    q_index, nq, nc, lat, ctx = _attn_specs(lay, tq, [H * P, H * P])
    return pl.pallas_call(
        functools.partial(_mla_attn_kernel, nq=nq, with_ctx=with_ctx_out),
        grid=(B, nq + (nc if with_ctx_out else 0)),
        in_specs=[pl.BlockSpec((tq, H * P), lambda b, i: (q_index(b, i), 0)), *lat, *ctx],
        out_specs=pl.BlockSpec((tq, H * VD), lambda b, i: (q_index(b, i), 0)),
        out_shape=jax.ShapeDtypeStruct((lay.n_all if with_ctx_out else lay.n_lat, H * VD), BF16),
        compiler_params=_params("arbitrary", "arbitrary"),
        name="mla_attn",
    )(q, k, v, k, v)


def kernel(x, c, ctx, c_ctx, ada_w, ada_b, norm_w, final_norm_w, ffn_w_gate, ffn_w_up, ffn_w_down,
           gqa_w_in, gqa_w_out, gqa_sinks, diff_w_in, diff_w_out, diff_lambda, diff_subln_w,
           hgrn_w_in, hgrn_w_out, hgrn_norm_w, hgrn_lower_bounds,
           mla_w_down, mla_q_norm_w, mla_kv_norm_w, mla_w_uq, mla_w_ukv, mla_w_out):
    B, S, D = x.shape
    C = ctx.shape[1]
    lay = _Layout(B, S, C)
    depth = ada_w.shape[0]

    rows = -(-(B + 1) // 8) * 8
    cc = jnp.zeros((rows, D), F32).at[:B].set(c).at[B].set(c_ctx)
    mods = _ada_table(cc, ada_w, ada_b).reshape(depth, rows, N_MOD, D)

    stacks = (ffn_w_gate, ffn_w_up, ffn_w_down)
    weights = tuple(w[0, 0].astype(BF16) for w in stacks)
    h = (x.reshape(B * S, D), ctx.reshape(B * C, D))
    for i in range(depth):
        kind, j = i % 4, i // 4
        last = i == depth - 1
        mod = mods[i]
        h, weights = _ffn(lay, h, mod, norm_w[i, 0], final_norm_w, weights, 0, cast_next=(*stacks, i, 1))
        if kind == 0:
            o = _gqa_mixer(lay, h, mod, norm_w[i, 1], gqa_w_in[j], gqa_sinks[j], not last)
            mix = (_plain_readout, [(o, 0)], gqa_w_out[j])
        elif kind == 1:
            lambda_init = 0.8 - 0.6 * math.exp(-0.3 * i)
            o = _diff_mixer(lay, h, mod, norm_w[i, 1], diff_w_in[j], diff_lambda[j], diff_subln_w[j],
                            lambda_init, not last)
            mix = (_plain_readout, [(o, 0)], diff_w_out[j])
        elif kind == 2:
            readout, operands = _hgrn_mixer(lay, h, mod, norm_w[i, 1], hgrn_w_in[j], hgrn_norm_w[j],
                                            hgrn_lower_bounds, i)
            mix = (readout, operands, hgrn_w_out[j])
        else:
            o = _mla_mixer(lay, h, mod, norm_w[i, 1], mla_w_down[j], mla_q_norm_w[j], mla_kv_norm_w[j],
                           mla_w_uq[j], mla_w_ukv[j], not last)
            mix = (_plain_readout, [(o, 0)], mla_w_out[j])
        h, weights = _ffn(lay, h, mod, norm_w[i, 2], final_norm_w, weights, 1, lat_only=last, final_norm=last,
                          mix=mix, cast_next=None if last else (*stacks, i + 1, 0))
    return h[:B * S].reshape(B, S, D)
```

```python
import functools
import math

import jax
import jax.numpy as jnp
from jax import lax
from jax.experimental import pallas as pl
from jax.experimental.pallas import tpu as pltpu

F32 = jnp.float32
BF16 = jnp.bfloat16

D_MODEL = 1024
GRID_W = 64
N_MOD = 9
ROPE_BASE = 10000.0
EPS = 1e-6
NEG_INF = -1e30
D_FF = 2816

GQA_HEADS = 16
GQA_KV_HEADS = 4
GQA_HEAD_DIM = 64
GQA_WINDOW = 128
GQA_BLOCK = 128

DIFF_HEADS = 8
DIFF_HEAD_DIM = 64

HGRN_HEADS = 8
HGRN_KEY_DIM = 128
HGRN_VAL_DIM = D_MODEL // HGRN_HEADS
HGRN_CHUNK = 64

MLA_HEADS = 16
MLA_Q_LORA = 256
MLA_KV_LORA = 256
MLA_NOPE = 64
MLA_ROPE = 32
MLA_V_DIM = 64

LOG2_E = 1.4426950408889634
LANES = 128
SUBLANES = 8
MXU_COLS = 2 * LANES
VMEM_LIMIT = 56 * 1024 * 1024
ROW_TILE = 1024
NARROW_ROW_TILE = 512
QUERY_TILE = 256


def _dot(a, b):
    return jnp.dot(a, b, preferred_element_type=F32)


def _dot_nt(a, b):
    return lax.dot_general(a, b, (((1,), (1,)), ((), ())), preferred_element_type=F32)


def _params(*sem):
    return pltpu.CompilerParams(dimension_semantics=sem, vmem_limit_bytes=VMEM_LIMIT)


def _divisor(n, pref):
    t = min(n, pref)
    while n % t:
        t -= 8
    return t


class _Layout:
    def __init__(self, B, S, C):
        self.B, self.S, self.C = B, S, C
        self.n_lat = B * S
        self.n_all = B * S + B * C

    def tile(self, pref):
        return _divisor(math.gcd(self.S, self.B * self.C), pref)

    def mod_index(self, tm):
        n_lat_tiles, per_batch, B = self.n_lat // tm, self.S // tm, self.B
        return lambda t: jnp.where(t < n_lat_tiles, t // per_batch, B)

    def pos_index(self, tm):
        n_lat_tiles, per_batch = self.n_lat // tm, self.S // tm
        return lambda t: jnp.where(t < n_lat_tiles, t % per_batch, per_batch)


def _rmsnorm(x, w):
    return (x * lax.rsqrt(jnp.mean(x * x, axis=-1, keepdims=True) + EPS)) * w


def _pre(h, gain, m, off):
    inv = lax.rsqrt(jnp.mean(h * h, axis=-1, keepdims=True) + EPS)
    return (h * inv) * (gain * (1.0 + m[off + 1:off + 2])) + m[off:off + 1]


def _silu(x):
    return x * jax.nn.sigmoid(x)


def _chunked_proj(a, w_ref, emit):
    n = w_ref.shape[1] // MXU_COLS
    cur = _dot(a, w_ref[:, :MXU_COLS])
    for c in range(n):
        nxt = _dot(a, w_ref[:, (c + 1) * MXU_COLS:(c + 2) * MXU_COLS]) if c + 1 < n else None
        emit(c * MXU_COLS, cur)
        cur = nxt


def _rope(x, cos, sin, shift):
    lane = lax.broadcasted_iota(jnp.int32, (1, LANES), 1)
    first = (lane & (2 * shift - 1)) < shift
    partner = jnp.where(first, pltpu.roll(x, LANES - shift, 1), pltpu.roll(x, shift, 1))
    return x * cos + partner * sin


def _ada_kernel(c_ref, w_ref, b_ref, o_ref):
    sc = _silu(c_ref[...]).astype(BF16)
    o_ref[0] = _dot(sc, w_ref[0].astype(BF16)) + b_ref[0]


ADA_COLS = N_MOD * LANES


def _ada_table(cc, ada_w, ada_b):
    rows, D = cc.shape
    depth, _, n_out = ada_w.shape
    tn = ADA_COLS
    return pl.pallas_call(
        _ada_kernel,
        grid=(depth, n_out // tn),
        in_specs=[pl.BlockSpec((rows, D), lambda i, j: (0, 0)),
                  pl.BlockSpec((1, D, tn), lambda i, j: (i, 0, j)),
                  pl.BlockSpec((1, 1, tn), lambda i, j: (i, 0, j))],
        out_specs=pl.BlockSpec((1, rows, tn), lambda i, j: (i, 0, j)),
        out_shape=jax.ShapeDtypeStruct((depth, rows, n_out), F32),
        compiler_params=_params("arbitrary", "arbitrary"),
        name="ada_table",
    )(cc, ada_w, ada_b.reshape(depth, 1, n_out))


FFN_CHUNK = MXU_COLS


def _ffn_kernel(*refs, off, final_norm, n_lat_tiles, readout, n_mix, cast_next):
    refs = list(refs)
    if n_lat_tiles is None:
        h = refs.pop(0)[...]
    else:
        hx_ref, hc_ref = refs.pop(0), refs.pop(0)
        h = jnp.where(pl.program_id(0) < n_lat_tiles, hx_ref[...], hc_ref[...])
    mix_refs = [refs.pop(0) for _ in range(n_mix)]
    if readout is not None:
        wo_ref = refs.pop(0)
    mod_ref, gain_ref, fin_ref, wg_ref, wu_ref, wd_ref = refs[:6]
    if cast_next:
        for src, dst in zip(refs[6:9], refs[10:13]):
            dst[...] = src[...].astype(BF16)
    o_ref, a_ref = refs[9 if cast_next else 6], refs[-1]
    m = mod_ref[0]
    if readout is not None:
        h = h + m[5:6] * _dot(readout(*mix_refs), wo_ref[...])
    xm = _pre(h, gain_ref[...], m, off).astype(BF16)
    n = D_FF // FFN_CHUNK

    def gate_up(c):
        cols = slice(c * FFN_CHUNK, (c + 1) * FFN_CHUNK)
        return _dot(xm, wg_ref[:, cols]), _dot(xm, wu_ref[:, cols])

    cur = gate_up(0)
    for c in range(n):
        nxt = gate_up(c + 1) if c + 1 < n else None
        g, u = cur
        a_ref[:, c * FFN_CHUNK:(c + 1) * FFN_CHUNK] = (_silu(g) * u).astype(BF16)
        cur = nxt
    out = h + (0.5 * m[off + 2:off + 3]) * _dot(a_ref[...], wd_ref[...])
    if final_norm:
        out = _rmsnorm(out, fin_ref[...])
    o_ref[...] = out


def _slabs(rows, n_steps):
    n = n_steps
    while rows % n or (rows // n) % (2 * SUBLANES):
        n -= 1
    return n


def _ffn(lay, h, mod, gain, fin_w, weights, half, lat_only=False, final_norm=False, mix=None, cast_next=None):
    split = isinstance(h, tuple)
    wg, wu, wd = weights
    D = wg.shape[0]
    wide_operands = mix is not None and any(a.dtype == F32 and a.shape[0] > 1 for a, _ in mix[1])
    tm = lay.tile(NARROW_ROW_TILE if (split or wide_operands) else ROW_TILE)
    n_rows = lay.n_lat if lat_only else lay.n_all
    n_steps = n_rows // tm
    midx = lay.mod_index(tm)
    n_lat_tiles = lay.n_lat // tm
    const = lambda t: (0, 0)
    one = pl.Buffered(1)
    weight = lambda shape: pl.BlockSpec(shape, const, pipeline_mode=one)
    if split:
        h_specs = [pl.BlockSpec((tm, D), lambda t: (jnp.minimum(t, n_lat_tiles - 1), 0)),
                   pl.BlockSpec((tm, D), lambda t: (jnp.maximum(t - n_lat_tiles, 0), 0))]
        h_args = list(h)
    else:
        h_specs = [pl.BlockSpec((tm, D), lambda t: (t, 0))]
        h_args = [h]
    readout, operands = None, []
    if mix is not None:
        readout, operands, w_out = mix
        for arr, col in operands:
            if arr.shape[0] == 1:
                h_specs.append(pl.BlockSpec(arr.shape, const))
            else:
                h_specs.append(pl.BlockSpec((tm, D), lambda t, col=col: (t, col)))
            h_args.append(arr)
        h_specs.append(pl.BlockSpec(w_out.shape, const, pipeline_mode=one))
        h_args.append(w_out.astype(BF16))
    cast_specs, cast_args, cast_out_specs, cast_out_shapes = [], [], [], []
    if cast_next is not None:
        *stacks, nl, nh = cast_next
        for w32 in stacks:
            rows, cols = w32.shape[2:]
            n = _slabs(rows, n_steps)
            slab = rows // n
            cast_specs.append(pl.BlockSpec((None, None, slab, cols),
                                           lambda t, n=n: (nl, nh, jnp.minimum(t, n - 1), 0)))
            cast_args.append(w32)
            cast_out_specs.append(pl.BlockSpec((slab, cols), lambda t, n=n: (jnp.minimum(t, n - 1), 0)))
            cast_out_shapes.append(jax.ShapeDtypeStruct((rows, cols), BF16))
    outs = pl.pallas_call(
        functools.partial(_ffn_kernel, off=6 * half, final_norm=final_norm,
                          n_lat_tiles=n_lat_tiles if split else None, readout=readout, n_mix=len(operands),
                          cast_next=cast_next is not None),
        grid=(n_steps,),
        in_specs=h_specs + [pl.BlockSpec((1, N_MOD, D), lambda t: (midx(t), 0, 0)),
                            pl.BlockSpec((1, D), const),
                            pl.BlockSpec((1, D), const),
                            weight((D, D_FF)), weight((D, D_FF)), weight((D_FF, D))] + cast_specs,
        out_specs=[pl.BlockSpec((tm, D), lambda t: (t, 0))] + cast_out_specs,
        out_shape=[jax.ShapeDtypeStruct((n_rows, D), F32)] + cast_out_shapes,
        scratch_shapes=[pltpu.VMEM((tm, D_FF), BF16)],
        compiler_params=_params("arbitrary"),
        name="ffn",
    )(*h_args, mod, gain.reshape(1, D), fin_w.reshape(1, D), wg, wu, wd, *cast_args)
    return outs[0], (tuple(outs[1:]) if cast_next is not None else None)


def _rope_tables(S, rot_dim, pad_rows, lane_off):
    rows = S // GRID_W
    row = jnp.repeat(jnp.arange(rows, dtype=F32), GRID_W)
    col = jnp.tile(jnp.arange(GRID_W, dtype=F32), rows)
    axis_dim = rot_dim // 2
    inv_freq = ROPE_BASE ** (-jnp.arange(0, axis_dim, 2, dtype=F32) / axis_dim)
    ang_r = row[:, None] * inv_freq[None, :]
    ang_c = col[:, None] * inv_freq[None, :]
    ang = jnp.concatenate([ang_r, ang_r, ang_c, ang_c], axis=-1)
    cos, sin = jnp.cos(ang), jnp.sin(ang)
    f = rot_dim // 4
    first = (jnp.arange(rot_dim) % (2 * f)) < f
    sa = jnp.where(first[None, :], -sin, 0.0)
    sb = jnp.where(first[None, :], 0.0, sin)
    period = 64 if lane_off + rot_dim <= 64 else LANES

    def widen(t, fill):
        blk = jnp.full((S, period), fill, F32).at[:, lane_off:lane_off + rot_dim].set(t)
        blk = jnp.tile(blk, (1, LANES // period))
        return jnp.concatenate([blk, jnp.full((pad_rows, LANES), fill, F32)], axis=0)

    return widen(cos, 1.0), widen(sa, 0.0), widen(sb, 0.0)


def _attn_specs(lay, tq, widths):
    S, C, B = lay.S, lay.C, lay.B
    nq, nc = S // tq, C // tq
    lat_blocks = lay.n_lat // C

    def q_index(b, i):
        return jnp.where(i < nq, b * nq + i, B * nq + b * nc + (i - nq))

    lat = [pl.BlockSpec((S, w), lambda b, i: (b, 0)) for w in widths]
    ctx = [pl.BlockSpec((C, w), lambda b, i: (lat_blocks + b, 0)) for w in widths]
    return q_index, nq, nc, lat, ctx


def _gqa_proj_kernel(h_ref, mod_ref, gain_ref, w_ref, cos_ref, sa_ref, sb_ref, q_ref, k_ref, v_ref):
    qd, kd = GQA_HEADS * GQA_HEAD_DIM, GQA_KV_HEADS * GQA_HEAD_DIM
    a = _pre(h_ref[...], gain_ref[...], mod_ref[0], 3).astype(BF16)
    cos, sin = cos_ref[...], sa_ref[...] + sb_ref[...]
    f = GQA_HEAD_DIM // 4
    scale = GQA_HEAD_DIM ** -0.5 * LOG2_E
    lane = lax.broadcasted_iota(jnp.int32, (1, MXU_COLS), 1)
    ones_half = jnp.where((lane & (2 * GQA_HEAD_DIM - 1)) >= GQA_HEAD_DIM, 1.0, 0.0)

    def emit(col, tile):
        if col >= qd + kd:
            v_ref[:, col - qd - kd:col - qd - kd + MXU_COLS] = (tile + ones_half).astype(BF16)
            return
        for j in range(MXU_COLS // LANES):
            x = _rope(tile[:, j * LANES:(j + 1) * LANES], cos, sin, f)
            lo = col + j * LANES
            if lo < qd:
                q_ref[:, lo:lo + LANES] = (x * scale).astype(BF16)
            else:
                k_ref[:, lo - qd:lo - qd + LANES] = x.astype(BF16)

    _chunked_proj(a, w_ref, emit)


def _gqa_attn_kernel(sink_ref, q_ref, kp_ref, kc_ref, kn_ref, vp_ref, vc_ref, vn_ref, kx_ref, vx_ref,
                     o_ref, kcat_ref, vcat_ref, *, nq):
    i = pl.program_id(1)
    blk = GQA_BLOCK
    G = GQA_HEADS // GQA_KV_HEADS
    HD = GQA_HEAD_DIM
    VW = 2 * HD
    span = 3 * blk

    def run(k_ref, v_ref, masks):
        kvs = range(GQA_KV_HEADS)
        qs = [jnp.concatenate([q_ref[:, (G * g + n) * HD:(G * g + n + 1) * HD] for n in range(G)], axis=0)
              for g in kvs]
        sinks = [jnp.concatenate([jnp.full((blk, 1), sink_ref[G * g + n] * LOG2_E, F32) for n in range(G)], axis=0)
                 for g in kvs]
        ss = [_dot_nt(qs[g], k_ref[:, g * HD:(g + 1) * HD]) for g in kvs]
        if masks is not None:
            prev_ok, next_ok = masks
            ss = [jnp.concatenate([jnp.where(prev_ok, s[:, :blk], NEG_INF), s[:, blk:2 * blk],
                                   jnp.where(next_ok, s[:, 2 * blk:span], NEG_INF), s[:, span:]], axis=1)
                  for s in ss]
        ms = [jnp.maximum(jnp.max(ss[g], axis=-1, keepdims=True), sinks[g]) for g in kvs]
        accs = [_dot(jnp.exp2(ss[g] - ms[g]).astype(BF16),
                     v_ref[:, (g // 2) * MXU_COLS:(g // 2 + 1) * MXU_COLS])[:, (g % 2) * VW:(g % 2 + 1) * VW]
                for g in kvs]
        outs = [accs[g][:, :HD] * (1.0 / (accs[g][:, HD:] + jnp.exp2(sinks[g] - ms[g]))) for g in kvs]
        heads = [outs[g][n * blk:(n + 1) * blk] for g in kvs for n in range(G)]
        o_ref[...] = jnp.concatenate(heads, axis=-1).astype(BF16)

    @pl.when(i < nq)
    def _():
        for n, (k_ref, v_ref) in enumerate([(kp_ref, vp_ref), (kc_ref, vc_ref), (kn_ref, vn_ref)]):
            kcat_ref[n * blk:(n + 1) * blk] = k_ref[...]
            vcat_ref[n * blk:(n + 1) * blk] = v_ref[...]
        kcat_ref[span:] = kx_ref[...]
        vcat_ref[span:] = vx_ref[...]
        row = lax.broadcasted_iota(jnp.int32, (G * blk, blk), 0) & (blk - 1)
        col = lax.broadcasted_iota(jnp.int32, (G * blk, blk), 1)
        prev_ok = (row + blk - col <= GQA_WINDOW) & (i >= 1)
        next_ok = (col + blk - row <= GQA_WINDOW) & (i + 1 < nq)
        run(kcat_ref, vcat_ref, (prev_ok, next_ok))

    @pl.when(i >= nq)
    def _():
        run(kx_ref, vx_ref, None)


def _gqa_mixer(lay, h, mod, gain, w_in, sinks, with_ctx_out):
    D = h.shape[1]
    B, S, C = lay.B, lay.S, lay.C
    qd, kd = GQA_HEADS * GQA_HEAD_DIM, GQA_KV_HEADS * GQA_HEAD_DIM
    w_v = jnp.pad(w_in[:, qd + kd:].reshape(D, GQA_KV_HEADS, GQA_HEAD_DIM), ((0, 0), (0, 0), (0, GQA_HEAD_DIM)))
    w_pad = jnp.concatenate([w_in[:, :qd + kd], w_v.reshape(D, 2 * kd)], axis=1)
    tm = lay.tile(ROW_TILE)
    midx, pidx = lay.mod_index(tm), lay.pos_index(tm)
    cos, sa, sb = _rope_tables(S, GQA_HEAD_DIM, tm, 0)
    tab = pl.BlockSpec((tm, LANES), lambda t: (pidx(t), 0))
    q, k, v = pl.pallas_call(
        _gqa_proj_kernel,
        grid=(lay.n_all // tm,),
        in_specs=[pl.BlockSpec((tm, D), lambda t: (t, 0)),
                  pl.BlockSpec((1, N_MOD, D), lambda t: (midx(t), 0, 0)),
                  pl.BlockSpec((1, D), lambda t: (0, 0)),
                  pl.BlockSpec((D, qd + 3 * kd), lambda t: (0, 0)),
                  tab, tab, tab],
        out_specs=[pl.BlockSpec((tm, qd), lambda t: (t, 0)),
                   pl.BlockSpec((tm, kd), lambda t: (t, 0)),
                   pl.BlockSpec((tm, 2 * kd), lambda t: (t, 0))],
        out_shape=[jax.ShapeDtypeStruct((lay.n_all, qd), BF16),
                   jax.ShapeDtypeStruct((lay.n_all, kd), BF16),
                   jax.ShapeDtypeStruct((lay.n_all, 2 * kd), BF16)],
        compiler_params=_params("arbitrary"),
        name="gqa_proj",
    )(h, mod, gain.reshape(1, D), w_pad.astype(BF16), cos, sa, sb)

    blk = GQA_BLOCK
    assert S % blk == 0 and C % blk == 0
    nq, nc = S // blk, C // blk
    n_i = nq + (nc if with_ctx_out else 0)
    lat_blocks = lay.n_lat // C

    def q_index(b, i):
        return jnp.where(i < nq, b * nq + i, B * nq + b * nc + (i - nq))

    def win(delta):
        return lambda b, i: (b * nq + jnp.clip(i + delta, 0, nq - 1), 0)

    def windows(width):
        return [pl.BlockSpec((blk, width), win(d)) for d in (-1, 0, 1)]

    def ctx_spec(width):
        return pl.BlockSpec((C, width), lambda b, i: (lat_blocks + b, 0))

    row = pl.BlockSpec((blk, qd), lambda b, i: (q_index(b, i), 0))
    return pl.pallas_call(
        functools.partial(_gqa_attn_kernel, nq=nq),
        grid=(B, n_i),
        in_specs=[pl.BlockSpec(memory_space=pltpu.SMEM), row,
                  *windows(kd), *windows(2 * kd), ctx_spec(kd), ctx_spec(2 * kd)],
        out_specs=row,
        out_shape=jax.ShapeDtypeStruct((lay.n_all if with_ctx_out else lay.n_lat, qd), BF16),
        scratch_shapes=[pltpu.VMEM((3 * blk + C, kd), BF16), pltpu.VMEM((3 * blk + C, 2 * kd), BF16)],
        compiler_params=_params("arbitrary", "arbitrary"),
        name="gqa_attn",
    )(sinks.astype(F32), q, k, k, k, v, v, v, k, v)


def _diff_proj_kernel(h_ref, mod_ref, gain_ref, w_ref, cos_ref, sa_ref, sb_ref, q_ref, k_ref, v_ref):
    qk = DIFF_HEADS * 2 * DIFF_HEAD_DIM
    a = _pre(h_ref[...], gain_ref[...], mod_ref[0], 3).astype(BF16)
    cos, sin = cos_ref[...], sa_ref[...] + sb_ref[...]
    f = DIFF_HEAD_DIM // 4
    scale = DIFF_HEAD_DIM ** -0.5 * LOG2_E
    vw = 2 * DIFF_HEAD_DIM
    ones = jnp.ones((a.shape[0], vw), BF16)

    def emit(col, tile):
        for j in range(MXU_COLS // LANES):
            lo = col + j * LANES
            x = tile[:, j * LANES:(j + 1) * LANES]
            if lo < qk:
                q_ref[:, lo:lo + LANES] = (_rope(x, cos, sin, f) * scale).astype(BF16)
            elif lo < 2 * qk:
                k_ref[:, lo - qk:lo - qk + LANES] = _rope(x, cos, sin, f).astype(BF16)
            else:
                hd = (lo - 2 * qk) // vw
                v_ref[:, 2 * hd * vw:(2 * hd + 1) * vw] = x.astype(BF16)
                v_ref[:, (2 * hd + 1) * vw:(2 * hd + 2) * vw] = ones

    _chunked_proj(a, w_ref, emit)


def _diff_attn_kernel(lam_ref, sub_ref, q_ref, kl_ref, vl_ref, kx_ref, vx_ref, o_ref, *, nq, lambda_init):
    i = pl.program_id(1)
    HD = DIFF_HEAD_DIM
    VW = 4 * HD
    lp = lam_ref[...]
    lam = (jnp.exp(jnp.sum(lp[0:1] * lp[1:2], keepdims=True))
           - jnp.exp(jnp.sum(lp[2:3] * lp[3:4], keepdims=True)) + lambda_init)

    def run(kv_pairs):
        def scores(h):
            los = [(2 * h + j) * HD for j in range(2)]
            return [[_dot_nt(q_ref[:, lo:lo + HD], k_ref[:, lo:lo + HD]) for k_ref, _ in kv_pairs] for lo in los]

        heads = []
        ss = scores(0)
        for h in range(DIFF_HEADS):
            nxt = scores(h + 1) if h + 1 < DIFF_HEADS else None
            maps = []
            for parts in ss:
                m = functools.reduce(jnp.maximum, [jnp.max(s, axis=-1, keepdims=True) for s in parts])
                acc = None
                for s, (_, v_ref) in zip(parts, kv_pairs):
                    t = _dot(jnp.exp2(s - m).astype(BF16), v_ref[:, h * VW:(h + 1) * VW])
                    acc = t if acc is None else acc + t
                maps.append(acc[:, :VW // 2] * (1.0 / acc[:, VW // 2:]))
            o = maps[0] - lam * maps[1]
            heads.append(_rmsnorm(o, sub_ref[...]) * (1.0 - lambda_init))
            ss = nxt
        o_ref[...] = jnp.concatenate(heads, axis=-1).astype(BF16)

    @pl.when(i < nq)
    def _():
        run([(kl_ref, vl_ref), (kx_ref, vx_ref)])

    @pl.when(i >= nq)
    def _():
        run([(kx_ref, vx_ref)])


def _diff_mixer(lay, h, mod, gain, w_in, lam_params, subln_w, lambda_init, with_ctx_out):
    D = h.shape[1]
    B, S, C = lay.B, lay.S, lay.C
    qk = DIFF_HEADS * 2 * DIFF_HEAD_DIM
    tm = lay.tile(ROW_TILE)
    midx, pidx = lay.mod_index(tm), lay.pos_index(tm)
    cos, sa, sb = _rope_tables(S, DIFF_HEAD_DIM, tm, 0)
    tab = pl.BlockSpec((tm, LANES), lambda t: (pidx(t), 0))
    out = jax.ShapeDtypeStruct((lay.n_all, qk), BF16)
    q, k, v = pl.pallas_call(
        _diff_proj_kernel,
        grid=(lay.n_all // tm,),
        in_specs=[pl.BlockSpec((tm, D), lambda t: (t, 0)),
                  pl.BlockSpec((1, N_MOD, D), lambda t: (midx(t), 0, 0)),
                  pl.BlockSpec((1, D), lambda t: (0, 0)),
                  pl.BlockSpec((D, 3 * qk), lambda t: (0, 0)),
                  tab, tab, tab],
        out_specs=[pl.BlockSpec((tm, qk), lambda t: (t, 0))] * 2 + [pl.BlockSpec((tm, 2 * qk), lambda t: (t, 0))],
        out_shape=[out, out, jax.ShapeDtypeStruct((lay.n_all, 2 * qk), BF16)],
        compiler_params=_params("arbitrary"),
        name="diff_proj",
    )(h, mod, gain.reshape(1, D), w_in.astype(BF16), cos, sa, sb)

    tq = _divisor(math.gcd(S, C), QUERY_TILE)
    q_index, nq, nc, lat, ctx = _attn_specs(lay, tq, [qk, 2 * qk])
    row = pl.BlockSpec((tq, qk), lambda b, i: (q_index(b, i), 0))
    return pl.pallas_call(
        functools.partial(_diff_attn_kernel, nq=nq, lambda_init=lambda_init),
        grid=(B, nq + (nc if with_ctx_out else 0)),
        in_specs=[pl.BlockSpec((4, DIFF_HEAD_DIM), lambda b, i: (0, 0)),
                  pl.BlockSpec((1, 2 * DIFF_HEAD_DIM), lambda b, i: (0, 0)),
                  row, *lat, *ctx],
        out_specs=row,
        out_shape=jax.ShapeDtypeStruct((lay.n_all if with_ctx_out else lay.n_lat, qk), BF16),
        compiler_params=_params("arbitrary", "arbitrary"),
        name="diff_attn",
    )(lam_params.astype(F32), subln_w.reshape(1, -1), q, k, v, k, v)


def _hgrn_proj_kernel(h_ref, mod_ref, gain_ref, lb_ref, w_ref, o_ref, *, layer):
    KD = HGRN_HEADS * HGRN_KEY_DIM
    W = 2 * LANES
    a = _pre(h_ref[...], gain_ref[...], mod_ref[0], 3).astype(BF16)
    raw = lb_ref[...]
    e = jnp.exp(raw - jnp.max(raw, axis=0, keepdims=True))
    soft = e * (1.0 / jnp.sum(e, axis=0, keepdims=True))
    lb = jnp.sum(soft[1:layer + 1], axis=0, keepdims=True) if layer else jnp.zeros_like(soft[0:1])
    n = w_ref.shape[1] // W

    def proj(c):
        return _dot(a, w_ref[:, c * W:(c + 1) * W])

    cur = proj(0)
    for c in range(n):
        nxt = proj(c + 1) if c + 1 < n else None
        group, col = divmod(c * W, KD)
        if group in (0, 4):
            out = _silu(cur)
        elif group in (1, 2):
            lbc = lb[:, col:col + W]
            out = lbc + (1.0 - lbc) * jax.nn.sigmoid(cur)
        else:
            out = cur
        o_ref[:, c * W:(c + 1) * W] = out
        cur = nxt


def _running_sum(x, reverse):
    n, w = x.shape
    n_groups = n // SUBLANES
    x = x.reshape(n_groups, SUBLANES, w)
    within = lax.broadcasted_iota(jnp.int32, (1, SUBLANES, 1), 1)
    step = 1
    while step < SUBLANES:
        if reverse:
            x = x + jnp.where(within < SUBLANES - step, pltpu.roll(x, SUBLANES - step, 1), 0.0)
        else:
            x = x + jnp.where(within >= step, pltpu.roll(x, step, 1), 0.0)
        step *= 2
    out = [None] * n_groups
    carry = None
    for j in (reversed(range(n_groups)) if reverse else range(n_groups)):
        out[j] = x[j] if carry is None else x[j] + carry
        carry = out[j][0:1] if reverse else out[j][SUBLANES - 1:SUBLANES]
    return jnp.concatenate(out, axis=0)


def _hgrn_scan_kernel(qf_ref, ff_ref, vf_ref, qb_ref, fb_ref, vb_ref, of_ref, ob_ref, st_ref, *, n_chunks):
    step = pl.program_id(1)
    C = HGRN_CHUNK
    DK, DV, H = HGRN_KEY_DIM, HGRN_VAL_DIM, HGRN_HEADS

    @pl.when(step == 0)
    def _():
        st_ref[...] = jnp.zeros_like(st_ref)

    r = lax.broadcasted_iota(jnp.int32, (C, C), 0)
    c = lax.broadcasted_iota(jnp.int32, (C, C), 1)
    ks = [slice(h * DK, (h + 1) * DK) for h in range(H)]
    vs = [slice(h * DV, (h + 1) * DV) for h in range(H)]
    dirs = [(qf_ref, ff_ref, vf_ref, of_ref, c <= r, C // 2, C - 1),
            (qb_ref, fb_ref, vb_ref, ob_ref, c >= r, C - 1 - C // 2, 0)]

    def prepare(d, n):
        q_ref, f_ref, v_ref, _, tri, mid_row, last_row = dirs[d]
        base = (n if d == 0 else n_chunks - 1 - n) * C
        q = q_ref[base:base + C, :]
        f = f_ref[base:base + C, :]
        vt = v_ref[base:base + C, :].T.astype(BF16)
        k = 1.0 - f
        g = _running_sum(jnp.log2(f), reverse=d == 1)
        g_mid = g[mid_row:mid_row + 1]
        g_last = g[last_row:last_row + 1]
        t = g - g_mid
        qa_f = q * jnp.exp2(t)
        kb_f = k * jnp.exp2(-t)
        qa = qa_f.astype(BF16)
        kb = kb_f.astype(BF16)
        qg = (qa_f * jnp.exp2(g_mid)).astype(BF16)
        kd = (kb_f * jnp.exp2(g_last - g_mid)).astype(BF16)
        a = [jnp.where(tri, _dot_nt(qa[:, ks[h]], kb[:, ks[h]]), 0.0).astype(BF16) for h in range(H)]
        lhs = [jnp.concatenate([qg[:, ks[h]], a[h]], axis=1) for h in range(H)]
        update = [_dot(vt[vs[h], :], kd[:, ks[h]]) for h in range(H)]
        return base, lhs, vt, jnp.exp2(g_last), update

    cur = [prepare(d, 0) for d in range(2)]
    for n in range(n_chunks):
        nxt = [prepare(d, n + 1) for d in range(2)] if n + 1 < n_chunks else None
        for d in range(2):
            base, lhs, vt, decay, update = cur[d]
            st = [st_ref[d, vs[h], :] for h in range(H)]
            outs = [_dot_nt(lhs[h], jnp.concatenate([st[h].astype(BF16), vt[vs[h], :]], axis=1)) for h in range(H)]
            for h in range(H):
                st_ref[d, vs[h], :] = st[h] * decay[:, ks[h]] + update[h]
            dirs[d][3][base:base + C, :] = jnp.concatenate(outs, axis=-1)
        cur = nxt


def _hgrn_readout(of_ref, ob_ref, gate_ref, nw_ref):
    DV = HGRN_VAL_DIM
    o = of_ref[...] + ob_ref[...]
    gate = gate_ref[...]
    nw = nw_ref[...]
    ys = [_rmsnorm(o[:, h * DV:(h + 1) * DV], nw) * gate[:, h * DV:(h + 1) * DV] for h in range(HGRN_HEADS)]
    return jnp.concatenate(ys, axis=-1).astype(BF16)


def _plain_readout(o_ref):
    return o_ref[...]


def _hgrn_mixer(lay, h, mod, gain, w_in, norm_w, lower_bounds, layer):
    D = h.shape[1]
    B, S, C = lay.B, lay.S, lay.C
    KD = HGRN_HEADS * HGRN_KEY_DIM
    tm = lay.tile(NARROW_ROW_TILE)
    midx = lay.mod_index(tm)
    n_cols = w_in.shape[1] // KD
    proj = pl.pallas_call(
        functools.partial(_hgrn_proj_kernel, layer=layer),
        grid=(lay.n_all // tm,),
        in_specs=[pl.BlockSpec((tm, D), lambda t: (t, 0)),
                  pl.BlockSpec((1, N_MOD, D), lambda t: (midx(t), 0, 0)),
                  pl.BlockSpec((1, D), lambda t: (0, 0)),
                  pl.BlockSpec(lower_bounds.shape, lambda t: (0, 0)),
                  pl.BlockSpec((D, n_cols * KD), lambda t: (0, 0), pipeline_mode=pl.Buffered(1))],
        out_specs=pl.BlockSpec((tm, n_cols * KD), lambda t: (t, 0)),
        out_shape=jax.ShapeDtypeStruct((lay.n_all, n_cols * KD), F32),
        compiler_params=_params("arbitrary"),
        name="hgrn_proj",
    )(h, mod, gain.reshape(1, D), lower_bounds.astype(F32), w_in.astype(BF16))

    ts = C
    n_lat_steps = S // ts
    lat_blocks = lay.n_lat // ts

    def rows(d):
        def index(b, s):
            lat = b * n_lat_steps + (s - 1 if d == 0 else n_lat_steps - s)
            return jnp.where(s == 0, lat_blocks + b, lat)
        return index

    def col(d, j):
        return pl.BlockSpec((ts, KD), lambda b, s: (rows(d)(b, s), j))

    o_f, o_b = pl.pallas_call(
        functools.partial(_hgrn_scan_kernel, n_chunks=ts // HGRN_CHUNK),
        grid=(B, 1 + n_lat_steps),
        in_specs=[col(0, 0), col(0, 1), col(0, 3), col(1, 0), col(1, 2), col(1, 3)],
        out_specs=[col(0, 0), col(1, 0)],
        out_shape=[jax.ShapeDtypeStruct((lay.n_all, KD), F32)] * 2,
        scratch_shapes=[pltpu.VMEM((2, HGRN_HEADS * HGRN_VAL_DIM, HGRN_KEY_DIM), F32)],
        compiler_params=_params("arbitrary", "arbitrary"),
        name="hgrn_scan",
    )(proj, proj, proj, proj, proj, proj)

    return _hgrn_readout, [(o_f, 0), (o_b, 0), (proj, 4), (norm_w.reshape(1, -1), 0)]


MLA_QK_PAD = LANES
MLA_KEY_SPLIT = 1024


def _mla_proj_kernel(h_ref, mod_ref, gain_ref, wd_ref, qn_ref, kvn_ref, wuq_ref, wukv_ref,
                     cos_ref, sa_ref, sb_ref, q_ref, k_ref, v_ref):
    H, P = MLA_HEADS, MLA_QK_PAD
    a = _pre(h_ref[...], gain_ref[...], mod_ref[0], 3).astype(BF16)
    dn = _dot(a, wd_ref[...])
    cos, sin = cos_ref[...], sa_ref[...] + sb_ref[...]
    f = MLA_ROPE // 4
    cq = _rmsnorm(dn[:, :MLA_Q_LORA], qn_ref[...]).astype(BF16)
    ckv = _rmsnorm(dn[:, MLA_Q_LORA:MLA_Q_LORA + MLA_KV_LORA], kvn_ref[...]).astype(BF16)
    kr = _rope(dn[:, MLA_Q_LORA + MLA_KV_LORA:MLA_Q_LORA + MLA_KV_LORA + P], cos, sin, f)
    lane = lax.broadcasted_iota(jnp.int32, (1, P), 1)
    ones_half = jnp.where(lane >= MLA_V_DIM, 1.0, 0.0)

    scale = (MLA_NOPE + MLA_ROPE) ** -0.5 * LOG2_E

    def emit_q(col, tile):
        for j in range(MXU_COLS // P):
            lo = col + j * P
            q_ref[:, lo:lo + P] = (_rope(tile[:, j * P:(j + 1) * P], cos, sin, f) * scale).astype(BF16)

    def emit_kv(col, tile):
        for j in range(MXU_COLS // P):
            lo = col + j * P
            x = tile[:, j * P:(j + 1) * P]
            if lo < H * P:
                k_ref[:, lo:lo + P] = (x + kr).astype(BF16)
            else:
                v_ref[:, lo - H * P:lo - H * P + P] = (x + ones_half).astype(BF16)

    _chunked_proj(cq, wuq_ref, emit_q)
    _chunked_proj(ckv, wukv_ref, emit_kv)


def _mla_attn_kernel(q_ref, kl_ref, vl_ref, kx_ref, vx_ref, o_ref, *, nq, with_ctx):
    i = pl.program_id(1)
    P, VD = MLA_QK_PAD, MLA_V_DIM

    def run(kv_pairs):
        def scores(hd):
            qh = q_ref[:, hd * P:(hd + 1) * P]
            return [_dot_nt(qh, k_ref[:, hd * P:(hd + 1) * P]) for k_ref, _ in kv_pairs]

        heads = []
        ss = scores(0)
        for hd in range(MLA_HEADS):
            nxt = scores(hd + 1) if hd + 1 < MLA_HEADS else None
            m = functools.reduce(jnp.maximum, [jnp.max(s, axis=-1, keepdims=True) for s in ss])
            pair = (hd // 2) * MXU_COLS
            acc = None
            for s, (_, v_ref) in zip(ss, kv_pairs):
                n_keys = s.shape[1]
                halves = 2 if n_keys > MLA_KEY_SPLIT else 1
                w = n_keys // halves
                for n in range(halves):
                    e = jnp.exp2(s[:, n * w:(n + 1) * w] - m).astype(BF16)
                    t = _dot(e, v_ref[n * w:(n + 1) * w, pair:pair + MXU_COLS])
                    acc = t if acc is None else acc + t
            acc = acc[:, (hd % 2) * P:(hd % 2 + 1) * P]
            heads.append(acc[:, :VD] * (1.0 / acc[:, VD:]))
            ss = nxt
        o_ref[...] = jnp.concatenate(heads, axis=-1).astype(BF16)

    if not with_ctx:
        run([(kl_ref, vl_ref), (kx_ref, vx_ref)])
        return

    @pl.when(i < nq)
    def _():
        run([(kl_ref, vl_ref), (kx_ref, vx_ref)])

    @pl.when(i >= nq)
    def _():
        run([(kx_ref, vx_ref)])


def _mla_mixer(lay, h, mod, gain, w_down, q_norm_w, kv_norm_w, w_uq, w_ukv, with_ctx_out):
    D = h.shape[1]
    B, S, C = lay.B, lay.S, lay.C
    H, P, VD = MLA_HEADS, MLA_QK_PAD, MLA_V_DIM
    qk = MLA_NOPE + MLA_ROPE
    lora = MLA_Q_LORA + MLA_KV_LORA
    kr_cols = jnp.zeros((D, MXU_COLS), F32).at[:, MLA_NOPE:qk].set(w_down[:, lora:])
    wd = jnp.concatenate([w_down[:, :lora], kr_cols], axis=1).astype(BF16)
    wuq = jnp.pad(w_uq.reshape(MLA_Q_LORA, H, qk), ((0, 0), (0, 0), (0, P - qk))).reshape(MLA_Q_LORA, H * P)
    ukv = w_ukv.reshape(MLA_KV_LORA, H, MLA_NOPE + VD)
    wkn = jnp.pad(ukv[..., :MLA_NOPE], ((0, 0), (0, 0), (0, P - MLA_NOPE))).reshape(MLA_KV_LORA, H * P)
    wv = jnp.pad(ukv[..., MLA_NOPE:], ((0, 0), (0, 0), (0, P - VD))).reshape(MLA_KV_LORA, H * P)
    wukv = jnp.concatenate([wkn, wv], axis=1)

    tm = lay.tile(ROW_TILE)
    midx, pidx = lay.mod_index(tm), lay.pos_index(tm)
    cos, sa, sb = _rope_tables(S, MLA_ROPE, tm, MLA_NOPE)
    tab = pl.BlockSpec((tm, LANES), lambda t: (pidx(t), 0))
    full = lambda shape: pl.BlockSpec(shape, lambda t: (0, 0))
    q, k, v = pl.pallas_call(
        _mla_proj_kernel,
        grid=(lay.n_all // tm,),
        in_specs=[pl.BlockSpec((tm, D), lambda t: (t, 0)),
                  pl.BlockSpec((1, N_MOD, D), lambda t: (midx(t), 0, 0)),
                  full((1, D)), full((D, lora + MXU_COLS)), full((1, MLA_Q_LORA)), full((1, MLA_KV_LORA)),
                  full((MLA_Q_LORA, H * P)), full((MLA_KV_LORA, 2 * H * P)),
                  tab, tab, tab],
        out_specs=[pl.BlockSpec((tm, H * P), lambda t: (t, 0))] * 3,
        out_shape=[jax.ShapeDtypeStruct((lay.n_all, H * P), BF16)] * 3,
        compiler_params=_params("arbitrary"),
        name="mla_proj",
    )(h, mod, gain.reshape(1, D), wd, q_norm_w.reshape(1, -1), kv_norm_w.reshape(1, -1),
      wuq.astype(BF16), wukv.astype(BF16), cos, sa, sb)

    tq = _divisor(math.gcd(S, C), QUERY_TILE)
    q_index, nq, nc, lat, ctx = _attn_specs(lay, tq, [H * P, H * P])
    return pl.pallas_call(
        functools.partial(_mla_attn_kernel, nq=nq, with_ctx=with_ctx_out),
        grid=(B, nq + (nc if with_ctx_out else 0)),
        in_specs=[pl.BlockSpec((tq, H * P), lambda b, i: (q_index(b, i), 0)), *lat, *ctx],
        out_specs=pl.BlockSpec((tq, H * VD), lambda b, i: (q_index(b, i), 0)),
        out_shape=jax.ShapeDtypeStruct((lay.n_all if with_ctx_out else lay.n_lat, H * VD), BF16),
        compiler_params=_params("arbitrary", "arbitrary"),
        name="mla_attn",
    )(q, k, v, k, v)


def kernel(x, c, ctx, c_ctx, ada_w, ada_b, norm_w, final_norm_w, ffn_w_gate, ffn_w_up, ffn_w_down,
           gqa_w_in, gqa_w_out, gqa_sinks, diff_w_in, diff_w_out, diff_lambda, diff_subln_w,
           hgrn_w_in, hgrn_w_out, hgrn_norm_w, hgrn_lower_bounds,
           mla_w_down, mla_q_norm_w, mla_kv_norm_w, mla_w_uq, mla_w_ukv, mla_w_out):
    B, S, D = x.shape
    C = ctx.shape[1]
    lay = _Layout(B, S, C)
    depth = ada_w.shape[0]

    rows = -(-(B + 1) // 8) * 8
    cc = jnp.zeros((rows, D), F32).at[:B].set(c).at[B].set(c_ctx)
    mods = _ada_table(cc, ada_w, ada_b).reshape(depth, rows, N_MOD, D)

    stacks = (ffn_w_gate, ffn_w_up, ffn_w_down)
    weights = tuple(w[0, 0].astype(BF16) for w in stacks)
    h = (x.reshape(B * S, D), ctx.reshape(B * C, D))
    for i in range(depth):
        kind, j = i % 4, i // 4
        last = i == depth - 1
        mod = mods[i]
        h, weights = _ffn(lay, h, mod, norm_w[i, 0], final_norm_w, weights, 0, cast_next=(*stacks, i, 1))
        if kind == 0:
            o = _gqa_mixer(lay, h, mod, norm_w[i, 1], gqa_w_in[j], gqa_sinks[j], not last)
            mix = (_plain_readout, [(o, 0)], gqa_w_out[j])
        elif kind == 1:
            lambda_init = 0.8 - 0.6 * math.exp(-0.3 * i)
            o = _diff_mixer(lay, h, mod, norm_w[i, 1], diff_w_in[j], diff_lambda[j], diff_subln_w[j],
                            lambda_init, not last)
            mix = (_plain_readout, [(o, 0)], diff_w_out[j])
        elif kind == 2:
            readout, operands = _hgrn_mixer(lay, h, mod, norm_w[i, 1], hgrn_w_in[j], hgrn_norm_w[j],
                                            hgrn_lower_bounds, i)
            mix = (readout, operands, hgrn_w_out[j])
        else:
            o = _mla_mixer(lay, h, mod, norm_w[i, 1], mla_w_down[j], mla_q_norm_w[j], mla_kv_norm_w[j],
                           mla_w_uq[j], mla_w_ukv[j], not last)
            mix = (_plain_readout, [(o, 0)], mla_w_out[j])
        h, weights = _ffn(lay, h, mod, norm_w[i, 2], final_norm_w, weights, 1, lat_only=last, final_norm=last,
                          mix=mix, cast_next=None if last else (*stacks, i + 1, 0))
    return h[:B * S].reshape(B, S, D)
```
